```python
import math
import jax, jax.numpy as jnp
from jax import lax
import numpy as np


D_MODEL = 2048
BATCH = 4
SEQ = 2048
DEPTH = 1
DEC_BATCH = 128
DEC_SEQ = 1
PAST_LEN = 16384
PAGE_SIZE = 128

N_MEM = 256
D_FF = int(round(8 * D_MODEL / 3 / 128)) * 128
POOL_WINDOWS = (2, 4, 8, 16)
N_POOL_GROUPS = len(POOL_WINDOWS)
POOL_W = D_MODEL // 4
POOL_GROUP_DIM = POOL_W // N_POOL_GROUPS
POOL_BUF = max(POOL_WINDOWS) - 1
RWKV_W = D_MODEL // 2
RWKV_HEAD = 64
RWKV_HEADS = RWKV_W // RWKV_HEAD
DECAY_LORA = 64
AAA_LORA = 64
GATE_LORA = 160
RWKV_PROJ_W = 3 * RWKV_W + DECAY_LORA + AAA_LORA + GATE_LORA
XA_W = D_MODEL // 4
XA_HEADS = 4
XA_HEAD_DIM = XA_W // XA_HEADS
N_BRANCH = 3
IN_W = POOL_W + RWKV_PROJ_W + XA_W + N_BRANCH * D_MODEL
RMS_EPS = 1e-6
GN_EPS = 64e-5

kernel_name = "gated_pool_rwkv7_memxattn_macaron_step"


def rmsnorm(x, g):
    xf = x.astype(jnp.float32)
    y = xf * lax.rsqrt(jnp.mean(xf * xf, axis=-1, keepdims=True) + RMS_EPS)
    return (y * g.astype(jnp.float32)).astype(x.dtype)


def swiglu_ffn(x, w_gate, w_up, w_down):
    return (jax.nn.silu(x @ w_gate) * (x @ w_up)) @ w_down


def pool_branch(zp, buf, start, group_w, scale, out_w):
    b, t, _ = zp.shape
    full = jnp.concatenate([buf.astype(zp.dtype), zp], axis=1).astype(jnp.float32)
    csum = jnp.concatenate([jnp.zeros((b, 1, POOL_W), jnp.float32), jnp.cumsum(full, axis=1)], axis=1)
    end = csum[:, POOL_BUF + 1:POOL_BUF + 1 + t]
    pos = start + jnp.arange(t)
    means = []
    for gi, w in enumerate(POOL_WINDOWS):
        sl = slice(gi * POOL_GROUP_DIM, (gi + 1) * POOL_GROUP_DIM)
        begin = csum[:, POOL_BUF + 1 - w:POOL_BUF + 1 - w + t, sl]
        cnt = jnp.minimum(pos + 1, w).astype(jnp.float32)[None, :, None]
        means.append((end[..., sl] - begin) / cnt)
    pooled = jnp.concatenate(means, axis=-1) - full[:, POOL_BUF:]
    grp = pooled.reshape(b, t, N_POOL_GROUPS, POOL_GROUP_DIM).astype(zp.dtype)
    mixed = jnp.einsum('btgc,gcd->btgd', grp, group_w).reshape(b, t, POOL_W) * scale
    return mixed @ out_w, full[:, -POOL_BUF:].astype(zp.dtype)


def wkv_step(s, inp):
    r, w, k, v, a, bb = inp
    sa = jnp.einsum('bhij,bhj->bhi', s, a)
    s = s * w[:, :, None, :] + sa[..., None] * bb[:, :, None, :] + v[..., None] * k[:, :, None, :]
    y = jnp.einsum('bhij,bhj->bhi', s, r)
    return s, y


def rwkv7_branch(zr, shift_buf, s0, mu, w0, w_up, a0, a_up, g_up, k_k, k_a, r_k, ln_g, ln_b, out_w):
    f32 = jnp.float32
    b, t, _ = zr.shape
    prev = jnp.concatenate([shift_buf.astype(zr.dtype), zr[:, :-1]], axis=1)
    xm = zr + (prev - zr) * mu
    r, k, v, wl, al, gl = jnp.split(
        xm, [RWKV_W, 2 * RWKV_W, 3 * RWKV_W, 3 * RWKV_W + DECAY_LORA, 3 * RWKV_W + DECAY_LORA + AAA_LORA], axis=-1)
    w_log = -jax.nn.softplus(-(w0 + jnp.tanh(wl) @ w_up).astype(f32)) - 0.5
    decay = jnp.exp(-jnp.exp(w_log))
    a = jax.nn.sigmoid((a0 + al @ a_up).astype(f32))
    g = jax.nn.sigmoid(gl) @ g_up
    heads = lambda x: x.reshape(b, t, RWKV_HEADS, RWKV_HEAD)
    kf = k.astype(f32)
    kk = heads(kf * k_k.astype(f32))
    kk = kk * lax.rsqrt(jnp.maximum(jnp.sum(kk * kk, axis=-1, keepdims=True), 1e-24))
    kmod = heads(kf * (1.0 + (a - 1.0) * k_a.astype(f32)))
    rh, vh, wh, ah = heads(r.astype(f32)), heads(v.astype(f32)), heads(decay), heads(a)
    xs = tuple(jnp.swapaxes(x, 0, 1) for x in (rh, wh, kmod, vh, -kk, kk * ah))
    s_final, ys = lax.scan(wkv_step, s0.astype(f32), xs)
    y = jnp.swapaxes(ys, 0, 1)
    mean = jnp.mean(y, axis=-1, keepdims=True)
    var = jnp.mean(jnp.square(y - mean), axis=-1, keepdims=True)
    y = ((y - mean) * lax.rsqrt(var + GN_EPS)).reshape(b, t, RWKV_W) * ln_g.astype(f32) + ln_b.astype(f32)
    bonus = jnp.sum(rh * kmod * r_k.astype(f32), axis=-1, keepdims=True) * vh
    y = (y + bonus.reshape(b, t, RWKV_W)).astype(zr.dtype) * g
    return y @ out_w, s_final.astype(zr.dtype), zr[:, -1:]


def mem_kv(mem, g, wk, wv):
    b = mem.shape[0]
    m = rmsnorm(mem, g)
    return ((m @ wk).reshape(b, N_MEM, XA_HEADS, XA_HEAD_DIM),
            (m @ wv).reshape(b, N_MEM, XA_HEADS, XA_HEAD_DIM))


def cross_attn(zq, mk, mv, out_w):
    b, t, _ = zq.shape
    q = zq.reshape(b, t, XA_HEADS, XA_HEAD_DIM)
    s = jnp.einsum('bthd,bmhd->bhtm', q, mk.astype(zq.dtype)).astype(jnp.float32) * (XA_HEAD_DIM ** -0.5)
    p = jax.nn.softmax(s, axis=-1).astype(zq.dtype)
    o = jnp.einsum('bhtm,bmhd->bthd', p, mv.astype(zq.dtype)).reshape(b, t, XA_W)
    return o @ out_w


def setup_inputs(seed: int = 0) -> dict:
    key = jax.random.key(seed)
    ks = iter(jax.random.split(key, 64))
    L, D, F = DEPTH, D_MODEL, D_FF

    def nrm(shape, scale=1.0):
        return jax.random.normal(next(ks), shape, jnp.float32) * scale

    def unif(shape):
        return jax.random.uniform(next(ks), shape, jnp.float32)

    return {
        'x_prompt': nrm((BATCH, SEQ, D)),
        'x_sample': nrm((DEC_BATCH, DEC_SEQ, D)),
        'mem_prompt': nrm((BATCH, N_MEM, D)),
        'cache_mem_k': nrm((L, DEC_BATCH, N_MEM, XA_HEADS, XA_HEAD_DIM)),
        'cache_mem_v': nrm((L, DEC_BATCH, N_MEM, XA_HEADS, XA_HEAD_DIM)),
        'state_wkv': nrm((L, DEC_BATCH, RWKV_HEADS, RWKV_HEAD, RWKV_HEAD), 0.5),
        'state_shift': nrm((L, DEC_BATCH, 1, RWKV_PROJ_W)),
        'state_pool': nrm((L, DEC_BATCH, POOL_BUF, POOL_W)),
        'ffn1_norm_g': 1.0 + nrm((L, D), 0.02),
        'ffn1_w_gate': nrm((L, D, F), D ** -0.5),
        'ffn1_w_up': nrm((L, D, F), D ** -0.5),
        'ffn1_w_down': nrm((L, F, D), F ** -0.5),
        'mix_norm_g': 1.0 + nrm((L, D), 0.02),
        'w_in': nrm((L, D, IN_W), D ** -0.5),
        'pool_group_w': nrm((L, N_POOL_GROUPS, POOL_GROUP_DIM, POOL_GROUP_DIM), POOL_GROUP_DIM ** -0.5),
        'pool_scale': 1.0 + nrm((L, POOL_W), 0.02),
        'pool_out': nrm((L, POOL_W, D), POOL_W ** -0.5),
        'rwkv_mu': unif((L, RWKV_PROJ_W)),
        'rwkv_w0': nrm((L, RWKV_W), 0.5),
        'rwkv_w_up': nrm((L, DECAY_LORA, RWKV_W), 0.5 * DECAY_LORA ** -0.5),
        'rwkv_a0': nrm((L, RWKV_W), 0.1),
        'rwkv_a_up': nrm((L, AAA_LORA, RWKV_W), 0.5 * AAA_LORA ** -0.5),
        'rwkv_g_up': nrm((L, GATE_LORA, RWKV_W), GATE_LORA ** -0.5),
        'rwkv_k_k': 1.0 + nrm((L, RWKV_W), 0.1),
        'rwkv_k_a': 1.0 + nrm((L, RWKV_W), 0.1),
        'rwkv_r_k': nrm((L, RWKV_HEADS, RWKV_HEAD), 0.1),
        'rwkv_ln_g': 1.0 + nrm((L, RWKV_W), 0.02),
        'rwkv_ln_b': nrm((L, RWKV_W), 0.01),
        'rwkv_out': nrm((L, RWKV_W, D), RWKV_W ** -0.5),
        'mem_norm_g': 1.0 + nrm((L, D), 0.02),
        'w_mem_k': nrm((L, D, XA_W), D ** -0.5),
        'w_mem_v': nrm((L, D, XA_W), D ** -0.5),
        'xattn_out': nrm((L, XA_W, D), XA_W ** -0.5),
        'w_o': nrm((L, D, D), D ** -0.5),
        'ffn2_norm_g': 1.0 + nrm((L, D), 0.02),
        'ffn2_w_gate': nrm((L, D, F), D ** -0.5),
        'ffn2_w_up': nrm((L, D, F), D ** -0.5),
        'ffn2_w_down': nrm((L, F, D), F ** -0.5),
        'final_norm_g': 1.0 + nrm((D,), 0.02),
    }


def reference(x_prompt, x_sample, mem_prompt, cache_mem_k, cache_mem_v, state_wkv, state_shift, state_pool,
              ffn1_norm_g, ffn1_w_gate, ffn1_w_up, ffn1_w_down, mix_norm_g, w_in,
              pool_group_w, pool_scale, pool_out,
              rwkv_mu, rwkv_w0, rwkv_w_up, rwkv_a0, rwkv_a_up, rwkv_g_up, rwkv_k_k, rwkv_k_a, rwkv_r_k,
              rwkv_ln_g, rwkv_ln_b, rwkv_out,
              mem_norm_g, w_mem_k, w_mem_v, xattn_out, w_o,
              ffn2_norm_g, ffn2_w_gate, ffn2_w_up, ffn2_w_down, final_norm_g):
    splits = [POOL_W, POOL_W + RWKV_PROJ_W, POOL_W + RWKV_PROJ_W + XA_W]

    def block(h, mk, mv, s0, shift_buf, pool_buf, start, l):
        b, t, _ = h.shape
        h = h + 0.5 * swiglu_ffn(rmsnorm(h, ffn1_norm_g[l]), ffn1_w_gate[l], ffn1_w_up[l], ffn1_w_down[l])
        u = rmsnorm(h, mix_norm_g[l])
        z = u @ w_in[l]
        zp, zr, zq, zg = jnp.split(z, splits, axis=-1)
        out_a, new_pool = pool_branch(zp, pool_buf, start, pool_group_w[l], pool_scale[l], pool_out[l])
        out_b, new_wkv, new_shift = rwkv7_branch(
            zr, shift_buf, s0, rwkv_mu[l], rwkv_w0[l], rwkv_w_up[l], rwkv_a0[l], rwkv_a_up[l], rwkv_g_up[l],
            rwkv_k_k[l], rwkv_k_a[l], rwkv_r_k[l], rwkv_ln_g[l], rwkv_ln_b[l], rwkv_out[l])
        out_c = cross_attn(zq, mk, mv, xattn_out[l])
        gates = jax.nn.sigmoid(zg.astype(jnp.float32)).astype(h.dtype).reshape(b, t, N_BRANCH, D_MODEL)
        merged = gates[:, :, 0] * out_a + gates[:, :, 1] * out_b + gates[:, :, 2] * out_c
        h = h + merged @ w_o[l]
        h = h + 0.5 * swiglu_ffn(rmsnorm(h, ffn2_norm_g[l]), ffn2_w_gate[l], ffn2_w_up[l], ffn2_w_down[l])
        return h, new_wkv, new_shift, new_pool

    hp, hs = x_prompt, x_sample
    bp = x_prompt.shape[0]
    mk_p_l, mv_p_l, wkv_p_l, sh_p_l, pl_p_l = [], [], [], [], []
    wkv_s_l, sh_s_l, pl_s_l = [], [], []
    for l in range(DEPTH):
        mk_p, mv_p = mem_kv(mem_prompt, mem_norm_g[l], w_mem_k[l], w_mem_v[l])
        s0p = jnp.zeros((bp, RWKV_HEADS, RWKV_HEAD, RWKV_HEAD), jnp.float32)
        shift0 = jnp.zeros((bp, 1, RWKV_PROJ_W), hp.dtype)
        pool0 = jnp.zeros((bp, POOL_BUF, POOL_W), hp.dtype)
        hp, wkv_p, sh_p, pl_p = block(hp, mk_p, mv_p, s0p, shift0, pool0, 0, l)
        hs, wkv_s, sh_s, pl_s = block(hs, cache_mem_k[l], cache_mem_v[l], state_wkv[l], state_shift[l],
                                      state_pool[l], PAST_LEN, l)
        mk_p_l.append(mk_p); mv_p_l.append(mv_p); wkv_p_l.append(wkv_p); sh_p_l.append(sh_p); pl_p_l.append(pl_p)
        wkv_s_l.append(wkv_s); sh_s_l.append(sh_s); pl_s_l.append(pl_s)
    y_prompt = rmsnorm(hp, final_norm_g)
    y_sample = rmsnorm(hs, final_norm_g)
    return (y_prompt, y_sample, jnp.stack(mk_p_l), jnp.stack(mv_p_l), jnp.stack(wkv_p_l), jnp.stack(sh_p_l),
            jnp.stack(pl_p_l), jnp.stack(wkv_s_l), jnp.stack(sh_s_l), jnp.stack(pl_s_l))
```

```python
import functools
import math

import jax
import jax.numpy as jnp
from jax import lax
from jax.experimental import pallas as pl
from jax.experimental.pallas import tpu as pltpu

F32 = jnp.float32
BF16 = jnp.bfloat16
HI = lax.Precision.HIGHEST

RMS_EPS = 1e-6
GN_EPS = 64e-5
POOL_WINDOWS = (2, 4, 8, 16)
HEAD = 64
LANES = 128
XA_HEADS = 4
XA_DIM = 128
PAST_LEN = 16384
VMEM_LIMIT = 56 * 1024 * 1024
EXP_M05 = math.exp(-0.5)

ZR_W = 3584
RWKV_W = 1024
ZR_TRUE = 3360


def _cparams(sem):
    return pltpu.CompilerParams(dimension_semantics=sem, vmem_limit_bytes=VMEM_LIMIT)


def _rms(x, g):
    ms = jnp.mean(x * x, axis=-1, keepdims=True)
    return x * lax.rsqrt(ms + RMS_EPS) * g


def _ffn_kernel(h_ref, g_ref, wg_ref, wu_ref, wd_ref, *rest, nf, final):
    if final:
        fg_ref, o_ref, xn_ref, acc_ref = rest
    else:
        o_ref, xn_ref, acc_ref = rest
    f = pl.program_id(1)

    @pl.when(f == 0)
    def _():
        xn_ref[...] = _rms(h_ref[...], g_ref[...]).astype(BF16)
        acc_ref[...] = jnp.zeros_like(acc_ref)

    xn = xn_ref[...]
    gate = jnp.dot(xn, wg_ref[...], preferred_element_type=F32)
    up = jnp.dot(xn, wu_ref[...], preferred_element_type=F32)
    act = (gate * jax.nn.sigmoid(gate) * up).astype(BF16)
    acc_ref[...] += jnp.dot(act, wd_ref[...], preferred_element_type=F32)

    @pl.when(f == nf - 1)
    def _():
        out = h_ref[...] + 0.5 * acc_ref[...]
        if final:
            out = _rms(out, fg_ref[...])
        o_ref[...] = out


def _ffn(h, in_blk0, nblk, tm, g, wg, wu, wd, *, out_rows, out_blk0, prev_out=None, final_g=None, tf=512):
    d = h.shape[1]
    fp = wg.shape[1]
    nf = fp // tf
    final = final_g is not None
    in_specs = [
        pl.BlockSpec((tm, d), lambda m, f: (in_blk0 + m, 0)),
        pl.BlockSpec((1, d), lambda m, f: (0, 0)),
        pl.BlockSpec((d, tf), lambda m, f: (0, f)),
        pl.BlockSpec((d, tf), lambda m, f: (0, f)),
        pl.BlockSpec((tf, d), lambda m, f: (f, 0)),
    ]
    args = [h, g, wg, wu, wd]
    if final:
        in_specs.append(pl.BlockSpec((1, d), lambda m, f: (0, 0)))
        args.append(final_g)
    aliases = {}
    if prev_out is not None:
        in_specs.append(pl.BlockSpec(memory_space=pl.ANY))
        args.append(prev_out)
        aliases = {len(args) - 1: 0}
    kern = functools.partial(_ffn_kernel, nf=nf, final=final)
    if prev_out is not None:
        inner = kern
        kern = lambda *refs: inner(*refs[:len(args) - 1], *refs[len(args):])
    return pl.pallas_call(
        kern,
        grid=(nblk, nf),
        in_specs=in_specs,
        out_specs=pl.BlockSpec((tm, d), lambda m, f: (out_blk0 + m, 0)),
        out_shape=jax.ShapeDtypeStruct((out_rows, d), F32),
        scratch_shapes=[pltpu.VMEM((tm, d), BF16), pltpu.VMEM((tm, d), F32)],
        input_output_aliases=aliases,
        compiler_params=_cparams(("parallel", "arbitrary")),
        name="ffn",
    )(*args)


def _norm_matmul_kernel(h_ref, g_ref, w_ref, o_ref, xn_ref):
    @pl.when(pl.program_id(1) == 0)
    def _():
        xn_ref[...] = _rms(h_ref[...], g_ref[...]).astype(BF16)

    o_ref[...] = jnp.dot(xn_ref[...], w_ref[...], preferred_element_type=F32)


def _norm_matmul(h, g, w, tm, tn):
    m, d = h.shape
    n = w.shape[1]
    return pl.pallas_call(
        _norm_matmul_kernel,
        grid=(m // tm, n // tn),
        in_specs=[
            pl.BlockSpec((tm, d), lambda i, j: (i, 0)),
            pl.BlockSpec((1, d), lambda i, j: (0, 0)),
            pl.BlockSpec((d, tn), lambda i, j: (0, j)),
        ],
        out_specs=pl.BlockSpec((tm, tn), lambda i, j: (i, j)),
        out_shape=jax.ShapeDtypeStruct((m, n), F32),
        scratch_shapes=[pltpu.VMEM((tm, d), BF16)],
        compiler_params=_cparams(("parallel", "arbitrary")),
        name="norm_matmul",
    )(h, g, w)


def _pool_mix(pooled_groups, gw_ref, scale_ref, o_ref):
    for gi, pooled in enumerate(pooled_groups):
        sl = slice(gi * LANES, (gi + 1) * LANES)
        mixed = jnp.dot(pooled.astype(BF16), gw_ref[gi], preferred_element_type=F32)
        o_ref[:, sl] = (mixed * scale_ref[:, sl]).astype(o_ref.dtype)


def _pool_prompt_kernel(zp_ref, gw_ref, scale_ref, o_ref, ext_ref, *, tt):
    t = pl.program_id(1)
    hist = 16

    @pl.when(t == 0)
    def _():
        ext_ref[0:hist, :] = jnp.zeros((hist, ext_ref.shape[1]), F32)

    x = zp_ref[...]
    ext_ref[hist:hist + tt, :] = x
    pos = t * tt + lax.broadcasted_iota(jnp.int32, (tt, LANES), 0)
    groups = []
    for gi, w in enumerate(POOL_WINDOWS):
        sl = slice(gi * LANES, (gi + 1) * LANES)
        acc = x[:, sl]
        for k in range(1, w):
            acc = acc + ext_ref[hist - k:hist - k + tt, sl]
        cnt = jnp.minimum(pos + 1, w).astype(F32)
        groups.append(acc / cnt - x[:, sl])
    _pool_mix(groups, gw_ref, scale_ref, o_ref)
    ext_ref[0:hist, :] = ext_ref[tt:tt + hist, :]


def _pool_prompt(z, nb, t_len, col_blk, gw, scale, tt=256):
    nt = t_len // tt
    pw = gw.shape[0] * LANES
    return pl.pallas_call(
        functools.partial(_pool_prompt_kernel, tt=tt),
        grid=(nb, nt),
        in_specs=[
            pl.BlockSpec((tt, pw), lambda b, t: (b * nt + t, col_blk)),
            pl.BlockSpec(gw.shape, lambda b, t: (0, 0, 0)),
            pl.BlockSpec((1, pw), lambda b, t: (0, 0)),
        ],
        out_specs=pl.BlockSpec((tt, pw), lambda b, t: (b * nt + t, 0)),
        out_shape=jax.ShapeDtypeStruct((nb * t_len, pw), BF16),
        scratch_shapes=[pltpu.VMEM((tt + 16, pw), F32)],
        compiler_params=_cparams(("parallel", "arbitrary")),
        name="pool_prompt",
    )(z, gw, scale)


def _pool_sample_kernel(zp_ref, buf_ref, gw_ref, scale_ref, o_ref):
    x = zp_ref[...]
    nbuf = buf_ref.shape[0]
    groups = []
    for gi, w in enumerate(POOL_WINDOWS):
        sl = slice(gi * LANES, (gi + 1) * LANES)
        acc = x[:, sl]
        for k in range(1, w):
            acc = acc + buf_ref[nbuf - k, :, sl]
        cnt = float(min(PAST_LEN + 1, w))
        groups.append(acc / cnt - x[:, sl])
    _pool_mix(groups, gw_ref, scale_ref, o_ref)


def _pool_sample(z, row_blk, nrows, col_blk, buf_t, gw, scale):
    pw = gw.shape[0] * LANES
    return pl.pallas_call(
        _pool_sample_kernel,
        grid=(1,),
        in_specs=[
            pl.BlockSpec((nrows, pw), lambda i: (row_blk, col_blk)),
            pl.BlockSpec(buf_t.shape, lambda i: (0, 0, 0)),
            pl.BlockSpec(gw.shape, lambda i: (0, 0, 0)),
            pl.BlockSpec((1, pw), lambda i: (0, 0)),
        ],
        out_specs=pl.BlockSpec((nrows, pw), lambda i: (0, 0)),
        out_shape=jax.ShapeDtypeStruct((nrows, pw), BF16),
        compiler_params=_cparams(("arbitrary",)),
        name="pool_sample",
    )(z, buf_t, gw, scale)


def _pair_block_ones():
    r = lax.broadcasted_iota(jnp.int32, (LANES, LANES), 0) // HEAD
    c = lax.broadcasted_iota(jnp.int32, (LANES, LANES), 1) // HEAD
    return (r == c).astype(F32)


def _head_sum(x, bd):
    cols = [jnp.dot(x[:, c * LANES:(c + 1) * LANES], bd, precision=HI, preferred_element_type=F32)
            for c in range(x.shape[1] // LANES)]
    return jnp.concatenate(cols, axis=1)


def _prep_math(x, prev, p_refs, out_refs):
    mu_ref, w0_ref, wup_ref, a0_ref, aup_ref, gup_ref, kk_ref, ka_ref, rk_ref = p_refs
    r_ref, lw_ref, k_ref, v_ref, a_ref, b_ref, g_ref, bonus_ref = out_refs
    xm = x + (prev - x) * mu_ref[...]
    w = RWKV_W
    r = xm[:, 0:w]
    k = xm[:, w:2 * w]
    v = xm[:, 2 * w:3 * w]
    wl = xm[:, 3 * w:3 * w + 128]
    al = xm[:, 3 * w + 128:3 * w + 256]
    gl = xm[:, 3 * w + 256:3 * w + 512]
    bd = _pair_block_ones()
    dw = w0_ref[...] + jnp.dot(jnp.tanh(wl).astype(BF16), wup_ref[...], preferred_element_type=F32)
    lw_ref[...] = -EXP_M05 * jax.nn.sigmoid(dw)
    a = jax.nn.sigmoid(a0_ref[...] + jnp.dot(al.astype(BF16), aup_ref[...], preferred_element_type=F32))
    g_ref[...] = jnp.dot(jax.nn.sigmoid(gl).astype(BF16), gup_ref[...], preferred_element_type=F32)
    kk = k * kk_ref[...]
    kk = kk * lax.rsqrt(jnp.maximum(_head_sum(kk * kk, bd), 1e-24))
    kmod = k * (1.0 + (a - 1.0) * ka_ref[...])
    r_ref[...] = r
    k_ref[...] = kmod
    v_ref[...] = v
    a_ref[...] = -kk
    b_ref[...] = kk * a
    bonus_ref[...] = _head_sum(r * kmod * rk_ref[...], bd) * v


def _prep_prompt_kernel(zr_ref, *refs):
    p_refs, out_refs, carry_ref = refs[:9], refs[9:17], refs[17]
    t = pl.program_id(1)

    @pl.when(t == 0)
    def _():
        carry_ref[...] = jnp.zeros_like(carry_ref)

    x = zr_ref[...]
    rolled = pltpu.roll(x, 1, axis=0)
    first = lax.broadcasted_iota(jnp.int32, x.shape, 0) == 0
    prev = jnp.where(first, carry_ref[0:1, :], rolled)
    carry_ref[...] = rolled[0:8, :]
    _prep_math(x, prev, p_refs, out_refs)


def _prep_sample_kernel(zr_ref, prev_ref, *refs):
    _prep_math(zr_ref[...], prev_ref[...], refs[:9], refs[9:17])


def _prep_param_specs(params, nidx):
    zero = (lambda *idx: (0, 0))
    return [pl.BlockSpec(p.shape, zero) for p in params]


def _prep_prompt(z, nb, t_len, params, tt=256):
    nt = t_len // tt
    rows = nb * t_len
    row_spec = pl.BlockSpec((tt, RWKV_W), lambda b, t: (b * nt + t, 0))
    return pl.pallas_call(
        _prep_prompt_kernel,
        grid=(nb, nt),
        in_specs=[pl.BlockSpec((tt, ZR_W), lambda b, t: (b * nt + t, 0))] + _prep_param_specs(params, 2),
        out_specs=[row_spec] * 8,
        out_shape=[jax.ShapeDtypeStruct((rows, RWKV_W), F32)] * 8,
        scratch_shapes=[pltpu.VMEM((8, ZR_W), F32)],
        compiler_params=_cparams(("parallel", "arbitrary")),
        name="prep_prompt",
    )(z, *params)


def _prep_sample(z, row_blk, nrows, prev, params):
    row_spec = pl.BlockSpec((nrows, RWKV_W), lambda i: (0, 0))
    return pl.pallas_call(
        _prep_sample_kernel,
        grid=(1,),
        in_specs=[pl.BlockSpec((nrows, ZR_W), lambda i: (row_blk, 0)),
                  pl.BlockSpec((nrows, ZR_W), lambda i: (0, 0))] + _prep_param_specs(params, 1),
        out_specs=[row_spec] * 8,
        out_shape=[jax.ShapeDtypeStruct((nrows, RWKV_W), F32)] * 8,
        compiler_params=_cparams(("arbitrary",)),
        name="prep_sample",
    )(z, prev, *params)


def _dot_hi(a, b):
    return jnp.dot(a, b, precision=HI, preferred_element_type=F32)


def _wkv_chunk_kernel(r_ref, lw_ref, k_ref, v_ref, a_ref, b_ref, y_ref, sout_ref, st_ref, *, c_len, pp, nc):
    c = pl.program_id(2)
    n2 = 2 * c_len

    @pl.when(c == 0)
    def _():
        st_ref[...] = jnp.zeros_like(st_ref)

    row = lax.broadcasted_iota(jnp.int32, (n2, n2), 0)
    col = lax.broadcasted_iota(jnp.int32, (n2, n2), 1)
    tr = row & (c_len - 1)
    tc = col & (c_len - 1)
    strict = tr > tc
    incl = tr >= tc
    eye = row == col
    eye_l = (lax.broadcasted_iota(jnp.int32, (LANES, LANES), 0)
             == lax.broadcasted_iota(jnp.int32, (LANES, LANES), 1))
    tri =(lax.broadcasted_iota(jnp.int32, (c_len, c_len), 0)
           >= lax.broadcasted_iota(jnp.int32, (c_len, c_len), 1)).astype(F32)
    head_a = lax.broadcasted_iota(jnp.int32, (c_len, LANES), 1) < HEAD

    def stack(x):
        return jnp.concatenate([jnp.where(head_a, x, 0.0), jnp.where(head_a, 0.0, x)], axis=0)

    for q in range(pp):
        sl = slice(q * LANES, (q + 1) * LANES)
        lw = lw_ref[:, sl]
        cum = _dot_hi(tri, lw)
        tot = cum[c_len - 1:c_len, :]
        e_inc = jnp.exp(cum)
        e_exc = jnp.exp(cum - lw)
        e_neg = jnp.exp(-cum)
        e_rem = jnp.exp(tot - cum)
        xr = stack(r_ref[:, sl] * e_inc)
        xa = stack(a_ref[:, sl] * e_exc)
        bq = b_ref[:, sl]
        kq = k_ref[:, sl]
        yb = stack(bq * e_neg)
        yk = stack(kq * e_neg)
        zb = stack(bq * e_rem)
        zk = stack(kq * e_rem)
        vs = stack(v_ref[:, sl])

        g = lax.dot_general(jnp.concatenate([xa, xr], axis=0), jnp.concatenate([yb, yk], axis=0),
                            (((1,), (1,)), ((), ())), precision=HI, preferred_element_type=F32)
        m_ab = jnp.where(strict, g[0:n2, 0:n2], 0.0)
        m_ak = jnp.where(strict, g[0:n2, n2:2 * n2], 0.0)
        n_rb = jnp.where(incl, g[n2:2 * n2, 0:n2], 0.0)
        n_rk = jnp.where(incl, g[n2:2 * n2, n2:2 * n2], 0.0)

        t_inv = jnp.where(eye, 1.0, 0.0) + m_ab
        mk = _dot_hi(m_ab, m_ab)
        nlev = int(math.log2(c_len))
        for lev in range(1, nlev):
            if lev < nlev - 1:
                res = _dot_hi(jnp.concatenate([mk, t_inv], axis=0), mk)
                mk = res[0:n2]
                t_inv = t_inv + res[n2:2 * n2]
            else:
                t_inv = t_inv + _dot_hi(t_inv, mk)

        st = st_ref[q]
        lhs = jnp.concatenate([jnp.concatenate([xa, m_ak], axis=1),
                               jnp.concatenate([xr, n_rk], axis=1)], axis=0)
        xy0 = _dot_hi(lhs, jnp.concatenate([st, vs], axis=0))
        u = _dot_hi(t_inv, xy0[0:n2])
        y_st = xy0[n2:2 * n2] + _dot_hi(n_rb, u)
        y_ref[:, sl] = y_st[0:c_len] + y_st[c_len:n2]

        dec = jnp.where(eye_l, jnp.broadcast_to(jnp.exp(tot), (LANES, LANES)), 0.0)
        st_new = lax.dot_general(jnp.concatenate([zb, zk, dec], axis=0), jnp.concatenate([u, vs, st], axis=0),
                                 (((0,), (0,)), ((), ())), precision=HI, preferred_element_type=F32)
        st_ref[q] = st_new

    @pl.when(c == nc - 1)
    def _():
        sout_ref[0] = st_ref[...]


def _wkv_chunk(seqs, nb, t_len, c_len=64, pp=2):
    nc = t_len // c_len
    npair = RWKV_W // LANES
    ng = npair // pp
    blk = pl.BlockSpec((c_len, pp * LANES), lambda b, g, c: (b * nc + c, g))
    return pl.pallas_call(
        functools.partial(_wkv_chunk_kernel, c_len=c_len, pp=pp, nc=nc),
        grid=(nb, ng, nc),
        in_specs=[blk] * 6,
        out_specs=[blk, pl.BlockSpec((1, pp, LANES, LANES), lambda b, g, c: (b, g, 0, 0))],
        out_shape=[jax.ShapeDtypeStruct((nb * t_len, RWKV_W), F32),
                   jax.ShapeDtypeStruct((nb, npair, LANES, LANES), F32)],
        scratch_shapes=[pltpu.VMEM((pp, LANES, LANES), F32)],
        compiler_params=_cparams(("parallel", "parallel", "arbitrary")),
        name="wkv_chunk",
    )(*seqs)


def _wkv_step_kernel(s_ref, r_ref, lw_ref, k_ref, v_ref, a_ref, b_ref, y_ref, so_ref):
    s = s_ref[...]
    eye = (lax.broadcasted_iota(jnp.int32, (HEAD, HEAD), 0)
           == lax.broadcasted_iota(jnp.int32, (HEAD, HEAD), 1))
    sa = jnp.sum(s * a_ref[...], axis=-1, keepdims=True)
    vcol = jnp.sum(jnp.where(eye, v_ref[...], 0.0), axis=-1, keepdims=True)
    s2 = s * jnp.exp(lw_ref[...]) + sa * b_ref[...] + vcol * k_ref[...]
    so_ref[...] = s2
    ycol = jnp.sum(s2 * r_ref[...], axis=-1, keepdims=True)
    y_ref[...] = jnp.sum(jnp.where(eye, ycol, 0.0), axis=-2, keepdims=True)


def _wkv_step(state, seqs, bb=8):
    nb, nh = state.shape[0], state.shape[1]
    vec = pl.BlockSpec((bb, nh, 1, HEAD), lambda i: (i, 0, 0, 0))
    st = pl.BlockSpec((bb, nh, HEAD, HEAD), lambda i: (i, 0, 0, 0))
    return pl.pallas_call(
        _wkv_step_kernel,
        grid=(nb // bb,),
        in_specs=[st] + [vec] * 6,
        out_specs=[vec, st],
        out_shape=[jax.ShapeDtypeStruct((nb, nh, 1, HEAD), F32), jax.ShapeDtypeStruct(state.shape, F32)],
        compiler_params=_cparams(("parallel",)),
        name="wkv_step",
    )(state, *seqs)


def _rwkv_post_kernel(y_ref, bonus_ref, g_ref, lng_ref, lnb_ref, o_ref):
    bd = _pair_block_ones() * (1.0 / HEAD)
    y = y_ref[...]
    d = y - _head_sum(y, bd)
    var = _head_sum(d * d, bd)
    yn = d * lax.rsqrt(var + GN_EPS) * lng_ref[...] + lnb_ref[...]
    o_ref[...] = ((yn + bonus_ref[...]) * g_ref[...]).astype(o_ref.dtype)


def _rwkv_post(y, bonus, g, ln_g, ln_b, tm):
    m = y.shape[0]
    row = pl.BlockSpec((tm, RWKV_W), lambda i: (i, 0))
    par = pl.BlockSpec((1, RWKV_W), lambda i: (0, 0))
    return pl.pallas_call(
        _rwkv_post_kernel,
        grid=(m // tm,),
        in_specs=[row, row, row, par, par],
        out_specs=row,
        out_shape=jax.ShapeDtypeStruct((m, RWKV_W), BF16),
        compiler_params=_cparams(("parallel",)),
        name="rwkv_post",
    )(y, bonus, g, ln_g, ln_b)


def _xattn_prompt_kernel(q_ref, k_ref, v_ref, o_ref):
    scale = XA_DIM ** -0.5
    q = q_ref[...]
    for h in range(XA_HEADS):
        sl = slice(h * XA_DIM, (h + 1) * XA_DIM)
        s = lax.dot_general(q[:, sl].astype(BF16), k_ref[0, :, sl].astype(BF16),
                            (((1,), (1,)), ((), ())), preferred_element_type=F32) * scale
        p = jnp.exp(s - jnp.max(s, axis=-1, keepdims=True))
        den = jnp.sum(p, axis=-1, keepdims=True)
        o = jnp.dot(p.astype(BF16), v_ref[0, :, sl].astype(BF16), preferred_element_type=F32)
        o_ref[:, sl] = (o / den).astype(o_ref.dtype)


def _xattn_prompt(z, nb, t_len, col_blk, mk, mv, tq=512):
    nt = t_len // tq
    xw = XA_HEADS * XA_DIM
    nmem = mk.shape[1]
    kv = pl.BlockSpec((1, nmem, xw), lambda b, t: (b, 0, 0))
    return pl.pallas_call(
        _xattn_prompt_kernel,
        grid=(nb, nt),
        in_specs=[pl.BlockSpec((tq, xw), lambda b, t: (b * nt + t, col_blk)), kv, kv],
        out_specs=pl.BlockSpec((tq, xw), lambda b, t: (b * nt + t, 0)),
        out_shape=jax.ShapeDtypeStruct((nb * t_len, xw), BF16),
        compiler_params=_cparams(("parallel", "parallel")),
        name="xattn_prompt",
    )(z, mk, mv)


def _xattn_sample_kernel(q_ref, k_ref, v_ref, o_ref):
    scale = XA_DIM ** -0.5
    for h in range(XA_HEADS):
        sl = slice(h * XA_DIM, (h + 1) * XA_DIM)
        q = q_ref[:, :, sl]
        s = jnp.sum(k_ref[:, :, sl] * q, axis=-1, keepdims=True) * scale
        p = jnp.exp(s - jnp.max(s, axis=1, keepdims=True))
        den = jnp.sum(p, axis=1, keepdims=True)
        o = jnp.sum(p * v_ref[:, :, sl], axis=1, keepdims=True)
        o_ref[:, :, sl] = (o / den).astype(o_ref.dtype)


def _xattn_sample(q3, mk, mv, bb=8):
    nb, nmem, xw = mk.shape
    kv = pl.BlockSpec((bb, nmem, xw), lambda i: (i, 0, 0))
    qs = pl.BlockSpec((bb, 1, xw), lambda i: (i, 0, 0))
    return pl.pallas_call(
        _xattn_sample_kernel,
        grid=(nb // bb,),
        in_specs=[qs, kv, kv],
        out_specs=qs,
        out_shape=jax.ShapeDtypeStruct((nb, 1, xw), BF16),
        compiler_params=_cparams(("parallel",)),
        name="xattn_sample",
    )(q3, mk, mv)


def _merge_kernel(pa_ref, pb_ref, pc_ref, g0_ref, g1_ref, g2_ref, h_ref, wa_ref, wb_ref, wc_ref, wo_ref, *rest):
    o_ref = rest[-1]
    oa = jnp.dot(pa_ref[...], wa_ref[...], preferred_element_type=F32)
    ob = jnp.dot(pb_ref[...], wb_ref[...], preferred_element_type=F32)
    oc = jnp.dot(pc_ref[...], wc_ref[...], preferred_element_type=F32)
    merged = (jax.nn.sigmoid(g0_ref[...]) * oa + jax.nn.sigmoid(g1_ref[...]) * ob
              + jax.nn.sigmoid(g2_ref[...]) * oc)
    o_ref[...] = h_ref[...] + jnp.dot(merged.astype(BF16), wo_ref[...], preferred_element_type=F32)


def _merge(pa, pb, pc, z, zg_blk0, h, row_blk0, nblk, tm, wa, wb, wc, wo, *, out_rows, prev_out=None):
    d = h.shape[1]
    const = lambda i: (0, 0)
    in_specs = [
        pl.BlockSpec((tm, pa.shape[1]), lambda i: (i, 0)),
        pl.BlockSpec((tm, pb.shape[1]), lambda i: (i, 0)),
        pl.BlockSpec((tm, pc.shape[1]), lambda i: (i, 0)),
        pl.BlockSpec((tm, d), lambda i: (row_blk0 + i, zg_blk0)),
        pl.BlockSpec((tm, d), lambda i: (row_blk0 + i, zg_blk0 + 1)),
        pl.BlockSpec((tm, d), lambda i: (row_blk0 + i, zg_blk0 + 2)),
        pl.BlockSpec((tm, d), lambda i: (row_blk0 + i, 0)),
        pl.BlockSpec(wa.shape, const, pipeline_mode=pl.Buffered(1)),
        pl.BlockSpec(wb.shape, const, pipeline_mode=pl.Buffered(1)),
        pl.BlockSpec(wc.shape, const, pipeline_mode=pl.Buffered(1)),
        pl.BlockSpec(wo.shape, const, pipeline_mode=pl.Buffered(1)),
    ]
    args = [pa, pb, pc, z, z, z, h, wa, wb, wc, wo]
    aliases = {}
    if prev_out is not None:
        in_specs.append(pl.BlockSpec(memory_space=pl.ANY))
        args.append(prev_out)
        aliases = {len(args) - 1: 0}
    return pl.pallas_call(
        _merge_kernel,
        grid=(nblk,),
        in_specs=in_specs,
        out_specs=pl.BlockSpec((tm, d), lambda i: (row_blk0 + i, 0)),
        out_shape=jax.ShapeDtypeStruct((out_rows, d), F32),
        input_output_aliases=aliases,
        compiler_params=_cparams(("parallel",)),
        name="merge",
    )(*args)


def _pack_zr_cols(x):
    w = RWKV_W
    pad = lambda n: jnp.zeros(x.shape[:-1] + (n,), x.dtype)
    return jnp.concatenate([x[..., :3 * w + 64], pad(64), x[..., 3 * w + 64:3 * w + 128], pad(64),
                            x[..., 3 * w + 128:], pad(96)], axis=-1)


def _unpack_zr_cols(x):
    w = RWKV_W
    return jnp.concatenate([x[..., :3 * w + 64], x[..., 3 * w + 128:3 * w + 192], x[..., 3 * w + 256:3 * w + 416]],
                           axis=-1)


def _pad_rows(x, n):
    return jnp.concatenate([x, jnp.zeros((n - x.shape[0],) + x.shape[1:], x.dtype)], axis=0)


def kernel(x_prompt, x_sample, mem_prompt, cache_mem_k, cache_mem_v, state_wkv, state_shift, state_pool,
           ffn1_norm_g, ffn1_w_gate, ffn1_w_up, ffn1_w_down, mix_norm_g, w_in,
           pool_group_w, pool_scale, pool_out,
           rwkv_mu, rwkv_w0, rwkv_w_up, rwkv_a0, rwkv_a_up, rwkv_g_up, rwkv_k_k, rwkv_k_a, rwkv_r_k,
           rwkv_ln_g, rwkv_ln_b, rwkv_out,
           mem_norm_g, w_mem_k, w_mem_v, xattn_out, w_o,
           ffn2_norm_g, ffn2_w_gate, ffn2_w_up, ffn2_w_down, final_norm_g):
    nb, t_len, d = x_prompt.shape
    ns = x_sample.shape[0]
    depth = w_in.shape[0]
    assert depth == 1 and x_sample.shape[1] == 1
    n_mem = mem_prompt.shape[1]
    pool_w = pool_out.shape[1]
    xa_w = xattn_out.shape[1]
    n_heads = RWKV_W // HEAD
    rows_p = nb * t_len
    rows = rows_p + ns
    l = 0

    d_ff = ffn1_w_gate.shape[2]
    fpad = (-d_ff) % 512

    def ffn_weights(wg, wu, wd):
        return (jnp.pad(wg, ((0, 0), (0, fpad))).astype(BF16), jnp.pad(wu, ((0, 0), (0, fpad))).astype(BF16),
                jnp.pad(wd, ((0, fpad), (0, 0))).astype(BF16))

    f1 = ffn_weights(ffn1_w_gate[l], ffn1_w_up[l], ffn1_w_down[l])
    f2 = ffn_weights(ffn2_w_gate[l], ffn2_w_up[l], ffn2_w_down[l])
    wi = w_in[l]
    o_zr, o_zq, o_zg = pool_w, pool_w + ZR_TRUE, pool_w + ZR_TRUE + xa_w
    w_in_p = jnp.concatenate([_pack_zr_cols(wi[:, o_zr:o_zq]), wi[:, :o_zr], wi[:, o_zg:], wi[:, o_zq:o_zg]],
                             axis=1).astype(BF16)
    col_zp = ZR_W // pool_w
    col_zg = (ZR_W + pool_w) // d
    col_zq = (ZR_W + pool_w + 3 * d) // xa_w
    row = lambda v: v.reshape(1, -1)
    prep_params = [row(_pack_zr_cols(rwkv_mu[l])), row(rwkv_w0[l]), _pad_rows(rwkv_w_up[l], 128).astype(BF16),
                   row(rwkv_a0[l]), _pad_rows(rwkv_a_up[l], 128).astype(BF16),
                   _pad_rows(rwkv_g_up[l], 256).astype(BF16), row(rwkv_k_k[l]), row(rwkv_k_a[l]),
                   row(rwkv_r_k[l])]
    gw = pool_group_w[l].astype(BF16)
    w_kv = jnp.concatenate([w_mem_k[l], w_mem_v[l]], axis=1).astype(BF16)
    wa, wb, wc, wo = (pool_out[l].astype(BF16), rwkv_out[l].astype(BF16), xattn_out[l].astype(BF16),
                      w_o[l].astype(BF16))

    tm_p = 512
    xp2 = x_prompt.reshape(rows_p, d)
    xs2 = x_sample.reshape(ns, d)
    h1 = _ffn(xp2, 0, rows_p // tm_p, tm_p, row(ffn1_norm_g[l]), *f1, out_rows=rows, out_blk0=0)
    h1 = _ffn(xs2, 0, 1, ns, row(ffn1_norm_g[l]), *f1, out_rows=rows, out_blk0=rows_p // ns, prev_out=h1)

    z = _norm_matmul(h1, row(mix_norm_g[l]), w_in_p, tm=rows // 10, tn=512)
    kv = _norm_matmul(mem_prompt.reshape(nb * n_mem, d), row(mem_norm_g[l]), w_kv, tm=512, tn=512)
    mk_p = kv[:, :xa_w].reshape(nb, n_mem, xa_w)
    mv_p = kv[:, xa_w:].reshape(nb, n_mem, xa_w)
    sblk = rows_p // ns

    pa_p = _pool_prompt(z, nb, t_len, col_zp, gw, row(pool_scale[l]))
    pa_s = _pool_sample(z, sblk, ns, col_zp, jnp.swapaxes(state_pool[l], 0, 1), gw, row(pool_scale[l]))

    prep_p = _prep_prompt(z, nb, t_len, prep_params)
    prep_s = _prep_sample(z, sblk, ns, _pack_zr_cols(state_shift[l][:, 0, :]), prep_params)
    y_p, st_p = _wkv_chunk(prep_p[:6], nb, t_len)
    vec4 = lambda v: v.reshape(ns, n_heads, 1, HEAD)
    y_s4, wkv_s = _wkv_step(state_wkv[l], [vec4(v) for v in prep_s[:6]])
    ln_g, ln_b = row(rwkv_ln_g[l]), row(rwkv_ln_b[l])
    pb_p = _rwkv_post(y_p, prep_p[7], prep_p[6], ln_g, ln_b, tm=512)
    pb_s = _rwkv_post(y_s4.reshape(ns, RWKV_W), prep_s[7], prep_s[6], ln_g, ln_b, tm=ns)

    pc_p = _xattn_prompt(z, nb, t_len, col_zq, mk_p, mv_p)
    q_s = z[rows_p:, ZR_W + pool_w + 3 * d:].reshape(ns, 1, xa_w)
    pc_s = _xattn_sample(q_s, cache_mem_k[l].reshape(ns, n_mem, xa_w), cache_mem_v[l].reshape(ns, n_mem, xa_w))
    pc_s = pc_s.reshape(ns, xa_w)

    tm_m = 256
    h2 = _merge(pa_p, pb_p, pc_p, z, col_zg, h1, 0, rows_p // tm_m, tm_m, wa, wb, wc, wo, out_rows=rows)
    h2 = _merge(pa_s, pb_s, pc_s, z, col_zg, h1, sblk, 1, ns, wa, wb, wc, wo, out_rows=rows, prev_out=h2)

    g2 = row(ffn2_norm_g[l])
    fg = row(final_norm_g)
    y_prompt = _ffn(h2, 0, rows_p // tm_p, tm_p, g2, *f2, out_rows=rows_p, out_blk0=0, final_g=fg)
    y_sample = _ffn(h2, sblk, 1, ns, g2, *f2, out_rows=ns, out_blk0=0, final_g=fg)

    zr_last = z[:rows_p].reshape(nb, t_len, -1)[:, t_len - 1:, :ZR_W]
    shift_p = _unpack_zr_cols(zr_last)[None]
    nbuf = state_pool.shape[2]
    pool_p = z[:rows_p].reshape(nb, t_len, -1)[:, t_len - nbuf:, ZR_W:ZR_W + pool_w][None]
    shift_s = _unpack_zr_cols(z[rows_p:, :ZR_W])[None, :, None, :]
    pool_s = jnp.concatenate([state_pool[l][:, 1:], z[rows_p:, None, ZR_W:ZR_W + pool_w]], axis=1)[None]
    st5 = st_p.reshape(nb, n_heads // 2, 2, HEAD, 2, HEAD)
    wkv_p = jnp.stack([st5[:, :, 0, :, 0, :], st5[:, :, 1, :, 1, :]], axis=2).reshape(nb, n_heads, HEAD, HEAD)
    wkv_p = jnp.swapaxes(wkv_p, -1, -2)[None]
    mem_k_p = mk_p.reshape(1, nb, n_mem, XA_HEADS, XA_DIM)
    mem_v_p = mv_p.reshape(1, nb, n_mem, XA_HEADS, XA_DIM)
    return (y_prompt.reshape(nb, t_len, d), y_sample.reshape(ns, 1, d), mem_k_p, mem_v_p, wkv_p, shift_p, pool_p,
            wkv_s[None], shift_s, pool_s)
```

```python
import functools
import math

import jax
import jax.numpy as jnp
from jax import lax
from jax.experimental import pallas as pl
from jax.experimental.pallas import tpu as pltpu

F32 = jnp.float32
BF16 = jnp.bfloat16
HI = lax.Precision.HIGHEST

RMS_EPS = 1e-6
GN_EPS = 64e-5
POOL_WINDOWS = (2, 4, 8, 16)
HEAD = 64
LANES = 128
XA_HEADS = 4
XA_DIM = 128
PAST_LEN = 16384
VMEM_LIMIT = 56 * 1024 * 1024
EXP_M05 = math.exp(-0.5)

ZR_W = 3584
RWKV_W = 1024
ZR_TRUE = 3360


def _cparams(sem):
    return pltpu.CompilerParams(dimension_semantics=sem, vmem_limit_bytes=VMEM_LIMIT)


def _rms(x, g):
    ms = jnp.mean(x * x, axis=-1, keepdims=True)
    return x * lax.rsqrt(ms + RMS_EPS) * g


def _ffn_kernel(h_ref, g_ref, wg_ref, wu_ref, wd_ref, *rest, nf, final):
    if final:
        fg_ref, o_ref, xn_ref, acc_ref = rest
    else:
        o_ref, xn_ref, acc_ref = rest
    f = pl.program_id(1)

    @pl.when(f == 0)
    def _():
        xn_ref[...] = _rms(h_ref[...], g_ref[...]).astype(BF16)
        acc_ref[...] = jnp.zeros_like(acc_ref)

    xn = xn_ref[...]
    gate = jnp.dot(xn, wg_ref[...], preferred_element_type=F32)
    up = jnp.dot(xn, wu_ref[...], preferred_element_type=F32)
    act = (gate * jax.nn.sigmoid(gate) * up).astype(BF16)
    acc_ref[...] += jnp.dot(act, wd_ref[...], preferred_element_type=F32)

    @pl.when(f == nf - 1)
    def _():
        out = h_ref[...] + 0.5 * acc_ref[...]
        if final:
            out = _rms(out, fg_ref[...])
        o_ref[...] = out


def _ffn(h, in_blk0, nblk, tm, g, wg, wu, wd, *, out_rows, out_blk0, prev_out=None, final_g=None, tf=512):
    d = h.shape[1]
    fp = wg.shape[1]
    nf = fp // tf
    final = final_g is not None
    in_specs = [
        pl.BlockSpec((tm, d), lambda m, f: (in_blk0 + m, 0)),
        pl.BlockSpec((1, d), lambda m, f: (0, 0)),
        pl.BlockSpec((d, tf), lambda m, f: (0, f)),
        pl.BlockSpec((d, tf), lambda m, f: (0, f)),
        pl.BlockSpec((tf, d), lambda m, f: (f, 0)),
    ]
    args = [h, g, wg, wu, wd]
    if final:
        in_specs.append(pl.BlockSpec((1, d), lambda m, f: (0, 0)))
        args.append(final_g)
    aliases = {}
    if prev_out is not None:
        in_specs.append(pl.BlockSpec(memory_space=pl.ANY))
        args.append(prev_out)
        aliases = {len(args) - 1: 0}
    kern = functools.partial(_ffn_kernel, nf=nf, final=final)
    if prev_out is not None:
        inner = kern
        kern = lambda *refs: inner(*refs[:len(args) - 1], *refs[len(args):])
    return pl.pallas_call(
        kern,
        grid=(nblk, nf),
        in_specs=in_specs,
        out_specs=pl.BlockSpec((tm, d), lambda m, f: (out_blk0 + m, 0)),
        out_shape=jax.ShapeDtypeStruct((out_rows, d), F32),
        scratch_shapes=[pltpu.VMEM((tm, d), BF16), pltpu.VMEM((tm, d), F32)],
        input_output_aliases=aliases,
        compiler_params=_cparams(("parallel", "arbitrary")),
        name="ffn",
    )(*args)


def _norm_matmul_kernel(h_ref, g_ref, w_ref, o_ref, xn_ref):
    @pl.when(pl.program_id(1) == 0)
    def _():
        xn_ref[...] = _rms(h_ref[...], g_ref[...]).astype(BF16)

    o_ref[...] = jnp.dot(xn_ref[...], w_ref[...], preferred_element_type=F32)


def _norm_matmul(h, g, w, tm, tn):
    m, d = h.shape
    n = w.shape[1]
    return pl.pallas_call(
        _norm_matmul_kernel,
        grid=(m // tm, n // tn),
        in_specs=[
            pl.BlockSpec((tm, d), lambda i, j: (i, 0)),
            pl.BlockSpec((1, d), lambda i, j: (0, 0)),
            pl.BlockSpec((d, tn), lambda i, j: (0, j)),
        ],
        out_specs=pl.BlockSpec((tm, tn), lambda i, j: (i, j)),
        out_shape=jax.ShapeDtypeStruct((m, n), F32),
        scratch_shapes=[pltpu.VMEM((tm, d), BF16)],
        compiler_params=_cparams(("parallel", "arbitrary")),
        name="norm_matmul",
    )(h, g, w)


def _pool_mix(pooled_groups, gw_ref, scale_ref, o_ref):
    for gi, pooled in enumerate(pooled_groups):
        sl = slice(gi * LANES, (gi + 1) * LANES)
        mixed = jnp.dot(pooled.astype(BF16), gw_ref[gi], preferred_element_type=F32)
        o_ref[:, sl] = (mixed * scale_ref[:, sl]).astype(o_ref.dtype)


def _pool_prompt_kernel(zp_ref, gw_ref, scale_ref, o_ref, ext_ref, *, tt):
    t = pl.program_id(1)
    hist = 16

    @pl.when(t == 0)
    def _():
        ext_ref[0:hist, :] = jnp.zeros((hist, ext_ref.shape[1]), F32)

    x = zp_ref[...]
    ext_ref[hist:hist + tt, :] = x
    pos = t * tt + lax.broadcasted_iota(jnp.int32, (tt, LANES), 0)
    groups = []
    for gi, w in enumerate(POOL_WINDOWS):
        sl = slice(gi * LANES, (gi + 1) * LANES)
        acc = x[:, sl]
        for k in range(1, w):
            acc = acc + ext_ref[hist - k:hist - k + tt, sl]
        cnt = jnp.minimum(pos + 1, w).astype(F32)
        groups.append(acc / cnt - x[:, sl])
    _pool_mix(groups, gw_ref, scale_ref, o_ref)
    ext_ref[0:hist, :] = ext_ref[tt:tt + hist, :]


def _pool_prompt(z, nb, t_len, col_blk, gw, scale, tt=256):
    nt = t_len // tt
    pw = gw.shape[0] * LANES
    return pl.pallas_call(
        functools.partial(_pool_prompt_kernel, tt=tt),
        grid=(nb, nt),
        in_specs=[
            pl.BlockSpec((tt, pw), lambda b, t: (b * nt + t, col_blk)),
            pl.BlockSpec(gw.shape, lambda b, t: (0, 0, 0)),
            pl.BlockSpec((1, pw), lambda b, t: (0, 0)),
        ],
        out_specs=pl.BlockSpec((tt, pw), lambda b, t: (b * nt + t, 0)),
        out_shape=jax.ShapeDtypeStruct((nb * t_len, pw), BF16),
        scratch_shapes=[pltpu.VMEM((tt + 16, pw), F32)],
        compiler_params=_cparams(("parallel", "arbitrary")),
        name="pool_prompt",
    )(z, gw, scale)


def _pool_sample_kernel(zp_ref, buf_ref, gw_ref, scale_ref, o_ref):
    x = zp_ref[...]
    nbuf = buf_ref.shape[0]
    groups = []
    for gi, w in enumerate(POOL_WINDOWS):
        sl = slice(gi * LANES, (gi + 1) * LANES)
        acc = x[:, sl]
        for k in range(1, w):
            acc = acc + buf_ref[nbuf - k, :, sl]
        cnt = float(min(PAST_LEN + 1, w))
        groups.append(acc / cnt - x[:, sl])
    _pool_mix(groups, gw_ref, scale_ref, o_ref)


def _pool_sample(z, row_blk, nrows, col_blk, buf_t, gw, scale):
    pw = gw.shape[0] * LANES
    return pl.pallas_call(
        _pool_sample_kernel,
        grid=(1,),
        in_specs=[
            pl.BlockSpec((nrows, pw), lambda i: (row_blk, col_blk)),
            pl.BlockSpec(buf_t.shape, lambda i: (0, 0, 0)),
            pl.BlockSpec(gw.shape, lambda i: (0, 0, 0)),
            pl.BlockSpec((1, pw), lambda i: (0, 0)),
        ],
        out_specs=pl.BlockSpec((nrows, pw), lambda i: (0, 0)),
        out_shape=jax.ShapeDtypeStruct((nrows, pw), BF16),
        compiler_params=_cparams(("arbitrary",)),
        name="pool_sample",
    )(z, buf_t, gw, scale)


def _pair_block_ones():
    r = lax.broadcasted_iota(jnp.int32, (LANES, LANES), 0) // HEAD
    c = lax.broadcasted_iota(jnp.int32, (LANES, LANES), 1) // HEAD
    return (r == c).astype(F32)


def _head_sum(x, bd):
    cols = [jnp.dot(x[:, c * LANES:(c + 1) * LANES], bd, precision=HI, preferred_element_type=F32)
            for c in range(x.shape[1] // LANES)]
    return jnp.concatenate(cols, axis=1)


def _prep_math(x, prev, p_refs, out_refs):
    mu_ref, w0_ref, wup_ref, a0_ref, aup_ref, gup_ref, kk_ref, ka_ref, rk_ref = p_refs
    r_ref, lw_ref, k_ref, v_ref, a_ref, b_ref, g_ref, bonus_ref = out_refs
    xm = x + (prev - x) * mu_ref[...]
    w = RWKV_W
    r = xm[:, 0:w]
    k = xm[:, w:2 * w]
    v = xm[:, 2 * w:3 * w]
    wl = xm[:, 3 * w:3 * w + 128]
    al = xm[:, 3 * w + 128:3 * w + 256]
    gl = xm[:, 3 * w + 256:3 * w + 512]
    bd = _pair_block_ones()
    dw = w0_ref[...] + jnp.dot(jnp.tanh(wl).astype(BF16), wup_ref[...], preferred_element_type=F32)
    lw_ref[...] = -EXP_M05 * jax.nn.sigmoid(dw)
    a = jax.nn.sigmoid(a0_ref[...] + jnp.dot(al.astype(BF16), aup_ref[...], preferred_element_type=F32))
    g_ref[...] = jnp.dot(jax.nn.sigmoid(gl).astype(BF16), gup_ref[...], preferred_element_type=F32)
    kk = k * kk_ref[...]
    kk = kk * lax.rsqrt(jnp.maximum(_head_sum(kk * kk, bd), 1e-24))
    kmod = k * (1.0 + (a - 1.0) * ka_ref[...])
    r_ref[...] = r
    k_ref[...] = kmod
    v_ref[...] = v
    a_ref[...] = -kk
    b_ref[...] = kk * a
    bonus_ref[...] = _head_sum(r * kmod * rk_ref[...], bd) * v


def _prep_prompt_kernel(zr_ref, *refs):
    p_refs, out_refs, carry_ref = refs[:9], refs[9:17], refs[17]
    t = pl.program_id(1)

    @pl.when(t == 0)
    def _():
        carry_ref[...] = jnp.zeros_like(carry_ref)

    x = zr_ref[...]
    rolled = pltpu.roll(x, 1, axis=0)
    first = lax.broadcasted_iota(jnp.int32, x.shape, 0) == 0
    prev = jnp.where(first, carry_ref[0:1, :], rolled)
    carry_ref[...] = rolled[0:8, :]
    _prep_math(x, prev, p_refs, out_refs)


def _prep_sample_kernel(zr_ref, prev_ref, *refs):
    _prep_math(zr_ref[...], prev_ref[...], refs[:9], refs[9:17])


def _prep_param_specs(params, nidx):
    zero = (lambda *idx: (0, 0))
    return [pl.BlockSpec(p.shape, zero) for p in params]


def _prep_prompt(z, nb, t_len, params, tt=256):
    nt = t_len // tt
    rows = nb * t_len
    row_spec = pl.BlockSpec((tt, RWKV_W), lambda b, t: (b * nt + t, 0))
    return pl.pallas_call(
        _prep_prompt_kernel,
        grid=(nb, nt),
        in_specs=[pl.BlockSpec((tt, ZR_W), lambda b, t: (b * nt + t, 0))] + _prep_param_specs(params, 2),
        out_specs=[row_spec] * 8,
        out_shape=[jax.ShapeDtypeStruct((rows, RWKV_W), F32)] * 8,
        scratch_shapes=[pltpu.VMEM((8, ZR_W), F32)],
        compiler_params=_cparams(("parallel", "arbitrary")),
        name="prep_prompt",
    )(z, *params)


def _prep_sample(z, row_blk, nrows, prev, params):
    row_spec = pl.BlockSpec((nrows, RWKV_W), lambda i: (0, 0))
    return pl.pallas_call(
        _prep_sample_kernel,
        grid=(1,),
        in_specs=[pl.BlockSpec((nrows, ZR_W), lambda i: (row_blk, 0)),
                  pl.BlockSpec((nrows, ZR_W), lambda i: (0, 0))] + _prep_param_specs(params, 1),
        out_specs=[row_spec] * 8,
        out_shape=[jax.ShapeDtypeStruct((nrows, RWKV_W), F32)] * 8,
        compiler_params=_cparams(("arbitrary",)),
        name="prep_sample",
    )(z, prev, *params)


NN_DIMS = (((1,), (0,)), ((), ()))
NT_DIMS = (((1,), (1,)), ((), ()))
TN_DIMS = (((0,), (0,)), ((), ()))


def _bdot(a, b, dims=NN_DIMS):
    return lax.dot_general(a.astype(BF16), b.astype(BF16), dims, preferred_element_type=F32)


def _split3(x):
    hi = x.astype(BF16)
    rest = x - hi.astype(F32)
    mid = rest.astype(BF16)
    lo = (rest - mid.astype(F32)).astype(BF16)
    return hi, mid, lo


def _select_dot(sel, x):
    sel = sel.astype(BF16)
    hi, mid, lo = _split3(x)
    return _bdot(sel, hi) + (_bdot(sel, mid) + _bdot(sel, lo))


def _wkv_chunk_kernel(r_ref, lw_ref, k_ref, v_ref, a_ref, b_ref, y_ref, sout_ref, st_ref, *, c_len, pp, nc):
    c = pl.program_id(2)
    n2 = 2 * c_len
    assert n2 == LANES

    @pl.when(c == 0)
    def _():
        st_ref[...] = jnp.zeros_like(st_ref)

    row = lax.broadcasted_iota(jnp.int32, (n2, n2), 0)
    col = lax.broadcasted_iota(jnp.int32, (n2, n2), 1)
    tr = row & (c_len - 1)
    tc = col & (c_len - 1)
    strict = tr > tc
    incl = tr >= tc
    eye = row == col
    ones = jnp.ones((LANES, LANES), BF16)
    tri = (lax.broadcasted_iota(jnp.int32, (c_len, c_len), 0)
           >= lax.broadcasted_iota(jnp.int32, (c_len, c_len), 1))
    head_a = lax.broadcasted_iota(jnp.int32, (c_len, LANES), 1) < HEAD

    def stack(x):
        return jnp.concatenate([jnp.where(head_a, x, 0.0), jnp.where(head_a, 0.0, x)], axis=0).astype(BF16)

    prs = range(pp)
    cat = jnp.concatenate
    sls = [slice(q * LANES, (q + 1) * LANES) for q in prs]
    lw = [lw_ref[:, sl] for sl in sls]
    cum = [_select_dot(tri, x) for x in lw]
    tot = [x[c_len - 1:c_len, :] for x in cum]
    xr = [stack(r_ref[:, sls[q]] * jnp.exp(cum[q])) for q in prs]
    xa = [stack(a_ref[:, sls[q]] * jnp.exp(cum[q] - lw[q])) for q in prs]
    e_neg = [jnp.exp(-x) for x in cum]
    e_rem = [jnp.exp(tot[q] - cum[q]) for q in prs]
    yb = [stack(b_ref[:, sls[q]] * e_neg[q]) for q in prs]
    yk = [stack(k_ref[:, sls[q]] * e_neg[q]) for q in prs]
    zb = [stack(b_ref[:, sls[q]] * e_rem[q]) for q in prs]
    zk = [stack(k_ref[:, sls[q]] * e_rem[q]) for q in prs]
    vs = [stack(v_ref[:, sl]) for sl in sls]

    g = [_bdot(cat([xa[q], xr[q]], axis=0), cat([yb[q], yk[q]], axis=0), NT_DIMS) for q in prs]
    m_ab = [jnp.where(strict, x[0:n2, 0:n2], 0.0) for x in g]
    m_ak = [jnp.where(strict, x[0:n2, n2:2 * n2], 0.0).astype(BF16) for x in g]
    n_rb = [jnp.where(incl, x[n2:2 * n2, 0:n2], 0.0) for x in g]
    n_rk = [jnp.where(incl, x[n2:2 * n2, n2:2 * n2], 0.0).astype(BF16) for x in g]

    t_inv = [jnp.where(eye, 1.0, 0.0) + x for x in m_ab]
    mk = [_bdot(x, x) for x in m_ab]
    nlev = int(math.log2(c_len))
    for lev in range(1, nlev):
        if lev < nlev - 1:
            res = [_bdot(cat([mk[q], t_inv[q]], axis=0), mk[q]) for q in prs]
            mk = [x[0:n2] for x in res]
            t_inv = [t_inv[q] + res[q][n2:2 * n2] for q in prs]
        else:
            t_inv = [t_inv[q] + _bdot(t_inv[q], mk[q]) for q in prs]

    st = [st_ref[q] for q in prs]
    lhs = [cat([cat([xa[q], m_ak[q]], axis=1), cat([xr[q], n_rk[q]], axis=1)], axis=0) for q in prs]
    xy0 = [_bdot(lhs[q], cat([st[q].astype(BF16), vs[q]], axis=0)) for q in prs]
    u = [_bdot(t_inv[q], xy0[q][0:n2]).astype(BF16) for q in prs]
    y_st = [xy0[q][n2:2 * n2] + _bdot(n_rb[q], u[q]) for q in prs]
    for q in prs:
        y_ref[:, sls[q]] = y_st[q][0:c_len] + y_st[q][c_len:n2]

    for q in prs:
        dhi, dmid, dlo = _split3(jnp.where(eye, jnp.broadcast_to(jnp.exp(tot[q]), (LANES, LANES)), 0.0))
        dec_col = _bdot(dhi, ones) + (_bdot(dmid, ones) + _bdot(dlo, ones))
        st_ref[q] = dec_col * st[q] + _bdot(cat([zb[q], zk[q]], axis=0), cat([u[q], vs[q]], axis=0), TN_DIMS)

    @pl.when(c == nc - 1)
    def _():
        sout_ref[0] = st_ref[...]


def _wkv_chunk(seqs, nb, t_len, c_len=64, pp=4):
    nc = t_len // c_len
    npair = RWKV_W // LANES
    ng = npair // pp
    blk = pl.BlockSpec((c_len, pp * LANES), lambda b, g, c: (b * nc + c, g))
    return pl.pallas_call(
        functools.partial(_wkv_chunk_kernel, c_len=c_len, pp=pp, nc=nc),
        grid=(nb, ng, nc),
        in_specs=[blk] * 6,
        out_specs=[blk, pl.BlockSpec((1, pp, LANES, LANES), lambda b, g, c: (b, g, 0, 0))],
        out_shape=[jax.ShapeDtypeStruct((nb * t_len, RWKV_W), F32),
                   jax.ShapeDtypeStruct((nb, npair, LANES, LANES), F32)],
        scratch_shapes=[pltpu.VMEM((pp, LANES, LANES), F32)],
        compiler_params=_cparams(("parallel", "parallel", "arbitrary")),
        name="wkv_chunk",
    )(*seqs)


def _wkv_step_kernel(s_ref, r_ref, lw_ref, k_ref, v_ref, a_ref, b_ref, y_ref, so_ref):
    s = s_ref[...]
    eye = (lax.broadcasted_iota(jnp.int32, (HEAD, HEAD), 0)
           == lax.broadcasted_iota(jnp.int32, (HEAD, HEAD), 1))
    sa = jnp.sum(s * a_ref[...], axis=-1, keepdims=True)
    vcol = jnp.sum(jnp.where(eye, v_ref[...], 0.0), axis=-1, keepdims=True)
    s2 = s * jnp.exp(lw_ref[...]) + sa * b_ref[...] + vcol * k_ref[...]
    so_ref[...] = s2
    ycol = jnp.sum(s2 * r_ref[...], axis=-1, keepdims=True)
    y_ref[...] = jnp.sum(jnp.where(eye, ycol, 0.0), axis=-2, keepdims=True)


def _wkv_step(state, seqs, bb=8):
    nb, nh = state.shape[0], state.shape[1]
    vec = pl.BlockSpec((bb, nh, 1, HEAD), lambda i: (i, 0, 0, 0))
    st = pl.BlockSpec((bb, nh, HEAD, HEAD), lambda i: (i, 0, 0, 0))
    return pl.pallas_call(
        _wkv_step_kernel,
        grid=(nb // bb,),
        in_specs=[st] + [vec] * 6,
        out_specs=[vec, st],
        out_shape=[jax.ShapeDtypeStruct((nb, nh, 1, HEAD), F32), jax.ShapeDtypeStruct(state.shape, F32)],
        compiler_params=_cparams(("parallel",)),
        name="wkv_step",
    )(state, *seqs)


def _rwkv_post_kernel(y_ref, bonus_ref, g_ref, lng_ref, lnb_ref, o_ref):
    bd = _pair_block_ones() * (1.0 / HEAD)
    y = y_ref[...]
    d = y - _head_sum(y, bd)
    var = _head_sum(d * d, bd)
    yn = d * lax.rsqrt(var + GN_EPS) * lng_ref[...] + lnb_ref[...]
    o_ref[...] = ((yn + bonus_ref[...]) * g_ref[...]).astype(o_ref.dtype)


def _rwkv_post(y, bonus, g, ln_g, ln_b, tm):
    m = y.shape[0]
    row = pl.BlockSpec((tm, RWKV_W), lambda i: (i, 0))
    par = pl.BlockSpec((1, RWKV_W), lambda i: (0, 0))
    return pl.pallas_call(
        _rwkv_post_kernel,
        grid=(m // tm,),
        in_specs=[row, row, row, par, par],
        out_specs=row,
        out_shape=jax.ShapeDtypeStruct((m, RWKV_W), BF16),
        compiler_params=_cparams(("parallel",)),
        name="rwkv_post",
    )(y, bonus, g, ln_g, ln_b)


def _xattn_prompt_kernel(q_ref, k_ref, v_ref, o_ref):
    scale = XA_DIM ** -0.5
    q = q_ref[...]
    for h in range(XA_HEADS):
        sl = slice(h * XA_DIM, (h + 1) * XA_DIM)
        s = lax.dot_general(q[:, sl].astype(BF16), k_ref[0, :, sl].astype(BF16),
                            (((1,), (1,)), ((), ())), preferred_element_type=F32) * scale
        p = jnp.exp(s - jnp.max(s, axis=-1, keepdims=True))
        den = jnp.sum(p, axis=-1, keepdims=True)
        o = jnp.dot(p.astype(BF16), v_ref[0, :, sl].astype(BF16), preferred_element_type=F32)
        o_ref[:, sl] = (o / den).astype(o_ref.dtype)


def _xattn_prompt(z, nb, t_len, col_blk, mk, mv, tq=512):
    nt = t_len // tq
    xw = XA_HEADS * XA_DIM
    nmem = mk.shape[1]
    kv = pl.BlockSpec((1, nmem, xw), lambda b, t: (b, 0, 0))
    return pl.pallas_call(
        _xattn_prompt_kernel,
        grid=(nb, nt),
        in_specs=[pl.BlockSpec((tq, xw), lambda b, t: (b * nt + t, col_blk)), kv, kv],
        out_specs=pl.BlockSpec((tq, xw), lambda b, t: (b * nt + t, 0)),
        out_shape=jax.ShapeDtypeStruct((nb * t_len, xw), BF16),
        compiler_params=_cparams(("parallel", "parallel")),
        name="xattn_prompt",
    )(z, mk, mv)


def _xattn_sample_kernel(q_ref, k_ref, v_ref, o_ref):
    scale = XA_DIM ** -0.5
    for h in range(XA_HEADS):
        sl = slice(h * XA_DIM, (h + 1) * XA_DIM)
        q = q_ref[:, :, sl]
        s = jnp.sum(k_ref[:, :, sl] * q, axis=-1, keepdims=True) * scale
        p = jnp.exp(s - jnp.max(s, axis=1, keepdims=True))
        den = jnp.sum(p, axis=1, keepdims=True)
        o = jnp.sum(p * v_ref[:, :, sl], axis=1, keepdims=True)
        o_ref[:, :, sl] = (o / den).astype(o_ref.dtype)


def _xattn_sample(q3, mk, mv, bb=8):
    nb, nmem, xw = mk.shape
    kv = pl.BlockSpec((bb, nmem, xw), lambda i: (i, 0, 0))
    qs = pl.BlockSpec((bb, 1, xw), lambda i: (i, 0, 0))
    return pl.pallas_call(
        _xattn_sample_kernel,
        grid=(nb // bb,),
        in_specs=[qs, kv, kv],
        out_specs=qs,
        out_shape=jax.ShapeDtypeStruct((nb, 1, xw), BF16),
        compiler_params=_cparams(("parallel",)),
        name="xattn_sample",
    )(q3, mk, mv)


def _merge_kernel(pa_ref, pb_ref, pc_ref, g0_ref, g1_ref, g2_ref, h_ref, wa_ref, wb_ref, wc_ref, wo_ref, *rest):
    o_ref = rest[-1]
    oa = jnp.dot(pa_ref[...], wa_ref[...], preferred_element_type=F32)
    ob = jnp.dot(pb_ref[...], wb_ref[...], preferred_element_type=F32)
    oc = jnp.dot(pc_ref[...], wc_ref[...], preferred_element_type=F32)
    merged = (jax.nn.sigmoid(g0_ref[...]) * oa + jax.nn.sigmoid(g1_ref[...]) * ob
              + jax.nn.sigmoid(g2_ref[...]) * oc)
    o_ref[...] = h_ref[...] + jnp.dot(merged.astype(BF16), wo_ref[...], preferred_element_type=F32)


def _merge(pa, pb, pc, z, zg_blk0, h, row_blk0, nblk, tm, wa, wb, wc, wo, *, out_rows, prev_out=None):
    d = h.shape[1]
    const = lambda i: (0, 0)
    in_specs = [
        pl.BlockSpec((tm, pa.shape[1]), lambda i: (i, 0)),
        pl.BlockSpec((tm, pb.shape[1]), lambda i: (i, 0)),
        pl.BlockSpec((tm, pc.shape[1]), lambda i: (i, 0)),
        pl.BlockSpec((tm, d), lambda i: (row_blk0 + i, zg_blk0)),
        pl.BlockSpec((tm, d), lambda i: (row_blk0 + i, zg_blk0 + 1)),
        pl.BlockSpec((tm, d), lambda i: (row_blk0 + i, zg_blk0 + 2)),
        pl.BlockSpec((tm, d), lambda i: (row_blk0 + i, 0)),
        pl.BlockSpec(wa.shape, const, pipeline_mode=pl.Buffered(1)),
        pl.BlockSpec(wb.shape, const, pipeline_mode=pl.Buffered(1)),
        pl.BlockSpec(wc.shape, const, pipeline_mode=pl.Buffered(1)),
        pl.BlockSpec(wo.shape, const, pipeline_mode=pl.Buffered(1)),
    ]
    args = [pa, pb, pc, z, z, z, h, wa, wb, wc, wo]
    aliases = {}
    if prev_out is not None:
        in_specs.append(pl.BlockSpec(memory_space=pl.ANY))
        args.append(prev_out)
        aliases = {len(args) - 1: 0}
    return pl.pallas_call(
        _merge_kernel,
        grid=(nblk,),
        in_specs=in_specs,
        out_specs=pl.BlockSpec((tm, d), lambda i: (row_blk0 + i, 0)),
        out_shape=jax.ShapeDtypeStruct((out_rows, d), F32),
        input_output_aliases=aliases,
        compiler_params=_cparams(("parallel",)),
        name="merge",
    )(*args)


def _pack_zr_cols(x):
    w = RWKV_W
    pad = lambda n: jnp.zeros(x.shape[:-1] + (n,), x.dtype)
    return jnp.concatenate([x[..., :3 * w + 64], pad(64), x[..., 3 * w + 64:3 * w + 128], pad(64),
                            x[..., 3 * w + 128:], pad(96)], axis=-1)


def _unpack_zr_cols(x):
    w = RWKV_W
    return jnp.concatenate([x[..., :3 * w + 64], x[..., 3 * w + 128:3 * w + 192], x[..., 3 * w + 256:3 * w + 416]],
                           axis=-1)


def _pad_rows(x, n):
    return jnp.concatenate([x, jnp.zeros((n - x.shape[0],) + x.shape[1:], x.dtype)], axis=0)


def kernel(x_prompt, x_sample, mem_prompt, cache_mem_k, cache_mem_v, state_wkv, state_shift, state_pool,
           ffn1_norm_g, ffn1_w_gate, ffn1_w_up, ffn1_w_down, mix_norm_g, w_in,
           pool_group_w, pool_scale, pool_out,
           rwkv_mu, rwkv_w0, rwkv_w_up, rwkv_a0, rwkv_a_up, rwkv_g_up, rwkv_k_k, rwkv_k_a, rwkv_r_k,
           rwkv_ln_g, rwkv_ln_b, rwkv_out,
           mem_norm_g, w_mem_k, w_mem_v, xattn_out, w_o,
           ffn2_norm_g, ffn2_w_gate, ffn2_w_up, ffn2_w_down, final_norm_g):
    nb, t_len, d = x_prompt.shape
    ns = x_sample.shape[0]
    depth = w_in.shape[0]
    assert depth == 1 and x_sample.shape[1] == 1
    n_mem = mem_prompt.shape[1]
    pool_w = pool_out.shape[1]
    xa_w = xattn_out.shape[1]
    n_heads = RWKV_W // HEAD
    rows_p = nb * t_len
    rows = rows_p + ns
    l = 0

    d_ff = ffn1_w_gate.shape[2]
    fpad = (-d_ff) % 512

    def ffn_weights(wg, wu, wd):
        return (jnp.pad(wg, ((0, 0), (0, fpad))).astype(BF16), jnp.pad(wu, ((0, 0), (0, fpad))).astype(BF16),
                jnp.pad(wd, ((0, fpad), (0, 0))).astype(BF16))

    f1 = ffn_weights(ffn1_w_gate[l], ffn1_w_up[l], ffn1_w_down[l])
    f2 = ffn_weights(ffn2_w_gate[l], ffn2_w_up[l], ffn2_w_down[l])
    wi = w_in[l]
    o_zr, o_zq, o_zg = pool_w, pool_w + ZR_TRUE, pool_w + ZR_TRUE + xa_w
    w_in_p = jnp.concatenate([_pack_zr_cols(wi[:, o_zr:o_zq]), wi[:, :o_zr], wi[:, o_zg:], wi[:, o_zq:o_zg]],
                             axis=1).astype(BF16)
    col_zp = ZR_W // pool_w
    col_zg = (ZR_W + pool_w) // d
    col_zq = (ZR_W + pool_w + 3 * d) // xa_w
    row = lambda v: v.reshape(1, -1)
    prep_params = [row(_pack_zr_cols(rwkv_mu[l])), row(rwkv_w0[l]), _pad_rows(rwkv_w_up[l], 128).astype(BF16),
                   row(rwkv_a0[l]), _pad_rows(rwkv_a_up[l], 128).astype(BF16),
                   _pad_rows(rwkv_g_up[l], 256).astype(BF16), row(rwkv_k_k[l]), row(rwkv_k_a[l]),
                   row(rwkv_r_k[l])]
    gw = pool_group_w[l].astype(BF16)
    w_kv = jnp.concatenate([w_mem_k[l], w_mem_v[l]], axis=1).astype(BF16)
    wa, wb, wc, wo = (pool_out[l].astype(BF16), rwkv_out[l].astype(BF16), xattn_out[l].astype(BF16),
                      w_o[l].astype(BF16))

    tm_p = 512
    xp2 = x_prompt.reshape(rows_p, d)
    xs2 = x_sample.reshape(ns, d)
    h1 = _ffn(xp2, 0, rows_p // tm_p, tm_p, row(ffn1_norm_g[l]), *f1, out_rows=rows, out_blk0=0)
    h1 = _ffn(xs2, 0, 1, ns, row(ffn1_norm_g[l]), *f1, out_rows=rows, out_blk0=rows_p // ns, prev_out=h1)

    z = _norm_matmul(h1, row(mix_norm_g[l]), w_in_p, tm=rows // 10, tn=512)
    kv = _norm_matmul(mem_prompt.reshape(nb * n_mem, d), row(mem_norm_g[l]), w_kv, tm=512, tn=512)
    mk_p = kv[:, :xa_w].reshape(nb, n_mem, xa_w)
    mv_p = kv[:, xa_w:].reshape(nb, n_mem, xa_w)
    sblk = rows_p // ns

    pa_p = _pool_prompt(z, nb, t_len, col_zp, gw, row(pool_scale[l]))
    nbuf = state_pool.shape[2]
    pool_state = state_pool.reshape(ns, nbuf, pool_w)
    pa_s = _pool_sample(z, sblk, ns, col_zp, jnp.swapaxes(pool_state, 0, 1), gw, row(pool_scale[l]))

    prep_p = _prep_prompt(z, nb, t_len, prep_params)
    prep_s = _prep_sample(z, sblk, ns, _pack_zr_cols(state_shift.reshape(ns, ZR_TRUE)), prep_params)
    y_p, st_p = _wkv_chunk(prep_p[:6], nb, t_len)
    vec4 = lambda v: v.reshape(ns, n_heads, 1, HEAD)
    y_s4, wkv_s = _wkv_step(state_wkv.reshape(ns, n_heads, HEAD, HEAD), [vec4(v) for v in prep_s[:6]])
    ln_g, ln_b = row(rwkv_ln_g[l]), row(rwkv_ln_b[l])
    pb_p = _rwkv_post(y_p, prep_p[7], prep_p[6], ln_g, ln_b, tm=512)
    pb_s = _rwkv_post(y_s4.reshape(ns, RWKV_W), prep_s[7], prep_s[6], ln_g, ln_b, tm=ns)

    pc_p = _xattn_prompt(z, nb, t_len, col_zq, mk_p, mv_p)
    q_s = z[rows_p:, ZR_W + pool_w + 3 * d:].reshape(ns, 1, xa_w)
    pc_s = _xattn_sample(q_s, cache_mem_k.reshape(ns, n_mem, xa_w), cache_mem_v.reshape(ns, n_mem, xa_w))
    pc_s = pc_s.reshape(ns, xa_w)

    tm_m = 256
    h2 = _merge(pa_p, pb_p, pc_p, z, col_zg, h1, 0, rows_p // tm_m, tm_m, wa, wb, wc, wo, out_rows=rows)
    h2 = _merge(pa_s, pb_s, pc_s, z, col_zg, h1, sblk, 1, ns, wa, wb, wc, wo, out_rows=rows, prev_out=h2)

    g2 = row(ffn2_norm_g[l])
    fg = row(final_norm_g)
    y_prompt = _ffn(h2, 0, rows_p // tm_p, tm_p, g2, *f2, out_rows=rows_p, out_blk0=0, final_g=fg)
    y_sample = _ffn(h2, sblk, 1, ns, g2, *f2, out_rows=ns, out_blk0=0, final_g=fg)

    ends = [(b + 1) * t_len for b in range(nb)]
    zr_last = jnp.stack([z[e - 1:e, :ZR_W] for e in ends])
    shift_p = _unpack_zr_cols(zr_last)[None]
    pool_p = jnp.stack([z[e - nbuf:e, ZR_W:ZR_W + pool_w] for e in ends])[None]
    shift_s = _unpack_zr_cols(z[rows_p:, :ZR_W])[None, :, None, :]
    pool_s = jnp.concatenate([pool_state[:, 1:], z[rows_p:, None, ZR_W:ZR_W + pool_w]], axis=1)[None]
    st5 = st_p.reshape(nb, n_heads // 2, 2, HEAD, 2, HEAD)
    wkv_p = jnp.stack([st5[:, :, 0, :, 0, :], st5[:, :, 1, :, 1, :]], axis=2).reshape(nb, n_heads, HEAD, HEAD)
    wkv_p = jnp.swapaxes(wkv_p, -1, -2)[None]
    mem_k_p = mk_p.reshape(1, nb, n_mem, XA_HEADS, XA_DIM)
    mem_v_p = mv_p.reshape(1, nb, n_mem, XA_HEADS, XA_DIM)
    return (y_prompt.reshape(nb, t_len, d), y_sample.reshape(ns, 1, d), mem_k_p, mem_v_p, wkv_p, shift_p, pool_p,
            wkv_s.reshape(state_wkv.shape), shift_s, pool_s)
```

```python
import functools
import math

import jax
import jax.numpy as jnp
from jax import lax
from jax.experimental import pallas as pl
from jax.experimental.pallas import tpu as pltpu

F32 = jnp.float32
BF16 = jnp.bfloat16
HI = lax.Precision.HIGHEST

RMS_EPS = 1e-6
GN_EPS = 64e-5
POOL_WINDOWS = (2, 4, 8, 16)
HEAD = 64
LANES = 128
SUBLANES = 8
XA_HEADS = 4
XA_DIM = 128
PAST_LEN = 16384
VMEM_LIMIT = 56 * 1024 * 1024
EXP_M05 = math.exp(-0.5)

ZR_W = 3584
RWKV_W = 1024
ZR_TRUE = 3360

NN_DIMS = (((1,), (0,)), ((), ()))
NT_DIMS = (((1,), (1,)), ((), ()))
TN_DIMS = (((0,), (0,)), ((), ()))


def _cparams(sem):
    return pltpu.CompilerParams(dimension_semantics=sem, vmem_limit_bytes=VMEM_LIMIT)


def _rms(x, g):
    ms = jnp.mean(x * x, axis=-1, keepdims=True)
    return x * lax.rsqrt(ms + RMS_EPS) * g


def _bdot(a, b, dims=NN_DIMS):
    return lax.dot_general(a.astype(BF16), b.astype(BF16), dims, preferred_element_type=F32)


def _ffn_kernel(h_ref, g_ref, wg_ref, wu_ref, wd_ref, *rest, nf, final):
    if final:
        fg_ref, o_ref, xn_ref, acc_ref = rest
    else:
        o_ref, xn_ref, acc_ref = rest
    f = pl.program_id(1)

    @pl.when(f == 0)
    def _():
        xn_ref[...] = _rms(h_ref[...], g_ref[...]).astype(BF16)
        acc_ref[...] = jnp.zeros_like(acc_ref)

    xn = xn_ref[...]
    gate = jnp.dot(xn, wg_ref[...], preferred_element_type=F32)
    up = jnp.dot(xn, wu_ref[...], preferred_element_type=F32)
    act = (gate * jax.nn.sigmoid(gate) * up).astype(BF16)
    acc_ref[...] += jnp.dot(act, wd_ref[...], preferred_element_type=F32)

    @pl.when(f == nf - 1)
    def _():
        out = h_ref[...] + 0.5 * acc_ref[...]
        if final:
            out = _rms(out, fg_ref[...])
        o_ref[...] = out


def _ffn(h, tm, g, wg, wu, wd, final_g=None, tf=512):
    m, d = h.shape
    nf = wg.shape[1] // tf
    final = final_g is not None
    vec = pl.BlockSpec((1, d), lambda i, f: (0, 0))
    in_specs = [
        pl.BlockSpec((tm, d), lambda i, f: (i, 0)),
        vec,
        pl.BlockSpec((d, tf), lambda i, f: (0, f)),
        pl.BlockSpec((d, tf), lambda i, f: (0, f)),
        pl.BlockSpec((tf, d), lambda i, f: (f, 0)),
    ]
    args = [h, g, wg, wu, wd]
    if final:
        in_specs.append(vec)
        args.append(final_g)
    return pl.pallas_call(
        functools.partial(_ffn_kernel, nf=nf, final=final),
        grid=(m // tm, nf),
        in_specs=in_specs,
        out_specs=pl.BlockSpec((tm, d), lambda i, f: (i, 0)),
        out_shape=jax.ShapeDtypeStruct((m, d), F32),
        scratch_shapes=[pltpu.VMEM((tm, d), BF16), pltpu.VMEM((tm, d), F32)],
        compiler_params=_cparams(("parallel", "arbitrary")),
        name="ffn",
    )(*args)


def _norm_matmul_kernel(h_ref, g_ref, wt_ref, o_ref, xn_ref):
    @pl.when(pl.program_id(1) == 0)
    def _():
        xn_ref[...] = _rms(h_ref[...], g_ref[...]).astype(BF16)

    o_ref[...] = lax.dot_general(xn_ref[...], wt_ref[...], NT_DIMS, preferred_element_type=F32)


def _norm_matmul(h, g, wt, tm, tn):
    m, d = h.shape
    n = wt.shape[0]
    return pl.pallas_call(
        _norm_matmul_kernel,
        grid=(m // tm, n // tn),
        in_specs=[
            pl.BlockSpec((tm, d), lambda i, j: (i, 0)),
            pl.BlockSpec((1, d), lambda i, j: (0, 0)),
            pl.BlockSpec((tn, d), lambda i, j: (j, 0)),
        ],
        out_specs=pl.BlockSpec((tm, tn), lambda i, j: (i, j)),
        out_shape=jax.ShapeDtypeStruct((m, n), F32),
        scratch_shapes=[pltpu.VMEM((tm, d), BF16)],
        compiler_params=_cparams(("parallel", "arbitrary")),
        name="norm_matmul",
    )(h, g, wt)


def _pool_mix(pooled_groups, gw_ref, scale_ref, o_ref):
    for gi, pooled in enumerate(pooled_groups):
        sl = slice(gi * LANES, (gi + 1) * LANES)
        mixed = jnp.dot(pooled.astype(BF16), gw_ref[gi], preferred_element_type=F32)
        o_ref[:, sl] = (mixed * scale_ref[:, sl]).astype(o_ref.dtype)


def _pool_prompt_kernel(zp_ref, gw_ref, scale_ref, o_ref, ext_ref, *, tt):
    t = pl.program_id(1)
    hist = 16

    @pl.when(t == 0)
    def _():
        ext_ref[0:hist, :] = jnp.zeros((hist, ext_ref.shape[1]), F32)

    x = zp_ref[...]
    ext_ref[hist:hist + tt, :] = x
    pos = t * tt + lax.broadcasted_iota(jnp.int32, (tt, LANES), 0)
    groups = []
    for gi, w in enumerate(POOL_WINDOWS):
        sl = slice(gi * LANES, (gi + 1) * LANES)
        acc = x[:, sl]
        for k in range(1, w):
            acc = acc + ext_ref[hist - k:hist - k + tt, sl]
        cnt = jnp.minimum(pos + 1, w).astype(F32)
        groups.append(acc / cnt - x[:, sl])
    _pool_mix(groups, gw_ref, scale_ref, o_ref)
    ext_ref[0:hist, :] = ext_ref[tt:tt + hist, :]


def _pool_prompt(z, nb, t_len, col_blk, gw, scale, tt=256):
    nt = t_len // tt
    pw = gw.shape[0] * LANES
    return pl.pallas_call(
        functools.partial(_pool_prompt_kernel, tt=tt),
        grid=(nb, nt),
        in_specs=[
            pl.BlockSpec((tt, pw), lambda b, t: (b * nt + t, col_blk)),
            pl.BlockSpec(gw.shape, lambda b, t: (0, 0, 0)),
            pl.BlockSpec((1, pw), lambda b, t: (0, 0)),
        ],
        out_specs=pl.BlockSpec((tt, pw), lambda b, t: (b * nt + t, 0)),
        out_shape=jax.ShapeDtypeStruct((nb * t_len, pw), BF16),
        scratch_shapes=[pltpu.VMEM((tt + 16, pw), F32)],
        compiler_params=_cparams(("parallel", "arbitrary")),
        name="pool_prompt",
    )(z, gw, scale)


def _pool_sample_kernel(zp_ref, buf_ref, gw_ref, scale_ref, o_ref):
    x = zp_ref[...]
    nbuf = buf_ref.shape[0]
    groups = []
    for gi, w in enumerate(POOL_WINDOWS):
        sl = slice(gi * LANES, (gi + 1) * LANES)
        acc = x[:, sl]
        for k in range(1, w):
            acc = acc + buf_ref[nbuf - k, :, sl]
        cnt = float(min(PAST_LEN + 1, w))
        groups.append(acc / cnt - x[:, sl])
    _pool_mix(groups, gw_ref, scale_ref, o_ref)


def _pool_sample(z, col_blk, buf_t, gw, scale):
    nrows = z.shape[0]
    pw = gw.shape[0] * LANES
    return pl.pallas_call(
        _pool_sample_kernel,
        grid=(1,),
        in_specs=[
            pl.BlockSpec((nrows, pw), lambda i: (0, col_blk)),
            pl.BlockSpec(buf_t.shape, lambda i: (0, 0, 0)),
            pl.BlockSpec(gw.shape, lambda i: (0, 0, 0)),
            pl.BlockSpec((1, pw), lambda i: (0, 0)),
        ],
        out_specs=pl.BlockSpec((nrows, pw), lambda i: (0, 0)),
        out_shape=jax.ShapeDtypeStruct((nrows, pw), BF16),
        compiler_params=_cparams(("arbitrary",)),
        name="pool_sample",
    )(z, buf_t, gw, scale)


def _pair_block_ones():
    r = lax.broadcasted_iota(jnp.int32, (LANES, LANES), 0) // HEAD
    c = lax.broadcasted_iota(jnp.int32, (LANES, LANES), 1) // HEAD
    return (r == c).astype(F32)


def _head_sum(x, bd):
    cols = [jnp.dot(x[:, c * LANES:(c + 1) * LANES], bd, precision=HI, preferred_element_type=F32)
            for c in range(x.shape[1] // LANES)]
    return jnp.concatenate(cols, axis=1)


def _prep_math(x, prev, p_refs):
    mu_ref, w0_ref, wup_ref, a0_ref, aup_ref, gup_ref, kk_ref, ka_ref, rk_ref = p_refs
    xm = x + (prev - x) * mu_ref[...]
    w = RWKV_W
    r = xm[:, 0:w]
    k = xm[:, w:2 * w]
    v = xm[:, 2 * w:3 * w]
    wl = xm[:, 3 * w:3 * w + 128]
    al = xm[:, 3 * w + 128:3 * w + 256]
    gl = xm[:, 3 * w + 256:3 * w + 512]
    bd = _pair_block_ones()
    dw = w0_ref[...] + jnp.dot(jnp.tanh(wl).astype(BF16), wup_ref[...], preferred_element_type=F32)
    lw = -EXP_M05 * jax.nn.sigmoid(dw)
    a = jax.nn.sigmoid(a0_ref[...] + jnp.dot(al.astype(BF16), aup_ref[...], preferred_element_type=F32))
    g = jnp.dot(jax.nn.sigmoid(gl).astype(BF16), gup_ref[...], preferred_element_type=F32)
    kk = k * kk_ref[...]
    kk = kk * lax.rsqrt(jnp.maximum(_head_sum(kk * kk, bd), 1e-24))
    kmod = k * (1.0 + (a - 1.0) * ka_ref[...])
    bonus = _head_sum(r * kmod * rk_ref[...], bd) * v
    return r, lw, kmod, v, -kk, kk * a, g, bonus


def _prep_prompt_kernel(zr_ref, *refs):
    p_refs, out_refs, carry_ref = refs[:9], refs[9:17], refs[17]
    t = pl.program_id(1)

    @pl.when(t == 0)
    def _():
        carry_ref[...] = jnp.zeros_like(carry_ref)

    x = zr_ref[...]
    rolled = pltpu.roll(x, 1, axis=0)
    first = lax.broadcasted_iota(jnp.int32, x.shape, 0) == 0
    prev = jnp.where(first, carry_ref[0:1, :], rolled)
    carry_ref[...] = rolled[0:SUBLANES, :]
    for o_ref, val in zip(out_refs, _prep_math(x, prev, p_refs)):
        o_ref[...] = val


def _prep_sample_kernel(zr_ref, prev_ref, *refs):
    p_refs, out_refs = refs[:9], refs[9:17]
    for o_ref, val in zip(out_refs, _prep_math(zr_ref[...], prev_ref[...], p_refs)):
        o_ref[...] = val


def _prep_param_specs(params):
    zero = (lambda *idx: (0, 0))
    return [pl.BlockSpec(p.shape, zero) for p in params]


def _prep_prompt(z, nb, t_len, params, tt=256):
    nt = t_len // tt
    rows = nb * t_len
    row_spec = pl.BlockSpec((tt, RWKV_W), lambda b, t: (b * nt + t, 0))
    return pl.pallas_call(
        _prep_prompt_kernel,
        grid=(nb, nt),
        in_specs=[pl.BlockSpec((tt, ZR_W), lambda b, t: (b * nt + t, 0))] + _prep_param_specs(params),
        out_specs=[row_spec] * 8,
        out_shape=[jax.ShapeDtypeStruct((rows, RWKV_W), F32)] * 8,
        scratch_shapes=[pltpu.VMEM((SUBLANES, ZR_W), F32)],
        compiler_params=_cparams(("parallel", "arbitrary")),
        name="prep_prompt",
    )(z, *params)


def _prep_sample(z, prev, params):
    nrows = z.shape[0]
    out_spec = pl.BlockSpec((nrows, RWKV_W), lambda i: (0, 0))
    return pl.pallas_call(
        _prep_sample_kernel,
        grid=(1,),
        in_specs=[pl.BlockSpec((nrows, ZR_W), lambda i: (0, 0)),
                  pl.BlockSpec((nrows, ZR_W), lambda i: (0, 0))] + _prep_param_specs(params),
        out_specs=[out_spec] * 8,
        out_shape=[jax.ShapeDtypeStruct((nrows, RWKV_W), F32)] * 8,
        compiler_params=_cparams(("arbitrary",)),
        name="prep_sample",
    )(z, prev, *params)


def _split3(x):
    hi = x.astype(BF16)
    rest = x - hi.astype(F32)
    mid = rest.astype(BF16)
    lo = (rest - mid.astype(F32)).astype(BF16)
    return hi, mid, lo


def _select_dot(sel, x):
    sel = sel.astype(BF16)
    hi, mid, lo = _split3(x)
    return _bdot(sel, hi) + (_bdot(sel, mid) + _bdot(sel, lo))


def _wkv_chunk_kernel(r_ref, lw_ref, k_ref, v_ref, a_ref, b_ref, y_ref, sout_ref, st_ref, *, c_len, pp, nc):
    c = pl.program_id(2)
    n2 = 2 * c_len
    assert n2 == LANES

    @pl.when(c == 0)
    def _():
        st_ref[...] = jnp.zeros_like(st_ref)

    row = lax.broadcasted_iota(jnp.int32, (n2, n2), 0)
    col = lax.broadcasted_iota(jnp.int32, (n2, n2), 1)
    tr = row & (c_len - 1)
    tc = col & (c_len - 1)
    strict = tr > tc
    incl = tr >= tc
    eye = row == col
    ones = jnp.ones((LANES, LANES), BF16)
    tri = (lax.broadcasted_iota(jnp.int32, (c_len, c_len), 0)
           >= lax.broadcasted_iota(jnp.int32, (c_len, c_len), 1))
    head_a = lax.broadcasted_iota(jnp.int32, (c_len, LANES), 1) < HEAD

    def stack(x):
        return jnp.concatenate([jnp.where(head_a, x, 0.0), jnp.where(head_a, 0.0, x)], axis=0).astype(BF16)

    prs = range(pp)
    cat = jnp.concatenate
    sls = [slice(q * LANES, (q + 1) * LANES) for q in prs]
    lw = [lw_ref[:, sl] for sl in sls]
    cum = [_select_dot(tri, x) for x in lw]
    tot = [x[c_len - 1:c_len, :] for x in cum]
    xr = [stack(r_ref[:, sls[q]] * jnp.exp(cum[q])) for q in prs]
    xa = [stack(a_ref[:, sls[q]] * jnp.exp(cum[q] - lw[q])) for q in prs]
    e_neg = [jnp.exp(-x) for x in cum]
    e_rem = [jnp.exp(tot[q] - cum[q]) for q in prs]
    yb = [stack(b_ref[:, sls[q]] * e_neg[q]) for q in prs]
    yk = [stack(k_ref[:, sls[q]] * e_neg[q]) for q in prs]
    zb = [stack(b_ref[:, sls[q]] * e_rem[q]) for q in prs]
    zk = [stack(k_ref[:, sls[q]] * e_rem[q]) for q in prs]
    vs = [stack(v_ref[:, sl]) for sl in sls]

    g = [_bdot(cat([xa[q], xr[q]], axis=0), cat([yb[q], yk[q]], axis=0), NT_DIMS) for q in prs]
    m_ab = [jnp.where(strict, x[0:n2, 0:n2], 0.0) for x in g]
    m_ak = [jnp.where(strict, x[0:n2, n2:2 * n2], 0.0).astype(BF16) for x in g]
    n_rb = [jnp.where(incl, x[n2:2 * n2, 0:n2], 0.0) for x in g]
    n_rk = [jnp.where(incl, x[n2:2 * n2, n2:2 * n2], 0.0).astype(BF16) for x in g]

    t_inv = [jnp.where(eye, 1.0, 0.0) + x for x in m_ab]
    mk = [_bdot(x, x) for x in m_ab]
    nlev = int(math.log2(c_len))
    for lev in range(1, nlev):
        if lev < nlev - 1:
            res = [_bdot(cat([mk[q], t_inv[q]], axis=0), mk[q]) for q in prs]
            mk = [x[0:n2] for x in res]
            t_inv = [t_inv[q] + res[q][n2:2 * n2] for q in prs]
        else:
            t_inv = [t_inv[q] + _bdot(t_inv[q], mk[q]) for q in prs]

    st = [st_ref[q] for q in prs]
    lhs = [cat([cat([xa[q], m_ak[q]], axis=1), cat([xr[q], n_rk[q]], axis=1)], axis=0) for q in prs]
    xy0 = [_bdot(lhs[q], cat([st[q].astype(BF16), vs[q]], axis=0)) for q in prs]
    u = [_bdot(t_inv[q], xy0[q][0:n2]).astype(BF16) for q in prs]
    y_st = [xy0[q][n2:2 * n2] + _bdot(n_rb[q], u[q]) for q in prs]
    for q in prs:
        y_ref[:, sls[q]] = y_st[q][0:c_len] + y_st[q][c_len:n2]

    for q in prs:
        dhi, dmid, dlo = _split3(jnp.where(eye, jnp.broadcast_to(jnp.exp(tot[q]), (LANES, LANES)), 0.0))
        dec_col = _bdot(dhi, ones) + (_bdot(dmid, ones) + _bdot(dlo, ones))
        st_ref[q] = dec_col * st[q] + _bdot(cat([zb[q], zk[q]], axis=0), cat([u[q], vs[q]], axis=0), TN_DIMS)

    @pl.when(c == nc - 1)
    def _():
        sout_ref[0] = st_ref[...]


def _wkv_chunk(seqs, nb, t_len, c_len=64, pp=4):
    nc = t_len // c_len
    npair = RWKV_W // LANES
    ng = npair // pp
    blk = pl.BlockSpec((c_len, pp * LANES), lambda b, g, c: (b * nc + c, g))
    return pl.pallas_call(
        functools.partial(_wkv_chunk_kernel, c_len=c_len, pp=pp, nc=nc),
        grid=(nb, ng, nc),
        in_specs=[blk] * 6,
        out_specs=[blk, pl.BlockSpec((1, pp, LANES, LANES), lambda b, g, c: (b, g, 0, 0))],
        out_shape=[jax.ShapeDtypeStruct((nb * t_len, RWKV_W), F32),
                   jax.ShapeDtypeStruct((nb, npair, LANES, LANES), F32)],
        scratch_shapes=[pltpu.VMEM((pp, LANES, LANES), F32)],
        compiler_params=_cparams(("parallel", "parallel", "arbitrary")),
        name="wkv_chunk",
    )(*seqs)


def _wkv_step_kernel(s_ref, r_ref, lw_ref, k_ref, v_ref, a_ref, b_ref, g_ref, bonus_ref, lng_ref, lnb_ref,
                     yb_ref, so_ref, vt_scr, y_scr):
    rt, wt, kt, at, bt = (x[...].T for x in (r_ref, lw_ref, k_ref, a_ref, b_ref))
    wt = jnp.exp(wt)
    vt_scr[...] = v_ref[...].T
    for hh in range(2):
        rows = slice(hh * HEAD, (hh + 1) * HEAD)
        r, w, k, a, b = (x[rows, :] for x in (rt, wt, kt, at, bt))

        def body(i, carry):
            si = s_ref[hh, i]
            sa = jnp.sum(si * a, axis=0, keepdims=True)
            vi = vt_scr[pl.ds(hh * HEAD + i, 1), :]
            s2 = si * w + sa * b + vi * k
            so_ref[hh, i] = s2
            y_scr[pl.ds(hh * HEAD + i, 1), :] = jnp.sum(s2 * r, axis=0, keepdims=True)
            return carry

        lax.fori_loop(0, HEAD, body, 0)

    outs = []
    for hh in range(2):
        y = y_scr[hh * HEAD:(hh + 1) * HEAD, :]
        d = y - jnp.mean(y, axis=0, keepdims=True)
        var = jnp.mean(d * d, axis=0, keepdims=True)
        outs.append(d * lax.rsqrt(var + GN_EPS))
    yn = jnp.concatenate(outs, axis=0).T * lng_ref[...] + lnb_ref[...]
    yb_ref[...] = ((yn + bonus_ref[...]) * g_ref[...]).astype(yb_ref.dtype)


def _wkv_step(state_t, vecs, ln_g, ln_b):
    nh, _, _, nb = state_t.shape
    st = pl.BlockSpec((2, HEAD, HEAD, nb), lambda p: (p, 0, 0, 0))
    vec = pl.BlockSpec((nb, 2 * HEAD), lambda p: (0, p))
    par = pl.BlockSpec((1, 2 * HEAD), lambda p: (0, p))
    return pl.pallas_call(
        _wkv_step_kernel,
        grid=(nh // 2,),
        in_specs=[st] + [vec] * 8 + [par, par],
        out_specs=[vec, st],
        out_shape=[jax.ShapeDtypeStruct((nb, nh * HEAD), BF16), jax.ShapeDtypeStruct(state_t.shape, F32)],
        scratch_shapes=[pltpu.VMEM((2 * HEAD, nb), F32), pltpu.VMEM((2 * HEAD, nb), F32)],
        compiler_params=_cparams(("parallel",)),
        name="wkv_step",
    )(state_t, *vecs, ln_g, ln_b)


def _rwkv_post_kernel(y_ref, bonus_ref, g_ref, lng_ref, lnb_ref, o_ref):
    bd = _pair_block_ones() * (1.0 / HEAD)
    y = y_ref[...]
    d = y - _head_sum(y, bd)
    var = _head_sum(d * d, bd)
    yn = d * lax.rsqrt(var + GN_EPS) * lng_ref[...] + lnb_ref[...]
    o_ref[...] = ((yn + bonus_ref[...]) * g_ref[...]).astype(o_ref.dtype)


def _rwkv_post(y, bonus, g, ln_g, ln_b, tm):
    m = y.shape[0]
    row = pl.BlockSpec((tm, RWKV_W), lambda i: (i, 0))
    par = pl.BlockSpec((1, RWKV_W), lambda i: (0, 0))
    return pl.pallas_call(
        _rwkv_post_kernel,
        grid=(m // tm,),
        in_specs=[row, row, row, par, par],
        out_specs=row,
        out_shape=jax.ShapeDtypeStruct((m, RWKV_W), BF16),
        compiler_params=_cparams(("parallel",)),
        name="rwkv_post",
    )(y, bonus, g, ln_g, ln_b)


def _xattn_prompt_kernel(q_ref, k_ref, v_ref, o_ref):
    scale = XA_DIM ** -0.5
    q = q_ref[...]
    for h in range(XA_HEADS):
        sl = slice(h * XA_DIM, (h + 1) * XA_DIM)
        s = lax.dot_general(q[:, sl].astype(BF16), k_ref[0, :, sl].astype(BF16),
                            NT_DIMS, preferred_element_type=F32) * scale
        p = jnp.exp(s - jnp.max(s, axis=-1, keepdims=True))
        den = jnp.sum(p, axis=-1, keepdims=True)
        o = jnp.dot(p.astype(BF16), v_ref[0, :, sl].astype(BF16), preferred_element_type=F32)
        o_ref[:, sl] = (o / den).astype(o_ref.dtype)


def _xattn_prompt(z, nb, t_len, col_blk, mk, mv, tq=512):
    nt = t_len // tq
    xw = XA_HEADS * XA_DIM
    nmem = mk.shape[1]
    kv = pl.BlockSpec((1, nmem, xw), lambda b, t: (b, 0, 0))
    return pl.pallas_call(
        _xattn_prompt_kernel,
        grid=(nb, nt),
        in_specs=[pl.BlockSpec((tq, xw), lambda b, t: (b * nt + t, col_blk)), kv, kv],
        out_specs=pl.BlockSpec((tq, xw), lambda b, t: (b * nt + t, 0)),
        out_shape=jax.ShapeDtypeStruct((nb * t_len, xw), BF16),
        compiler_params=_cparams(("parallel", "parallel")),
        name="xattn_prompt",
    )(z, mk, mv)


def _xattn_sample_kernel(q_ref, k_ref, v_ref, o_ref):
    bb = q_ref.shape[0]
    nrow = k_ref.shape[1] // SUBLANES
    full = (bb, nrow, SUBLANES, XA_DIM)
    q = q_ref[...] * (XA_DIM ** -0.5)
    q8 = jnp.concatenate([q, q], axis=1)[:, None]
    k = k_ref[...].reshape(full)
    s = jnp.broadcast_to(jnp.sum(k * q8, axis=-1, keepdims=True), full)
    mx = jnp.max(s, axis=1, keepdims=True)
    mx = jnp.maximum(mx, pltpu.roll(mx, XA_HEADS, axis=2))
    p = jnp.exp(s - mx)
    den = jnp.sum(p, axis=1, keepdims=True)
    den = den + pltpu.roll(den, XA_HEADS, axis=2)
    o = jnp.sum(p * v_ref[...].reshape(full), axis=1, keepdims=True)
    o = o + pltpu.roll(o, XA_HEADS, axis=2)
    o_ref[...] = (o / den)[:, 0, 0:XA_HEADS, :].astype(o_ref.dtype)


def _xattn_sample(q3, mk, mv, bb=8):
    nb, rows, _ = mk.shape
    kv = pl.BlockSpec((bb, rows, XA_DIM), lambda i: (i, 0, 0))
    qs = pl.BlockSpec((bb, XA_HEADS, XA_DIM), lambda i: (i, 0, 0))
    return pl.pallas_call(
        _xattn_sample_kernel,
        grid=(nb // bb,),
        in_specs=[qs, kv, kv],
        out_specs=qs,
        out_shape=jax.ShapeDtypeStruct((nb, XA_HEADS, XA_DIM), BF16),
        compiler_params=_cparams(("parallel",)),
        name="xattn_sample",
    )(q3, mk, mv)


def _merge_kernel(pa_ref, pb_ref, pc_ref, g0_ref, g1_ref, g2_ref, h_ref, wa_ref, wb_ref, wc_ref, wo_ref, o_ref):
    oa = jnp.dot(pa_ref[...], wa_ref[...], preferred_element_type=F32)
    ob = jnp.dot(pb_ref[...], wb_ref[...], preferred_element_type=F32)
    oc = jnp.dot(pc_ref[...], wc_ref[...], preferred_element_type=F32)
    merged = (jax.nn.sigmoid(g0_ref[...]) * oa + jax.nn.sigmoid(g1_ref[...]) * ob
              + jax.nn.sigmoid(g2_ref[...]) * oc)
    o_ref[...] = h_ref[...] + jnp.dot(merged.astype(BF16), wo_ref[...], preferred_element_type=F32)


def _merge(pa, pb, pc, z, zg_blk0, h, tm, wa, wb, wc, wo):
    m, d = h.shape
    const = lambda i: (0, 0)
    resident = lambda w: pl.BlockSpec(w.shape, const, pipeline_mode=pl.Buffered(1))
    in_specs = [
        pl.BlockSpec((tm, pa.shape[1]), lambda i: (i, 0)),
        pl.BlockSpec((tm, pb.shape[1]), lambda i: (i, 0)),
        pl.BlockSpec((tm, pc.shape[1]), lambda i: (i, 0)),
        pl.BlockSpec((tm, d), lambda i: (i, zg_blk0)),
        pl.BlockSpec((tm, d), lambda i: (i, zg_blk0 + 1)),
        pl.BlockSpec((tm, d), lambda i: (i, zg_blk0 + 2)),
        pl.BlockSpec((tm, d), lambda i: (i, 0)),
        resident(wa), resident(wb), resident(wc), resident(wo),
    ]
    return pl.pallas_call(
        _merge_kernel,
        grid=(m // tm,),
        in_specs=in_specs,
        out_specs=pl.BlockSpec((tm, d), lambda i: (i, 0)),
        out_shape=jax.ShapeDtypeStruct((m, d), F32),
        compiler_params=_cparams(("parallel",)),
        name="merge",
    )(pa, pb, pc, z, z, z, h, wa, wb, wc, wo)


def _pack_zr(x, axis):
    w = RWKV_W
    take = lambda a, b: lax.slice_in_dim(x, a, b, axis=axis)

    def pad(n):
        shape = list(x.shape)
        shape[axis] = n
        return jnp.zeros(shape, x.dtype)

    return jnp.concatenate([take(0, 3 * w + 64), pad(64), take(3 * w + 64, 3 * w + 128), pad(64),
                            take(3 * w + 128, ZR_TRUE), pad(96)], axis=axis)


def _unpack_zr_cols(x):
    w = RWKV_W
    return jnp.concatenate([x[..., :3 * w + 64], x[..., 3 * w + 128:3 * w + 192], x[..., 3 * w + 256:3 * w + 416]],
                           axis=-1)


def _pad_rows(x, n):
    return jnp.concatenate([x, jnp.zeros((n - x.shape[0],) + x.shape[1:], x.dtype)], axis=0)


def kernel(x_prompt, x_sample, mem_prompt, cache_mem_k, cache_mem_v, state_wkv, state_shift, state_pool,
           ffn1_norm_g, ffn1_w_gate, ffn1_w_up, ffn1_w_down, mix_norm_g, w_in,
           pool_group_w, pool_scale, pool_out,
           rwkv_mu, rwkv_w0, rwkv_w_up, rwkv_a0, rwkv_a_up, rwkv_g_up, rwkv_k_k, rwkv_k_a, rwkv_r_k,
           rwkv_ln_g, rwkv_ln_b, rwkv_out,
           mem_norm_g, w_mem_k, w_mem_v, xattn_out, w_o,
           ffn2_norm_g, ffn2_w_gate, ffn2_w_up, ffn2_w_down, final_norm_g):
    nb, t_len, d = x_prompt.shape
    ns = x_sample.shape[0]
    assert w_in.shape[0] == 1 and x_sample.shape[1] == 1
    n_mem = mem_prompt.shape[1]
    pool_w = pool_out.shape[1]
    xa_w = xattn_out.shape[1]
    n_heads = RWKV_W // HEAD
    nbuf = state_pool.shape[2]
    rows_p = nb * t_len
    l = 0

    d_ff = ffn1_w_gate.shape[2]
    fpad = (-d_ff) % 512

    def ffn_weights(wg, wu, wd):
        return (jnp.pad(wg, ((0, 0), (0, fpad))).astype(BF16), jnp.pad(wu, ((0, 0), (0, fpad))).astype(BF16),
                jnp.pad(wd, ((0, fpad), (0, 0))).astype(BF16))

    f1 = ffn_weights(ffn1_w_gate[l], ffn1_w_up[l], ffn1_w_down[l])
    f2 = ffn_weights(ffn2_w_gate[l], ffn2_w_up[l], ffn2_w_down[l])
    wit = jnp.swapaxes(w_in[l], 0, 1)
    o_zr, o_zq, o_zg = pool_w, pool_w + ZR_TRUE, pool_w + ZR_TRUE + xa_w
    w_in_t = jnp.concatenate([_pack_zr(wit[o_zr:o_zq], 0), wit[:o_zr], wit[o_zg:], wit[o_zq:o_zg]],
                             axis=0).astype(BF16)
    col_zp = ZR_W // pool_w
    col_zg = (ZR_W + pool_w) // d
    col_zq = (ZR_W + pool_w + 3 * d) // xa_w
    row = lambda v: v.reshape(1, -1)
    prep_params = [row(_pack_zr(rwkv_mu[l], 0)), row(rwkv_w0[l]), _pad_rows(rwkv_w_up[l], 128).astype(BF16),
                   row(rwkv_a0[l]), _pad_rows(rwkv_a_up[l], 128).astype(BF16),
                   _pad_rows(rwkv_g_up[l], 256).astype(BF16), row(rwkv_k_k[l]), row(rwkv_k_a[l]),
                   row(rwkv_r_k[l])]
    gw = pool_group_w[l].astype(BF16)
    w_kv_t = jnp.concatenate([jnp.swapaxes(w_mem_k[l], 0, 1), jnp.swapaxes(w_mem_v[l], 0, 1)], axis=0).astype(BF16)
    wa, wb, wc, wo = (pool_out[l].astype(BF16), rwkv_out[l].astype(BF16), xattn_out[l].astype(BF16),
                      w_o[l].astype(BF16))
    g1, gm, g2, fg = row(ffn1_norm_g[l]), row(mix_norm_g[l]), row(ffn2_norm_g[l]), row(final_norm_g)
    scale = row(pool_scale[l])
    ln_g, ln_b = rwkv_ln_g[l], rwkv_ln_b[l]

    tm_p = 512
    h1_p = _ffn(x_prompt.reshape(rows_p, d), tm_p, g1, *f1)
    h1_s = _ffn(x_sample.reshape(ns, d), ns, g1, *f1)
    z_p = _norm_matmul(h1_p, gm, w_in_t, tm=1024, tn=512)
    z_s = _norm_matmul(h1_s, gm, w_in_t, tm=ns, tn=512)
    kv = _norm_matmul(mem_prompt.reshape(nb * n_mem, d), row(mem_norm_g[l]), w_kv_t, tm=512, tn=512)
    mk_p = kv[:, :xa_w].reshape(nb, n_mem, xa_w)
    mv_p = kv[:, xa_w:].reshape(nb, n_mem, xa_w)

    pool_state = state_pool.reshape(ns, nbuf, pool_w)
    pa_p = _pool_prompt(z_p, nb, t_len, col_zp, gw, scale)
    pa_s = _pool_sample(z_s, col_zp, jnp.swapaxes(pool_state, 0, 1), gw, scale)

    prep_p = _prep_prompt(z_p, nb, t_len, prep_params)
    y_p, st_p = _wkv_chunk(prep_p[:6], nb, t_len)
    pb_p = _rwkv_post(y_p, prep_p[7], prep_p[6], row(ln_g), row(ln_b), tm=512)
    prep_s = _prep_sample(z_s, _pack_zr(state_shift.reshape(ns, ZR_TRUE), 1), prep_params)
    state_t = jnp.transpose(state_wkv.reshape(ns, n_heads, HEAD, HEAD), (1, 2, 3, 0))
    pb_s, wkv_s_t = _wkv_step(state_t, prep_s, row(ln_g), row(ln_b))

    pc_p = _xattn_prompt(z_p, nb, t_len, col_zq, mk_p, mv_p)
    q_s = z_s[:, ZR_W + pool_w + 3 * d:].reshape(ns, XA_HEADS, XA_DIM)
    pc_s = _xattn_sample(q_s, cache_mem_k.reshape(ns, n_mem * XA_HEADS, XA_DIM),
                         cache_mem_v.reshape(ns, n_mem * XA_HEADS, XA_DIM)).reshape(ns, xa_w)

    h2_p = _merge(pa_p, pb_p, pc_p, z_p, col_zg, h1_p, 256, wa, wb, wc, wo)
    h2_s = _merge(pa_s, pb_s, pc_s, z_s, col_zg, h1_s, ns, wa, wb, wc, wo)
    y_prompt = _ffn(h2_p, tm_p, g2, *f2, final_g=fg)
    y_sample = _ffn(h2_s, ns, g2, *f2, final_g=fg)

    ends = [(b + 1) * t_len for b in range(nb)]
    shift_p = _unpack_zr_cols(jnp.stack([z_p[e - 1:e, :ZR_W] for e in ends]))[None]
    pool_p = jnp.stack([z_p[e - nbuf:e, ZR_W:ZR_W + pool_w] for e in ends])[None]
    shift_s = _unpack_zr_cols(z_s[:, :ZR_W])[None, :, None, :]
    pool_s = jnp.concatenate([pool_state[:, 1:], z_s[:, None, ZR_W:ZR_W + pool_w]], axis=1)[None]
    st5 = st_p.reshape(nb, n_heads // 2, 2, HEAD, 2, HEAD)
    wkv_p = jnp.stack([st5[:, :, 0, :, 0, :], st5[:, :, 1, :, 1, :]], axis=2).reshape(nb, n_heads, HEAD, HEAD)
    wkv_p = jnp.swapaxes(wkv_p, -1, -2)[None]
    wkv_s = jnp.transpose(wkv_s_t, (3, 0, 1, 2)).reshape(state_wkv.shape)
    mem_k_p = mk_p.reshape(1, nb, n_mem, XA_HEADS, XA_DIM)
    mem_v_p = mv_p.reshape(1, nb, n_mem, XA_HEADS, XA_DIM)
    return (y_prompt.reshape(nb, t_len, d), y_sample.reshape(ns, 1, d), mem_k_p, mem_v_p, wkv_p, shift_p, pool_p,
            wkv_s, shift_s, pool_s)
```

```python
import functools
import math

import jax
import jax.numpy as jnp
from jax import lax
from jax.experimental import pallas as pl
from jax.experimental.pallas import tpu as pltpu

F32 = jnp.float32
BF16 = jnp.bfloat16
HI = lax.Precision.HIGHEST

RMS_EPS = 1e-6
GN_EPS = 64e-5
POOL_WINDOWS = (2, 4, 8, 16)
HEAD = 64
LANES = 128
SUBLANES = 8
XA_HEADS = 4
XA_DIM = 128
PAST_LEN = 16384
VMEM_LIMIT = 56 * 1024 * 1024
EXP_M05 = math.exp(-0.5)

ZR_W = 3584
RWKV_W = 1024
ZR_TRUE = 3360

NN_DIMS = (((1,), (0,)), ((), ()))
NT_DIMS = (((1,), (1,)), ((), ()))
TN_DIMS = (((0,), (0,)), ((), ()))


def _cparams(sem):
    return pltpu.CompilerParams(dimension_semantics=sem, vmem_limit_bytes=VMEM_LIMIT)


def _rms(x, g):
    ms = jnp.mean(x * x, axis=-1, keepdims=True)
    return x * lax.rsqrt(ms + RMS_EPS) * g


def _bdot(a, b, dims=NN_DIMS):
    return lax.dot_general(a.astype(BF16), b.astype(BF16), dims, preferred_element_type=F32)


def _ffn_kernel(hp_ref, hs_ref, g_ref, fg_ref, wg_ref, wu_ref, wd_ref, wgt_ref, wut_ref, wdt_ref,
                op_ref, os_ref, xp_ref, xs_ref, *, nfull, final):
    m = pl.program_id(0)
    f = pl.program_id(1)

    def start(h_ref, x_ref, o_ref):
        x_ref[...] = _rms(h_ref[...], g_ref[...]).astype(BF16)
        o_ref[...] = jnp.zeros_like(o_ref)

    def contribution(x_ref, wg, wu, wd):
        xn = x_ref[...]
        gate = jnp.dot(xn, wg, preferred_element_type=F32)
        up = jnp.dot(xn, wu, preferred_element_type=F32)
        act = (gate * jax.nn.sigmoid(gate) * up).astype(BF16)
        return jnp.dot(act, wd, preferred_element_type=F32)

    def finish(h_ref, o_ref, last):
        out = h_ref[...] + 0.5 * (o_ref[...] + last)
        if final:
            out = _rms(out, fg_ref[...])
        o_ref[...] = out

    @pl.when(f == 0)
    def _():
        start(hp_ref, xp_ref, op_ref)

    @pl.when((f == 0) & (m == 0))
    def _():
        start(hs_ref, xs_ref, os_ref)

    @pl.when(f < nfull)
    def _():
        wg, wu, wd = wg_ref[...].astype(BF16), wu_ref[...].astype(BF16), wd_ref[...].astype(BF16)
        op_ref[...] += contribution(xp_ref, wg, wu, wd)

        @pl.when(m == 0)
        def _():
            os_ref[...] += contribution(xs_ref, wg, wu, wd)

    @pl.when(f == nfull)
    def _():
        wg, wu, wd = wgt_ref[...].astype(BF16), wut_ref[...].astype(BF16), wdt_ref[...].astype(BF16)
        finish(hp_ref, op_ref, contribution(xp_ref, wg, wu, wd))

        @pl.when(m == 0)
        def _():
            finish(hs_ref, os_ref, contribution(xs_ref, wg, wu, wd))


def _ffn(hp, hs, tm, g, wg, wu, wd, final_g=None, tf=256):
    mp, d = hp.shape
    ms = hs.shape[0]
    d_ff = wg.shape[1]
    nfull, tail = divmod(d_ff, tf)
    assert tail > 0 and tail % LANES == 0 and tf % tail == 0 and mp % tm == 0
    last_main = nfull - 1
    tail_blk = d_ff // tail - 1
    final = final_g is not None
    vec = pl.BlockSpec((1, d), lambda i, f: (0, 0))
    once = pl.Buffered(1)
    in_specs = [
        pl.BlockSpec((tm, d), lambda i, f: (i, 0), pipeline_mode=once),
        pl.BlockSpec((ms, d), lambda i, f: (0, 0), pipeline_mode=once),
        vec, vec,
        pl.BlockSpec((d, tf), lambda i, f: (0, jnp.minimum(f, last_main))),
        pl.BlockSpec((d, tf), lambda i, f: (0, jnp.minimum(f, last_main))),
        pl.BlockSpec((tf, d), lambda i, f: (jnp.minimum(f, last_main), 0)),
        pl.BlockSpec((d, tail), lambda i, f: (0, tail_blk), pipeline_mode=once),
        pl.BlockSpec((d, tail), lambda i, f: (0, tail_blk), pipeline_mode=once),
        pl.BlockSpec((tail, d), lambda i, f: (tail_blk, 0), pipeline_mode=once),
    ]
    return pl.pallas_call(
        functools.partial(_ffn_kernel, nfull=nfull, final=final),
        grid=(mp // tm, nfull + 1),
        in_specs=in_specs,
        out_specs=[pl.BlockSpec((tm, d), lambda i, f: (i, 0)), pl.BlockSpec((ms, d), lambda i, f: (0, 0))],
        out_shape=[jax.ShapeDtypeStruct((mp, d), F32), jax.ShapeDtypeStruct((ms, d), F32)],
        scratch_shapes=[pltpu.VMEM((tm, d), BF16), pltpu.VMEM((ms, d), BF16)],
        compiler_params=_cparams(("arbitrary", "arbitrary")),
        name="ffn",
    )(hp, hs, g, g if final_g is None else final_g, wg, wu, wd, wg, wu, wd)


def _norm_matmul_kernel(h_ref, g_ref, wt_ref, o_ref, xn_ref):
    @pl.when(pl.program_id(1) == 0)
    def _():
        xn_ref[...] = _rms(h_ref[...], g_ref[...]).astype(BF16)

    o_ref[...] = lax.dot_general(xn_ref[...], wt_ref[...], NT_DIMS, preferred_element_type=F32)


def _norm_matmul(h, g, wt, tm, tn):
    m, d = h.shape
    n = wt.shape[0]
    return pl.pallas_call(
        _norm_matmul_kernel,
        grid=(m // tm, n // tn),
        in_specs=[
            pl.BlockSpec((tm, d), lambda i, j: (i, 0)),
            pl.BlockSpec((1, d), lambda i, j: (0, 0)),
            pl.BlockSpec((tn, d), lambda i, j: (j, 0)),
        ],
        out_specs=pl.BlockSpec((tm, tn), lambda i, j: (i, j)),
        out_shape=jax.ShapeDtypeStruct((m, n), F32),
        scratch_shapes=[pltpu.VMEM((tm, d), BF16)],
        compiler_params=_cparams(("parallel", "arbitrary")),
        name="norm_matmul",
    )(h, g, wt)


def _pool_mix(pooled_groups, gw_ref, scale_ref, o_ref):
    for gi, pooled in enumerate(pooled_groups):
        sl = slice(gi * LANES, (gi + 1) * LANES)
        mixed = jnp.dot(pooled.astype(BF16), gw_ref[gi], preferred_element_type=F32)
        o_ref[:, sl] = (mixed * scale_ref[:, sl]).astype(o_ref.dtype)


def _pool_prompt_kernel(zp_ref, gw_ref, scale_ref, o_ref, ext_ref, *, tt):
    t = pl.program_id(1)
    hist = 16

    @pl.when(t == 0)
    def _():
        ext_ref[0:hist, :] = jnp.zeros((hist, ext_ref.shape[1]), F32)

    x = zp_ref[...]
    ext_ref[hist:hist + tt, :] = x
    pos = t * tt + lax.broadcasted_iota(jnp.int32, (tt, LANES), 0)
    groups = []
    for gi, w in enumerate(POOL_WINDOWS):
        sl = slice(gi * LANES, (gi + 1) * LANES)
        acc = x[:, sl]
        for k in range(1, w):
            acc = acc + ext_ref[hist - k:hist - k + tt, sl]
        cnt = jnp.minimum(pos + 1, w).astype(F32)
        groups.append(acc / cnt - x[:, sl])
    _pool_mix(groups, gw_ref, scale_ref, o_ref)
    ext_ref[0:hist, :] = ext_ref[tt:tt + hist, :]


def _pool_prompt(z, nb, t_len, col_blk, gw, scale, tt=256):
    nt = t_len // tt
    pw = gw.shape[0] * LANES
    return pl.pallas_call(
        functools.partial(_pool_prompt_kernel, tt=tt),
        grid=(nb, nt),
        in_specs=[
            pl.BlockSpec((tt, pw), lambda b, t: (b * nt + t, col_blk)),
            pl.BlockSpec(gw.shape, lambda b, t: (0, 0, 0)),
            pl.BlockSpec((1, pw), lambda b, t: (0, 0)),
        ],
        out_specs=pl.BlockSpec((tt, pw), lambda b, t: (b * nt + t, 0)),
        out_shape=jax.ShapeDtypeStruct((nb * t_len, pw), BF16),
        scratch_shapes=[pltpu.VMEM((tt + 16, pw), F32)],
        compiler_params=_cparams(("parallel", "arbitrary")),
        name="pool_prompt",
    )(z, gw, scale)


def _pool_sample_kernel(zp_ref, buf_ref, gw_ref, scale_ref, o_ref):
    x = zp_ref[...]
    nbuf = buf_ref.shape[0]
    groups = []
    for gi, w in enumerate(POOL_WINDOWS):
        sl = slice(gi * LANES, (gi + 1) * LANES)
        acc = x[:, sl]
        for k in range(1, w):
            acc = acc + buf_ref[nbuf - k, :, sl]
        cnt = float(min(PAST_LEN + 1, w))
        groups.append(acc / cnt - x[:, sl])
    _pool_mix(groups, gw_ref, scale_ref, o_ref)


def _pool_sample(z, col_blk, buf_t, gw, scale):
    nrows = z.shape[0]
    pw = gw.shape[0] * LANES
    return pl.pallas_call(
        _pool_sample_kernel,
        grid=(1,),
        in_specs=[
            pl.BlockSpec((nrows, pw), lambda i: (0, col_blk)),
            pl.BlockSpec(buf_t.shape, lambda i: (0, 0, 0)),
            pl.BlockSpec(gw.shape, lambda i: (0, 0, 0)),
            pl.BlockSpec((1, pw), lambda i: (0, 0)),
        ],
        out_specs=pl.BlockSpec((nrows, pw), lambda i: (0, 0)),
        out_shape=jax.ShapeDtypeStruct((nrows, pw), BF16),
        compiler_params=_cparams(("arbitrary",)),
        name="pool_sample",
    )(z, buf_t, gw, scale)


def _pair_block_ones():
    r = lax.broadcasted_iota(jnp.int32, (LANES, LANES), 0) // HEAD
    c = lax.broadcasted_iota(jnp.int32, (LANES, LANES), 1) // HEAD
    return (r == c).astype(F32)


def _head_sum(x, bd):
    cols = [jnp.dot(x[:, c * LANES:(c + 1) * LANES], bd, precision=HI, preferred_element_type=F32)
            for c in range(x.shape[1] // LANES)]
    return jnp.concatenate(cols, axis=1)


def _prep_math(x, prev, p_refs):
    mu_ref, w0_ref, wup_ref, a0_ref, aup_ref, gup_ref, kk_ref, ka_ref, rk_ref = p_refs
    xm = x + (prev - x) * mu_ref[...]
    w = RWKV_W
    r = xm[:, 0:w]
    k = xm[:, w:2 * w]
    v = xm[:, 2 * w:3 * w]
    wl = xm[:, 3 * w:3 * w + 128]
    al = xm[:, 3 * w + 128:3 * w + 256]
    gl = xm[:, 3 * w + 256:3 * w + 512]
    bd = _pair_block_ones()
    dw = w0_ref[...] + jnp.dot(jnp.tanh(wl).astype(BF16), wup_ref[...], preferred_element_type=F32)
    lw = -EXP_M05 * jax.nn.sigmoid(dw)
    a = jax.nn.sigmoid(a0_ref[...] + jnp.dot(al.astype(BF16), aup_ref[...], preferred_element_type=F32))
    g = jnp.dot(jax.nn.sigmoid(gl).astype(BF16), gup_ref[...], preferred_element_type=F32)
    kk = k * kk_ref[...]
    kk = kk * lax.rsqrt(jnp.maximum(_head_sum(kk * kk, bd), 1e-24))
    kmod = k * (1.0 + (a - 1.0) * ka_ref[...])
    bonus = _head_sum(r * kmod * rk_ref[...], bd) * v
    return r, lw, kmod, v, -kk, kk * a, g, bonus


def _prep_prompt_kernel(zr_ref, *refs):
    p_refs, out_refs, carry_ref = refs[:9], refs[9:17], refs[17]
    t = pl.program_id(1)

    @pl.when(t == 0)
    def _():
        carry_ref[...] = jnp.zeros_like(carry_ref)

    x = zr_ref[...]
    rolled = pltpu.roll(x, 1, axis=0)
    first = lax.broadcasted_iota(jnp.int32, x.shape, 0) == 0
    prev = jnp.where(first, carry_ref[0:1, :], rolled)
    carry_ref[...] = rolled[0:SUBLANES, :]
    for o_ref, val in zip(out_refs, _prep_math(x, prev, p_refs)):
        o_ref[...] = val


def _prep_sample_kernel(zr_ref, prev_ref, *refs):
    p_refs, out_refs = refs[:9], refs[9:17]
    for o_ref, val in zip(out_refs, _prep_math(zr_ref[...], prev_ref[...], p_refs)):
        o_ref[...] = val


def _prep_param_specs(params):
    zero = (lambda *idx: (0, 0))
    return [pl.BlockSpec(p.shape, zero) for p in params]


def _prep_prompt(z, nb, t_len, params, tt=256):
    nt = t_len // tt
    rows = nb * t_len
    row_spec = pl.BlockSpec((tt, RWKV_W), lambda b, t: (b * nt + t, 0))
    return pl.pallas_call(
        _prep_prompt_kernel,
        grid=(nb, nt),
        in_specs=[pl.BlockSpec((tt, ZR_W), lambda b, t: (b * nt + t, 0))] + _prep_param_specs(params),
        out_specs=[row_spec] * 8,
        out_shape=[jax.ShapeDtypeStruct((rows, RWKV_W), F32)] * 8,
        scratch_shapes=[pltpu.VMEM((SUBLANES, ZR_W), F32)],
        compiler_params=_cparams(("parallel", "arbitrary")),
        name="prep_prompt",
    )(z, *params)


def _prep_sample(z, prev, params):
    nrows = z.shape[0]
    out_spec = pl.BlockSpec((nrows, RWKV_W), lambda i: (0, 0))
    return pl.pallas_call(
        _prep_sample_kernel,
        grid=(1,),
        in_specs=[pl.BlockSpec((nrows, ZR_W), lambda i: (0, 0)),
                  pl.BlockSpec((nrows, ZR_W), lambda i: (0, 0))] + _prep_param_specs(params),
        out_specs=[out_spec] * 8,
        out_shape=[jax.ShapeDtypeStruct((nrows, RWKV_W), F32)] * 8,
        compiler_params=_cparams(("arbitrary",)),
        name="prep_sample",
    )(z, prev, *params)


def _split3(x):
    hi = x.astype(BF16)
    rest = x - hi.astype(F32)
    mid = rest.astype(BF16)
    lo = (rest - mid.astype(F32)).astype(BF16)
    return hi, mid, lo


def _select_dot(sel, x):
    sel = sel.astype(BF16)
    hi, mid, lo = _split3(x)
    return _bdot(sel, hi) + (_bdot(sel, mid) + _bdot(sel, lo))


def _wkv_chunk_kernel(r_ref, lw_ref, k_ref, v_ref, a_ref, b_ref, y_ref, sout_ref, st_ref, *, c_len, pp, nc):
    c = pl.program_id(2)
    n2 = 2 * c_len
    assert n2 == LANES

    @pl.when(c == 0)
    def _():
        st_ref[...] = jnp.zeros_like(st_ref)

    row = lax.broadcasted_iota(jnp.int32, (n2, n2), 0)
    col = lax.broadcasted_iota(jnp.int32, (n2, n2), 1)
    tr = row & (c_len - 1)
    tc = col & (c_len - 1)
    strict = tr > tc
    incl = tr >= tc
    eye = row == col
    ones = jnp.ones((LANES, LANES), BF16)
    tri = (lax.broadcasted_iota(jnp.int32, (c_len, c_len), 0)
           >= lax.broadcasted_iota(jnp.int32, (c_len, c_len), 1))
    head_a = lax.broadcasted_iota(jnp.int32, (c_len, LANES), 1) < HEAD

    def stack(x):
        return jnp.concatenate([jnp.where(head_a, x, 0.0), jnp.where(head_a, 0.0, x)], axis=0).astype(BF16)

    prs = range(pp)
    cat = jnp.concatenate
    sls = [slice(q * LANES, (q + 1) * LANES) for q in prs]
    lw = [lw_ref[:, sl] for sl in sls]
    cum = [_select_dot(tri, x) for x in lw]
    tot = [x[c_len - 1:c_len, :] for x in cum]
    xr = [stack(r_ref[:, sls[q]] * jnp.exp(cum[q])) for q in prs]
    xa = [stack(a_ref[:, sls[q]] * jnp.exp(cum[q] - lw[q])) for q in prs]
    e_neg = [jnp.exp(-x) for x in cum]
    e_rem = [jnp.exp(tot[q] - cum[q]) for q in prs]
    yb = [stack(b_ref[:, sls[q]] * e_neg[q]) for q in prs]
    yk = [stack(k_ref[:, sls[q]] * e_neg[q]) for q in prs]
    zb = [stack(b_ref[:, sls[q]] * e_rem[q]) for q in prs]
    zk = [stack(k_ref[:, sls[q]] * e_rem[q]) for q in prs]
    vs = [stack(v_ref[:, sl]) for sl in sls]

    g = [_bdot(cat([xa[q], xr[q]], axis=0), cat([yb[q], yk[q]], axis=0), NT_DIMS) for q in prs]
    m_ab = [jnp.where(strict, x[0:n2, 0:n2], 0.0) for x in g]
    m_ak = [jnp.where(strict, x[0:n2, n2:2 * n2], 0.0).astype(BF16) for x in g]
    n_rb = [jnp.where(incl, x[n2:2 * n2, 0:n2], 0.0) for x in g]
    n_rk = [jnp.where(incl, x[n2:2 * n2, n2:2 * n2], 0.0).astype(BF16) for x in g]

    t_inv = [jnp.where(eye, 1.0, 0.0) + x for x in m_ab]
    mk = [_bdot(x, x) for x in m_ab]
    nlev = int(math.log2(c_len))
    for lev in range(1, nlev):
        if lev < nlev - 1:
            res = [_bdot(cat([mk[q], t_inv[q]], axis=0), mk[q]) for q in prs]
            mk = [x[0:n2] for x in res]
            t_inv = [t_inv[q] + res[q][n2:2 * n2] for q in prs]
        else:
            t_inv = [t_inv[q] + _bdot(t_inv[q], mk[q]) for q in prs]

    st = [st_ref[q] for q in prs]
    lhs = [cat([cat([xa[q], m_ak[q]], axis=1), cat([xr[q], n_rk[q]], axis=1)], axis=0) for q in prs]
    xy0 = [_bdot(lhs[q], cat([st[q].astype(BF16), vs[q]], axis=0)) for q in prs]
    u = [_bdot(t_inv[q], xy0[q][0:n2]).astype(BF16) for q in prs]
    y_st = [xy0[q][n2:2 * n2] + _bdot(n_rb[q], u[q]) for q in prs]
    for q in prs:
        y_ref[:, sls[q]] = y_st[q][0:c_len] + y_st[q][c_len:n2]

    for q in prs:
        dhi, dmid, dlo = _split3(jnp.where(eye, jnp.broadcast_to(jnp.exp(tot[q]), (LANES, LANES)), 0.0))
        dec_col = _bdot(dhi, ones) + (_bdot(dmid, ones) + _bdot(dlo, ones))
        st_ref[q] = dec_col * st[q] + _bdot(cat([zb[q], zk[q]], axis=0), cat([u[q], vs[q]], axis=0), TN_DIMS)

    @pl.when(c == nc - 1)
    def _():
        sout_ref[0] = st_ref[...]


def _wkv_chunk(seqs, nb, t_len, c_len=64, pp=4):
    nc = t_len // c_len
    npair = RWKV_W // LANES
    ng = npair // pp
    blk = pl.BlockSpec((c_len, pp * LANES), lambda b, g, c: (b * nc + c, g))
    return pl.pallas_call(
        functools.partial(_wkv_chunk_kernel, c_len=c_len, pp=pp, nc=nc),
        grid=(nb, ng, nc),
        in_specs=[blk] * 6,
        out_specs=[blk, pl.BlockSpec((1, pp, LANES, LANES), lambda b, g, c: (b, g, 0, 0))],
        out_shape=[jax.ShapeDtypeStruct((nb * t_len, RWKV_W), F32),
                   jax.ShapeDtypeStruct((nb, npair, LANES, LANES), F32)],
        scratch_shapes=[pltpu.VMEM((pp, LANES, LANES), F32)],
        compiler_params=_cparams(("parallel", "parallel", "arbitrary")),
        name="wkv_chunk",
    )(*seqs)


def _wkv_step_kernel(s_ref, r_ref, lw_ref, k_ref, v_ref, a_ref, b_ref, g_ref, bonus_ref, lng_ref, lnb_ref,
                     yb_ref, so_ref, vt_scr, y_scr):
    rt, wt, kt, at, bt = (x[...].T for x in (r_ref, lw_ref, k_ref, a_ref, b_ref))
    wt = jnp.exp(wt)
    vt_scr[...] = v_ref[...].T
    for hh in range(2):
        rows = slice(hh * HEAD, (hh + 1) * HEAD)
        r, w, k, a, b = (x[rows, :] for x in (rt, wt, kt, at, bt))

        def body(i, carry):
            si = s_ref[hh, i]
            sa = jnp.sum(si * a, axis=0, keepdims=True)
            vi = vt_scr[pl.ds(hh * HEAD + i, 1), :]
            s2 = si * w + sa * b + vi * k
            so_ref[hh, i] = s2
            y_scr[pl.ds(hh * HEAD + i, 1), :] = jnp.sum(s2 * r, axis=0, keepdims=True)
            return carry

        lax.fori_loop(0, HEAD, body, 0)

    outs = []
    for hh in range(2):
        y = y_scr[hh * HEAD:(hh + 1) * HEAD, :]
        d = y - jnp.mean(y, axis=0, keepdims=True)
        var = jnp.mean(d * d, axis=0, keepdims=True)
        outs.append(d * lax.rsqrt(var + GN_EPS))
    yn = jnp.concatenate(outs, axis=0).T * lng_ref[...] + lnb_ref[...]
    yb_ref[...] = ((yn + bonus_ref[...]) * g_ref[...]).astype(yb_ref.dtype)


def _wkv_step(state_t, vecs, ln_g, ln_b):
    nh, _, _, nb = state_t.shape
    st = pl.BlockSpec((2, HEAD, HEAD, nb), lambda p: (p, 0, 0, 0))
    vec = pl.BlockSpec((nb, 2 * HEAD), lambda p: (0, p))
    par = pl.BlockSpec((1, 2 * HEAD), lambda p: (0, p))
    return pl.pallas_call(
        _wkv_step_kernel,
        grid=(nh // 2,),
        in_specs=[st] + [vec] * 8 + [par, par],
        out_specs=[vec, st],
        out_shape=[jax.ShapeDtypeStruct((nb, nh * HEAD), BF16), jax.ShapeDtypeStruct(state_t.shape, F32)],
        scratch_shapes=[pltpu.VMEM((2 * HEAD, nb), F32), pltpu.VMEM((2 * HEAD, nb), F32)],
        compiler_params=_cparams(("parallel",)),
        name="wkv_step",
    )(state_t, *vecs, ln_g, ln_b)


def _rwkv_post_kernel(y_ref, bonus_ref, g_ref, lng_ref, lnb_ref, o_ref):
    bd = _pair_block_ones() * (1.0 / HEAD)
    y = y_ref[...]
    d = y - _head_sum(y, bd)
    var = _head_sum(d * d, bd)
    yn = d * lax.rsqrt(var + GN_EPS) * lng_ref[...] + lnb_ref[...]
    o_ref[...] = ((yn + bonus_ref[...]) * g_ref[...]).astype(o_ref.dtype)


def _rwkv_post(y, bonus, g, ln_g, ln_b, tm):
    m = y.shape[0]
    row = pl.BlockSpec((tm, RWKV_W), lambda i: (i, 0))
    par = pl.BlockSpec((1, RWKV_W), lambda i: (0, 0))
    return pl.pallas_call(
        _rwkv_post_kernel,
        grid=(m // tm,),
        in_specs=[row, row, row, par, par],
        out_specs=row,
        out_shape=jax.ShapeDtypeStruct((m, RWKV_W), BF16),
        compiler_params=_cparams(("parallel",)),
        name="rwkv_post",
    )(y, bonus, g, ln_g, ln_b)


def _xattn_prompt_kernel(q_ref, k_ref, v_ref, o_ref):
    scale = XA_DIM ** -0.5
    q = q_ref[...]
    for h in range(XA_HEADS):
        sl = slice(h * XA_DIM, (h + 1) * XA_DIM)
        s = lax.dot_general(q[:, sl].astype(BF16), k_ref[0, :, sl].astype(BF16),
                            NT_DIMS, preferred_element_type=F32) * scale
        p = jnp.exp(s - jnp.max(s, axis=-1, keepdims=True))
        den = jnp.sum(p, axis=-1, keepdims=True)
        o = jnp.dot(p.astype(BF16), v_ref[0, :, sl].astype(BF16), preferred_element_type=F32)
        o_ref[:, sl] = (o / den).astype(o_ref.dtype)


def _xattn_prompt(z, nb, t_len, col_blk, mk, mv, tq=512):
    nt = t_len // tq
    xw = XA_HEADS * XA_DIM
    nmem = mk.shape[1]
    kv = pl.BlockSpec((1, nmem, xw), lambda b, t: (b, 0, 0))
    return pl.pallas_call(
        _xattn_prompt_kernel,
        grid=(nb, nt),
        in_specs=[pl.BlockSpec((tq, xw), lambda b, t: (b * nt + t, col_blk)), kv, kv],
        out_specs=pl.BlockSpec((tq, xw), lambda b, t: (b * nt + t, 0)),
        out_shape=jax.ShapeDtypeStruct((nb * t_len, xw), BF16),
        compiler_params=_cparams(("parallel", "parallel")),
        name="xattn_prompt",
    )(z, mk, mv)


def _xattn_sample_kernel(q_ref, k_ref, v_ref, o_ref):
    bb = q_ref.shape[0]
    nrow = k_ref.shape[1] // SUBLANES
    full = (bb, nrow, SUBLANES, XA_DIM)
    q = q_ref[...] * (XA_DIM ** -0.5)
    q8 = jnp.concatenate([q, q], axis=1)[:, None]
    k = k_ref[...].reshape(full)
    s = jnp.broadcast_to(jnp.sum(k * q8, axis=-1, keepdims=True), full)
    mx = jnp.max(s, axis=1, keepdims=True)
    mx = jnp.maximum(mx, pltpu.roll(mx, XA_HEADS, axis=2))
    p = jnp.exp(s - mx)
    den = jnp.sum(p, axis=1, keepdims=True)
    den = den + pltpu.roll(den, XA_HEADS, axis=2)
    o = jnp.sum(p * v_ref[...].reshape(full), axis=1, keepdims=True)
    o = o + pltpu.roll(o, XA_HEADS, axis=2)
    o_ref[...] = (o / den)[:, 0, 0:XA_HEADS, :].astype(o_ref.dtype)


def _xattn_sample(q3, mk, mv, bb=8):
    nb, rows, _ = mk.shape
    kv = pl.BlockSpec((bb, rows, XA_DIM), lambda i: (i, 0, 0))
    qs = pl.BlockSpec((bb, XA_HEADS, XA_DIM), lambda i: (i, 0, 0))
    return pl.pallas_call(
        _xattn_sample_kernel,
        grid=(nb // bb,),
        in_specs=[qs, kv, kv],
        out_specs=qs,
        out_shape=jax.ShapeDtypeStruct((nb, XA_HEADS, XA_DIM), BF16),
        compiler_params=_cparams(("parallel",)),
        name="xattn_sample",
    )(q3, mk, mv)


def _merge_kernel(pa_ref, pb_ref, pc_ref, g0_ref, g1_ref, g2_ref, h_ref, wa_ref, wb_ref, wc_ref, wo_ref, o_ref):
    oa = jnp.dot(pa_ref[...], wa_ref[...], preferred_element_type=F32)
    ob = jnp.dot(pb_ref[...], wb_ref[...], preferred_element_type=F32)
    oc = jnp.dot(pc_ref[...], wc_ref[...], preferred_element_type=F32)
    merged = (jax.nn.sigmoid(g0_ref[...]) * oa + jax.nn.sigmoid(g1_ref[...]) * ob
              + jax.nn.sigmoid(g2_ref[...]) * oc)
    o_ref[...] = h_ref[...] + jnp.dot(merged.astype(BF16), wo_ref[...], preferred_element_type=F32)


def _merge(pa, pb, pc, z, zg_blk0, h, tm, wa, wb, wc, wo):
    m, d = h.shape
    const = lambda i: (0, 0)
    resident = lambda w: pl.BlockSpec(w.shape, const, pipeline_mode=pl.Buffered(1))
    in_specs = [
        pl.BlockSpec((tm, pa.shape[1]), lambda i: (i, 0)),
        pl.BlockSpec((tm, pb.shape[1]), lambda i: (i, 0)),
        pl.BlockSpec((tm, pc.shape[1]), lambda i: (i, 0)),
        pl.BlockSpec((tm, d), lambda i: (i, zg_blk0)),
        pl.BlockSpec((tm, d), lambda i: (i, zg_blk0 + 1)),
        pl.BlockSpec((tm, d), lambda i: (i, zg_blk0 + 2)),
        pl.BlockSpec((tm, d), lambda i: (i, 0)),
        resident(wa), resident(wb), resident(wc), resident(wo),
    ]
    return pl.pallas_call(
        _merge_kernel,
        grid=(m // tm,),
        in_specs=in_specs,
        out_specs=pl.BlockSpec((tm, d), lambda i: (i, 0)),
        out_shape=jax.ShapeDtypeStruct((m, d), F32),
        compiler_params=_cparams(("parallel",)),
        name="merge",
    )(pa, pb, pc, z, z, z, h, wa, wb, wc, wo)


def _pack_zr(x, axis):
    w = RWKV_W
    take = lambda a, b: lax.slice_in_dim(x, a, b, axis=axis)

    def pad(n):
        shape = list(x.shape)
        shape[axis] = n
        return jnp.zeros(shape, x.dtype)

    return jnp.concatenate([take(0, 3 * w + 64), pad(64), take(3 * w + 64, 3 * w + 128), pad(64),
                            take(3 * w + 128, ZR_TRUE), pad(96)], axis=axis)


def _unpack_zr_cols(x):
    w = RWKV_W
    return jnp.concatenate([x[..., :3 * w + 64], x[..., 3 * w + 128:3 * w + 192], x[..., 3 * w + 256:3 * w + 416]],
                           axis=-1)


def _pad_rows(x, n):
    return jnp.concatenate([x, jnp.zeros((n - x.shape[0],) + x.shape[1:], x.dtype)], axis=0)


def kernel(x_prompt, x_sample, mem_prompt, cache_mem_k, cache_mem_v, state_wkv, state_shift, state_pool,
           ffn1_norm_g, ffn1_w_gate, ffn1_w_up, ffn1_w_down, mix_norm_g, w_in,
           pool_group_w, pool_scale, pool_out,
           rwkv_mu, rwkv_w0, rwkv_w_up, rwkv_a0, rwkv_a_up, rwkv_g_up, rwkv_k_k, rwkv_k_a, rwkv_r_k,
           rwkv_ln_g, rwkv_ln_b, rwkv_out,
           mem_norm_g, w_mem_k, w_mem_v, xattn_out, w_o,
           ffn2_norm_g, ffn2_w_gate, ffn2_w_up, ffn2_w_down, final_norm_g):
    nb, t_len, d = x_prompt.shape
    ns = x_sample.shape[0]
    assert w_in.shape[0] == 1 and x_sample.shape[1] == 1
    n_mem = mem_prompt.shape[1]
    pool_w = pool_out.shape[1]
    xa_w = xattn_out.shape[1]
    n_heads = RWKV_W // HEAD
    nbuf = state_pool.shape[2]
    rows_p = nb * t_len
    l = 0

    d_ff = ffn1_w_gate.shape[2]
    f1 = (ffn1_w_gate.reshape(d, d_ff), ffn1_w_up.reshape(d, d_ff), ffn1_w_down.reshape(d_ff, d))
    f2 = (ffn2_w_gate.reshape(d, d_ff), ffn2_w_up.reshape(d, d_ff), ffn2_w_down.reshape(d_ff, d))
    wit = jnp.swapaxes(w_in[l], 0, 1)
    o_zr, o_zq, o_zg = pool_w, pool_w + ZR_TRUE, pool_w + ZR_TRUE + xa_w
    w_in_t = jnp.concatenate([_pack_zr(wit[o_zr:o_zq], 0), wit[:o_zr], wit[o_zg:], wit[o_zq:o_zg]],
                             axis=0).astype(BF16)
    col_zp = ZR_W // pool_w
    col_zg = (ZR_W + pool_w) // d
    col_zq = (ZR_W + pool_w + 3 * d) // xa_w
    row = lambda v: v.reshape(1, -1)
    prep_params = [row(_pack_zr(rwkv_mu[l], 0)), row(rwkv_w0[l]), _pad_rows(rwkv_w_up[l], 128).astype(BF16),
                   row(rwkv_a0[l]), _pad_rows(rwkv_a_up[l], 128).astype(BF16),
                   _pad_rows(rwkv_g_up[l], 256).astype(BF16), row(rwkv_k_k[l]), row(rwkv_k_a[l]),
                   row(rwkv_r_k[l])]
    gw = pool_group_w[l].astype(BF16)
    w_kv_t = jnp.concatenate([jnp.swapaxes(w_mem_k[l], 0, 1), jnp.swapaxes(w_mem_v[l], 0, 1)], axis=0).astype(BF16)
    wa, wb, wc, wo = (pool_out[l].astype(BF16), rwkv_out[l].astype(BF16), xattn_out[l].astype(BF16),
                      w_o[l].astype(BF16))
    g1, gm, g2, fg = row(ffn1_norm_g[l]), row(mix_norm_g[l]), row(ffn2_norm_g[l]), row(final_norm_g)
    scale = row(pool_scale[l])
    ln_g, ln_b = rwkv_ln_g[l], rwkv_ln_b[l]

    tm_p = 1024
    h1_p, h1_s = _ffn(x_prompt.reshape(rows_p, d), x_sample.reshape(ns, d), tm_p, g1, *f1)
    z_p = _norm_matmul(h1_p, gm, w_in_t, tm=1024, tn=512)
    z_s = _norm_matmul(h1_s, gm, w_in_t, tm=ns, tn=512)
    kv = _norm_matmul(mem_prompt.reshape(nb * n_mem, d), row(mem_norm_g[l]), w_kv_t, tm=512, tn=512)
    mk_p = kv[:, :xa_w].reshape(nb, n_mem, xa_w)
    mv_p = kv[:, xa_w:].reshape(nb, n_mem, xa_w)

    pool_state = state_pool.reshape(ns, nbuf, pool_w)
    pa_p = _pool_prompt(z_p, nb, t_len, col_zp, gw, scale)
    pa_s = _pool_sample(z_s, col_zp, jnp.swapaxes(pool_state, 0, 1), gw, scale)

    prep_p = _prep_prompt(z_p, nb, t_len, prep_params)
    y_p, st_p = _wkv_chunk(prep_p[:6], nb, t_len)
    pb_p = _rwkv_post(y_p, prep_p[7], prep_p[6], row(ln_g), row(ln_b), tm=512)
    prep_s = _prep_sample(z_s, _pack_zr(state_shift.reshape(ns, ZR_TRUE), 1), prep_params)
    state_t = jnp.transpose(state_wkv.reshape(ns, n_heads, HEAD, HEAD), (1, 2, 3, 0))
    pb_s, wkv_s_t = _wkv_step(state_t, prep_s, row(ln_g), row(ln_b))

    pc_p = _xattn_prompt(z_p, nb, t_len, col_zq, mk_p, mv_p)
    q_s = z_s[:, ZR_W + pool_w + 3 * d:].reshape(ns, XA_HEADS, XA_DIM)
    pc_s = _xattn_sample(q_s, cache_mem_k.reshape(ns, n_mem * XA_HEADS, XA_DIM),
                         cache_mem_v.reshape(ns, n_mem * XA_HEADS, XA_DIM)).reshape(ns, xa_w)

    h2_p = _merge(pa_p, pb_p, pc_p, z_p, col_zg, h1_p, 256, wa, wb, wc, wo)
    h2_s = _merge(pa_s, pb_s, pc_s, z_s, col_zg, h1_s, ns, wa, wb, wc, wo)
    y_prompt, y_sample = _ffn(h2_p, h2_s, tm_p, g2, *f2, final_g=fg)

    ends = [(b + 1) * t_len for b in range(nb)]
    shift_p = _unpack_zr_cols(jnp.stack([z_p[e - 1:e, :ZR_W] for e in ends]))[None]
    pool_p = jnp.stack([z_p[e - nbuf:e, ZR_W:ZR_W + pool_w] for e in ends])[None]
    shift_s = _unpack_zr_cols(z_s[:, :ZR_W])[None, :, None, :]
    pool_s = jnp.concatenate([pool_state[:, 1:], z_s[:, None, ZR_W:ZR_W + pool_w]], axis=1)[None]
    st5 = st_p.reshape(nb, n_heads // 2, 2, HEAD, 2, HEAD)
    wkv_p = jnp.stack([st5[:, :, 0, :, 0, :], st5[:, :, 1, :, 1, :]], axis=2).reshape(nb, n_heads, HEAD, HEAD)
    wkv_p = jnp.swapaxes(wkv_p, -1, -2)[None]
    wkv_s = jnp.transpose(wkv_s_t, (3, 0, 1, 2)).reshape(state_wkv.shape)
    mem_k_p = mk_p.reshape(1, nb, n_mem, XA_HEADS, XA_DIM)
    mem_v_p = mv_p.reshape(1, nb, n_mem, XA_HEADS, XA_DIM)
    return (y_prompt.reshape(nb, t_len, d), y_sample.reshape(ns, 1, d), mem_k_p, mem_v_p, wkv_p, shift_p, pool_p,
            wkv_s, shift_s, pool_s)
```

```python
import functools
import math

import jax
import jax.numpy as jnp
from jax import lax
from jax.experimental import pallas as pl
from jax.experimental.pallas import tpu as pltpu

F32 = jnp.float32
BF16 = jnp.bfloat16
HI = lax.Precision.HIGHEST

RMS_EPS = 1e-6
GN_EPS = 64e-5
POOL_WINDOWS = (2, 4, 8, 16)
HEAD = 64
LANES = 128
SUBLANES = 8
XA_HEADS = 4
XA_DIM = 128
PAST_LEN = 16384
VMEM_LIMIT = 56 * 1024 * 1024
EXP_M05 = math.exp(-0.5)

ZR_W = 3584
RWKV_W = 1024
ZR_TRUE = 3360

NN_DIMS = (((1,), (0,)), ((), ()))
NT_DIMS = (((1,), (1,)), ((), ()))
TN_DIMS = (((0,), (0,)), ((), ()))


def _cparams(sem):
    return pltpu.CompilerParams(dimension_semantics=sem, vmem_limit_bytes=VMEM_LIMIT)


def _rms(x, g):
    ms = jnp.mean(x * x, axis=-1, keepdims=True)
    return x * lax.rsqrt(ms + RMS_EPS) * g


def _bdot(a, b, dims=NN_DIMS):
    return lax.dot_general(a.astype(BF16), b.astype(BF16), dims, preferred_element_type=F32)


def _ffn_kernel(hp_ref, hs_ref, g_ref, fg_ref, wg_ref, wu_ref, wd_ref, wgt_ref, wut_ref, wdt_ref,
                op_ref, os_ref, xp_ref, xs_ref, *, nfull, final):
    m = pl.program_id(0)
    f = pl.program_id(1)

    def start(h_ref, x_ref, o_ref):
        x_ref[...] = _rms(h_ref[...], g_ref[...]).astype(BF16)
        o_ref[...] = jnp.zeros_like(o_ref)

    def contribution(x_ref, wg, wu, wd):
        xn = x_ref[...]
        gate = jnp.dot(xn, wg, preferred_element_type=F32)
        up = jnp.dot(xn, wu, preferred_element_type=F32)
        act = (gate * jax.nn.sigmoid(gate) * up).astype(BF16)
        return jnp.dot(act, wd, preferred_element_type=F32)

    def finish(h_ref, o_ref, last):
        out = h_ref[...] + 0.5 * (o_ref[...] + last)
        if final:
            out = _rms(out, fg_ref[...])
        o_ref[...] = out

    @pl.when(f == 0)
    def _():
        start(hp_ref, xp_ref, op_ref)

    @pl.when((f == 0) & (m == 0))
    def _():
        start(hs_ref, xs_ref, os_ref)

    @pl.when(f < nfull)
    def _():
        wg, wu, wd = wg_ref[...], wu_ref[...], wd_ref[...]
        op_ref[...] += contribution(xp_ref, wg, wu, wd)

        @pl.when(m == 0)
        def _():
            os_ref[...] += contribution(xs_ref, wg, wu, wd)

    @pl.when(f == nfull)
    def _():
        wg, wu, wd = wgt_ref[...], wut_ref[...], wdt_ref[...]
        finish(hp_ref, op_ref, contribution(xp_ref, wg, wu, wd))

        @pl.when(m == 0)
        def _():
            finish(hs_ref, os_ref, contribution(xs_ref, wg, wu, wd))


def _ffn(hp, hs, tm, g, wg, wu, wd, final_g=None, tf=256):
    mp, d = hp.shape
    ms = hs.shape[0]
    d_ff = wg.shape[1]
    nfull, tail = divmod(d_ff, tf)
    assert tail > 0 and tail % LANES == 0 and tf % tail == 0 and mp % tm == 0
    last_main = nfull - 1
    tail_blk = d_ff // tail - 1
    final = final_g is not None
    vec = pl.BlockSpec((1, d), lambda i, f: (0, 0))
    once = pl.Buffered(1)
    in_specs = [
        pl.BlockSpec((tm, d), lambda i, f: (i, 0), pipeline_mode=once),
        pl.BlockSpec((ms, d), lambda i, f: (0, 0), pipeline_mode=once),
        vec, vec,
        pl.BlockSpec((d, tf), lambda i, f: (0, jnp.minimum(f, last_main))),
        pl.BlockSpec((d, tf), lambda i, f: (0, jnp.minimum(f, last_main))),
        pl.BlockSpec((tf, d), lambda i, f: (jnp.minimum(f, last_main), 0)),
        pl.BlockSpec((d, tail), lambda i, f: (0, tail_blk), pipeline_mode=once),
        pl.BlockSpec((d, tail), lambda i, f: (0, tail_blk), pipeline_mode=once),
        pl.BlockSpec((tail, d), lambda i, f: (tail_blk, 0), pipeline_mode=once),
    ]
    return pl.pallas_call(
        functools.partial(_ffn_kernel, nfull=nfull, final=final),
        grid=(mp // tm, nfull + 1),
        in_specs=in_specs,
        out_specs=[pl.BlockSpec((tm, d), lambda i, f: (i, 0)), pl.BlockSpec((ms, d), lambda i, f: (0, 0))],
        out_shape=[jax.ShapeDtypeStruct((mp, d), F32), jax.ShapeDtypeStruct((ms, d), F32)],
        scratch_shapes=[pltpu.VMEM((tm, d), BF16), pltpu.VMEM((ms, d), BF16)],
        compiler_params=_cparams(("arbitrary", "arbitrary")),
        name="ffn",
    )(hp, hs, g, g if final_g is None else final_g, wg, wu, wd, wg, wu, wd)


def _norm_matmul_kernel(h_ref, g_ref, wt_ref, o_ref, xn_ref):
    @pl.when(pl.program_id(1) == 0)
    def _():
        xn_ref[...] = _rms(h_ref[...], g_ref[...]).astype(BF16)

    o_ref[...] = lax.dot_general(xn_ref[...], wt_ref[...], NT_DIMS, preferred_element_type=F32)


def _norm_matmul(h, g, wt, tm, tn):
    m, d = h.shape
    n = wt.shape[0]
    return pl.pallas_call(
        _norm_matmul_kernel,
        grid=(m // tm, n // tn),
        in_specs=[
            pl.BlockSpec((tm, d), lambda i, j: (i, 0)),
            pl.BlockSpec((1, d), lambda i, j: (0, 0)),
            pl.BlockSpec((tn, d), lambda i, j: (j, 0)),
        ],
        out_specs=pl.BlockSpec((tm, tn), lambda i, j: (i, j)),
        out_shape=jax.ShapeDtypeStruct((m, n), F32),
        scratch_shapes=[pltpu.VMEM((tm, d), BF16)],
        compiler_params=_cparams(("parallel", "arbitrary")),
        name="norm_matmul",
    )(h, g, wt)


def _pool_mix(pooled_groups, gw_ref, scale_ref, o_ref):
    for gi, pooled in enumerate(pooled_groups):
        sl = slice(gi * LANES, (gi + 1) * LANES)
        mixed = jnp.dot(pooled.astype(BF16), gw_ref[gi], preferred_element_type=F32)
        o_ref[:, sl] = (mixed * scale_ref[:, sl]).astype(o_ref.dtype)


def _pool_prompt_kernel(zp_ref, gw_ref, scale_ref, o_ref, ext_ref, *, tt):
    t = pl.program_id(1)
    hist = 16

    @pl.when(t == 0)
    def _():
        ext_ref[0:hist, :] = jnp.zeros((hist, ext_ref.shape[1]), F32)

    x = zp_ref[...]
    ext_ref[hist:hist + tt, :] = x
    pos = t * tt + lax.broadcasted_iota(jnp.int32, (tt, LANES), 0)
    groups = []
    for gi, w in enumerate(POOL_WINDOWS):
        sl = slice(gi * LANES, (gi + 1) * LANES)
        acc = x[:, sl]
        for k in range(1, w):
            acc = acc + ext_ref[hist - k:hist - k + tt, sl]
        cnt = jnp.minimum(pos + 1, w).astype(F32)
        groups.append(acc / cnt - x[:, sl])
    _pool_mix(groups, gw_ref, scale_ref, o_ref)
    ext_ref[0:hist, :] = ext_ref[tt:tt + hist, :]


def _pool_prompt(z, nb, t_len, col_blk, gw, scale, tt=256):
    nt = t_len // tt
    pw = gw.shape[0] * LANES
    return pl.pallas_call(
        functools.partial(_pool_prompt_kernel, tt=tt),
        grid=(nb, nt),
        in_specs=[
            pl.BlockSpec((tt, pw), lambda b, t: (b * nt + t, col_blk)),
            pl.BlockSpec(gw.shape, lambda b, t: (0, 0, 0)),
            pl.BlockSpec((1, pw), lambda b, t: (0, 0)),
        ],
        out_specs=pl.BlockSpec((tt, pw), lambda b, t: (b * nt + t, 0)),
        out_shape=jax.ShapeDtypeStruct((nb * t_len, pw), BF16),
        scratch_shapes=[pltpu.VMEM((tt + 16, pw), F32)],
        compiler_params=_cparams(("parallel", "arbitrary")),
        name="pool_prompt",
    )(z, gw, scale)


def _pool_sample_kernel(zp_ref, buf_ref, gw_ref, scale_ref, o_ref):
    x = zp_ref[...]
    nbuf = buf_ref.shape[0]
    groups = []
    for gi, w in enumerate(POOL_WINDOWS):
        sl = slice(gi * LANES, (gi + 1) * LANES)
        acc = x[:, sl]
        for k in range(1, w):
            acc = acc + buf_ref[nbuf - k, :, sl]
        cnt = float(min(PAST_LEN + 1, w))
        groups.append(acc / cnt - x[:, sl])
    _pool_mix(groups, gw_ref, scale_ref, o_ref)


def _pool_sample(z, col_blk, buf_t, gw, scale):
    nrows = z.shape[0]
    pw = gw.shape[0] * LANES
    return pl.pallas_call(
        _pool_sample_kernel,
        grid=(1,),
        in_specs=[
            pl.BlockSpec((nrows, pw), lambda i: (0, col_blk)),
            pl.BlockSpec(buf_t.shape, lambda i: (0, 0, 0)),
            pl.BlockSpec(gw.shape, lambda i: (0, 0, 0)),
            pl.BlockSpec((1, pw), lambda i: (0, 0)),
        ],
        out_specs=pl.BlockSpec((nrows, pw), lambda i: (0, 0)),
        out_shape=jax.ShapeDtypeStruct((nrows, pw), BF16),
        compiler_params=_cparams(("arbitrary",)),
        name="pool_sample",
    )(z, buf_t, gw, scale)


def _pair_block_ones():
    r = lax.broadcasted_iota(jnp.int32, (LANES, LANES), 0) // HEAD
    c = lax.broadcasted_iota(jnp.int32, (LANES, LANES), 1) // HEAD
    return (r == c).astype(F32)


def _head_sum(x, bd):
    cols = [jnp.dot(x[:, c * LANES:(c + 1) * LANES], bd, precision=HI, preferred_element_type=F32)
            for c in range(x.shape[1] // LANES)]
    return jnp.concatenate(cols, axis=1)


def _prep_math(x, prev, p_refs):
    mu_ref, w0_ref, wup_ref, a0_ref, aup_ref, gup_ref, kk_ref, ka_ref, rk_ref = p_refs
    xm = x + (prev - x) * mu_ref[...]
    w = RWKV_W
    r = xm[:, 0:w]
    k = xm[:, w:2 * w]
    v = xm[:, 2 * w:3 * w]
    wl = xm[:, 3 * w:3 * w + 128]
    al = xm[:, 3 * w + 128:3 * w + 256]
    gl = xm[:, 3 * w + 256:3 * w + 512]
    bd = _pair_block_ones()
    dw = w0_ref[...] + jnp.dot(jnp.tanh(wl).astype(BF16), wup_ref[...], preferred_element_type=F32)
    lw = -EXP_M05 * jax.nn.sigmoid(dw)
    a = jax.nn.sigmoid(a0_ref[...] + jnp.dot(al.astype(BF16), aup_ref[...], preferred_element_type=F32))
    g = jnp.dot(jax.nn.sigmoid(gl).astype(BF16), gup_ref[...], preferred_element_type=F32)
    kk = k * kk_ref[...]
    kk = kk * lax.rsqrt(jnp.maximum(_head_sum(kk * kk, bd), 1e-24))
    kmod = k * (1.0 + (a - 1.0) * ka_ref[...])
    bonus = _head_sum(r * kmod * rk_ref[...], bd) * v
    return r, lw, kmod, v, -kk, kk * a, g, bonus


def _prep_prompt_kernel(zr_ref, *refs):
    p_refs, out_refs, carry_ref = refs[:9], refs[9:17], refs[17]
    t = pl.program_id(1)

    @pl.when(t == 0)
    def _():
        carry_ref[...] = jnp.zeros_like(carry_ref)

    x = zr_ref[...]
    rolled = pltpu.roll(x, 1, axis=0)
    first = lax.broadcasted_iota(jnp.int32, x.shape, 0) == 0
    prev = jnp.where(first, carry_ref[0:1, :], rolled)
    carry_ref[...] = rolled[0:SUBLANES, :]
    for o_ref, val in zip(out_refs, _prep_math(x, prev, p_refs)):
        o_ref[...] = val


def _prep_sample_kernel(zr_ref, prev_ref, *refs):
    p_refs, out_refs = refs[:9], refs[9:17]
    for o_ref, val in zip(out_refs, _prep_math(zr_ref[...], prev_ref[...], p_refs)):
        o_ref[...] = val


def _prep_param_specs(params):
    zero = (lambda *idx: (0, 0))
    return [pl.BlockSpec(p.shape, zero) for p in params]


def _prep_prompt(z, nb, t_len, params, tt=256):
    nt = t_len // tt
    rows = nb * t_len
    row_spec = pl.BlockSpec((tt, RWKV_W), lambda b, t: (b * nt + t, 0))
    return pl.pallas_call(
        _prep_prompt_kernel,
        grid=(nb, nt),
        in_specs=[pl.BlockSpec((tt, ZR_W), lambda b, t: (b * nt + t, 0))] + _prep_param_specs(params),
        out_specs=[row_spec] * 8,
        out_shape=[jax.ShapeDtypeStruct((rows, RWKV_W), F32)] * 8,
        scratch_shapes=[pltpu.VMEM((SUBLANES, ZR_W), F32)],
        compiler_params=_cparams(("parallel", "arbitrary")),
        name="prep_prompt",
    )(z, *params)


def _prep_sample(z, prev, params):
    nrows = z.shape[0]
    out_spec = pl.BlockSpec((nrows, RWKV_W), lambda i: (0, 0))
    return pl.pallas_call(
        _prep_sample_kernel,
        grid=(1,),
        in_specs=[pl.BlockSpec((nrows, ZR_W), lambda i: (0, 0)),
                  pl.BlockSpec((nrows, ZR_W), lambda i: (0, 0))] + _prep_param_specs(params),
        out_specs=[out_spec] * 8,
        out_shape=[jax.ShapeDtypeStruct((nrows, RWKV_W), F32)] * 8,
        compiler_params=_cparams(("arbitrary",)),
        name="prep_sample",
    )(z, prev, *params)


def _split3(x):
    hi = x.astype(BF16)
    rest = x - hi.astype(F32)
    mid = rest.astype(BF16)
    lo = (rest - mid.astype(F32)).astype(BF16)
    return hi, mid, lo


def _select_dot(sel, x):
    sel = sel.astype(BF16)
    hi, mid, lo = _split3(x)
    return _bdot(sel, hi) + (_bdot(sel, mid) + _bdot(sel, lo))


def _wkv_chunk_kernel(r_ref, lw_ref, k_ref, v_ref, a_ref, b_ref, y_ref, sout_ref, s_ref, *, c_len, pp, nc):
    c = pl.program_id(2)
    n2 = 2 * c_len
    assert n2 == LANES

    @pl.when(c == 0)
    def _():
        s_ref[...] = jnp.zeros_like(s_ref)

    row = lax.broadcasted_iota(jnp.int32, (n2, n2), 0)
    col = lax.broadcasted_iota(jnp.int32, (n2, n2), 1)
    tr = row & (c_len - 1)
    tc = col & (c_len - 1)
    strict = tr > tc
    incl = tr >= tc
    tri = (lax.broadcasted_iota(jnp.int32, (c_len, c_len), 0)
           >= lax.broadcasted_iota(jnp.int32, (c_len, c_len), 1))
    head_a = lax.broadcasted_iota(jnp.int32, (c_len, LANES), 1) < HEAD

    def stack(x):
        return jnp.concatenate([jnp.where(head_a, x, 0.0), jnp.where(head_a, 0.0, x)], axis=0).astype(BF16)

    prs = range(pp)
    cat = jnp.concatenate
    sls = [slice(q * LANES, (q + 1) * LANES) for q in prs]
    lw = [lw_ref[:, sl] for sl in sls]
    cum = [_select_dot(tri, x) for x in lw]
    tot = [x[c_len - 1:c_len, :] for x in cum]
    xr = [stack(r_ref[:, sls[q]] * jnp.exp(cum[q])) for q in prs]
    xa = [stack(a_ref[:, sls[q]] * jnp.exp(cum[q] - lw[q])) for q in prs]
    e_neg = [jnp.exp(-x) for x in cum]
    e_rem = [jnp.exp(tot[q] - cum[q]) for q in prs]
    yb = [stack(b_ref[:, sls[q]] * e_neg[q]) for q in prs]
    yk = [stack(k_ref[:, sls[q]] * e_neg[q]) for q in prs]
    zb = [stack(b_ref[:, sls[q]] * e_rem[q]) for q in prs]
    zk = [stack(k_ref[:, sls[q]] * e_rem[q]) for q in prs]
    vs = [stack(v_ref[:, sl]) for sl in sls]

    g = [_bdot(cat([xa[q], xr[q]], axis=0), cat([yb[q], yk[q]], axis=0), NT_DIMS) for q in prs]
    m_ab = [jnp.where(strict, x[0:n2, 0:n2], 0.0) for x in g]
    m_ak = [jnp.where(strict, x[0:n2, n2:2 * n2], 0.0).astype(BF16) for x in g]
    n_rb = [jnp.where(incl, x[n2:2 * n2, 0:n2], 0.0) for x in g]
    n_rk = [jnp.where(incl, x[n2:2 * n2, n2:2 * n2], 0.0).astype(BF16) for x in g]

    s_old = [s_ref[q] for q in prs]
    lhs = [cat([cat([xa[q], m_ak[q]], axis=1), cat([xr[q], n_rk[q]], axis=1)], axis=0) for q in prs]
    xy0 = [_bdot(lhs[q], cat([s_old[q].T.astype(BF16), vs[q]], axis=0)) for q in prs]

    x = [xy0[q][0:n2] for q in prs]
    mk = m_ab
    nlev = int(math.log2(c_len))
    for lev in range(nlev):
        if lev < nlev - 1:
            res = [_bdot(mk[q], cat([mk[q], x[q]], axis=1)) for q in prs]
            mk = [r[:, 0:n2] for r in res]
            x = [x[q] + res[q][:, n2:2 * n2] for q in prs]
        else:
            x = [x[q] + _bdot(mk[q], x[q]) for q in prs]
    u = [v.astype(BF16) for v in x]

    y_st = [xy0[q][n2:2 * n2] + _bdot(n_rb[q], u[q]) for q in prs]
    for q in prs:
        y_ref[:, sls[q]] = y_st[q][0:c_len] + y_st[q][c_len:n2]
    for q in prs:
        s_ref[q] = s_old[q] * jnp.exp(tot[q]) + _bdot(cat([u[q], vs[q]], axis=0), cat([zb[q], zk[q]], axis=0),
                                                     TN_DIMS)

    @pl.when(c == nc - 1)
    def _():
        sout_ref[0] = s_ref[...]


def _wkv_chunk(seqs, nb, t_len, c_len=64, pp=8):
    nc = t_len // c_len
    npair = RWKV_W // LANES
    ng = npair // pp
    blk = pl.BlockSpec((c_len, pp * LANES), lambda b, g, c: (b * nc + c, g))
    return pl.pallas_call(
        functools.partial(_wkv_chunk_kernel, c_len=c_len, pp=pp, nc=nc),
        grid=(nb, ng, nc),
        in_specs=[blk] * 6,
        out_specs=[blk, pl.BlockSpec((1, pp, LANES, LANES), lambda b, g, c: (b, g, 0, 0))],
        out_shape=[jax.ShapeDtypeStruct((nb * t_len, RWKV_W), F32),
                   jax.ShapeDtypeStruct((nb, npair, LANES, LANES), F32)],
        scratch_shapes=[pltpu.VMEM((pp, LANES, LANES), F32)],
        compiler_params=_cparams(("parallel", "parallel", "arbitrary")),
        name="wkv_chunk",
    )(*seqs)


def _wkv_step_kernel(s_ref, r_ref, lw_ref, k_ref, v_ref, a_ref, b_ref, g_ref, bonus_ref, lng_ref, lnb_ref,
                     yb_ref, so_ref, vt_scr, y_scr):
    rt, wt, kt, at, bt = (x[...].T for x in (r_ref, lw_ref, k_ref, a_ref, b_ref))
    wt = jnp.exp(wt)
    vt_scr[...] = v_ref[...].T
    for hh in range(2):
        rows = slice(hh * HEAD, (hh + 1) * HEAD)
        r, w, k, a, b = (x[rows, :] for x in (rt, wt, kt, at, bt))

        def body(i, carry):
            si = s_ref[hh, i]
            sa = jnp.sum(si * a, axis=0, keepdims=True)
            vi = vt_scr[pl.ds(hh * HEAD + i, 1), :]
            s2 = si * w + sa * b + vi * k
            so_ref[hh, i] = s2
            y_scr[pl.ds(hh * HEAD + i, 1), :] = jnp.sum(s2 * r, axis=0, keepdims=True)
            return carry

        lax.fori_loop(0, HEAD, body, 0)

    outs = []
    for hh in range(2):
        y = y_scr[hh * HEAD:(hh + 1) * HEAD, :]
        d = y - jnp.mean(y, axis=0, keepdims=True)
        var = jnp.mean(d * d, axis=0, keepdims=True)
        outs.append(d * lax.rsqrt(var + GN_EPS))
    yn = jnp.concatenate(outs, axis=0).T * lng_ref[...] + lnb_ref[...]
    yb_ref[...] = ((yn + bonus_ref[...]) * g_ref[...]).astype(yb_ref.dtype)


def _wkv_step(state_t, vecs, ln_g, ln_b):
    nh, _, _, nb = state_t.shape
    st = pl.BlockSpec((2, HEAD, HEAD, nb), lambda p: (p, 0, 0, 0))
    vec = pl.BlockSpec((nb, 2 * HEAD), lambda p: (0, p))
    par = pl.BlockSpec((1, 2 * HEAD), lambda p: (0, p))
    return pl.pallas_call(
        _wkv_step_kernel,
        grid=(nh // 2,),
        in_specs=[st] + [vec] * 8 + [par, par],
        out_specs=[vec, st],
        out_shape=[jax.ShapeDtypeStruct((nb, nh * HEAD), BF16), jax.ShapeDtypeStruct(state_t.shape, F32)],
        scratch_shapes=[pltpu.VMEM((2 * HEAD, nb), F32), pltpu.VMEM((2 * HEAD, nb), F32)],
        compiler_params=_cparams(("parallel",)),
        name="wkv_step",
    )(state_t, *vecs, ln_g, ln_b)


def _rwkv_post_kernel(y_ref, bonus_ref, g_ref, lng_ref, lnb_ref, o_ref):
    bd = _pair_block_ones() * (1.0 / HEAD)
    y = y_ref[...]
    d = y - _head_sum(y, bd)
    var = _head_sum(d * d, bd)
    yn = d * lax.rsqrt(var + GN_EPS) * lng_ref[...] + lnb_ref[...]
    o_ref[...] = ((yn + bonus_ref[...]) * g_ref[...]).astype(o_ref.dtype)


def _rwkv_post(y, bonus, g, ln_g, ln_b, tm):
    m = y.shape[0]
    row = pl.BlockSpec((tm, RWKV_W), lambda i: (i, 0))
    par = pl.BlockSpec((1, RWKV_W), lambda i: (0, 0))
    return pl.pallas_call(
        _rwkv_post_kernel,
        grid=(m // tm,),
        in_specs=[row, row, row, par, par],
        out_specs=row,
        out_shape=jax.ShapeDtypeStruct((m, RWKV_W), BF16),
        compiler_params=_cparams(("parallel",)),
        name="rwkv_post",
    )(y, bonus, g, ln_g, ln_b)


def _xattn_prompt_kernel(q_ref, k_ref, v_ref, o_ref):
    scale = XA_DIM ** -0.5
    q = q_ref[...]
    for h in range(XA_HEADS):
        sl = slice(h * XA_DIM, (h + 1) * XA_DIM)
        s = lax.dot_general(q[:, sl].astype(BF16), k_ref[0, :, sl].astype(BF16),
                            NT_DIMS, preferred_element_type=F32) * scale
        p = jnp.exp(s - jnp.max(s, axis=-1, keepdims=True))
        den = jnp.sum(p, axis=-1, keepdims=True)
        o = jnp.dot(p.astype(BF16), v_ref[0, :, sl].astype(BF16), preferred_element_type=F32)
        o_ref[:, sl] = (o / den).astype(o_ref.dtype)


def _xattn_prompt(z, nb, t_len, col_blk, mk, mv, tq=512):
    nt = t_len // tq
    xw = XA_HEADS * XA_DIM
    nmem = mk.shape[1]
    kv = pl.BlockSpec((1, nmem, xw), lambda b, t: (b, 0, 0))
    return pl.pallas_call(
        _xattn_prompt_kernel,
        grid=(nb, nt),
        in_specs=[pl.BlockSpec((tq, xw), lambda b, t: (b * nt + t, col_blk)), kv, kv],
        out_specs=pl.BlockSpec((tq, xw), lambda b, t: (b * nt + t, 0)),
        out_shape=jax.ShapeDtypeStruct((nb * t_len, xw), BF16),
        compiler_params=_cparams(("parallel", "parallel")),
        name="xattn_prompt",
    )(z, mk, mv)


def _xattn_sample_kernel(q_ref, k_ref, v_ref, o_ref):
    bb = q_ref.shape[0]
    nrow = k_ref.shape[1] // SUBLANES
    full = (bb, nrow, SUBLANES, XA_DIM)
    q = q_ref[...] * (XA_DIM ** -0.5)
    q8 = jnp.concatenate([q, q], axis=1)[:, None]
    k = k_ref[...].reshape(full)
    s = jnp.broadcast_to(jnp.sum(k * q8, axis=-1, keepdims=True), full)
    mx = jnp.max(s, axis=1, keepdims=True)
    mx = jnp.maximum(mx, pltpu.roll(mx, XA_HEADS, axis=2))
    p = jnp.exp(s - mx)
    den = jnp.sum(p, axis=1, keepdims=True)
    den = den + pltpu.roll(den, XA_HEADS, axis=2)
    o = jnp.sum(p * v_ref[...].reshape(full), axis=1, keepdims=True)
    o = o + pltpu.roll(o, XA_HEADS, axis=2)
    o_ref[...] = (o / den)[:, 0, 0:XA_HEADS, :].astype(o_ref.dtype)


def _xattn_sample(q3, mk, mv, bb=8):
    nb, rows, _ = mk.shape
    kv = pl.BlockSpec((bb, rows, XA_DIM), lambda i: (i, 0, 0))
    qs = pl.BlockSpec((bb, XA_HEADS, XA_DIM), lambda i: (i, 0, 0))
    return pl.pallas_call(
        _xattn_sample_kernel,
        grid=(nb // bb,),
        in_specs=[qs, kv, kv],
        out_specs=qs,
        out_shape=jax.ShapeDtypeStruct((nb, XA_HEADS, XA_DIM), BF16),
        compiler_params=_cparams(("parallel",)),
        name="xattn_sample",
    )(q3, mk, mv)


def _merge_kernel(pa_ref, pb_ref, pc_ref, g0_ref, g1_ref, g2_ref, h_ref, wa_ref, wb_ref, wc_ref, wo_ref, o_ref):
    oa = jnp.dot(pa_ref[...], wa_ref[...], preferred_element_type=F32)
    ob = jnp.dot(pb_ref[...], wb_ref[...], preferred_element_type=F32)
    oc = jnp.dot(pc_ref[...], wc_ref[...], preferred_element_type=F32)
    merged = (jax.nn.sigmoid(g0_ref[...]) * oa + jax.nn.sigmoid(g1_ref[...]) * ob
              + jax.nn.sigmoid(g2_ref[...]) * oc)
    o_ref[...] = h_ref[...] + jnp.dot(merged.astype(BF16), wo_ref[...], preferred_element_type=F32)


def _merge(pa, pb, pc, z, zg_blk0, h, tm, wa, wb, wc, wo):
    m, d = h.shape
    const = lambda i: (0, 0)
    resident = lambda w: pl.BlockSpec(w.shape, const, pipeline_mode=pl.Buffered(1))
    in_specs = [
        pl.BlockSpec((tm, pa.shape[1]), lambda i: (i, 0)),
        pl.BlockSpec((tm, pb.shape[1]), lambda i: (i, 0)),
        pl.BlockSpec((tm, pc.shape[1]), lambda i: (i, 0)),
        pl.BlockSpec((tm, d), lambda i: (i, zg_blk0)),
        pl.BlockSpec((tm, d), lambda i: (i, zg_blk0 + 1)),
        pl.BlockSpec((tm, d), lambda i: (i, zg_blk0 + 2)),
        pl.BlockSpec((tm, d), lambda i: (i, 0)),
        resident(wa), resident(wb), resident(wc), resident(wo),
    ]
    return pl.pallas_call(
        _merge_kernel,
        grid=(m // tm,),
        in_specs=in_specs,
        out_specs=pl.BlockSpec((tm, d), lambda i: (i, 0)),
        out_shape=jax.ShapeDtypeStruct((m, d), F32),
        compiler_params=_cparams(("parallel",)),
        name="merge",
    )(pa, pb, pc, z, z, z, h, wa, wb, wc, wo)


def _pack_zr(x, axis):
    w = RWKV_W
    take = lambda a, b: lax.slice_in_dim(x, a, b, axis=axis)

    def pad(n):
        shape = list(x.shape)
        shape[axis] = n
        return jnp.zeros(shape, x.dtype)

    return jnp.concatenate([take(0, 3 * w + 64), pad(64), take(3 * w + 64, 3 * w + 128), pad(64),
                            take(3 * w + 128, ZR_TRUE), pad(96)], axis=axis)


def _unpack_zr_cols(x):
    w = RWKV_W
    return jnp.concatenate([x[..., :3 * w + 64], x[..., 3 * w + 128:3 * w + 192], x[..., 3 * w + 256:3 * w + 416]],
                           axis=-1)


def _pad_rows(x, n):
    return jnp.concatenate([x, jnp.zeros((n - x.shape[0],) + x.shape[1:], x.dtype)], axis=0)


def kernel(x_prompt, x_sample, mem_prompt, cache_mem_k, cache_mem_v, state_wkv, state_shift, state_pool,
           ffn1_norm_g, ffn1_w_gate, ffn1_w_up, ffn1_w_down, mix_norm_g, w_in,
           pool_group_w, pool_scale, pool_out,
           rwkv_mu, rwkv_w0, rwkv_w_up, rwkv_a0, rwkv_a_up, rwkv_g_up, rwkv_k_k, rwkv_k_a, rwkv_r_k,
           rwkv_ln_g, rwkv_ln_b, rwkv_out,
           mem_norm_g, w_mem_k, w_mem_v, xattn_out, w_o,
           ffn2_norm_g, ffn2_w_gate, ffn2_w_up, ffn2_w_down, final_norm_g):
    nb, t_len, d = x_prompt.shape
    ns = x_sample.shape[0]
    assert w_in.shape[0] == 1 and x_sample.shape[1] == 1
    n_mem = mem_prompt.shape[1]
    pool_w = pool_out.shape[1]
    xa_w = xattn_out.shape[1]
    n_heads = RWKV_W // HEAD
    nbuf = state_pool.shape[2]
    rows_p = nb * t_len
    l = 0

    d_ff = ffn1_w_gate.shape[2]
    f1 = [w.reshape(w.shape[1:]).astype(BF16) for w in (ffn1_w_gate, ffn1_w_up, ffn1_w_down)]
    f2 = [w.reshape(w.shape[1:]).astype(BF16) for w in (ffn2_w_gate, ffn2_w_up, ffn2_w_down)]
    wit = jnp.swapaxes(w_in[l], 0, 1)
    o_zr, o_zq, o_zg = pool_w, pool_w + ZR_TRUE, pool_w + ZR_TRUE + xa_w
    w_in_t = jnp.concatenate([_pack_zr(wit[o_zr:o_zq], 0), wit[:o_zr], wit[o_zg:], wit[o_zq:o_zg]],
                             axis=0).astype(BF16)
    col_zp = ZR_W // pool_w
    col_zg = (ZR_W + pool_w) // d
    col_zq = (ZR_W + pool_w + 3 * d) // xa_w
    row = lambda v: v.reshape(1, -1)
    prep_params = [row(_pack_zr(rwkv_mu[l], 0)), row(rwkv_w0[l]), _pad_rows(rwkv_w_up[l], 128).astype(BF16),
                   row(rwkv_a0[l]), _pad_rows(rwkv_a_up[l], 128).astype(BF16),
                   _pad_rows(rwkv_g_up[l], 256).astype(BF16), row(rwkv_k_k[l]), row(rwkv_k_a[l]),
                   row(rwkv_r_k[l])]
    gw = pool_group_w[l].astype(BF16)
    w_kv_t = jnp.concatenate([jnp.swapaxes(w_mem_k[l], 0, 1), jnp.swapaxes(w_mem_v[l], 0, 1)], axis=0).astype(BF16)
    wa, wb, wc, wo = (pool_out[l].astype(BF16), rwkv_out[l].astype(BF16), xattn_out[l].astype(BF16),
                      w_o[l].astype(BF16))
    g1, gm, g2, fg = row(ffn1_norm_g[l]), row(mix_norm_g[l]), row(ffn2_norm_g[l]), row(final_norm_g)
    scale = row(pool_scale[l])
    ln_g, ln_b = rwkv_ln_g[l], rwkv_ln_b[l]

    tm_p = 1024
    h1_p, h1_s = _ffn(x_prompt.reshape(rows_p, d), x_sample.reshape(ns, d), tm_p, g1, *f1)
    z_p = _norm_matmul(h1_p, gm, w_in_t, tm=1024, tn=512)
    z_s = _norm_matmul(h1_s, gm, w_in_t, tm=ns, tn=512)
    kv = _norm_matmul(mem_prompt.reshape(nb * n_mem, d), row(mem_norm_g[l]), w_kv_t, tm=512, tn=512)
    mk_p = kv[:, :xa_w].reshape(nb, n_mem, xa_w)
    mv_p = kv[:, xa_w:].reshape(nb, n_mem, xa_w)

    pool_state = state_pool.reshape(ns, nbuf, pool_w)
    pa_p = _pool_prompt(z_p, nb, t_len, col_zp, gw, scale)
    pa_s = _pool_sample(z_s, col_zp, jnp.swapaxes(pool_state, 0, 1), gw, scale)

    prep_p = _prep_prompt(z_p, nb, t_len, prep_params)
    y_p, st_p = _wkv_chunk(prep_p[:6], nb, t_len)
    pb_p = _rwkv_post(y_p, prep_p[7], prep_p[6], row(ln_g), row(ln_b), tm=512)
    prep_s = _prep_sample(z_s, _pack_zr(state_shift.reshape(ns, ZR_TRUE), 1), prep_params)
    state_t = jnp.transpose(state_wkv.reshape(ns, n_heads, HEAD, HEAD), (1, 2, 3, 0))
    pb_s, wkv_s_t = _wkv_step(state_t, prep_s, row(ln_g), row(ln_b))

    pc_p = _xattn_prompt(z_p, nb, t_len, col_zq, mk_p, mv_p)
    q_s = z_s[:, ZR_W + pool_w + 3 * d:].reshape(ns, XA_HEADS, XA_DIM)
    pc_s = _xattn_sample(q_s, cache_mem_k.reshape(ns, n_mem * XA_HEADS, XA_DIM),
                         cache_mem_v.reshape(ns, n_mem * XA_HEADS, XA_DIM)).reshape(ns, xa_w)

    h2_p = _merge(pa_p, pb_p, pc_p, z_p, col_zg, h1_p, 256, wa, wb, wc, wo)
    h2_s = _merge(pa_s, pb_s, pc_s, z_s, col_zg, h1_s, ns, wa, wb, wc, wo)
    y_prompt, y_sample = _ffn(h2_p, h2_s, tm_p, g2, *f2, final_g=fg)

    ends = [(b + 1) * t_len for b in range(nb)]
    shift_p = _unpack_zr_cols(jnp.stack([z_p[e - 1:e, :ZR_W] for e in ends]))[None]
    pool_p = jnp.stack([z_p[e - nbuf:e, ZR_W:ZR_W + pool_w] for e in ends])[None]
    shift_s = _unpack_zr_cols(z_s[:, :ZR_W])[None, :, None, :]
    pool_s = jnp.concatenate([pool_state[:, 1:], z_s[:, None, ZR_W:ZR_W + pool_w]], axis=1)[None]
    st5 = st_p.reshape(nb, n_heads // 2, 2, HEAD, 2, HEAD)
    wkv_p = jnp.stack([st5[:, :, 0, :, 0, :], st5[:, :, 1, :, 1, :]], axis=2).reshape(1, nb, n_heads, HEAD, HEAD)
    wkv_s = jnp.transpose(wkv_s_t, (3, 0, 1, 2)).reshape(state_wkv.shape)
    mem_k_p = mk_p.reshape(1, nb, n_mem, XA_HEADS, XA_DIM)
    mem_v_p = mv_p.reshape(1, nb, n_mem, XA_HEADS, XA_DIM)
    return (y_prompt.reshape(nb, t_len, d), y_sample.reshape(ns, 1, d), mem_k_p, mem_v_p, wkv_p, shift_p, pool_p,
            wkv_s, shift_s, pool_s)
```

```python
import functools
import math

import jax
import jax.numpy as jnp
from jax import lax
from jax.experimental import pallas as pl
from jax.experimental.pallas import tpu as pltpu

F32 = jnp.float32
BF16 = jnp.bfloat16
HI = lax.Precision.HIGHEST

RMS_EPS = 1e-6
GN_EPS = 64e-5
POOL_WINDOWS = (2, 4, 8, 16)
HEAD = 64
LANES = 128
SUBLANES = 8
XA_HEADS = 4
XA_DIM = 128
PAST_LEN = 16384
VMEM_LIMIT = 60 * 1024 * 1024
EXP_M05 = math.exp(-0.5)

ZR_W = 3584
RWKV_W = 1024
ZR_TRUE = 3360

NN_DIMS = (((1,), (0,)), ((), ()))
NT_DIMS = (((1,), (1,)), ((), ()))
TN_DIMS = (((0,), (0,)), ((), ()))


def _cparams(sem):
    return pltpu.CompilerParams(dimension_semantics=sem, vmem_limit_bytes=VMEM_LIMIT)


def _rms(x, g):
    ms = jnp.mean(x * x, axis=-1, keepdims=True)
    return x * lax.rsqrt(ms + RMS_EPS) * g


def _bdot(a, b, dims=NN_DIMS):
    return lax.dot_general(a.astype(BF16), b.astype(BF16), dims, preferred_element_type=F32)


def _ffn_kernel(hp_ref, hs_ref, g_ref, fg_ref, wg_ref, wu_ref, wd_ref, wgt_ref, wut_ref, wdt_ref,
                op_ref, os_ref, xp_ref, xs_ref, *, nfull, final):
    m = pl.program_id(0)
    f = pl.program_id(1)

    def start(h_ref, x_ref, o_ref):
        x_ref[...] = _rms(h_ref[...], g_ref[...]).astype(BF16)
        o_ref[...] = jnp.zeros_like(o_ref)

    def contribution(x_ref, wg, wu, wd):
        xn = x_ref[...]
        gate = jnp.dot(xn, wg, preferred_element_type=F32)
        up = jnp.dot(xn, wu, preferred_element_type=F32)
        act = (gate * jax.nn.sigmoid(gate) * up).astype(BF16)
        return jnp.dot(act, wd, preferred_element_type=F32)

    def finish(h_ref, o_ref, last):
        out = h_ref[...] + 0.5 * (o_ref[...] + last)
        if final:
            out = _rms(out, fg_ref[...])
        o_ref[...] = out

    @pl.when(f == 0)
    def _():
        start(hp_ref, xp_ref, op_ref)

    @pl.when((f == 0) & (m == 0))
    def _():
        start(hs_ref, xs_ref, os_ref)

    @pl.when(f < nfull)
    def _():
        wg, wu, wd = wg_ref[...], wu_ref[...], wd_ref[...]
        op_ref[...] += contribution(xp_ref, wg, wu, wd)

        @pl.when(m == 0)
        def _():
            os_ref[...] += contribution(xs_ref, wg, wu, wd)

    @pl.when(f == nfull)
    def _():
        wg, wu, wd = wgt_ref[...], wut_ref[...], wdt_ref[...]
        finish(hp_ref, op_ref, contribution(xp_ref, wg, wu, wd))

        @pl.when(m == 0)
        def _():
            finish(hs_ref, os_ref, contribution(xs_ref, wg, wu, wd))


def _ffn(hp, hs, tm, g, wg, wu, wd, final_g=None, tf=1024):
    mp, d = hp.shape
    ms = hs.shape[0]
    d_ff = wg.shape[1]
    nfull, tail = divmod(d_ff, tf)
    assert tail > 0 and tail % LANES == 0 and mp % tm == 0
    last_main = nfull - 1
    split = nfull * tf
    wgt, wut, wdt = wg[:, split:], wu[:, split:], wd[split:]
    final = final_g is not None
    vec = pl.BlockSpec((1, d), lambda i, f: (0, 0))
    once = pl.Buffered(1)
    in_specs = [
        pl.BlockSpec((tm, d), lambda i, f: (i, 0), pipeline_mode=once),
        pl.BlockSpec((ms, d), lambda i, f: (0, 0), pipeline_mode=once),
        vec, vec,
        pl.BlockSpec((d, tf), lambda i, f: (0, jnp.minimum(f, last_main))),
        pl.BlockSpec((d, tf), lambda i, f: (0, jnp.minimum(f, last_main))),
        pl.BlockSpec((tf, d), lambda i, f: (jnp.minimum(f, last_main), 0)),
        pl.BlockSpec((d, tail), lambda i, f: (0, 0), pipeline_mode=once),
        pl.BlockSpec((d, tail), lambda i, f: (0, 0), pipeline_mode=once),
        pl.BlockSpec((tail, d), lambda i, f: (0, 0), pipeline_mode=once),
    ]
    return pl.pallas_call(
        functools.partial(_ffn_kernel, nfull=nfull, final=final),
        grid=(mp // tm, nfull + 1),
        in_specs=in_specs,
        out_specs=[pl.BlockSpec((tm, d), lambda i, f: (i, 0)), pl.BlockSpec((ms, d), lambda i, f: (0, 0))],
        out_shape=[jax.ShapeDtypeStruct((mp, d), F32), jax.ShapeDtypeStruct((ms, d), F32)],
        scratch_shapes=[pltpu.VMEM((tm, d), BF16), pltpu.VMEM((ms, d), BF16)],
        compiler_params=_cparams(("arbitrary", "arbitrary")),
        name="ffn",
    )(hp, hs, g, g if final_g is None else final_g, wg, wu, wd, wgt, wut, wdt)


def _norm_matmul_kernel(h_ref, g_ref, wt_ref, o_ref, xn_ref):
    @pl.when(pl.program_id(1) == 0)
    def _():
        xn_ref[...] = _rms(h_ref[...], g_ref[...]).astype(BF16)

    o_ref[...] = lax.dot_general(xn_ref[...], wt_ref[...], NT_DIMS, preferred_element_type=F32)


def _norm_matmul(h, g, wt, tm, tn):
    m, d = h.shape
    n = wt.shape[0]
    return pl.pallas_call(
        _norm_matmul_kernel,
        grid=(m // tm, n // tn),
        in_specs=[
            pl.BlockSpec((tm, d), lambda i, j: (i, 0)),
            pl.BlockSpec((1, d), lambda i, j: (0, 0)),
            pl.BlockSpec((tn, d), lambda i, j: (j, 0)),
        ],
        out_specs=pl.BlockSpec((tm, tn), lambda i, j: (i, j)),
        out_shape=jax.ShapeDtypeStruct((m, n), F32),
        scratch_shapes=[pltpu.VMEM((tm, d), BF16)],
        compiler_params=_cparams(("parallel", "arbitrary")),
        name="norm_matmul",
    )(h, g, wt)


def _pool_mix(pooled_groups, gw_ref, scale_ref, o_ref):
    for gi, pooled in enumerate(pooled_groups):
        sl = slice(gi * LANES, (gi + 1) * LANES)
        mixed = jnp.dot(pooled.astype(BF16), gw_ref[gi], preferred_element_type=F32)
        o_ref[:, sl] = (mixed * scale_ref[:, sl]).astype(o_ref.dtype)


def _pool_prompt_kernel(zp_ref, gw_ref, scale_ref, o_ref, ext_ref, *, tt):
    t = pl.program_id(1)
    hist = 16

    @pl.when(t == 0)
    def _():
        ext_ref[0:hist, :] = jnp.zeros((hist, ext_ref.shape[1]), F32)

    x = zp_ref[...]
    ext_ref[hist:hist + tt, :] = x
    pos = t * tt + lax.broadcasted_iota(jnp.int32, (tt, LANES), 0)
    groups = []
    for gi, w in enumerate(POOL_WINDOWS):
        sl = slice(gi * LANES, (gi + 1) * LANES)
        acc = x[:, sl]
        for k in range(1, w):
            acc = acc + ext_ref[hist - k:hist - k + tt, sl]
        cnt = jnp.minimum(pos + 1, w).astype(F32)
        groups.append(acc / cnt - x[:, sl])
    _pool_mix(groups, gw_ref, scale_ref, o_ref)
    ext_ref[0:hist, :] = ext_ref[tt:tt + hist, :]


def _pool_prompt(z, nb, t_len, col_blk, gw, scale, tt=256):
    nt = t_len // tt
    pw = gw.shape[0] * LANES
    return pl.pallas_call(
        functools.partial(_pool_prompt_kernel, tt=tt),
        grid=(nb, nt),
        in_specs=[
            pl.BlockSpec((tt, pw), lambda b, t: (b * nt + t, col_blk)),
            pl.BlockSpec(gw.shape, lambda b, t: (0, 0, 0)),
            pl.BlockSpec((1, pw), lambda b, t: (0, 0)),
        ],
        out_specs=pl.BlockSpec((tt, pw), lambda b, t: (b * nt + t, 0)),
        out_shape=jax.ShapeDtypeStruct((nb * t_len, pw), BF16),
        scratch_shapes=[pltpu.VMEM((tt + 16, pw), F32)],
        compiler_params=_cparams(("parallel", "arbitrary")),
        name="pool_prompt",
    )(z, gw, scale)


def _pool_sample_kernel(zp_ref, buf_ref, gw_ref, scale_ref, o_ref):
    x = zp_ref[...]
    nbuf = buf_ref.shape[0]
    groups = []
    for gi, w in enumerate(POOL_WINDOWS):
        sl = slice(gi * LANES, (gi + 1) * LANES)
        acc = x[:, sl]
        for k in range(1, w):
            acc = acc + buf_ref[nbuf - k, :, sl]
        cnt = float(min(PAST_LEN + 1, w))
        groups.append(acc / cnt - x[:, sl])
    _pool_mix(groups, gw_ref, scale_ref, o_ref)


def _pool_sample(z, col_blk, buf_t, gw, scale):
    nrows = z.shape[0]
    pw = gw.shape[0] * LANES
    return pl.pallas_call(
        _pool_sample_kernel,
        grid=(1,),
        in_specs=[
            pl.BlockSpec((nrows, pw), lambda i: (0, col_blk)),
            pl.BlockSpec(buf_t.shape, lambda i: (0, 0, 0)),
            pl.BlockSpec(gw.shape, lambda i: (0, 0, 0)),
            pl.BlockSpec((1, pw), lambda i: (0, 0)),
        ],
        out_specs=pl.BlockSpec((nrows, pw), lambda i: (0, 0)),
        out_shape=jax.ShapeDtypeStruct((nrows, pw), BF16),
        compiler_params=_cparams(("arbitrary",)),
        name="pool_sample",
    )(z, buf_t, gw, scale)


def _pair_block_ones():
    r = lax.broadcasted_iota(jnp.int32, (LANES, LANES), 0) // HEAD
    c = lax.broadcasted_iota(jnp.int32, (LANES, LANES), 1) // HEAD
    return (r == c).astype(F32)


def _head_sum(x, bd):
    cols = [jnp.dot(x[:, c * LANES:(c + 1) * LANES], bd, precision=HI, preferred_element_type=F32)
            for c in range(x.shape[1] // LANES)]
    return jnp.concatenate(cols, axis=1)


def _prep_math(x, prev, p_refs):
    mu_ref, w0_ref, wup_ref, a0_ref, aup_ref, gup_ref, kk_ref, ka_ref, rk_ref = p_refs
    xm = x + (prev - x) * mu_ref[...]
    w = RWKV_W
    r = xm[:, 0:w]
    k = xm[:, w:2 * w]
    v = xm[:, 2 * w:3 * w]
    wl = xm[:, 3 * w:3 * w + 128]
    al = xm[:, 3 * w + 128:3 * w + 256]
    gl = xm[:, 3 * w + 256:3 * w + 512]
    bd = _pair_block_ones()
    dw = w0_ref[...] + jnp.dot(jnp.tanh(wl).astype(BF16), wup_ref[...], preferred_element_type=F32)
    lw = -EXP_M05 * jax.nn.sigmoid(dw)
    a = jax.nn.sigmoid(a0_ref[...] + jnp.dot(al.astype(BF16), aup_ref[...], preferred_element_type=F32))
    g = jnp.dot(jax.nn.sigmoid(gl).astype(BF16), gup_ref[...], preferred_element_type=F32)
    kk = k * kk_ref[...]
    kk = kk * lax.rsqrt(jnp.maximum(_head_sum(kk * kk, bd), 1e-24))
    kmod = k * (1.0 + (a - 1.0) * ka_ref[...])
    bonus = _head_sum(r * kmod * rk_ref[...], bd) * v
    return r, lw, kmod, v, -kk, kk * a, g, bonus


def _prep_prompt_kernel(zr_ref, *refs):
    p_refs, out_refs, carry_ref = refs[:9], refs[9:17], refs[17]
    t = pl.program_id(1)

    @pl.when(t == 0)
    def _():
        carry_ref[...] = jnp.zeros_like(carry_ref)

    x = zr_ref[...]
    rolled = pltpu.roll(x, 1, axis=0)
    first = lax.broadcasted_iota(jnp.int32, x.shape, 0) == 0
    prev = jnp.where(first, carry_ref[0:1, :], rolled)
    carry_ref[...] = rolled[0:SUBLANES, :]
    for o_ref, val in zip(out_refs, _prep_math(x, prev, p_refs)):
        o_ref[...] = val


def _prep_sample_kernel(zr_ref, prev_ref, *refs):
    p_refs, out_refs = refs[:9], refs[9:17]
    for o_ref, val in zip(out_refs, _prep_math(zr_ref[...], prev_ref[...], p_refs)):
        o_ref[...] = val


def _prep_param_specs(params):
    zero = (lambda *idx: (0, 0))
    return [pl.BlockSpec(p.shape, zero) for p in params]


def _prep_prompt(z, nb, t_len, params, tt=256):
    nt = t_len // tt
    rows = nb * t_len
    row_spec = pl.BlockSpec((tt, RWKV_W), lambda b, t: (b * nt + t, 0))
    return pl.pallas_call(
        _prep_prompt_kernel,
        grid=(nb, nt),
        in_specs=[pl.BlockSpec((tt, ZR_W), lambda b, t: (b * nt + t, 0))] + _prep_param_specs(params),
        out_specs=[row_spec] * 8,
        out_shape=[jax.ShapeDtypeStruct((rows, RWKV_W), F32)] * 8,
        scratch_shapes=[pltpu.VMEM((SUBLANES, ZR_W), F32)],
        compiler_params=_cparams(("parallel", "arbitrary")),
        name="prep_prompt",
    )(z, *params)


def _prep_sample(z, prev, params):
    nrows = z.shape[0]
    out_spec = pl.BlockSpec((nrows, RWKV_W), lambda i: (0, 0))
    return pl.pallas_call(
        _prep_sample_kernel,
        grid=(1,),
        in_specs=[pl.BlockSpec((nrows, ZR_W), lambda i: (0, 0)),
                  pl.BlockSpec((nrows, ZR_W), lambda i: (0, 0))] + _prep_param_specs(params),
        out_specs=[out_spec] * 8,
        out_shape=[jax.ShapeDtypeStruct((nrows, RWKV_W), F32)] * 8,
        compiler_params=_cparams(("arbitrary",)),
        name="prep_sample",
    )(z, prev, *params)


def _split3(x):
    hi = x.astype(BF16)
    rest = x - hi.astype(F32)
    mid = rest.astype(BF16)
    lo = (rest - mid.astype(F32)).astype(BF16)
    return hi, mid, lo


def _select_dot(sel, x):
    sel = sel.astype(BF16)
    hi, mid, lo = _split3(x)
    return _bdot(sel, hi) + (_bdot(sel, mid) + _bdot(sel, lo))


def _wkv_chunk_kernel(r_ref, lw_ref, k_ref, v_ref, a_ref, b_ref, y_ref, sout_ref, s_ref, *, c_len, pp, nc):
    c = pl.program_id(2)
    n2 = 2 * c_len
    assert n2 == LANES

    @pl.when(c == 0)
    def _():
        s_ref[...] = jnp.zeros_like(s_ref)

    row = lax.broadcasted_iota(jnp.int32, (n2, n2), 0)
    col = lax.broadcasted_iota(jnp.int32, (n2, n2), 1)
    tr = row & (c_len - 1)
    tc = col & (c_len - 1)
    strict = tr > tc
    incl = tr >= tc
    tri = (lax.broadcasted_iota(jnp.int32, (c_len, c_len), 0)
           >= lax.broadcasted_iota(jnp.int32, (c_len, c_len), 1))
    head_a = lax.broadcasted_iota(jnp.int32, (c_len, LANES), 1) < HEAD

    def stack(x):
        return jnp.concatenate([jnp.where(head_a, x, 0.0), jnp.where(head_a, 0.0, x)], axis=0).astype(BF16)

    prs = range(pp)
    cat = jnp.concatenate
    sls = [slice(q * LANES, (q + 1) * LANES) for q in prs]
    lw = [lw_ref[:, sl] for sl in sls]
    cum = [_select_dot(tri, x) for x in lw]
    tot = [x[c_len - 1:c_len, :] for x in cum]
    xr = [stack(r_ref[:, sls[q]] * jnp.exp(cum[q])) for q in prs]
    xa = [stack(a_ref[:, sls[q]] * jnp.exp(cum[q] - lw[q])) for q in prs]
    e_neg = [jnp.exp(-x) for x in cum]
    e_rem = [jnp.exp(tot[q] - cum[q]) for q in prs]
    yb = [stack(b_ref[:, sls[q]] * e_neg[q]) for q in prs]
    yk = [stack(k_ref[:, sls[q]] * e_neg[q]) for q in prs]
    zb = [stack(b_ref[:, sls[q]] * e_rem[q]) for q in prs]
    zk = [stack(k_ref[:, sls[q]] * e_rem[q]) for q in prs]
    vs = [stack(v_ref[:, sl]) for sl in sls]

    g = [_bdot(cat([xa[q], xr[q]], axis=0), cat([yb[q], yk[q]], axis=0), NT_DIMS) for q in prs]
    m_ab = [jnp.where(strict, x[0:n2, 0:n2], 0.0) for x in g]
    m_ak = [jnp.where(strict, x[0:n2, n2:2 * n2], 0.0).astype(BF16) for x in g]
    n_rb = [jnp.where(incl, x[n2:2 * n2, 0:n2], 0.0) for x in g]
    n_rk = [jnp.where(incl, x[n2:2 * n2, n2:2 * n2], 0.0).astype(BF16) for x in g]

    s_old = [s_ref[q] for q in prs]
    lhs = [cat([cat([xa[q], m_ak[q]], axis=1), cat([xr[q], n_rk[q]], axis=1)], axis=0) for q in prs]
    xy0 = [_bdot(lhs[q], cat([s_old[q].T.astype(BF16), vs[q]], axis=0)) for q in prs]

    x = [xy0[q][0:n2] for q in prs]
    mk = m_ab
    nlev = int(math.log2(c_len))
    for lev in range(nlev):
        if lev < nlev - 1:
            res = [_bdot(mk[q], cat([mk[q], x[q]], axis=1)) for q in prs]
            mk = [r[:, 0:n2] for r in res]
            x = [x[q] + res[q][:, n2:2 * n2] for q in prs]
        else:
            x = [x[q] + _bdot(mk[q], x[q]) for q in prs]
    u = [v.astype(BF16) for v in x]

    y_st = [xy0[q][n2:2 * n2] + _bdot(n_rb[q], u[q]) for q in prs]
    for q in prs:
        y_ref[:, sls[q]] = y_st[q][0:c_len] + y_st[q][c_len:n2]
    for q in prs:
        s_ref[q] = s_old[q] * jnp.exp(tot[q]) + _bdot(cat([u[q], vs[q]], axis=0), cat([zb[q], zk[q]], axis=0),
                                                     TN_DIMS)

    @pl.when(c == nc - 1)
    def _():
        sout_ref[0] = s_ref[...]


def _wkv_chunk(seqs, nb, t_len, c_len=64, pp=8):
    nc = t_len // c_len
    npair = RWKV_W // LANES
    ng = npair // pp
    blk = pl.BlockSpec((c_len, pp * LANES), lambda b, g, c: (b * nc + c, g))
    return pl.pallas_call(
        functools.partial(_wkv_chunk_kernel, c_len=c_len, pp=pp, nc=nc),
        grid=(nb, ng, nc),
        in_specs=[blk] * 6,
        out_specs=[blk, pl.BlockSpec((1, pp, LANES, LANES), lambda b, g, c: (b, g, 0, 0))],
        out_shape=[jax.ShapeDtypeStruct((nb * t_len, RWKV_W), F32),
                   jax.ShapeDtypeStruct((nb, npair, LANES, LANES), F32)],
        scratch_shapes=[pltpu.VMEM((pp, LANES, LANES), F32)],
        compiler_params=_cparams(("parallel", "parallel", "arbitrary")),
        name="wkv_chunk",
    )(*seqs)


def _wkv_step_kernel(s_ref, r_ref, lw_ref, k_ref, v_ref, a_ref, b_ref, g_ref, bonus_ref, lng_ref, lnb_ref,
                     yb_ref, so_ref, vt_scr, y_scr):
    rt, wt, kt, at, bt = (x[...].T for x in (r_ref, lw_ref, k_ref, a_ref, b_ref))
    wt = jnp.exp(wt)
    vt_scr[...] = v_ref[...].T
    for hh in range(2):
        rows = slice(hh * HEAD, (hh + 1) * HEAD)
        r, w, k, a, b = (x[rows, :] for x in (rt, wt, kt, at, bt))

        def body(i, carry):
            si = s_ref[hh, i]
            sa = jnp.sum(si * a, axis=0, keepdims=True)
            vi = vt_scr[pl.ds(hh * HEAD + i, 1), :]
            s2 = si * w + sa * b + vi * k
            so_ref[hh, i] = s2
            y_scr[pl.ds(hh * HEAD + i, 1), :] = jnp.sum(s2 * r, axis=0, keepdims=True)
            return carry

        lax.fori_loop(0, HEAD, body, 0)

    outs = []
    for hh in range(2):
        y = y_scr[hh * HEAD:(hh + 1) * HEAD, :]
        d = y - jnp.mean(y, axis=0, keepdims=True)
        var = jnp.mean(d * d, axis=0, keepdims=True)
        outs.append(d * lax.rsqrt(var + GN_EPS))
    yn = jnp.concatenate(outs, axis=0).T * lng_ref[...] + lnb_ref[...]
    yb_ref[...] = ((yn + bonus_ref[...]) * g_ref[...]).astype(yb_ref.dtype)


def _wkv_step(state_t, vecs, ln_g, ln_b):
    nh, _, _, nb = state_t.shape
    st = pl.BlockSpec((2, HEAD, HEAD, nb), lambda p: (p, 0, 0, 0))
    vec = pl.BlockSpec((nb, 2 * HEAD), lambda p: (0, p))
    par = pl.BlockSpec((1, 2 * HEAD), lambda p: (0, p))
    return pl.pallas_call(
        _wkv_step_kernel,
        grid=(nh // 2,),
        in_specs=[st] + [vec] * 8 + [par, par],
        out_specs=[vec, st],
        out_shape=[jax.ShapeDtypeStruct((nb, nh * HEAD), BF16), jax.ShapeDtypeStruct(state_t.shape, F32)],
        scratch_shapes=[pltpu.VMEM((2 * HEAD, nb), F32), pltpu.VMEM((2 * HEAD, nb), F32)],
        compiler_params=_cparams(("parallel",)),
        name="wkv_step",
    )(state_t, *vecs, ln_g, ln_b)


def _rwkv_post_kernel(y_ref, bonus_ref, g_ref, lng_ref, lnb_ref, o_ref):
    bd = _pair_block_ones() * (1.0 / HEAD)
    y = y_ref[...]
    d = y - _head_sum(y, bd)
    var = _head_sum(d * d, bd)
    yn = d * lax.rsqrt(var + GN_EPS) * lng_ref[...] + lnb_ref[...]
    o_ref[...] = ((yn + bonus_ref[...]) * g_ref[...]).astype(o_ref.dtype)


def _rwkv_post(y, bonus, g, ln_g, ln_b, tm):
    m = y.shape[0]
    row = pl.BlockSpec((tm, RWKV_W), lambda i: (i, 0))
    par = pl.BlockSpec((1, RWKV_W), lambda i: (0, 0))
    return pl.pallas_call(
        _rwkv_post_kernel,
        grid=(m // tm,),
        in_specs=[row, row, row, par, par],
        out_specs=row,
        out_shape=jax.ShapeDtypeStruct((m, RWKV_W), BF16),
        compiler_params=_cparams(("parallel",)),
        name="rwkv_post",
    )(y, bonus, g, ln_g, ln_b)


def _xattn_prompt_kernel(q_ref, k_ref, v_ref, o_ref):
    scale = XA_DIM ** -0.5
    q = q_ref[...]
    for h in range(XA_HEADS):
        sl = slice(h * XA_DIM, (h + 1) * XA_DIM)
        s = lax.dot_general(q[:, sl].astype(BF16), k_ref[0, :, sl].astype(BF16),
                            NT_DIMS, preferred_element_type=F32) * scale
        p = jnp.exp(s - jnp.max(s, axis=-1, keepdims=True))
        den = jnp.sum(p, axis=-1, keepdims=True)
        o = jnp.dot(p.astype(BF16), v_ref[0, :, sl].astype(BF16), preferred_element_type=F32)
        o_ref[:, sl] = (o / den).astype(o_ref.dtype)


def _xattn_prompt(z, nb, t_len, col_blk, mk, mv, tq=512):
    nt = t_len // tq
    xw = XA_HEADS * XA_DIM
    nmem = mk.shape[1]
    kv = pl.BlockSpec((1, nmem, xw), lambda b, t: (b, 0, 0))
    return pl.pallas_call(
        _xattn_prompt_kernel,
        grid=(nb, nt),
        in_specs=[pl.BlockSpec((tq, xw), lambda b, t: (b * nt + t, col_blk)), kv, kv],
        out_specs=pl.BlockSpec((tq, xw), lambda b, t: (b * nt + t, 0)),
        out_shape=jax.ShapeDtypeStruct((nb * t_len, xw), BF16),
        compiler_params=_cparams(("parallel", "parallel")),
        name="xattn_prompt",
    )(z, mk, mv)


def _xattn_sample_kernel(q_ref, k_ref, v_ref, o_ref):
    bb = q_ref.shape[0]
    nrow = k_ref.shape[1] // SUBLANES
    full = (bb, nrow, SUBLANES, XA_DIM)
    q = q_ref[...] * (XA_DIM ** -0.5)
    q8 = jnp.concatenate([q, q], axis=1)[:, None]
    k = k_ref[...].reshape(full)
    s = jnp.broadcast_to(jnp.sum(k * q8, axis=-1, keepdims=True), full)
    mx = jnp.max(s, axis=1, keepdims=True)
    mx = jnp.maximum(mx, pltpu.roll(mx, XA_HEADS, axis=2))
    p = jnp.exp(s - mx)
    den = jnp.sum(p, axis=1, keepdims=True)
    den = den + pltpu.roll(den, XA_HEADS, axis=2)
    o = jnp.sum(p * v_ref[...].reshape(full), axis=1, keepdims=True)
    o = o + pltpu.roll(o, XA_HEADS, axis=2)
    o_ref[...] = (o / den)[:, 0, 0:XA_HEADS, :].astype(o_ref.dtype)


def _xattn_sample(q3, mk, mv, bb=8):
    nb, rows, _ = mk.shape
    kv = pl.BlockSpec((bb, rows, XA_DIM), lambda i: (i, 0, 0))
    qs = pl.BlockSpec((bb, XA_HEADS, XA_DIM), lambda i: (i, 0, 0))
    return pl.pallas_call(
        _xattn_sample_kernel,
        grid=(nb // bb,),
        in_specs=[qs, kv, kv],
        out_specs=qs,
        out_shape=jax.ShapeDtypeStruct((nb, XA_HEADS, XA_DIM), BF16),
        compiler_params=_cparams(("parallel",)),
        name="xattn_sample",
    )(q3, mk, mv)


def _merge_kernel(pa_ref, pb_ref, pc_ref, g0_ref, g1_ref, g2_ref, h_ref, wa_ref, wb_ref, wc_ref, wo_ref, o_ref):
    oa = jnp.dot(pa_ref[...], wa_ref[...], preferred_element_type=F32)
    ob = jnp.dot(pb_ref[...], wb_ref[...], preferred_element_type=F32)
    oc = jnp.dot(pc_ref[...], wc_ref[...], preferred_element_type=F32)
    merged = (jax.nn.sigmoid(g0_ref[...]) * oa + jax.nn.sigmoid(g1_ref[...]) * ob
              + jax.nn.sigmoid(g2_ref[...]) * oc)
    o_ref[...] = h_ref[...] + jnp.dot(merged.astype(BF16), wo_ref[...], preferred_element_type=F32)


def _merge(pa, pb, pc, z, zg_blk0, h, tm, wa, wb, wc, wo):
    m, d = h.shape
    const = lambda i: (0, 0)
    resident = lambda w: pl.BlockSpec(w.shape, const, pipeline_mode=pl.Buffered(1))
    in_specs = [
        pl.BlockSpec((tm, pa.shape[1]), lambda i: (i, 0)),
        pl.BlockSpec((tm, pb.shape[1]), lambda i: (i, 0)),
        pl.BlockSpec((tm, pc.shape[1]), lambda i: (i, 0)),
        pl.BlockSpec((tm, d), lambda i: (i, zg_blk0)),
        pl.BlockSpec((tm, d), lambda i: (i, zg_blk0 + 1)),
        pl.BlockSpec((tm, d), lambda i: (i, zg_blk0 + 2)),
        pl.BlockSpec((tm, d), lambda i: (i, 0)),
        resident(wa), resident(wb), resident(wc), resident(wo),
    ]
    return pl.pallas_call(
        _merge_kernel,
        grid=(m // tm,),
        in_specs=in_specs,
        out_specs=pl.BlockSpec((tm, d), lambda i: (i, 0)),
        out_shape=jax.ShapeDtypeStruct((m, d), F32),
        compiler_params=_cparams(("parallel",)),
        name="merge",
    )(pa, pb, pc, z, z, z, h, wa, wb, wc, wo)


def _pack_zr(x, axis):
    w = RWKV_W
    take = lambda a, b: lax.slice_in_dim(x, a, b, axis=axis)

    def pad(n):
        shape = list(x.shape)
        shape[axis] = n
        return jnp.zeros(shape, x.dtype)

    return jnp.concatenate([take(0, 3 * w + 64), pad(64), take(3 * w + 64, 3 * w + 128), pad(64),
                            take(3 * w + 128, ZR_TRUE), pad(96)], axis=axis)


def _unpack_zr_cols(x):
    w = RWKV_W
    return jnp.concatenate([x[..., :3 * w + 64], x[..., 3 * w + 128:3 * w + 192], x[..., 3 * w + 256:3 * w + 416]],
                           axis=-1)


def _pad_rows(x, n):
    return jnp.concatenate([x, jnp.zeros((n - x.shape[0],) + x.shape[1:], x.dtype)], axis=0)


def kernel(x_prompt, x_sample, mem_prompt, cache_mem_k, cache_mem_v, state_wkv, state_shift, state_pool,
           ffn1_norm_g, ffn1_w_gate, ffn1_w_up, ffn1_w_down, mix_norm_g, w_in,
           pool_group_w, pool_scale, pool_out,
           rwkv_mu, rwkv_w0, rwkv_w_up, rwkv_a0, rwkv_a_up, rwkv_g_up, rwkv_k_k, rwkv_k_a, rwkv_r_k,
           rwkv_ln_g, rwkv_ln_b, rwkv_out,
           mem_norm_g, w_mem_k, w_mem_v, xattn_out, w_o,
           ffn2_norm_g, ffn2_w_gate, ffn2_w_up, ffn2_w_down, final_norm_g):
    nb, t_len, d = x_prompt.shape
    ns = x_sample.shape[0]
    assert w_in.shape[0] == 1 and x_sample.shape[1] == 1
    n_mem = mem_prompt.shape[1]
    pool_w = pool_out.shape[1]
    xa_w = xattn_out.shape[1]
    n_heads = RWKV_W // HEAD
    nbuf = state_pool.shape[2]
    rows_p = nb * t_len
    l = 0

    d_ff = ffn1_w_gate.shape[2]
    f1 = [w.reshape(w.shape[1:]).astype(BF16) for w in (ffn1_w_gate, ffn1_w_up, ffn1_w_down)]
    f2 = [w.reshape(w.shape[1:]).astype(BF16) for w in (ffn2_w_gate, ffn2_w_up, ffn2_w_down)]
    wit = jnp.swapaxes(w_in[l], 0, 1)
    o_zr, o_zq, o_zg = pool_w, pool_w + ZR_TRUE, pool_w + ZR_TRUE + xa_w
    w_in_t = jnp.concatenate([_pack_zr(wit[o_zr:o_zq], 0), wit[:o_zr], wit[o_zg:], wit[o_zq:o_zg]],
                             axis=0).astype(BF16)
    col_zp = ZR_W // pool_w
    col_zg = (ZR_W + pool_w) // d
    col_zq = (ZR_W + pool_w + 3 * d) // xa_w
    row = lambda v: v.reshape(1, -1)
    prep_params = [row(_pack_zr(rwkv_mu[l], 0)), row(rwkv_w0[l]), _pad_rows(rwkv_w_up[l], 128).astype(BF16),
                   row(rwkv_a0[l]), _pad_rows(rwkv_a_up[l], 128).astype(BF16),
                   _pad_rows(rwkv_g_up[l], 256).astype(BF16), row(rwkv_k_k[l]), row(rwkv_k_a[l]),
                   row(rwkv_r_k[l])]
    gw = pool_group_w[l].astype(BF16)
    w_kv_t = jnp.concatenate([jnp.swapaxes(w_mem_k[l], 0, 1), jnp.swapaxes(w_mem_v[l], 0, 1)], axis=0).astype(BF16)
    wa, wb, wc, wo = (pool_out[l].astype(BF16), rwkv_out[l].astype(BF16), xattn_out[l].astype(BF16),
                      w_o[l].astype(BF16))
    g1, gm, g2, fg = row(ffn1_norm_g[l]), row(mix_norm_g[l]), row(ffn2_norm_g[l]), row(final_norm_g)
    scale = row(pool_scale[l])
    ln_g, ln_b = rwkv_ln_g[l], rwkv_ln_b[l]

    tm_p = 512
    h1_p, h1_s = _ffn(x_prompt.reshape(rows_p, d), x_sample.reshape(ns, d), tm_p, g1, *f1)
    z_p = _norm_matmul(h1_p, gm, w_in_t, tm=1024, tn=1536)
    z_s = _norm_matmul(h1_s, gm, w_in_t, tm=ns, tn=1536)
    kv = _norm_matmul(mem_prompt.reshape(nb * n_mem, d), row(mem_norm_g[l]), w_kv_t, tm=512, tn=512)
    mk_p = kv[:, :xa_w].reshape(nb, n_mem, xa_w)
    mv_p = kv[:, xa_w:].reshape(nb, n_mem, xa_w)

    pool_state = state_pool.reshape(ns, nbuf, pool_w)
    pa_p = _pool_prompt(z_p, nb, t_len, col_zp, gw, scale)
    pa_s = _pool_sample(z_s, col_zp, jnp.swapaxes(pool_state, 0, 1), gw, scale)

    prep_p = _prep_prompt(z_p, nb, t_len, prep_params)
    y_p, st_p = _wkv_chunk(prep_p[:6], nb, t_len)
    pb_p = _rwkv_post(y_p, prep_p[7], prep_p[6], row(ln_g), row(ln_b), tm=512)
    prep_s = _prep_sample(z_s, _pack_zr(state_shift.reshape(ns, ZR_TRUE), 1), prep_params)
    state_t = jnp.transpose(state_wkv.reshape(ns, n_heads, HEAD, HEAD), (1, 2, 3, 0))
    pb_s, wkv_s_t = _wkv_step(state_t, prep_s, row(ln_g), row(ln_b))

    pc_p = _xattn_prompt(z_p, nb, t_len, col_zq, mk_p, mv_p)
    q_s = z_s[:, ZR_W + pool_w + 3 * d:].reshape(ns, XA_HEADS, XA_DIM)
    pc_s = _xattn_sample(q_s, cache_mem_k.reshape(ns, n_mem * XA_HEADS, XA_DIM),
                         cache_mem_v.reshape(ns, n_mem * XA_HEADS, XA_DIM)).reshape(ns, xa_w)

    h2_p = _merge(pa_p, pb_p, pc_p, z_p, col_zg, h1_p, 256, wa, wb, wc, wo)
    h2_s = _merge(pa_s, pb_s, pc_s, z_s, col_zg, h1_s, ns, wa, wb, wc, wo)
    y_prompt, y_sample = _ffn(h2_p, h2_s, tm_p, g2, *f2, final_g=fg)

    ends = [(b + 1) * t_len for b in range(nb)]
    shift_p = _unpack_zr_cols(jnp.stack([z_p[e - 1:e, :ZR_W] for e in ends]))[None]
    pool_p = jnp.stack([z_p[e - nbuf:e, ZR_W:ZR_W + pool_w] for e in ends])[None]
    shift_s = _unpack_zr_cols(z_s[:, :ZR_W])[None, :, None, :]
    pool_s = jnp.concatenate([pool_state[:, 1:], z_s[:, None, ZR_W:ZR_W + pool_w]], axis=1)[None]
    st5 = st_p.reshape(nb, n_heads // 2, 2, HEAD, 2, HEAD)
    wkv_p = jnp.stack([st5[:, :, 0, :, 0, :], st5[:, :, 1, :, 1, :]], axis=2).reshape(1, nb, n_heads, HEAD, HEAD)
    wkv_s = jnp.transpose(wkv_s_t, (3, 0, 1, 2)).reshape(state_wkv.shape)
    mem_k_p = mk_p.reshape(1, nb, n_mem, XA_HEADS, XA_DIM)
    mem_v_p = mv_p.reshape(1, nb, n_mem, XA_HEADS, XA_DIM)
    return (y_prompt.reshape(nb, t_len, d), y_sample.reshape(ns, 1, d), mem_k_p, mem_v_p, wkv_p, shift_p, pool_p,
            wkv_s, shift_s, pool_s)
```

```python
import functools
import math

import jax
import jax.numpy as jnp
from jax import lax
from jax.experimental import pallas as pl
from jax.experimental.pallas import tpu as pltpu

F32 = jnp.float32
BF16 = jnp.bfloat16

RMS_EPS = 1e-6
GN_EPS = 64e-5
POOL_WINDOWS = (2, 4, 8, 16)
HEAD = 64
LANES = 128
SUBLANES = 8
XA_HEADS = 4
XA_DIM = 128
PAST_LEN = 16384
VMEM_LIMIT = 60 * 1024 * 1024
EXP_M05 = math.exp(-0.5)

ZR_W = 3584
RWKV_W = 1024
ZR_TRUE = 3360

NN_DIMS = (((1,), (0,)), ((), ()))
NT_DIMS = (((1,), (1,)), ((), ()))
TN_DIMS = (((0,), (0,)), ((), ()))


def _cparams(sem):
    return pltpu.CompilerParams(dimension_semantics=sem, vmem_limit_bytes=VMEM_LIMIT)


def _rms(x, g):
    ms = jnp.mean(x * x, axis=-1, keepdims=True)
    return x * lax.rsqrt(ms + RMS_EPS) * g


def _bdot(a, b, dims=NN_DIMS):
    return lax.dot_general(a.astype(BF16), b.astype(BF16), dims, preferred_element_type=F32)


def _ffn_kernel(hp_ref, hs_ref, g_ref, fg_ref, wg_ref, wu_ref, wd_ref, wgt_ref, wut_ref, wdt_ref,
                op_ref, os_ref, xp_ref, xs_ref, *, nfull, final):
    m = pl.program_id(0)
    f = pl.program_id(1)

    def start(h_ref, x_ref, o_ref):
        x_ref[...] = _rms(h_ref[...], g_ref[...]).astype(BF16)
        o_ref[...] = jnp.zeros_like(o_ref)

    def contribution(x_ref, wg, wu, wd):
        xn = x_ref[...]
        gate = jnp.dot(xn, wg, preferred_element_type=F32)
        up = jnp.dot(xn, wu, preferred_element_type=F32)
        act = (gate * jax.nn.sigmoid(gate) * up).astype(BF16)
        return jnp.dot(act, wd, preferred_element_type=F32)

    def finish(h_ref, o_ref, last):
        out = h_ref[...] + 0.5 * (o_ref[...] + last)
        if final:
            out = _rms(out, fg_ref[...])
        o_ref[...] = out

    @pl.when(f == 0)
    def _():
        start(hp_ref, xp_ref, op_ref)

    @pl.when((f == 0) & (m == 0))
    def _():
        start(hs_ref, xs_ref, os_ref)

    @pl.when(f < nfull)
    def _():
        wg, wu, wd = wg_ref[...], wu_ref[...], wd_ref[...]
        op_ref[...] += contribution(xp_ref, wg, wu, wd)

        @pl.when(m == 0)
        def _():
            os_ref[...] += contribution(xs_ref, wg, wu, wd)

    @pl.when(f == nfull)
    def _():
        wg, wu, wd = wgt_ref[...], wut_ref[...], wdt_ref[...]
        finish(hp_ref, op_ref, contribution(xp_ref, wg, wu, wd))

        @pl.when(m == 0)
        def _():
            finish(hs_ref, os_ref, contribution(xs_ref, wg, wu, wd))


def _ffn(hp, hs, tm, g, wg, wu, wd, final_g=None, tf=1024):
    mp, d = hp.shape
    ms = hs.shape[0]
    d_ff = wg.shape[1]
    nfull, tail = divmod(d_ff, tf)
    assert tail > 0 and tail % LANES == 0 and mp % tm == 0
    last_main = nfull - 1
    split = nfull * tf
    wgt, wut, wdt = wg[:, split:], wu[:, split:], wd[split:]
    final = final_g is not None
    vec = pl.BlockSpec((1, d), lambda i, f: (0, 0))
    once = pl.Buffered(1)
    in_specs = [
        pl.BlockSpec((tm, d), lambda i, f: (i, 0), pipeline_mode=once),
        pl.BlockSpec((ms, d), lambda i, f: (0, 0), pipeline_mode=once),
        vec, vec,
        pl.BlockSpec((d, tf), lambda i, f: (0, jnp.minimum(f, last_main))),
        pl.BlockSpec((d, tf), lambda i, f: (0, jnp.minimum(f, last_main))),
        pl.BlockSpec((tf, d), lambda i, f: (jnp.minimum(f, last_main), 0)),
        pl.BlockSpec((d, tail), lambda i, f: (0, 0), pipeline_mode=once),
        pl.BlockSpec((d, tail), lambda i, f: (0, 0), pipeline_mode=once),
        pl.BlockSpec((tail, d), lambda i, f: (0, 0), pipeline_mode=once),
    ]
    return pl.pallas_call(
        functools.partial(_ffn_kernel, nfull=nfull, final=final),
        grid=(mp // tm, nfull + 1),
        in_specs=in_specs,
        out_specs=[pl.BlockSpec((tm, d), lambda i, f: (i, 0)), pl.BlockSpec((ms, d), lambda i, f: (0, 0))],
        out_shape=[jax.ShapeDtypeStruct((mp, d), F32), jax.ShapeDtypeStruct((ms, d), F32)],
        scratch_shapes=[pltpu.VMEM((tm, d), BF16), pltpu.VMEM((ms, d), BF16)],
        compiler_params=_cparams(("parallel", "arbitrary")),
        name="ffn",
    )(hp, hs, g, g if final_g is None else final_g, wg, wu, wd, wgt, wut, wdt)


def _norm_matmul_kernel(h_ref, g_ref, wt_ref, o_ref, xn_ref):
    @pl.when(pl.program_id(1) == 0)
    def _():
        xn_ref[...] = _rms(h_ref[...], g_ref[...]).astype(BF16)

    o_ref[...] = lax.dot_general(xn_ref[...], wt_ref[...], NT_DIMS, preferred_element_type=F32)


def _norm_matmul(h, g, wt, tm, tn):
    m, d = h.shape
    n = wt.shape[0]
    return pl.pallas_call(
        _norm_matmul_kernel,
        grid=(m // tm, n // tn),
        in_specs=[
            pl.BlockSpec((tm, d), lambda i, j: (i, 0)),
            pl.BlockSpec((1, d), lambda i, j: (0, 0)),
            pl.BlockSpec((tn, d), lambda i, j: (j, 0)),
        ],
        out_specs=pl.BlockSpec((tm, tn), lambda i, j: (i, j)),
        out_shape=jax.ShapeDtypeStruct((m, n), F32),
        scratch_shapes=[pltpu.VMEM((tm, d), BF16)],
        compiler_params=_cparams(("parallel", "arbitrary")),
        name="norm_matmul",
    )(h, g, wt)


def _pool_mix(pooled_groups, gw_ref, scale_ref, o_ref):
    for gi, pooled in enumerate(pooled_groups):
        sl = slice(gi * LANES, (gi + 1) * LANES)
        mixed = jnp.dot(pooled.astype(BF16), gw_ref[gi], preferred_element_type=F32)
        o_ref[:, sl] = (mixed * scale_ref[:, sl]).astype(o_ref.dtype)


def _pool_prompt_kernel(zp_ref, gw_ref, scale_ref, o_ref, ext_ref, *, tt):
    t = pl.program_id(1)
    hist = 16

    @pl.when(t == 0)
    def _():
        ext_ref[0:hist, :] = jnp.zeros((hist, ext_ref.shape[1]), F32)

    x = zp_ref[...]
    ext_ref[hist:hist + tt, :] = x
    pos = t * tt + lax.broadcasted_iota(jnp.int32, (tt, LANES), 0)
    groups = []
    for gi, w in enumerate(POOL_WINDOWS):
        sl = slice(gi * LANES, (gi + 1) * LANES)
        acc = x[:, sl]
        for k in range(1, w):
            acc = acc + ext_ref[hist - k:hist - k + tt, sl]
        cnt = jnp.minimum(pos + 1, w).astype(F32)
        groups.append(acc / cnt - x[:, sl])
    _pool_mix(groups, gw_ref, scale_ref, o_ref)
    ext_ref[0:hist, :] = ext_ref[tt:tt + hist, :]


def _pool_prompt(z, nb, t_len, col_blk, gw, scale, tt=256):
    nt = t_len // tt
    pw = gw.shape[0] * LANES
    return pl.pallas_call(
        functools.partial(_pool_prompt_kernel, tt=tt),
        grid=(nb, nt),
        in_specs=[
            pl.BlockSpec((tt, pw), lambda b, t: (b * nt + t, col_blk)),
            pl.BlockSpec(gw.shape, lambda b, t: (0, 0, 0)),
            pl.BlockSpec((1, pw), lambda b, t: (0, 0)),
        ],
        out_specs=pl.BlockSpec((tt, pw), lambda b, t: (b * nt + t, 0)),
        out_shape=jax.ShapeDtypeStruct((nb * t_len, pw), BF16),
        scratch_shapes=[pltpu.VMEM((tt + 16, pw), F32)],
        compiler_params=_cparams(("parallel", "arbitrary")),
        name="pool_prompt",
    )(z, gw, scale)


def _pool_sample_kernel(zp_ref, buf_ref, gw_ref, scale_ref, o_ref):
    x = zp_ref[...]
    nbuf = buf_ref.shape[0]
    groups = []
    for gi, w in enumerate(POOL_WINDOWS):
        sl = slice(gi * LANES, (gi + 1) * LANES)
        acc = x[:, sl]
        for k in range(1, w):
            acc = acc + buf_ref[nbuf - k, :, sl]
        cnt = float(min(PAST_LEN + 1, w))
        groups.append(acc / cnt - x[:, sl])
    _pool_mix(groups, gw_ref, scale_ref, o_ref)


def _pool_sample(z, col_blk, buf_t, gw, scale):
    nrows = z.shape[0]
    pw = gw.shape[0] * LANES
    return pl.pallas_call(
        _pool_sample_kernel,
        grid=(1,),
        in_specs=[
            pl.BlockSpec((nrows, pw), lambda i: (0, col_blk)),
            pl.BlockSpec(buf_t.shape, lambda i: (0, 0, 0)),
            pl.BlockSpec(gw.shape, lambda i: (0, 0, 0)),
            pl.BlockSpec((1, pw), lambda i: (0, 0)),
        ],
        out_specs=pl.BlockSpec((nrows, pw), lambda i: (0, 0)),
        out_shape=jax.ShapeDtypeStruct((nrows, pw), BF16),
        compiler_params=_cparams(("arbitrary",)),
        name="pool_sample",
    )(z, buf_t, gw, scale)


def _head_sum(x):
    head_a = lax.broadcasted_iota(jnp.int32, (x.shape[0], LANES), 1) < HEAD
    cols = []
    for c in range(x.shape[1] // LANES):
        t = x[:, c * LANES:(c + 1) * LANES]
        sa = jnp.sum(jnp.where(head_a, t, 0.0), axis=-1, keepdims=True)
        sb = jnp.sum(jnp.where(head_a, 0.0, t), axis=-1, keepdims=True)
        cols.append(jnp.where(head_a, sa, sb))
    return jnp.concatenate(cols, axis=1)


def _prep_math(x, prev, p_refs):
    mu_ref, w0_ref, wup_ref, a0_ref, aup_ref, gup_ref, kk_ref, ka_ref, rk_ref = p_refs
    xm = x + (prev - x) * mu_ref[...]
    w = RWKV_W
    r = xm[:, 0:w]
    k = xm[:, w:2 * w]
    v = xm[:, 2 * w:3 * w]
    wl = xm[:, 3 * w:3 * w + 128]
    al = xm[:, 3 * w + 128:3 * w + 256]
    gl = xm[:, 3 * w + 256:3 * w + 512]
    dw = w0_ref[...] + jnp.dot(jnp.tanh(wl).astype(BF16), wup_ref[...], preferred_element_type=F32)
    lw = -EXP_M05 * jax.nn.sigmoid(dw)
    a = jax.nn.sigmoid(a0_ref[...] + jnp.dot(al.astype(BF16), aup_ref[...], preferred_element_type=F32))
    g = jnp.dot(jax.nn.sigmoid(gl).astype(BF16), gup_ref[...], preferred_element_type=F32)
    kk = k * kk_ref[...]
    kk = kk * lax.rsqrt(jnp.maximum(_head_sum(kk * kk), 1e-24))
    kmod = k * (1.0 + (a - 1.0) * ka_ref[...])
    bonus = _head_sum(r * kmod * rk_ref[...]) * v
    return r, lw, kmod, v, -kk, kk * a, g, bonus


def _group_norm_gate(y, bonus, g, lng_ref, lnb_ref):
    d = y - _head_sum(y) * (1.0 / HEAD)
    var = _head_sum(d * d) * (1.0 / HEAD)
    yn = d * lax.rsqrt(var + GN_EPS) * lng_ref[...] + lnb_ref[...]
    return (yn + bonus) * g


def _prep_sample_kernel(zr_ref, prev_ref, *refs):
    p_refs, out_refs = refs[:9], refs[9:17]
    for o_ref, val in zip(out_refs, _prep_math(zr_ref[...], prev_ref[...], p_refs)):
        o_ref[...] = val


def _prep_param_specs(params):
    zero = (lambda *idx: (0, 0))
    return [pl.BlockSpec(p.shape, zero) for p in params]


def _prep_sample(z, prev, params):
    nrows = z.shape[0]
    out_spec = pl.BlockSpec((nrows, RWKV_W), lambda i: (0, 0))
    return pl.pallas_call(
        _prep_sample_kernel,
        grid=(1,),
        in_specs=[pl.BlockSpec((nrows, ZR_W), lambda i: (0, 0)),
                  pl.BlockSpec((nrows, ZR_W), lambda i: (0, 0))] + _prep_param_specs(params),
        out_specs=[out_spec] * 8,
        out_shape=[jax.ShapeDtypeStruct((nrows, RWKV_W), F32)] * 8,
        compiler_params=_cparams(("arbitrary",)),
        name="prep_sample",
    )(z, prev, *params)


def _split3(x):
    hi = x.astype(BF16)
    rest = x - hi.astype(F32)
    mid = rest.astype(BF16)
    lo = (rest - mid.astype(F32)).astype(BF16)
    return hi, mid, lo


def _select_dot(sel, x):
    sel = sel.astype(BF16)
    hi, mid, lo = _split3(x)
    return _bdot(sel, hi) + (_bdot(sel, mid) + _bdot(sel, lo))


def _chunk_scan(r, lw, k, v, a, b, s_ref, c_len):
    n2 = 2 * c_len
    assert n2 == LANES
    row = lax.broadcasted_iota(jnp.int32, (n2, n2), 0)
    col = lax.broadcasted_iota(jnp.int32, (n2, n2), 1)
    tr = row & (c_len - 1)
    tc = col & (c_len - 1)
    strict = tr > tc
    incl = tr >= tc
    tri = (lax.broadcasted_iota(jnp.int32, (c_len, c_len), 0)
           >= lax.broadcasted_iota(jnp.int32, (c_len, c_len), 1))
    head_a = lax.broadcasted_iota(jnp.int32, (c_len, LANES), 1) < HEAD

    def stack(x):
        return jnp.concatenate([jnp.where(head_a, x, 0.0), jnp.where(head_a, 0.0, x)], axis=0).astype(BF16)

    prs = range(r.shape[1] // LANES)
    cat = jnp.concatenate
    sls = [slice(q * LANES, (q + 1) * LANES) for q in prs]
    lwq = [lw[:, sl] for sl in sls]
    cum = [_select_dot(tri, x) for x in lwq]
    tot = [x[c_len - 1:c_len, :] for x in cum]
    xr = [stack(r[:, sls[q]] * jnp.exp(cum[q])) for q in prs]
    xa = [stack(a[:, sls[q]] * jnp.exp(cum[q] - lwq[q])) for q in prs]
    e_neg = [jnp.exp(-x) for x in cum]
    e_rem = [jnp.exp(tot[q] - cum[q]) for q in prs]
    yb = [stack(b[:, sls[q]] * e_neg[q]) for q in prs]
    yk = [stack(k[:, sls[q]] * e_neg[q]) for q in prs]
    zb = [stack(b[:, sls[q]] * e_rem[q]) for q in prs]
    zk = [stack(k[:, sls[q]] * e_rem[q]) for q in prs]
    vs = [stack(v[:, sl]) for sl in sls]

    g = [_bdot(cat([xa[q], xr[q]], axis=0), cat([yb[q], yk[q]], axis=0), NT_DIMS) for q in prs]
    m_ab = [jnp.where(strict, x[0:n2, 0:n2], 0.0) for x in g]
    m_ak = [jnp.where(strict, x[0:n2, n2:2 * n2], 0.0).astype(BF16) for x in g]
    n_rb = [jnp.where(incl, x[n2:2 * n2, 0:n2], 0.0) for x in g]
    n_rk = [jnp.where(incl, x[n2:2 * n2, n2:2 * n2], 0.0).astype(BF16) for x in g]

    s_old = [s_ref[q] for q in prs]
    lhs = [cat([cat([xa[q], m_ak[q]], axis=1), cat([xr[q], n_rk[q]], axis=1)], axis=0) for q in prs]
    xy0 = [_bdot(lhs[q], cat([s_old[q].T.astype(BF16), vs[q]], axis=0)) for q in prs]

    x = [xy0[q][0:n2] for q in prs]
    mk = m_ab
    nlev = int(math.log2(c_len))
    for lev in range(nlev):
        if lev < nlev - 1:
            res = [_bdot(mk[q], cat([mk[q], x[q]], axis=1)) for q in prs]
            mk = [t[:, 0:n2] for t in res]
            x = [x[q] + res[q][:, n2:2 * n2] for q in prs]
        else:
            x = [x[q] + _bdot(mk[q], x[q]) for q in prs]
    u = [t.astype(BF16) for t in x]

    y_st = [xy0[q][n2:2 * n2] + _bdot(n_rb[q], u[q]) for q in prs]
    for q in prs:
        s_ref[q] = s_old[q] * jnp.exp(tot[q]) + _bdot(cat([u[q], vs[q]], axis=0), cat([zb[q], zk[q]], axis=0),
                                                     TN_DIMS)
    return cat([t[0:c_len] + t[c_len:n2] for t in y_st], axis=1)


def _rwkv_prompt_kernel(zr_ref, *refs, nc):
    p_refs = refs[:9]
    lng_ref, lnb_ref, yb_ref, sout_ref, s_ref, carry_ref = refs[9:]
    c = pl.program_id(1)

    @pl.when(c == 0)
    def _():
        s_ref[...] = jnp.zeros_like(s_ref)
        carry_ref[...] = jnp.zeros_like(carry_ref)

    x = zr_ref[...]
    rolled = pltpu.roll(x, 1, axis=0)
    first = lax.broadcasted_iota(jnp.int32, x.shape, 0) == 0
    prev = jnp.where(first, carry_ref[0:1, :], rolled)
    carry_ref[...] = rolled[0:SUBLANES, :]
    r, lw, k, v, a, b, g, bonus = _prep_math(x, prev, p_refs)
    y = _chunk_scan(r, lw, k, v, a, b, s_ref, x.shape[0])
    yb_ref[...] = _group_norm_gate(y, bonus, g, lng_ref, lnb_ref).astype(yb_ref.dtype)

    @pl.when(c == nc - 1)
    def _():
        sout_ref[0] = s_ref[...]


def _rwkv_prompt(z, nb, t_len, params, ln_g, ln_b, c_len=64):
    nc = t_len // c_len
    npair = RWKV_W // LANES
    par = pl.BlockSpec((1, RWKV_W), lambda b, c: (0, 0))
    return pl.pallas_call(
        functools.partial(_rwkv_prompt_kernel, nc=nc),
        grid=(nb, nc),
        in_specs=[pl.BlockSpec((c_len, ZR_W), lambda b, c: (b * nc + c, 0))] + _prep_param_specs(params) + [par, par],
        out_specs=[pl.BlockSpec((c_len, RWKV_W), lambda b, c: (b * nc + c, 0)),
                   pl.BlockSpec((1, npair, LANES, LANES), lambda b, c: (b, 0, 0, 0))],
        out_shape=[jax.ShapeDtypeStruct((nb * t_len, RWKV_W), BF16),
                   jax.ShapeDtypeStruct((nb, npair, LANES, LANES), F32)],
        scratch_shapes=[pltpu.VMEM((npair, LANES, LANES), F32), pltpu.VMEM((SUBLANES, ZR_W), F32)],
        compiler_params=_cparams(("parallel", "arbitrary")),
        name="rwkv_prompt",
    )(z, *params, ln_g, ln_b)


def _wkv_step_kernel(s_ref, r_ref, lw_ref, k_ref, v_ref, a_ref, b_ref, g_ref, bonus_ref, lng_ref, lnb_ref,
                     yb_ref, so_ref, vt_scr, y_scr):
    rt, wt, kt, at, bt = (x[...].T for x in (r_ref, lw_ref, k_ref, a_ref, b_ref))
    wt = jnp.exp(wt)
    vt_scr[...] = v_ref[...].T
    for hh in range(2):
        rows = slice(hh * HEAD, (hh + 1) * HEAD)
        r, w, k, a, b = (x[rows, :] for x in (rt, wt, kt, at, bt))

        def body(i, carry):
            si = s_ref[hh, i]
            sa = jnp.sum(si * a, axis=0, keepdims=True)
            vi = vt_scr[pl.ds(hh * HEAD + i, 1), :]
            s2 = si * w + sa * b + vi * k
            so_ref[hh, i] = s2
            y_scr[pl.ds(hh * HEAD + i, 1), :] = jnp.sum(s2 * r, axis=0, keepdims=True)
            return carry

        lax.fori_loop(0, HEAD, body, 0)

    outs = []
    for hh in range(2):
        y = y_scr[hh * HEAD:(hh + 1) * HEAD, :]
        d = y - jnp.mean(y, axis=0, keepdims=True)
        var = jnp.mean(d * d, axis=0, keepdims=True)
        outs.append(d * lax.rsqrt(var + GN_EPS))
    yn = jnp.concatenate(outs, axis=0).T * lng_ref[...] + lnb_ref[...]
    yb_ref[...] = ((yn + bonus_ref[...]) * g_ref[...]).astype(yb_ref.dtype)


def _wkv_step(state_t, vecs, ln_g, ln_b):
    nh, _, _, nb = state_t.shape
    st = pl.BlockSpec((2, HEAD, HEAD, nb), lambda p: (p, 0, 0, 0))
    vec = pl.BlockSpec((nb, 2 * HEAD), lambda p: (0, p))
    par = pl.BlockSpec((1, 2 * HEAD), lambda p: (0, p))
    return pl.pallas_call(
        _wkv_step_kernel,
        grid=(nh // 2,),
        in_specs=[st] + [vec] * 8 + [par, par],
        out_specs=[vec, st],
        out_shape=[jax.ShapeDtypeStruct((nb, nh * HEAD), BF16), jax.ShapeDtypeStruct(state_t.shape, F32)],
        scratch_shapes=[pltpu.VMEM((2 * HEAD, nb), F32), pltpu.VMEM((2 * HEAD, nb), F32)],
        compiler_params=_cparams(("parallel",)),
        name="wkv_step",
    )(state_t, *vecs, ln_g, ln_b)


def _xattn_prompt_kernel(q_ref, k_ref, v_ref, o_ref):
    scale = XA_DIM ** -0.5
    q = q_ref[...]
    for h in range(XA_HEADS):
        sl = slice(h * XA_DIM, (h + 1) * XA_DIM)
        s = lax.dot_general(q[:, sl].astype(BF16), k_ref[0, :, sl].astype(BF16),
                            NT_DIMS, preferred_element_type=F32) * scale
        p = jnp.exp(s - jnp.max(s, axis=-1, keepdims=True))
        den = jnp.sum(p, axis=-1, keepdims=True)
        o = jnp.dot(p.astype(BF16), v_ref[0, :, sl].astype(BF16), preferred_element_type=F32)
        o_ref[:, sl] = (o / den).astype(o_ref.dtype)


def _xattn_prompt(z, nb, t_len, col_blk, mk, mv, tq=512):
    nt = t_len // tq
    xw = XA_HEADS * XA_DIM
    nmem = mk.shape[1]
    kv = pl.BlockSpec((1, nmem, xw), lambda b, t: (b, 0, 0))
    return pl.pallas_call(
        _xattn_prompt_kernel,
        grid=(nb, nt),
        in_specs=[pl.BlockSpec((tq, xw), lambda b, t: (b * nt + t, col_blk)), kv, kv],
        out_specs=pl.BlockSpec((tq, xw), lambda b, t: (b * nt + t, 0)),
        out_shape=jax.ShapeDtypeStruct((nb * t_len, xw), BF16),
        compiler_params=_cparams(("parallel", "parallel")),
        name="xattn_prompt",
    )(z, mk, mv)


def _xattn_sample_kernel(q_ref, k_ref, v_ref, o_ref):
    bb = q_ref.shape[0]
    nrow = k_ref.shape[1] // SUBLANES
    full = (bb, nrow, SUBLANES, XA_DIM)
    q = q_ref[...] * (XA_DIM ** -0.5)
    q8 = jnp.concatenate([q, q], axis=1)[:, None]
    k = k_ref[...].reshape(full)
    s = jnp.broadcast_to(jnp.sum(k * q8, axis=-1, keepdims=True), full)
    mx = jnp.max(s, axis=1, keepdims=True)
    mx = jnp.maximum(mx, pltpu.roll(mx, XA_HEADS, axis=2))
    p = jnp.exp(s - mx)
    den = jnp.sum(p, axis=1, keepdims=True)
    den = den + pltpu.roll(den, XA_HEADS, axis=2)
    o = jnp.sum(p * v_ref[...].reshape(full), axis=1, keepdims=True)
    o = o + pltpu.roll(o, XA_HEADS, axis=2)
    o_ref[...] = (o / den)[:, 0, 0:XA_HEADS, :].astype(o_ref.dtype)


def _xattn_sample(q3, mk, mv, bb=8):
    nb, rows, _ = mk.shape
    kv = pl.BlockSpec((bb, rows, XA_DIM), lambda i: (i, 0, 0))
    qs = pl.BlockSpec((bb, XA_HEADS, XA_DIM), lambda i: (i, 0, 0))
    return pl.pallas_call(
        _xattn_sample_kernel,
        grid=(nb // bb,),
        in_specs=[qs, kv, kv],
        out_specs=qs,
        out_shape=jax.ShapeDtypeStruct((nb, XA_HEADS, XA_DIM), BF16),
        compiler_params=_cparams(("parallel",)),
        name="xattn_sample",
    )(q3, mk, mv)


def _merge_kernel(pa_ref, pb_ref, pc_ref, g0_ref, g1_ref, g2_ref, h_ref, wa_ref, wb_ref, wc_ref, wo_ref, o_ref):
    oa = jnp.dot(pa_ref[...], wa_ref[...], preferred_element_type=F32)
    ob = jnp.dot(pb_ref[...], wb_ref[...], preferred_element_type=F32)
    oc = jnp.dot(pc_ref[...], wc_ref[...], preferred_element_type=F32)
    merged = (jax.nn.sigmoid(g0_ref[...]) * oa + jax.nn.sigmoid(g1_ref[...]) * ob
              + jax.nn.sigmoid(g2_ref[...]) * oc)
    o_ref[...] = h_ref[...] + jnp.dot(merged.astype(BF16), wo_ref[...], preferred_element_type=F32)


def _merge(pa, pb, pc, z, zg_blk0, h, tm, wa, wb, wc, wo):
    m, d = h.shape
    const = lambda i: (0, 0)
    resident = lambda w: pl.BlockSpec(w.shape, const, pipeline_mode=pl.Buffered(1))
    in_specs = [
        pl.BlockSpec((tm, pa.shape[1]), lambda i: (i, 0)),
        pl.BlockSpec((tm, pb.shape[1]), lambda i: (i, 0)),
        pl.BlockSpec((tm, pc.shape[1]), lambda i: (i, 0)),
        pl.BlockSpec((tm, d), lambda i: (i, zg_blk0)),
        pl.BlockSpec((tm, d), lambda i: (i, zg_blk0 + 1)),
        pl.BlockSpec((tm, d), lambda i: (i, zg_blk0 + 2)),
        pl.BlockSpec((tm, d), lambda i: (i, 0)),
        resident(wa), resident(wb), resident(wc), resident(wo),
    ]
    return pl.pallas_call(
        _merge_kernel,
        grid=(m // tm,),
        in_specs=in_specs,
        out_specs=pl.BlockSpec((tm, d), lambda i: (i, 0)),
        out_shape=jax.ShapeDtypeStruct((m, d), F32),
        compiler_params=_cparams(("parallel",)),
        name="merge",
    )(pa, pb, pc, z, z, z, h, wa, wb, wc, wo)


def _pack_zr(x, axis):
    w = RWKV_W
    take = lambda a, b: lax.slice_in_dim(x, a, b, axis=axis)

    def pad(n):
        shape = list(x.shape)
        shape[axis] = n
        return jnp.zeros(shape, x.dtype)

    return jnp.concatenate([take(0, 3 * w + 64), pad(64), take(3 * w + 64, 3 * w + 128), pad(64),
                            take(3 * w + 128, ZR_TRUE), pad(96)], axis=axis)


def _unpack_zr_cols(x):
    w = RWKV_W
    return jnp.concatenate([x[..., :3 * w + 64], x[..., 3 * w + 128:3 * w + 192], x[..., 3 * w + 256:3 * w + 416]],
                           axis=-1)


def _pad_rows(x, n):
    return jnp.concatenate([x, jnp.zeros((n - x.shape[0],) + x.shape[1:], x.dtype)], axis=0)


def kernel(x_prompt, x_sample, mem_prompt, cache_mem_k, cache_mem_v, state_wkv, state_shift, state_pool,
           ffn1_norm_g, ffn1_w_gate, ffn1_w_up, ffn1_w_down, mix_norm_g, w_in,
           pool_group_w, pool_scale, pool_out,
           rwkv_mu, rwkv_w0, rwkv_w_up, rwkv_a0, rwkv_a_up, rwkv_g_up, rwkv_k_k, rwkv_k_a, rwkv_r_k,
           rwkv_ln_g, rwkv_ln_b, rwkv_out,
           mem_norm_g, w_mem_k, w_mem_v, xattn_out, w_o,
           ffn2_norm_g, ffn2_w_gate, ffn2_w_up, ffn2_w_down, final_norm_g):
    nb, t_len, d = x_prompt.shape
    ns = x_sample.shape[0]
    assert w_in.shape[0] == 1 and x_sample.shape[1] == 1
    n_mem = mem_prompt.shape[1]
    pool_w = pool_out.shape[1]
    xa_w = xattn_out.shape[1]
    n_heads = RWKV_W // HEAD
    nbuf = state_pool.shape[2]
    rows_p = nb * t_len
    l = 0

    d_ff = ffn1_w_gate.shape[2]
    f1 = [w.reshape(w.shape[1:]).astype(BF16) for w in (ffn1_w_gate, ffn1_w_up, ffn1_w_down)]
    f2 = [w.reshape(w.shape[1:]).astype(BF16) for w in (ffn2_w_gate, ffn2_w_up, ffn2_w_down)]
    wit = jnp.swapaxes(w_in[l], 0, 1)
    o_zr, o_zq, o_zg = pool_w, pool_w + ZR_TRUE, pool_w + ZR_TRUE + xa_w
    w_in_t = jnp.concatenate([_pack_zr(wit[o_zr:o_zq], 0), wit[:o_zr], wit[o_zg:], wit[o_zq:o_zg]],
                             axis=0).astype(BF16)
    col_zp = ZR_W // pool_w
    col_zg = (ZR_W + pool_w) // d
    col_zq = (ZR_W + pool_w + 3 * d) // xa_w
    row = lambda v: v.reshape(1, -1)
    prep_params = [row(_pack_zr(rwkv_mu[l], 0)), row(rwkv_w0[l]), _pad_rows(rwkv_w_up[l], 128).astype(BF16),
                   row(rwkv_a0[l]), _pad_rows(rwkv_a_up[l], 128).astype(BF16),
                   _pad_rows(rwkv_g_up[l], 256).astype(BF16), row(rwkv_k_k[l]), row(rwkv_k_a[l]),
                   row(rwkv_r_k[l])]
    gw = pool_group_w[l].astype(BF16)
    w_kv_t = jnp.concatenate([jnp.swapaxes(w_mem_k[l], 0, 1), jnp.swapaxes(w_mem_v[l], 0, 1)], axis=0).astype(BF16)
    wa, wb, wc, wo = (pool_out[l].astype(BF16), rwkv_out[l].astype(BF16), xattn_out[l].astype(BF16),
                      w_o[l].astype(BF16))
    g1, gm, g2, fg = row(ffn1_norm_g[l]), row(mix_norm_g[l]), row(ffn2_norm_g[l]), row(final_norm_g)
    scale = row(pool_scale[l])
    ln_g, ln_b = rwkv_ln_g[l], rwkv_ln_b[l]

    tm_p = 512
    h1_p, h1_s = _ffn(x_prompt.reshape(rows_p, d), x_sample.reshape(ns, d), tm_p, g1, *f1)
    z_p = _norm_matmul(h1_p, gm, w_in_t, tm=1024, tn=1536)
    z_s = _norm_matmul(h1_s, gm, w_in_t, tm=ns, tn=1536)
    kv = _norm_matmul(mem_prompt.reshape(nb * n_mem, d), row(mem_norm_g[l]), w_kv_t, tm=512, tn=512)
    mk_p = kv[:, :xa_w].reshape(nb, n_mem, xa_w)
    mv_p = kv[:, xa_w:].reshape(nb, n_mem, xa_w)

    pool_state = state_pool.reshape(ns, nbuf, pool_w)
    pa_p = _pool_prompt(z_p, nb, t_len, col_zp, gw, scale)
    pa_s = _pool_sample(z_s, col_zp, jnp.swapaxes(pool_state, 0, 1), gw, scale)

    pb_p, st_p = _rwkv_prompt(z_p, nb, t_len, prep_params, row(ln_g), row(ln_b))
    prep_s = _prep_sample(z_s, _pack_zr(state_shift.reshape(ns, ZR_TRUE), 1), prep_params)
    state_t = jnp.transpose(state_wkv.reshape(ns, n_heads, HEAD, HEAD), (1, 2, 3, 0))
    pb_s, wkv_s_t = _wkv_step(state_t, prep_s, row(ln_g), row(ln_b))

    pc_p = _xattn_prompt(z_p, nb, t_len, col_zq, mk_p, mv_p)
    q_s = z_s[:, ZR_W + pool_w + 3 * d:].reshape(ns, XA_HEADS, XA_DIM)
    pc_s = _xattn_sample(q_s, cache_mem_k.reshape(ns, n_mem * XA_HEADS, XA_DIM),
                         cache_mem_v.reshape(ns, n_mem * XA_HEADS, XA_DIM)).reshape(ns, xa_w)

    h2_p = _merge(pa_p, pb_p, pc_p, z_p, col_zg, h1_p, 256, wa, wb, wc, wo)
    h2_s = _merge(pa_s, pb_s, pc_s, z_s, col_zg, h1_s, ns, wa, wb, wc, wo)
    y_prompt, y_sample = _ffn(h2_p, h2_s, tm_p, g2, *f2, final_g=fg)

    ends = [(b + 1) * t_len for b in range(nb)]
    shift_p = _unpack_zr_cols(jnp.stack([z_p[e - 1:e, :ZR_W] for e in ends]))[None]
    pool_p = jnp.stack([z_p[e - nbuf:e, ZR_W:ZR_W + pool_w] for e in ends])[None]
    shift_s = _unpack_zr_cols(z_s[:, :ZR_W])[None, :, None, :]
    pool_s = jnp.concatenate([pool_state[:, 1:], z_s[:, None, ZR_W:ZR_W + pool_w]], axis=1)[None]
    st5 = st_p.reshape(nb, n_heads // 2, 2, HEAD, 2, HEAD)
    wkv_p = jnp.stack([st5[:, :, 0, :, 0, :], st5[:, :, 1, :, 1, :]], axis=2).reshape(1, nb, n_heads, HEAD, HEAD)
    wkv_s = jnp.transpose(wkv_s_t, (3, 0, 1, 2)).reshape(state_wkv.shape)
    mem_k_p = mk_p.reshape(1, nb, n_mem, XA_HEADS, XA_DIM)
    mem_v_p = mv_p.reshape(1, nb, n_mem, XA_HEADS, XA_DIM)
    return (y_prompt.reshape(nb, t_len, d), y_sample.reshape(ns, 1, d), mem_k_p, mem_v_p, wkv_p, shift_p, pool_p,
            wkv_s, shift_s, pool_s)
```

```python
import functools
import math

import jax
import jax.numpy as jnp
from jax import lax
from jax.experimental import pallas as pl
from jax.experimental.pallas import tpu as pltpu

F32 = jnp.float32
BF16 = jnp.bfloat16

RMS_EPS = 1e-6
GN_EPS = 64e-5
POOL_WINDOWS = (2, 4, 8, 16)
HEAD = 64
LANES = 128
SUBLANES = 8
XA_HEADS = 4
XA_DIM = 128
PAST_LEN = 16384
VMEM_LIMIT = 60 * 1024 * 1024
EXP_M05 = math.exp(-0.5)

ZR_W = 3584
RWKV_W = 1024
ZR_TRUE = 3360

NN_DIMS = (((1,), (0,)), ((), ()))
NT_DIMS = (((1,), (1,)), ((), ()))
TN_DIMS = (((0,), (0,)), ((), ()))


def _cparams(sem):
    return pltpu.CompilerParams(dimension_semantics=sem, vmem_limit_bytes=VMEM_LIMIT)


def _rms(x, g):
    ms = jnp.mean(x * x, axis=-1, keepdims=True)
    return x * lax.rsqrt(ms + RMS_EPS) * g


def _bdot(a, b, dims=NN_DIMS):
    return lax.dot_general(a.astype(BF16), b.astype(BF16), dims, preferred_element_type=F32)


def _ffn_kernel(hp_ref, hs_ref, g_ref, fg_ref, wg_ref, wu_ref, wd_ref, wgt_ref, wut_ref, wdt_ref,
                op_ref, os_ref, xp_ref, xs_ref, ap_ref, as_ref, *, nfull, final):
    m = pl.program_id(0)
    f = pl.program_id(1)

    def start(h_ref, x_ref, acc_ref):
        x_ref[...] = _rms(h_ref[...], g_ref[...]).astype(BF16)
        acc_ref[...] = jnp.zeros_like(acc_ref)

    def contribution(x_ref, wg, wu, wd):
        xn = x_ref[...]
        gate = jnp.dot(xn, wg, preferred_element_type=F32)
        up = jnp.dot(xn, wu, preferred_element_type=F32)
        act = (gate * jax.nn.sigmoid(gate) * up).astype(BF16)
        return jnp.dot(act, wd, preferred_element_type=F32)

    def finish(h_ref, acc_ref, o_ref, last):
        out = h_ref[...] + 0.5 * (acc_ref[...] + last)
        if final:
            out = _rms(out, fg_ref[...])
        o_ref[...] = out

    @pl.when(f == 0)
    def _():
        start(hp_ref, xp_ref, ap_ref)

    @pl.when((f == 0) & (m == 0))
    def _():
        start(hs_ref, xs_ref, as_ref)

    @pl.when(f < nfull)
    def _():
        wg, wu, wd = wg_ref[...], wu_ref[...], wd_ref[...]
        ap_ref[...] += contribution(xp_ref, wg, wu, wd)

        @pl.when(m == 0)
        def _():
            as_ref[...] += contribution(xs_ref, wg, wu, wd)

    @pl.when(f == nfull)
    def _():
        wg, wu, wd = wgt_ref[...], wut_ref[...], wdt_ref[...]
        finish(hp_ref, ap_ref, op_ref, contribution(xp_ref, wg, wu, wd))

        @pl.when(m == 0)
        def _():
            finish(hs_ref, as_ref, os_ref, contribution(xs_ref, wg, wu, wd))


def _ffn(hp, hs, tm, g, wg, wu, wd, final_g=None, tf=512):
    mp, d = hp.shape
    ms = hs.shape[0]
    d_ff = wg.shape[1]
    nfull, tail = divmod(d_ff, tf)
    assert tail > 0 and tail % LANES == 0 and mp % tm == 0
    last_main = nfull - 1
    split = nfull * tf
    wgt, wut, wdt = wg[:, split:], wu[:, split:], wd[split:]
    final = final_g is not None
    vec = pl.BlockSpec((1, d), lambda i, f: (0, 0))
    once = pl.Buffered(1)
    in_specs = [
        pl.BlockSpec((tm, d), lambda i, f: (i, 0)),
        pl.BlockSpec((ms, d), lambda i, f: (0, 0), pipeline_mode=once),
        vec, vec,
        pl.BlockSpec((d, tf), lambda i, f: (0, jnp.minimum(f, last_main))),
        pl.BlockSpec((d, tf), lambda i, f: (0, jnp.minimum(f, last_main))),
        pl.BlockSpec((tf, d), lambda i, f: (jnp.minimum(f, last_main), 0)),
        pl.BlockSpec((d, tail), lambda i, f: (0, 0), pipeline_mode=once),
        pl.BlockSpec((d, tail), lambda i, f: (0, 0), pipeline_mode=once),
        pl.BlockSpec((tail, d), lambda i, f: (0, 0), pipeline_mode=once),
    ]
    return pl.pallas_call(
        functools.partial(_ffn_kernel, nfull=nfull, final=final),
        grid=(mp // tm, nfull + 1),
        in_specs=in_specs,
        out_specs=[pl.BlockSpec((tm, d), lambda i, f: (i, 0)), pl.BlockSpec((ms, d), lambda i, f: (0, 0))],
        out_shape=[jax.ShapeDtypeStruct((mp, d), F32), jax.ShapeDtypeStruct((ms, d), F32)],
        scratch_shapes=[pltpu.VMEM((tm, d), BF16), pltpu.VMEM((ms, d), BF16),
                        pltpu.VMEM((tm, d), F32), pltpu.VMEM((ms, d), F32)],
        compiler_params=_cparams(("parallel", "arbitrary")),
        name="ffn",
    )(hp, hs, g, g if final_g is None else final_g, wg, wu, wd, wgt, wut, wdt)


def _norm_matmul_kernel(h_ref, g_ref, wt_ref, o_ref, xn_ref):
    @pl.when(pl.program_id(1) == 0)
    def _():
        xn_ref[...] = _rms(h_ref[...], g_ref[...]).astype(BF16)

    o_ref[...] = lax.dot_general(xn_ref[...], wt_ref[...], NT_DIMS, preferred_element_type=F32)


def _norm_matmul(h, g, wt, tm, tn):
    m, d = h.shape
    n = wt.shape[0]
    return pl.pallas_call(
        _norm_matmul_kernel,
        grid=(m // tm, n // tn),
        in_specs=[
            pl.BlockSpec((tm, d), lambda i, j: (i, 0)),
            pl.BlockSpec((1, d), lambda i, j: (0, 0)),
            pl.BlockSpec((tn, d), lambda i, j: (j, 0)),
        ],
        out_specs=pl.BlockSpec((tm, tn), lambda i, j: (i, j)),
        out_shape=jax.ShapeDtypeStruct((m, n), F32),
        scratch_shapes=[pltpu.VMEM((tm, d), BF16)],
        compiler_params=_cparams(("parallel", "arbitrary")),
        name="norm_matmul",
    )(h, g, wt)


def _pool_mix(pooled_groups, gw_ref, scale_ref, o_ref):
    for gi, pooled in enumerate(pooled_groups):
        sl = slice(gi * LANES, (gi + 1) * LANES)
        mixed = jnp.dot(pooled.astype(BF16), gw_ref[gi], preferred_element_type=F32)
        o_ref[:, sl] = (mixed * scale_ref[:, sl]).astype(o_ref.dtype)


def _pool_prompt_kernel(zp_ref, gw_ref, scale_ref, o_ref, ext_ref, *, tt):
    t = pl.program_id(1)
    hist = 16

    @pl.when(t == 0)
    def _():
        ext_ref[0:hist, :] = jnp.zeros((hist, ext_ref.shape[1]), F32)

    x = zp_ref[...]
    ext_ref[hist:hist + tt, :] = x
    pos = t * tt + lax.broadcasted_iota(jnp.int32, (tt, LANES), 0)
    groups = []
    for gi, w in enumerate(POOL_WINDOWS):
        sl = slice(gi * LANES, (gi + 1) * LANES)
        acc = x[:, sl]
        for k in range(1, w):
            acc = acc + ext_ref[hist - k:hist - k + tt, sl]
        cnt = jnp.minimum(pos + 1, w).astype(F32)
        groups.append(acc / cnt - x[:, sl])
    _pool_mix(groups, gw_ref, scale_ref, o_ref)
    ext_ref[0:hist, :] = ext_ref[tt:tt + hist, :]


def _pool_prompt(z, nb, t_len, col_blk, gw, scale, tt=256):
    nt = t_len // tt
    pw = gw.shape[0] * LANES
    return pl.pallas_call(
        functools.partial(_pool_prompt_kernel, tt=tt),
        grid=(nb, nt),
        in_specs=[
            pl.BlockSpec((tt, pw), lambda b, t: (b * nt + t, col_blk)),
            pl.BlockSpec(gw.shape, lambda b, t: (0, 0, 0)),
            pl.BlockSpec((1, pw), lambda b, t: (0, 0)),
        ],
        out_specs=pl.BlockSpec((tt, pw), lambda b, t: (b * nt + t, 0)),
        out_shape=jax.ShapeDtypeStruct((nb * t_len, pw), BF16),
        scratch_shapes=[pltpu.VMEM((tt + 16, pw), F32)],
        compiler_params=_cparams(("parallel", "arbitrary")),
        name="pool_prompt",
    )(z, gw, scale)


def _pool_sample_kernel(zp_ref, buf_ref, gw_ref, scale_ref, o_ref):
    x = zp_ref[...]
    nbuf = buf_ref.shape[0]
    groups = []
    for gi, w in enumerate(POOL_WINDOWS):
        sl = slice(gi * LANES, (gi + 1) * LANES)
        acc = x[:, sl]
        for k in range(1, w):
            acc = acc + buf_ref[nbuf - k, :, sl]
        cnt = float(min(PAST_LEN + 1, w))
        groups.append(acc / cnt - x[:, sl])
    _pool_mix(groups, gw_ref, scale_ref, o_ref)


def _pool_sample(z, col_blk, buf_t, gw, scale):
    nrows = z.shape[0]
    pw = gw.shape[0] * LANES
    return pl.pallas_call(
        _pool_sample_kernel,
        grid=(1,),
        in_specs=[
            pl.BlockSpec((nrows, pw), lambda i: (0, col_blk)),
            pl.BlockSpec(buf_t.shape, lambda i: (0, 0, 0)),
            pl.BlockSpec(gw.shape, lambda i: (0, 0, 0)),
            pl.BlockSpec((1, pw), lambda i: (0, 0)),
        ],
        out_specs=pl.BlockSpec((nrows, pw), lambda i: (0, 0)),
        out_shape=jax.ShapeDtypeStruct((nrows, pw), BF16),
        compiler_params=_cparams(("arbitrary",)),
        name="pool_sample",
    )(z, buf_t, gw, scale)


def _head_sum(x):
    head_a = lax.broadcasted_iota(jnp.int32, (x.shape[0], LANES), 1) < HEAD
    cols = []
    for c in range(x.shape[1] // LANES):
        t = x[:, c * LANES:(c + 1) * LANES]
        sa = jnp.sum(jnp.where(head_a, t, 0.0), axis=-1, keepdims=True)
        sb = jnp.sum(jnp.where(head_a, 0.0, t), axis=-1, keepdims=True)
        cols.append(jnp.where(head_a, sa, sb))
    return jnp.concatenate(cols, axis=1)


def _prep_math(x, prev, p_refs):
    mu_ref, w0_ref, wup_ref, a0_ref, aup_ref, gup_ref, kk_ref, ka_ref, rk_ref = p_refs
    xm = x + (prev - x) * mu_ref[...]
    w = RWKV_W
    r = xm[:, 0:w]
    k = xm[:, w:2 * w]
    v = xm[:, 2 * w:3 * w]
    wl = xm[:, 3 * w:3 * w + 128]
    al = xm[:, 3 * w + 128:3 * w + 256]
    gl = xm[:, 3 * w + 256:3 * w + 512]
    dw = w0_ref[...] + jnp.dot(jnp.tanh(wl).astype(BF16), wup_ref[...], preferred_element_type=F32)
    lw = -EXP_M05 * jax.nn.sigmoid(dw)
    a = jax.nn.sigmoid(a0_ref[...] + jnp.dot(al.astype(BF16), aup_ref[...], preferred_element_type=F32))
    g = jnp.dot(jax.nn.sigmoid(gl).astype(BF16), gup_ref[...], preferred_element_type=F32)
    kk = k * kk_ref[...]
    kk = kk * lax.rsqrt(jnp.maximum(_head_sum(kk * kk), 1e-24))
    kmod = k * (1.0 + (a - 1.0) * ka_ref[...])
    bonus = _head_sum(r * kmod * rk_ref[...]) * v
    return r, lw, kmod, v, -kk, kk * a, g, bonus


def _group_norm_gate(y, bonus, g, lng_ref, lnb_ref):
    d = y - _head_sum(y) * (1.0 / HEAD)
    var = _head_sum(d * d) * (1.0 / HEAD)
    yn = d * lax.rsqrt(var + GN_EPS) * lng_ref[...] + lnb_ref[...]
    return (yn + bonus) * g


def _prep_sample_kernel(zr_ref, prev_ref, *refs):
    p_refs, out_refs = refs[:9], refs[9:17]
    for o_ref, val in zip(out_refs, _prep_math(zr_ref[...], prev_ref[...], p_refs)):
        o_ref[...] = val


def _prep_param_specs(params):
    zero = (lambda *idx: (0, 0))
    return [pl.BlockSpec(p.shape, zero) for p in params]


def _prep_sample(z, prev, params):
    nrows = z.shape[0]
    out_spec = pl.BlockSpec((nrows, RWKV_W), lambda i: (0, 0))
    return pl.pallas_call(
        _prep_sample_kernel,
        grid=(1,),
        in_specs=[pl.BlockSpec((nrows, ZR_W), lambda i: (0, 0)),
                  pl.BlockSpec((nrows, ZR_W), lambda i: (0, 0))] + _prep_param_specs(params),
        out_specs=[out_spec] * 8,
        out_shape=[jax.ShapeDtypeStruct((nrows, RWKV_W), F32)] * 8,
        compiler_params=_cparams(("arbitrary",)),
        name="prep_sample",
    )(z, prev, *params)


def _split3(x):
    hi = x.astype(BF16)
    rest = x - hi.astype(F32)
    mid = rest.astype(BF16)
    lo = (rest - mid.astype(F32)).astype(BF16)
    return hi, mid, lo


def _select_dot(sel, x):
    sel = sel.astype(BF16)
    hi, mid, lo = _split3(x)
    return _bdot(sel, hi) + (_bdot(sel, mid) + _bdot(sel, lo))


def _chunk_scan(r, lw, k, v, a, b, s_ref, c_len):
    n2 = 2 * c_len
    assert n2 == LANES
    row = lax.broadcasted_iota(jnp.int32, (n2, n2), 0)
    col = lax.broadcasted_iota(jnp.int32, (n2, n2), 1)
    tr = row & (c_len - 1)
    tc = col & (c_len - 1)
    strict = tr > tc
    incl = tr >= tc
    tri = (lax.broadcasted_iota(jnp.int32, (c_len, c_len), 0)
           >= lax.broadcasted_iota(jnp.int32, (c_len, c_len), 1))
    head_a = lax.broadcasted_iota(jnp.int32, (c_len, LANES), 1) < HEAD

    def stack(x):
        return jnp.concatenate([jnp.where(head_a, x, 0.0), jnp.where(head_a, 0.0, x)], axis=0).astype(BF16)

    prs = range(r.shape[1] // LANES)
    cat = jnp.concatenate
    sls = [slice(q * LANES, (q + 1) * LANES) for q in prs]
    lwq = [lw[:, sl] for sl in sls]
    cum = [_select_dot(tri, x) for x in lwq]
    tot = [x[c_len - 1:c_len, :] for x in cum]
    xr = [stack(r[:, sls[q]] * jnp.exp(cum[q])) for q in prs]
    xa = [stack(a[:, sls[q]] * jnp.exp(cum[q] - lwq[q])) for q in prs]
    e_neg = [jnp.exp(-x) for x in cum]
    e_rem = [jnp.exp(tot[q] - cum[q]) for q in prs]
    yb = [stack(b[:, sls[q]] * e_neg[q]) for q in prs]
    yk = [stack(k[:, sls[q]] * e_neg[q]) for q in prs]
    zb = [stack(b[:, sls[q]] * e_rem[q]) for q in prs]
    zk = [stack(k[:, sls[q]] * e_rem[q]) for q in prs]
    vs = [stack(v[:, sl]) for sl in sls]

    g = [_bdot(cat([xa[q], xr[q]], axis=0), cat([yb[q], yk[q]], axis=0), NT_DIMS) for q in prs]
    m_ab = [jnp.where(strict, x[0:n2, 0:n2], 0.0) for x in g]
    m_ak = [jnp.where(strict, x[0:n2, n2:2 * n2], 0.0).astype(BF16) for x in g]
    n_rb = [jnp.where(incl, x[n2:2 * n2, 0:n2], 0.0) for x in g]
    n_rk = [jnp.where(incl, x[n2:2 * n2, n2:2 * n2], 0.0).astype(BF16) for x in g]

    s_old = [s_ref[q] for q in prs]
    lhs = [cat([cat([xa[q], m_ak[q]], axis=1), cat([xr[q], n_rk[q]], axis=1)], axis=0) for q in prs]
    xy0 = [_bdot(lhs[q], cat([s_old[q].T.astype(BF16), vs[q]], axis=0)) for q in prs]

    x = [xy0[q][0:n2] for q in prs]
    mk = m_ab
    nlev = int(math.log2(c_len))
    for lev in range(nlev):
        if lev < nlev - 1:
            res = [_bdot(mk[q], cat([mk[q], x[q]], axis=1)) for q in prs]
            mk = [t[:, 0:n2] for t in res]
            x = [x[q] + res[q][:, n2:2 * n2] for q in prs]
        else:
            x = [x[q] + _bdot(mk[q], x[q]) for q in prs]
    u = [t.astype(BF16) for t in x]

    y_st = [xy0[q][n2:2 * n2] + _bdot(n_rb[q], u[q]) for q in prs]
    for q in prs:
        s_ref[q] = s_old[q] * jnp.exp(tot[q]) + _bdot(cat([u[q], vs[q]], axis=0), cat([zb[q], zk[q]], axis=0),
                                                     TN_DIMS)
    return cat([t[0:c_len] + t[c_len:n2] for t in y_st], axis=1)


def _rwkv_prompt_kernel(zr_ref, *refs, nc):
    p_refs = refs[:9]
    lng_ref, lnb_ref, yb_ref, sout_ref, s_ref, carry_ref = refs[9:]
    c = pl.program_id(1)

    @pl.when(c == 0)
    def _():
        s_ref[...] = jnp.zeros_like(s_ref)
        carry_ref[...] = jnp.zeros_like(carry_ref)

    x = zr_ref[...]
    rolled = pltpu.roll(x, 1, axis=0)
    first = lax.broadcasted_iota(jnp.int32, x.shape, 0) == 0
    prev = jnp.where(first, carry_ref[0:1, :], rolled)
    carry_ref[...] = rolled[0:SUBLANES, :]
    r, lw, k, v, a, b, g, bonus = _prep_math(x, prev, p_refs)
    y = _chunk_scan(r, lw, k, v, a, b, s_ref, x.shape[0])
    yb_ref[...] = _group_norm_gate(y, bonus, g, lng_ref, lnb_ref).astype(yb_ref.dtype)

    @pl.when(c == nc - 1)
    def _():
        sout_ref[0] = s_ref[...]


def _rwkv_prompt(z, nb, t_len, params, ln_g, ln_b, c_len=64):
    nc = t_len // c_len
    npair = RWKV_W // LANES
    par = pl.BlockSpec((1, RWKV_W), lambda b, c: (0, 0))
    return pl.pallas_call(
        functools.partial(_rwkv_prompt_kernel, nc=nc),
        grid=(nb, nc),
        in_specs=[pl.BlockSpec((c_len, ZR_W), lambda b, c: (b * nc + c, 0))] + _prep_param_specs(params) + [par, par],
        out_specs=[pl.BlockSpec((c_len, RWKV_W), lambda b, c: (b * nc + c, 0)),
                   pl.BlockSpec((1, npair, LANES, LANES), lambda b, c: (b, 0, 0, 0))],
        out_shape=[jax.ShapeDtypeStruct((nb * t_len, RWKV_W), BF16),
                   jax.ShapeDtypeStruct((nb, npair, LANES, LANES), F32)],
        scratch_shapes=[pltpu.VMEM((npair, LANES, LANES), F32), pltpu.VMEM((SUBLANES, ZR_W), F32)],
        compiler_params=_cparams(("parallel", "arbitrary")),
        name="rwkv_prompt",
    )(z, *params, ln_g, ln_b)


def _wkv_step_kernel(s_ref, r_ref, lw_ref, k_ref, v_ref, a_ref, b_ref, g_ref, bonus_ref, lng_ref, lnb_ref,
                     yb_ref, so_ref, vt_scr, y_scr):
    rt, wt, kt, at, bt = (x[...].T for x in (r_ref, lw_ref, k_ref, a_ref, b_ref))
    wt = jnp.exp(wt)
    vt_scr[...] = v_ref[...].T
    for hh in range(2):
        rows = slice(hh * HEAD, (hh + 1) * HEAD)
        r, w, k, a, b = (x[rows, :] for x in (rt, wt, kt, at, bt))

        def body(i, carry):
            si = s_ref[hh, i]
            sa = jnp.sum(si * a, axis=0, keepdims=True)
            vi = vt_scr[pl.ds(hh * HEAD + i, 1), :]
            s2 = si * w + sa * b + vi * k
            so_ref[hh, i] = s2
            y_scr[pl.ds(hh * HEAD + i, 1), :] = jnp.sum(s2 * r, axis=0, keepdims=True)
            return carry

        lax.fori_loop(0, HEAD, body, 0, unroll=4)

    outs = []
    for hh in range(2):
        y = y_scr[hh * HEAD:(hh + 1) * HEAD, :]
        d = y - jnp.mean(y, axis=0, keepdims=True)
        var = jnp.mean(d * d, axis=0, keepdims=True)
        outs.append(d * lax.rsqrt(var + GN_EPS))
    yn = jnp.concatenate(outs, axis=0).T * lng_ref[...] + lnb_ref[...]
    yb_ref[...] = ((yn + bonus_ref[...]) * g_ref[...]).astype(yb_ref.dtype)


def _wkv_step(state_t, vecs, ln_g, ln_b):
    nh, _, _, nb = state_t.shape
    st = pl.BlockSpec((2, HEAD, HEAD, nb), lambda p: (p, 0, 0, 0))
    vec = pl.BlockSpec((nb, 2 * HEAD), lambda p: (0, p))
    par = pl.BlockSpec((1, 2 * HEAD), lambda p: (0, p))
    return pl.pallas_call(
        _wkv_step_kernel,
        grid=(nh // 2,),
        in_specs=[st] + [vec] * 8 + [par, par],
        out_specs=[vec, st],
        out_shape=[jax.ShapeDtypeStruct((nb, nh * HEAD), BF16), jax.ShapeDtypeStruct(state_t.shape, F32)],
        scratch_shapes=[pltpu.VMEM((2 * HEAD, nb), F32), pltpu.VMEM((2 * HEAD, nb), F32)],
        compiler_params=_cparams(("parallel",)),
        name="wkv_step",
    )(state_t, *vecs, ln_g, ln_b)


def _xattn_prompt_kernel(q_ref, k_ref, v_ref, o_ref):
    scale = XA_DIM ** -0.5
    q = q_ref[...]
    for h in range(XA_HEADS):
        sl = slice(h * XA_DIM, (h + 1) * XA_DIM)
        s = lax.dot_general(q[:, sl].astype(BF16), k_ref[0, :, sl].astype(BF16),
                            NT_DIMS, preferred_element_type=F32) * scale
        p = jnp.exp(s - jnp.max(s, axis=-1, keepdims=True))
        den = jnp.sum(p, axis=-1, keepdims=True)
        o = jnp.dot(p.astype(BF16), v_ref[0, :, sl].astype(BF16), preferred_element_type=F32)
        o_ref[:, sl] = (o / den).astype(o_ref.dtype)


def _xattn_prompt(z, nb, t_len, col_blk, mk, mv, tq=512):
    nt = t_len // tq
    xw = XA_HEADS * XA_DIM
    nmem = mk.shape[1]
    kv = pl.BlockSpec((1, nmem, xw), lambda b, t: (b, 0, 0))
    return pl.pallas_call(
        _xattn_prompt_kernel,
        grid=(nb, nt),
        in_specs=[pl.BlockSpec((tq, xw), lambda b, t: (b * nt + t, col_blk)), kv, kv],
        out_specs=pl.BlockSpec((tq, xw), lambda b, t: (b * nt + t, 0)),
        out_shape=jax.ShapeDtypeStruct((nb * t_len, xw), BF16),
        compiler_params=_cparams(("parallel", "parallel")),
        name="xattn_prompt",
    )(z, mk, mv)


def _xattn_sample_kernel(q_ref, k_ref, v_ref, o_ref):
    bb = q_ref.shape[0]
    nrow = k_ref.shape[1] // SUBLANES
    full = (bb, nrow, SUBLANES, XA_DIM)
    q = q_ref[...] * (XA_DIM ** -0.5)
    q8 = jnp.concatenate([q, q], axis=1)[:, None]
    k = k_ref[...].reshape(full)
    s = jnp.broadcast_to(jnp.sum(k * q8, axis=-1, keepdims=True), full)
    mx = jnp.max(s, axis=1, keepdims=True)
    mx = jnp.maximum(mx, pltpu.roll(mx, XA_HEADS, axis=2))
    p = jnp.exp(s - mx)
    den = jnp.sum(p, axis=1, keepdims=True)
    den = den + pltpu.roll(den, XA_HEADS, axis=2)
    o = jnp.sum(p * v_ref[...].reshape(full), axis=1, keepdims=True)
    o = o + pltpu.roll(o, XA_HEADS, axis=2)
    o_ref[...] = (o / den)[:, 0, 0:XA_HEADS, :].astype(o_ref.dtype)


def _xattn_sample(q3, mk, mv, bb=8):
    nb, rows, _ = mk.shape
    kv = pl.BlockSpec((bb, rows, XA_DIM), lambda i: (i, 0, 0))
    qs = pl.BlockSpec((bb, XA_HEADS, XA_DIM), lambda i: (i, 0, 0))
    return pl.pallas_call(
        _xattn_sample_kernel,
        grid=(nb // bb,),
        in_specs=[qs, kv, kv],
        out_specs=qs,
        out_shape=jax.ShapeDtypeStruct((nb, XA_HEADS, XA_DIM), BF16),
        compiler_params=_cparams(("parallel",)),
        name="xattn_sample",
    )(q3, mk, mv)


def _merge_kernel(pa_ref, pb_ref, pc_ref, g0_ref, g1_ref, g2_ref, h_ref, wa_ref, wb_ref, wc_ref, wo_ref, o_ref):
    oa = jnp.dot(pa_ref[...], wa_ref[...], preferred_element_type=F32)
    ob = jnp.dot(pb_ref[...], wb_ref[...], preferred_element_type=F32)
    oc = jnp.dot(pc_ref[...], wc_ref[...], preferred_element_type=F32)
    merged = (jax.nn.sigmoid(g0_ref[...]) * oa + jax.nn.sigmoid(g1_ref[...]) * ob
              + jax.nn.sigmoid(g2_ref[...]) * oc)
    o_ref[...] = h_ref[...] + jnp.dot(merged.astype(BF16), wo_ref[...], preferred_element_type=F32)


def _merge(pa, pb, pc, z, zg_blk0, h, tm, wa, wb, wc, wo):
    m, d = h.shape
    const = lambda i: (0, 0)
    resident = lambda w: pl.BlockSpec(w.shape, const, pipeline_mode=pl.Buffered(1))
    in_specs = [
        pl.BlockSpec((tm, pa.shape[1]), lambda i: (i, 0)),
        pl.BlockSpec((tm, pb.shape[1]), lambda i: (i, 0)),
        pl.BlockSpec((tm, pc.shape[1]), lambda i: (i, 0)),
        pl.BlockSpec((tm, d), lambda i: (i, zg_blk0)),
        pl.BlockSpec((tm, d), lambda i: (i, zg_blk0 + 1)),
        pl.BlockSpec((tm, d), lambda i: (i, zg_blk0 + 2)),
        pl.BlockSpec((tm, d), lambda i: (i, 0)),
        resident(wa), resident(wb), resident(wc), resident(wo),
    ]
    return pl.pallas_call(
        _merge_kernel,
        grid=(m // tm,),
        in_specs=in_specs,
        out_specs=pl.BlockSpec((tm, d), lambda i: (i, 0)),
        out_shape=jax.ShapeDtypeStruct((m, d), F32),
        compiler_params=_cparams(("parallel",)),
        name="merge",
    )(pa, pb, pc, z, z, z, h, wa, wb, wc, wo)


def _pack_zr(x, axis):
    w = RWKV_W
    take = lambda a, b: lax.slice_in_dim(x, a, b, axis=axis)

    def pad(n):
        shape = list(x.shape)
        shape[axis] = n
        return jnp.zeros(shape, x.dtype)

    return jnp.concatenate([take(0, 3 * w + 64), pad(64), take(3 * w + 64, 3 * w + 128), pad(64),
                            take(3 * w + 128, ZR_TRUE), pad(96)], axis=axis)


def _unpack_zr_cols(x):
    w = RWKV_W
    return jnp.concatenate([x[..., :3 * w + 64], x[..., 3 * w + 128:3 * w + 192], x[..., 3 * w + 256:3 * w + 416]],
                           axis=-1)


def _pad_rows(x, n):
    return jnp.concatenate([x, jnp.zeros((n - x.shape[0],) + x.shape[1:], x.dtype)], axis=0)


def kernel(x_prompt, x_sample, mem_prompt, cache_mem_k, cache_mem_v, state_wkv, state_shift, state_pool,
           ffn1_norm_g, ffn1_w_gate, ffn1_w_up, ffn1_w_down, mix_norm_g, w_in,
           pool_group_w, pool_scale, pool_out,
           rwkv_mu, rwkv_w0, rwkv_w_up, rwkv_a0, rwkv_a_up, rwkv_g_up, rwkv_k_k, rwkv_k_a, rwkv_r_k,
           rwkv_ln_g, rwkv_ln_b, rwkv_out,
           mem_norm_g, w_mem_k, w_mem_v, xattn_out, w_o,
           ffn2_norm_g, ffn2_w_gate, ffn2_w_up, ffn2_w_down, final_norm_g):
    nb, t_len, d = x_prompt.shape
    ns = x_sample.shape[0]
    assert w_in.shape[0] == 1 and x_sample.shape[1] == 1
    n_mem = mem_prompt.shape[1]
    pool_w = pool_out.shape[1]
    xa_w = xattn_out.shape[1]
    n_heads = RWKV_W // HEAD
    nbuf = state_pool.shape[2]
    rows_p = nb * t_len
    l = 0

    d_ff = ffn1_w_gate.shape[2]
    f1 = [w.reshape(w.shape[1:]).astype(BF16) for w in (ffn1_w_gate, ffn1_w_up, ffn1_w_down)]
    f2 = [w.reshape(w.shape[1:]).astype(BF16) for w in (ffn2_w_gate, ffn2_w_up, ffn2_w_down)]
    wit = jnp.swapaxes(w_in[l], 0, 1)
    o_zr, o_zq, o_zg = pool_w, pool_w + ZR_TRUE, pool_w + ZR_TRUE + xa_w
    w_in_t = jnp.concatenate([_pack_zr(wit[o_zr:o_zq], 0), wit[:o_zr], wit[o_zg:], wit[o_zq:o_zg]],
                             axis=0).astype(BF16)
    col_zp = ZR_W // pool_w
    col_zg = (ZR_W + pool_w) // d
    col_zq = (ZR_W + pool_w + 3 * d) // xa_w
    row = lambda v: v.reshape(1, -1)
    prep_params = [row(_pack_zr(rwkv_mu[l], 0)), row(rwkv_w0[l]), _pad_rows(rwkv_w_up[l], 128).astype(BF16),
                   row(rwkv_a0[l]), _pad_rows(rwkv_a_up[l], 128).astype(BF16),
                   _pad_rows(rwkv_g_up[l], 256).astype(BF16), row(rwkv_k_k[l]), row(rwkv_k_a[l]),
                   row(rwkv_r_k[l])]
    gw = pool_group_w[l].astype(BF16)
    w_kv_t = jnp.concatenate([jnp.swapaxes(w_mem_k[l], 0, 1), jnp.swapaxes(w_mem_v[l], 0, 1)], axis=0).astype(BF16)
    wa, wb, wc, wo = (pool_out[l].astype(BF16), rwkv_out[l].astype(BF16), xattn_out[l].astype(BF16),
                      w_o[l].astype(BF16))
    g1, gm, g2, fg = row(ffn1_norm_g[l]), row(mix_norm_g[l]), row(ffn2_norm_g[l]), row(final_norm_g)
    scale = row(pool_scale[l])
    ln_g, ln_b = rwkv_ln_g[l], rwkv_ln_b[l]

    tm_p = 512
    h1_p, h1_s = _ffn(x_prompt.reshape(rows_p, d), x_sample.reshape(ns, d), tm_p, g1, *f1)
    z_p = _norm_matmul(h1_p, gm, w_in_t, tm=1024, tn=1536)
    z_s = _norm_matmul(h1_s, gm, w_in_t, tm=ns, tn=1536)
    kv = _norm_matmul(mem_prompt.reshape(nb * n_mem, d), row(mem_norm_g[l]), w_kv_t, tm=512, tn=512)
    mk_p = kv[:, :xa_w].reshape(nb, n_mem, xa_w)
    mv_p = kv[:, xa_w:].reshape(nb, n_mem, xa_w)

    pool_state = state_pool.reshape(ns, nbuf, pool_w)
    pa_p = _pool_prompt(z_p, nb, t_len, col_zp, gw, scale)
    pa_s = _pool_sample(z_s, col_zp, jnp.swapaxes(pool_state, 0, 1), gw, scale)

    pb_p, st_p = _rwkv_prompt(z_p, nb, t_len, prep_params, row(ln_g), row(ln_b))
    prep_s = _prep_sample(z_s, _pack_zr(state_shift.reshape(ns, ZR_TRUE), 1), prep_params)
    state_t = jnp.transpose(state_wkv.reshape(ns, n_heads, HEAD, HEAD), (1, 2, 3, 0))
    pb_s, wkv_s_t = _wkv_step(state_t, prep_s, row(ln_g), row(ln_b))

    pc_p = _xattn_prompt(z_p, nb, t_len, col_zq, mk_p, mv_p)
    q_s = z_s[:, ZR_W + pool_w + 3 * d:].reshape(ns, XA_HEADS, XA_DIM)
    pc_s = _xattn_sample(q_s, cache_mem_k.reshape(ns, n_mem * XA_HEADS, XA_DIM),
                         cache_mem_v.reshape(ns, n_mem * XA_HEADS, XA_DIM)).reshape(ns, xa_w)

    h2_p = _merge(pa_p, pb_p, pc_p, z_p, col_zg, h1_p, 256, wa, wb, wc, wo)
    h2_s = _merge(pa_s, pb_s, pc_s, z_s, col_zg, h1_s, ns, wa, wb, wc, wo)
    y_prompt, y_sample = _ffn(h2_p, h2_s, tm_p, g2, *f2, final_g=fg)

    ends = [(b + 1) * t_len for b in range(nb)]
    shift_p = _unpack_zr_cols(jnp.stack([z_p[e - 1:e, :ZR_W] for e in ends]))[None]
    pool_p = jnp.stack([z_p[e - nbuf:e, ZR_W:ZR_W + pool_w] for e in ends])[None]
    shift_s = _unpack_zr_cols(z_s[:, :ZR_W])[None, :, None, :]
    pool_s = jnp.concatenate([pool_state[:, 1:], z_s[:, None, ZR_W:ZR_W + pool_w]], axis=1)[None]
    st5 = st_p.reshape(nb, n_heads // 2, 2, HEAD, 2, HEAD)
    wkv_p = jnp.stack([st5[:, :, 0, :, 0, :], st5[:, :, 1, :, 1, :]], axis=2).reshape(1, nb, n_heads, HEAD, HEAD)
    wkv_s = jnp.transpose(wkv_s_t, (3, 0, 1, 2)).reshape(state_wkv.shape)
    mem_k_p = mk_p.reshape(1, nb, n_mem, XA_HEADS, XA_DIM)
    mem_v_p = mv_p.reshape(1, nb, n_mem, XA_HEADS, XA_DIM)
    return (y_prompt.reshape(nb, t_len, d), y_sample.reshape(ns, 1, d), mem_k_p, mem_v_p, wkv_p, shift_p, pool_p,
            wkv_s, shift_s, pool_s)
```

```python
import functools
import math

import jax
import jax.numpy as jnp
from jax import lax
from jax.experimental import pallas as pl
from jax.experimental.pallas import tpu as pltpu

F32 = jnp.float32
BF16 = jnp.bfloat16

RMS_EPS = 1e-6
GN_EPS = 64e-5
POOL_WINDOWS = (2, 4, 8, 16)
HEAD = 64
LANES = 128
SUBLANES = 8
XA_HEADS = 4
XA_DIM = 128
PAST_LEN = 16384
VMEM_LIMIT = 60 * 1024 * 1024
EXP_M05 = math.exp(-0.5)

ZR_W = 3584
RWKV_W = 1024
ZR_TRUE = 3360

NN_DIMS = (((1,), (0,)), ((), ()))
NT_DIMS = (((1,), (1,)), ((), ()))
TN_DIMS = (((0,), (0,)), ((), ()))


def _cparams(sem):
    return pltpu.CompilerParams(dimension_semantics=sem, vmem_limit_bytes=VMEM_LIMIT)


def _rms(x, g):
    ms = jnp.mean(x * x, axis=-1, keepdims=True)
    return x * lax.rsqrt(ms + RMS_EPS) * g


def _bdot(a, b, dims=NN_DIMS):
    return lax.dot_general(a.astype(BF16), b.astype(BF16), dims, preferred_element_type=F32)


def _ffn_kernel(hp_ref, hs_ref, g_ref, fg_ref, wg_ref, wu_ref, wd_ref, wgt_ref, wut_ref, wdt_ref,
                op_ref, os_ref, xp_ref, xs_ref, ap_ref, as_ref, *, nfull, final):
    m = pl.program_id(0)
    f = pl.program_id(1)

    def start(h_ref, x_ref, acc_ref):
        x_ref[...] = _rms(h_ref[...], g_ref[...]).astype(BF16)
        acc_ref[...] = jnp.zeros_like(acc_ref)

    def contribution(x_ref, wg, wu, wd):
        xn = x_ref[...]
        gate = jnp.dot(xn, wg, preferred_element_type=F32)
        up = jnp.dot(xn, wu, preferred_element_type=F32)
        act = (gate * jax.nn.sigmoid(gate) * up).astype(BF16)
        return jnp.dot(act, wd, preferred_element_type=F32)

    def finish(h_ref, acc_ref, o_ref, last):
        out = h_ref[...] + 0.5 * (acc_ref[...] + last)
        if final:
            out = _rms(out, fg_ref[...])
        o_ref[...] = out

    @pl.when(f == 0)
    def _():
        start(hp_ref, xp_ref, ap_ref)

    @pl.when((f == 0) & (m == 0))
    def _():
        start(hs_ref, xs_ref, as_ref)

    @pl.when(f < nfull)
    def _():
        wg, wu, wd = wg_ref[...], wu_ref[...], wd_ref[...]
        ap_ref[...] += contribution(xp_ref, wg, wu, wd)

        @pl.when(m == 0)
        def _():
            as_ref[...] += contribution(xs_ref, wg, wu, wd)

    @pl.when(f == nfull)
    def _():
        wg, wu, wd = wgt_ref[...], wut_ref[...], wdt_ref[...]
        finish(hp_ref, ap_ref, op_ref, contribution(xp_ref, wg, wu, wd))

        @pl.when(m == 0)
        def _():
            finish(hs_ref, as_ref, os_ref, contribution(xs_ref, wg, wu, wd))


def _ffn(hp, hs, tm, g, wg, wu, wd, final_g=None, tf=512):
    mp, d = hp.shape
    ms = hs.shape[0]
    d_ff = wg.shape[1]
    nfull, tail = divmod(d_ff, tf)
    assert tail > 0 and tail % LANES == 0 and mp % tm == 0
    last_main = nfull - 1
    split = nfull * tf
    wgt, wut, wdt = wg[:, split:], wu[:, split:], wd[split:]
    final = final_g is not None
    vec = pl.BlockSpec((1, d), lambda i, f: (0, 0))
    once = pl.Buffered(1)
    in_specs = [
        pl.BlockSpec((tm, d), lambda i, f: (i, 0)),
        pl.BlockSpec((ms, d), lambda i, f: (0, 0), pipeline_mode=once),
        vec, vec,
        pl.BlockSpec((d, tf), lambda i, f: (0, jnp.minimum(f, last_main))),
        pl.BlockSpec((d, tf), lambda i, f: (0, jnp.minimum(f, last_main))),
        pl.BlockSpec((tf, d), lambda i, f: (jnp.minimum(f, last_main), 0)),
        pl.BlockSpec((d, tail), lambda i, f: (0, 0), pipeline_mode=once),
        pl.BlockSpec((d, tail), lambda i, f: (0, 0), pipeline_mode=once),
        pl.BlockSpec((tail, d), lambda i, f: (0, 0), pipeline_mode=once),
    ]
    return pl.pallas_call(
        functools.partial(_ffn_kernel, nfull=nfull, final=final),
        grid=(mp // tm, nfull + 1),
        in_specs=in_specs,
        out_specs=[pl.BlockSpec((tm, d), lambda i, f: (i, 0)), pl.BlockSpec((ms, d), lambda i, f: (0, 0))],
        out_shape=[jax.ShapeDtypeStruct((mp, d), F32), jax.ShapeDtypeStruct((ms, d), F32)],
        scratch_shapes=[pltpu.VMEM((tm, d), BF16), pltpu.VMEM((ms, d), BF16),
                        pltpu.VMEM((tm, d), F32), pltpu.VMEM((ms, d), F32)],
        compiler_params=_cparams(("parallel", "arbitrary")),
        name="ffn",
    )(hp, hs, g, g if final_g is None else final_g, wg, wu, wd, wgt, wut, wdt)


def _norm_matmul_kernel(h_ref, g_ref, wt_ref, o_ref, xn_ref):
    @pl.when(pl.program_id(1) == 0)
    def _():
        xn_ref[...] = _rms(h_ref[...], g_ref[...]).astype(BF16)

    o_ref[...] = lax.dot_general(xn_ref[...], wt_ref[...], NT_DIMS, preferred_element_type=F32)


def _norm_matmul(h, g, wt, tm, tn):
    m, d = h.shape
    n = wt.shape[0]
    return pl.pallas_call(
        _norm_matmul_kernel,
        grid=(m // tm, n // tn),
        in_specs=[
            pl.BlockSpec((tm, d), lambda i, j: (i, 0)),
            pl.BlockSpec((1, d), lambda i, j: (0, 0)),
            pl.BlockSpec((tn, d), lambda i, j: (j, 0)),
        ],
        out_specs=pl.BlockSpec((tm, tn), lambda i, j: (i, j)),
        out_shape=jax.ShapeDtypeStruct((m, n), F32),
        scratch_shapes=[pltpu.VMEM((tm, d), BF16)],
        compiler_params=_cparams(("parallel", "arbitrary")),
        name="norm_matmul",
    )(h, g, wt)


def _pool_mix(pooled_groups, gw_ref, scale_ref, o_ref):
    for gi, pooled in enumerate(pooled_groups):
        sl = slice(gi * LANES, (gi + 1) * LANES)
        mixed = jnp.dot(pooled.astype(BF16), gw_ref[gi], preferred_element_type=F32)
        o_ref[:, sl] = (mixed * scale_ref[:, sl]).astype(o_ref.dtype)


def _pool_prompt_kernel(zp_ref, gw_ref, scale_ref, o_ref, ext_ref, *, tt):
    t = pl.program_id(1)
    hist = 16

    @pl.when(t == 0)
    def _():
        ext_ref[0:hist, :] = jnp.zeros((hist, ext_ref.shape[1]), F32)

    x = zp_ref[...]
    ext_ref[hist:hist + tt, :] = x
    pos = t * tt + lax.broadcasted_iota(jnp.int32, (tt, LANES), 0)
    groups = []
    for gi, w in enumerate(POOL_WINDOWS):
        sl = slice(gi * LANES, (gi + 1) * LANES)
        acc = x[:, sl]
        for k in range(1, w):
            acc = acc + ext_ref[hist - k:hist - k + tt, sl]
        cnt = jnp.minimum(pos + 1, w).astype(F32)
        groups.append(acc / cnt - x[:, sl])
    _pool_mix(groups, gw_ref, scale_ref, o_ref)
    ext_ref[0:hist, :] = ext_ref[tt:tt + hist, :]


def _pool_prompt(z, nb, t_len, col_blk, gw, scale, tt=256):
    nt = t_len // tt
    pw = gw.shape[0] * LANES
    return pl.pallas_call(
        functools.partial(_pool_prompt_kernel, tt=tt),
        grid=(nb, nt),
        in_specs=[
            pl.BlockSpec((tt, pw), lambda b, t: (b * nt + t, col_blk)),
            pl.BlockSpec(gw.shape, lambda b, t: (0, 0, 0)),
            pl.BlockSpec((1, pw), lambda b, t: (0, 0)),
        ],
        out_specs=pl.BlockSpec((tt, pw), lambda b, t: (b * nt + t, 0)),
        out_shape=jax.ShapeDtypeStruct((nb * t_len, pw), BF16),
        scratch_shapes=[pltpu.VMEM((tt + 16, pw), F32)],
        compiler_params=_cparams(("parallel", "arbitrary")),
        name="pool_prompt",
    )(z, gw, scale)


def _pool_sample_kernel(zp_ref, buf_ref, gw_ref, scale_ref, o_ref):
    x = zp_ref[...]
    nbuf = buf_ref.shape[0]
    groups = []
    for gi, w in enumerate(POOL_WINDOWS):
        sl = slice(gi * LANES, (gi + 1) * LANES)
        acc = x[:, sl]
        for k in range(1, w):
            acc = acc + buf_ref[nbuf - k, :, sl]
        cnt = float(min(PAST_LEN + 1, w))
        groups.append(acc / cnt - x[:, sl])
    _pool_mix(groups, gw_ref, scale_ref, o_ref)


def _pool_sample(z, col_blk, buf_t, gw, scale):
    nrows = z.shape[0]
    pw = gw.shape[0] * LANES
    return pl.pallas_call(
        _pool_sample_kernel,
        grid=(1,),
        in_specs=[
            pl.BlockSpec((nrows, pw), lambda i: (0, col_blk)),
            pl.BlockSpec(buf_t.shape, lambda i: (0, 0, 0)),
            pl.BlockSpec(gw.shape, lambda i: (0, 0, 0)),
            pl.BlockSpec((1, pw), lambda i: (0, 0)),
        ],
        out_specs=pl.BlockSpec((nrows, pw), lambda i: (0, 0)),
        out_shape=jax.ShapeDtypeStruct((nrows, pw), BF16),
        compiler_params=_cparams(("arbitrary",)),
        name="pool_sample",
    )(z, buf_t, gw, scale)


def _head_sum(x):
    head_a = lax.broadcasted_iota(jnp.int32, (x.shape[0], LANES), 1) < HEAD
    cols = []
    for c in range(x.shape[1] // LANES):
        t = x[:, c * LANES:(c + 1) * LANES]
        sa = jnp.sum(jnp.where(head_a, t, 0.0), axis=-1, keepdims=True)
        sb = jnp.sum(jnp.where(head_a, 0.0, t), axis=-1, keepdims=True)
        cols.append(jnp.where(head_a, sa, sb))
    return jnp.concatenate(cols, axis=1)


def _prep_math(x, prev, p_refs):
    mu_ref, w0_ref, wup_ref, a0_ref, aup_ref, gup_ref, kk_ref, ka_ref, rk_ref = p_refs
    xm = x + (prev - x) * mu_ref[...]
    w = RWKV_W
    r = xm[:, 0:w]
    k = xm[:, w:2 * w]
    v = xm[:, 2 * w:3 * w]
    wl = xm[:, 3 * w:3 * w + 128]
    al = xm[:, 3 * w + 128:3 * w + 256]
    gl = xm[:, 3 * w + 256:3 * w + 512]
    dw = w0_ref[...] + jnp.dot(jnp.tanh(wl).astype(BF16), wup_ref[...], preferred_element_type=F32)
    lw = -EXP_M05 * jax.nn.sigmoid(dw)
    a = jax.nn.sigmoid(a0_ref[...] + jnp.dot(al.astype(BF16), aup_ref[...], preferred_element_type=F32))
    g = jnp.dot(jax.nn.sigmoid(gl).astype(BF16), gup_ref[...], preferred_element_type=F32)
    kk = k * kk_ref[...]
    kk = kk * lax.rsqrt(jnp.maximum(_head_sum(kk * kk), 1e-24))
    kmod = k * (1.0 + (a - 1.0) * ka_ref[...])
    bonus = _head_sum(r * kmod * rk_ref[...]) * v
    return r, lw, kmod, v, -kk, kk * a, g, bonus


def _group_norm_gate(y, bonus, g, lng_ref, lnb_ref):
    d = y - _head_sum(y) * (1.0 / HEAD)
    var = _head_sum(d * d) * (1.0 / HEAD)
    yn = d * lax.rsqrt(var + GN_EPS) * lng_ref[...] + lnb_ref[...]
    return (yn + bonus) * g


def _prep_sample_kernel(zr_ref, prev_ref, *refs):
    p_refs, out_refs = refs[:9], refs[9:17]
    for o_ref, val in zip(out_refs, _prep_math(zr_ref[...], prev_ref[...], p_refs)):
        o_ref[...] = val


def _prep_param_specs(params):
    zero = (lambda *idx: (0, 0))
    return [pl.BlockSpec(p.shape, zero) for p in params]


def _prep_sample(z, prev, params):
    nrows = z.shape[0]
    out_spec = pl.BlockSpec((nrows, RWKV_W), lambda i: (0, 0))
    return pl.pallas_call(
        _prep_sample_kernel,
        grid=(1,),
        in_specs=[pl.BlockSpec((nrows, ZR_W), lambda i: (0, 0)),
                  pl.BlockSpec((nrows, ZR_W), lambda i: (0, 0))] + _prep_param_specs(params),
        out_specs=[out_spec] * 8,
        out_shape=[jax.ShapeDtypeStruct((nrows, RWKV_W), F32)] * 8,
        compiler_params=_cparams(("arbitrary",)),
        name="prep_sample",
    )(z, prev, *params)


def _split3(x):
    hi = x.astype(BF16)
    rest = x - hi.astype(F32)
    mid = rest.astype(BF16)
    lo = (rest - mid.astype(F32)).astype(BF16)
    return hi, mid, lo


def _select_dot(sel, x):
    sel = sel.astype(BF16)
    hi, mid, lo = _split3(x)
    return _bdot(sel, hi) + (_bdot(sel, mid) + _bdot(sel, lo))


def _chunk_scan(r, lw, k, v, a, b, s_ref, c_len):
    n2 = 2 * c_len
    assert n2 == LANES
    row = lax.broadcasted_iota(jnp.int32, (n2, n2), 0)
    col = lax.broadcasted_iota(jnp.int32, (n2, n2), 1)
    tr = row & (c_len - 1)
    tc = col & (c_len - 1)
    strict = tr > tc
    incl = tr >= tc
    tri = (lax.broadcasted_iota(jnp.int32, (c_len, c_len), 0)
           >= lax.broadcasted_iota(jnp.int32, (c_len, c_len), 1))
    head_a = lax.broadcasted_iota(jnp.int32, (c_len, LANES), 1) < HEAD

    def stack(x):
        return jnp.concatenate([jnp.where(head_a, x, 0.0), jnp.where(head_a, 0.0, x)], axis=0).astype(BF16)

    prs = range(r.shape[1] // LANES)
    cat = jnp.concatenate
    sls = [slice(q * LANES, (q + 1) * LANES) for q in prs]
    lwq = [lw[:, sl] for sl in sls]
    cum = [_select_dot(tri, x) for x in lwq]
    tot = [x[c_len - 1:c_len, :] for x in cum]
    xr = [stack(r[:, sls[q]] * jnp.exp(cum[q])) for q in prs]
    xa = [stack(a[:, sls[q]] * jnp.exp(cum[q] - lwq[q])) for q in prs]
    e_neg = [jnp.exp(-x) for x in cum]
    e_rem = [jnp.exp(tot[q] - cum[q]) for q in prs]
    yb = [stack(b[:, sls[q]] * e_neg[q]) for q in prs]
    yk = [stack(k[:, sls[q]] * e_neg[q]) for q in prs]
    zb = [stack(b[:, sls[q]] * e_rem[q]) for q in prs]
    zk = [stack(k[:, sls[q]] * e_rem[q]) for q in prs]
    vs = [stack(v[:, sl]) for sl in sls]

    g = [_bdot(cat([xa[q], xr[q]], axis=0), cat([yb[q], yk[q]], axis=0), NT_DIMS) for q in prs]
    m_ab = [jnp.where(strict, x[0:n2, 0:n2], 0.0) for x in g]
    m_ak = [jnp.where(strict, x[0:n2, n2:2 * n2], 0.0).astype(BF16) for x in g]
    n_rb = [jnp.where(incl, x[n2:2 * n2, 0:n2], 0.0) for x in g]
    n_rk = [jnp.where(incl, x[n2:2 * n2, n2:2 * n2], 0.0).astype(BF16) for x in g]

    s_old = [s_ref[q] for q in prs]
    lhs = [cat([cat([xa[q], m_ak[q]], axis=1), cat([xr[q], n_rk[q]], axis=1)], axis=0) for q in prs]
    xy0 = [_bdot(lhs[q], cat([s_old[q].T.astype(BF16), vs[q]], axis=0)) for q in prs]

    x = [xy0[q][0:n2] for q in prs]
    mk = m_ab
    nlev = int(math.log2(c_len))
    for lev in range(nlev):
        if lev < nlev - 1:
            res = [_bdot(mk[q], cat([mk[q], x[q]], axis=1)) for q in prs]
            mk = [t[:, 0:n2] for t in res]
            x = [x[q] + res[q][:, n2:2 * n2] for q in prs]
        else:
            x = [x[q] + _bdot(mk[q], x[q]) for q in prs]
    u = [t.astype(BF16) for t in x]

    y_st = [xy0[q][n2:2 * n2] + _bdot(n_rb[q], u[q]) for q in prs]
    for q in prs:
        s_ref[q] = s_old[q] * jnp.exp(tot[q]) + _bdot(cat([u[q], vs[q]], axis=0), cat([zb[q], zk[q]], axis=0),
                                                     TN_DIMS)
    return cat([t[0:c_len] + t[c_len:n2] for t in y_st], axis=1)


def _rwkv_prompt_kernel(zr_ref, *refs, nc):
    p_refs = refs[:9]
    lng_ref, lnb_ref, yb_ref, sout_ref, s_ref, carry_ref = refs[9:]
    c = pl.program_id(1)
    nseq, c_len, _ = zr_ref.shape

    @pl.when(c == 0)
    def _():
        s_ref[...] = jnp.zeros_like(s_ref)
        carry_ref[...] = jnp.zeros_like(carry_ref)

    first = lax.broadcasted_iota(jnp.int32, (c_len, ZR_W), 0) == 0
    preps = []
    for q in range(nseq):
        x = zr_ref[q]
        rolled = pltpu.roll(x, 1, axis=0)
        prev = jnp.where(first, carry_ref[q, 0:1, :], rolled)
        carry_ref[q] = rolled[0:SUBLANES, :]
        preps.append(_prep_math(x, prev, p_refs))
    r, lw, k, v, a, b = (jnp.concatenate([p[i] for p in preps], axis=1) for i in range(6))
    y = _chunk_scan(r, lw, k, v, a, b, s_ref, c_len)
    for q in range(nseq):
        yq = y[:, q * RWKV_W:(q + 1) * RWKV_W]
        yb_ref[q] = _group_norm_gate(yq, preps[q][7], preps[q][6], lng_ref, lnb_ref).astype(yb_ref.dtype)

    @pl.when(c == nc - 1)
    def _():
        sout_ref[...] = s_ref[...].reshape(sout_ref.shape)


def _rwkv_prompt(z, nb, t_len, params, ln_g, ln_b, c_len=64, nseq=2):
    nc = t_len // c_len
    npair = RWKV_W // LANES
    par = pl.BlockSpec((1, RWKV_W), lambda b, c: (0, 0))
    yb, state = pl.pallas_call(
        functools.partial(_rwkv_prompt_kernel, nc=nc),
        grid=(nb // nseq, nc),
        in_specs=[pl.BlockSpec((nseq, c_len, ZR_W), lambda b, c: (b, c, 0))] + _prep_param_specs(params) + [par, par],
        out_specs=[pl.BlockSpec((nseq, c_len, RWKV_W), lambda b, c: (b, c, 0)),
                   pl.BlockSpec((nseq, npair, LANES, LANES), lambda b, c: (b, 0, 0, 0))],
        out_shape=[jax.ShapeDtypeStruct((nb, t_len, RWKV_W), BF16),
                   jax.ShapeDtypeStruct((nb, npair, LANES, LANES), F32)],
        scratch_shapes=[pltpu.VMEM((nseq * npair, LANES, LANES), F32), pltpu.VMEM((nseq, SUBLANES, ZR_W), F32)],
        compiler_params=_cparams(("parallel", "arbitrary")),
        name="rwkv_prompt",
    )(z.reshape(nb, t_len, z.shape[1]), *params, ln_g, ln_b)
    return yb.reshape(nb * t_len, RWKV_W), state


def _wkv_step_kernel(s_ref, r_ref, lw_ref, k_ref, v_ref, a_ref, b_ref, g_ref, bonus_ref, lng_ref, lnb_ref,
                     yb_ref, so_ref, vt_scr, y_scr):
    rt, wt, kt, at, bt = (x[...].T for x in (r_ref, lw_ref, k_ref, a_ref, b_ref))
    wt = jnp.exp(wt)
    vt_scr[...] = v_ref[...].T
    for hh in range(2):
        rows = slice(hh * HEAD, (hh + 1) * HEAD)
        r, w, k, a, b = (x[rows, :] for x in (rt, wt, kt, at, bt))

        def body(i, carry):
            si = s_ref[hh, i]
            sa = jnp.sum(si * a, axis=0, keepdims=True)
            vi = vt_scr[pl.ds(hh * HEAD + i, 1), :]
            s2 = si * w + sa * b + vi * k
            so_ref[hh, i] = s2
            y_scr[pl.ds(hh * HEAD + i, 1), :] = jnp.sum(s2 * r, axis=0, keepdims=True)
            return carry

        lax.fori_loop(0, HEAD, body, 0, unroll=4)

    outs = []
    for hh in range(2):
        y = y_scr[hh * HEAD:(hh + 1) * HEAD, :]
        d = y - jnp.mean(y, axis=0, keepdims=True)
        var = jnp.mean(d * d, axis=0, keepdims=True)
        outs.append(d * lax.rsqrt(var + GN_EPS))
    yn = jnp.concatenate(outs, axis=0).T * lng_ref[...] + lnb_ref[...]
    yb_ref[...] = ((yn + bonus_ref[...]) * g_ref[...]).astype(yb_ref.dtype)


def _wkv_step(state_t, vecs, ln_g, ln_b):
    nh, _, _, nb = state_t.shape
    st = pl.BlockSpec((2, HEAD, HEAD, nb), lambda p: (p, 0, 0, 0))
    vec = pl.BlockSpec((nb, 2 * HEAD), lambda p: (0, p))
    par = pl.BlockSpec((1, 2 * HEAD), lambda p: (0, p))
    return pl.pallas_call(
        _wkv_step_kernel,
        grid=(nh // 2,),
        in_specs=[st] + [vec] * 8 + [par, par],
        out_specs=[vec, st],
        out_shape=[jax.ShapeDtypeStruct((nb, nh * HEAD), BF16), jax.ShapeDtypeStruct(state_t.shape, F32)],
        scratch_shapes=[pltpu.VMEM((2 * HEAD, nb), F32), pltpu.VMEM((2 * HEAD, nb), F32)],
        compiler_params=_cparams(("parallel",)),
        name="wkv_step",
    )(state_t, *vecs, ln_g, ln_b)


def _xattn_prompt_kernel(q_ref, k_ref, v_ref, o_ref):
    scale = XA_DIM ** -0.5
    q = q_ref[...]
    for h in range(XA_HEADS):
        sl = slice(h * XA_DIM, (h + 1) * XA_DIM)
        s = lax.dot_general(q[:, sl].astype(BF16), k_ref[0, :, sl].astype(BF16),
                            NT_DIMS, preferred_element_type=F32) * scale
        p = jnp.exp(s - jnp.max(s, axis=-1, keepdims=True))
        den = jnp.sum(p, axis=-1, keepdims=True)
        o = jnp.dot(p.astype(BF16), v_ref[0, :, sl].astype(BF16), preferred_element_type=F32)
        o_ref[:, sl] = (o / den).astype(o_ref.dtype)


def _xattn_prompt(z, nb, t_len, col_blk, mk, mv, tq=512):
    nt = t_len // tq
    xw = XA_HEADS * XA_DIM
    nmem = mk.shape[1]
    kv = pl.BlockSpec((1, nmem, xw), lambda b, t: (b, 0, 0))
    return pl.pallas_call(
        _xattn_prompt_kernel,
        grid=(nb, nt),
        in_specs=[pl.BlockSpec((tq, xw), lambda b, t: (b * nt + t, col_blk)), kv, kv],
        out_specs=pl.BlockSpec((tq, xw), lambda b, t: (b * nt + t, 0)),
        out_shape=jax.ShapeDtypeStruct((nb * t_len, xw), BF16),
        compiler_params=_cparams(("parallel", "parallel")),
        name="xattn_prompt",
    )(z, mk, mv)


def _xattn_sample_kernel(q_ref, k_ref, v_ref, o_ref):
    bb = q_ref.shape[0]
    nrow = k_ref.shape[1] // SUBLANES
    full = (bb, nrow, SUBLANES, XA_DIM)
    q = q_ref[...] * (XA_DIM ** -0.5)
    q8 = jnp.concatenate([q, q], axis=1)[:, None]
    k = k_ref[...].reshape(full)
    s = jnp.broadcast_to(jnp.sum(k * q8, axis=-1, keepdims=True), full)
    mx = jnp.max(s, axis=1, keepdims=True)
    mx = jnp.maximum(mx, pltpu.roll(mx, XA_HEADS, axis=2))
    p = jnp.exp(s - mx)
    den = jnp.sum(p, axis=1, keepdims=True)
    den = den + pltpu.roll(den, XA_HEADS, axis=2)
    o = jnp.sum(p * v_ref[...].reshape(full), axis=1, keepdims=True)
    o = o + pltpu.roll(o, XA_HEADS, axis=2)
    o_ref[...] = (o / den)[:, 0, 0:XA_HEADS, :].astype(o_ref.dtype)


def _xattn_sample(q3, mk, mv, bb=8):
    nb, rows, _ = mk.shape
    kv = pl.BlockSpec((bb, rows, XA_DIM), lambda i: (i, 0, 0))
    qs = pl.BlockSpec((bb, XA_HEADS, XA_DIM), lambda i: (i, 0, 0))
    return pl.pallas_call(
        _xattn_sample_kernel,
        grid=(nb // bb,),
        in_specs=[qs, kv, kv],
        out_specs=qs,
        out_shape=jax.ShapeDtypeStruct((nb, XA_HEADS, XA_DIM), BF16),
        compiler_params=_cparams(("parallel",)),
        name="xattn_sample",
    )(q3, mk, mv)


def _merge_kernel(pa_ref, pb_ref, pc_ref, g0_ref, g1_ref, g2_ref, h_ref, wa_ref, wb_ref, wc_ref, wo_ref, o_ref):
    oa = jnp.dot(pa_ref[...], wa_ref[...], preferred_element_type=F32)
    ob = jnp.dot(pb_ref[...], wb_ref[...], preferred_element_type=F32)
    oc = jnp.dot(pc_ref[...], wc_ref[...], preferred_element_type=F32)
    merged = (jax.nn.sigmoid(g0_ref[...]) * oa + jax.nn.sigmoid(g1_ref[...]) * ob
              + jax.nn.sigmoid(g2_ref[...]) * oc)
    o_ref[...] = h_ref[...] + jnp.dot(merged.astype(BF16), wo_ref[...], preferred_element_type=F32)


def _merge(pa, pb, pc, z, zg_blk0, h, tm, wa, wb, wc, wo):
    m, d = h.shape
    const = lambda i: (0, 0)
    resident = lambda w: pl.BlockSpec(w.shape, const, pipeline_mode=pl.Buffered(1))
    in_specs = [
        pl.BlockSpec((tm, pa.shape[1]), lambda i: (i, 0)),
        pl.BlockSpec((tm, pb.shape[1]), lambda i: (i, 0)),
        pl.BlockSpec((tm, pc.shape[1]), lambda i: (i, 0)),
        pl.BlockSpec((tm, d), lambda i: (i, zg_blk0)),
        pl.BlockSpec((tm, d), lambda i: (i, zg_blk0 + 1)),
        pl.BlockSpec((tm, d), lambda i: (i, zg_blk0 + 2)),
        pl.BlockSpec((tm, d), lambda i: (i, 0)),
        resident(wa), resident(wb), resident(wc), resident(wo),
    ]
    return pl.pallas_call(
        _merge_kernel,
        grid=(m // tm,),
        in_specs=in_specs,
        out_specs=pl.BlockSpec((tm, d), lambda i: (i, 0)),
        out_shape=jax.ShapeDtypeStruct((m, d), F32),
        compiler_params=_cparams(("parallel",)),
        name="merge",
    )(pa, pb, pc, z, z, z, h, wa, wb, wc, wo)


def _pack_zr(x, axis):
    w = RWKV_W
    take = lambda a, b: lax.slice_in_dim(x, a, b, axis=axis)

    def pad(n):
        shape = list(x.shape)
        shape[axis] = n
        return jnp.zeros(shape, x.dtype)

    return jnp.concatenate([take(0, 3 * w + 64), pad(64), take(3 * w + 64, 3 * w + 128), pad(64),
                            take(3 * w + 128, ZR_TRUE), pad(96)], axis=axis)


def _unpack_zr_cols(x):
    w = RWKV_W
    return jnp.concatenate([x[..., :3 * w + 64], x[..., 3 * w + 128:3 * w + 192], x[..., 3 * w + 256:3 * w + 416]],
                           axis=-1)


def _pad_rows(x, n):
    return jnp.concatenate([x, jnp.zeros((n - x.shape[0],) + x.shape[1:], x.dtype)], axis=0)


def kernel(x_prompt, x_sample, mem_prompt, cache_mem_k, cache_mem_v, state_wkv, state_shift, state_pool,
           ffn1_norm_g, ffn1_w_gate, ffn1_w_up, ffn1_w_down, mix_norm_g, w_in,
           pool_group_w, pool_scale, pool_out,
           rwkv_mu, rwkv_w0, rwkv_w_up, rwkv_a0, rwkv_a_up, rwkv_g_up, rwkv_k_k, rwkv_k_a, rwkv_r_k,
           rwkv_ln_g, rwkv_ln_b, rwkv_out,
           mem_norm_g, w_mem_k, w_mem_v, xattn_out, w_o,
           ffn2_norm_g, ffn2_w_gate, ffn2_w_up, ffn2_w_down, final_norm_g):
    nb, t_len, d = x_prompt.shape
    ns = x_sample.shape[0]
    assert w_in.shape[0] == 1 and x_sample.shape[1] == 1
    n_mem = mem_prompt.shape[1]
    pool_w = pool_out.shape[1]
    xa_w = xattn_out.shape[1]
    n_heads = RWKV_W // HEAD
    nbuf = state_pool.shape[2]
    rows_p = nb * t_len
    l = 0

    d_ff = ffn1_w_gate.shape[2]
    f1 = [w.reshape(w.shape[1:]).astype(BF16) for w in (ffn1_w_gate, ffn1_w_up, ffn1_w_down)]
    f2 = [w.reshape(w.shape[1:]).astype(BF16) for w in (ffn2_w_gate, ffn2_w_up, ffn2_w_down)]
    wit = jnp.swapaxes(w_in[l], 0, 1)
    o_zr, o_zq, o_zg = pool_w, pool_w + ZR_TRUE, pool_w + ZR_TRUE + xa_w
    w_in_t = jnp.concatenate([_pack_zr(wit[o_zr:o_zq], 0), wit[:o_zr], wit[o_zg:], wit[o_zq:o_zg]],
                             axis=0).astype(BF16)
    col_zp = ZR_W // pool_w
    col_zg = (ZR_W + pool_w) // d
    col_zq = (ZR_W + pool_w + 3 * d) // xa_w
    row = lambda v: v.reshape(1, -1)
    prep_params = [row(_pack_zr(rwkv_mu[l], 0)), row(rwkv_w0[l]), _pad_rows(rwkv_w_up[l], 128).astype(BF16),
                   row(rwkv_a0[l]), _pad_rows(rwkv_a_up[l], 128).astype(BF16),
                   _pad_rows(rwkv_g_up[l], 256).astype(BF16), row(rwkv_k_k[l]), row(rwkv_k_a[l]),
                   row(rwkv_r_k[l])]
    gw = pool_group_w[l].astype(BF16)
    w_kv_t = jnp.concatenate([jnp.swapaxes(w_mem_k[l], 0, 1), jnp.swapaxes(w_mem_v[l], 0, 1)], axis=0).astype(BF16)
    wa, wb, wc, wo = (pool_out[l].astype(BF16), rwkv_out[l].astype(BF16), xattn_out[l].astype(BF16),
                      w_o[l].astype(BF16))
    g1, gm, g2, fg = row(ffn1_norm_g[l]), row(mix_norm_g[l]), row(ffn2_norm_g[l]), row(final_norm_g)
    scale = row(pool_scale[l])
    ln_g, ln_b = rwkv_ln_g[l], rwkv_ln_b[l]

    tm_p = 512
    h1_p, h1_s = _ffn(x_prompt.reshape(rows_p, d), x_sample.reshape(ns, d), tm_p, g1, *f1)
    z_p = _norm_matmul(h1_p, gm, w_in_t, tm=1024, tn=1536)
    z_s = _norm_matmul(h1_s, gm, w_in_t, tm=ns, tn=1536)
    kv = _norm_matmul(mem_prompt.reshape(nb * n_mem, d), row(mem_norm_g[l]), w_kv_t, tm=512, tn=512)
    mk_p = kv[:, :xa_w].reshape(nb, n_mem, xa_w)
    mv_p = kv[:, xa_w:].reshape(nb, n_mem, xa_w)

    pool_state = state_pool.reshape(ns, nbuf, pool_w)
    pa_p = _pool_prompt(z_p, nb, t_len, col_zp, gw, scale)
    pa_s = _pool_sample(z_s, col_zp, jnp.swapaxes(pool_state, 0, 1), gw, scale)

    pb_p, st_p = _rwkv_prompt(z_p, nb, t_len, prep_params, row(ln_g), row(ln_b))
    prep_s = _prep_sample(z_s, _pack_zr(state_shift.reshape(ns, ZR_TRUE), 1), prep_params)
    state_t = jnp.transpose(state_wkv.reshape(ns, n_heads, HEAD, HEAD), (1, 2, 3, 0))
    pb_s, wkv_s_t = _wkv_step(state_t, prep_s, row(ln_g), row(ln_b))

    pc_p = _xattn_prompt(z_p, nb, t_len, col_zq, mk_p, mv_p)
    q_s = z_s[:, ZR_W + pool_w + 3 * d:].reshape(ns, XA_HEADS, XA_DIM)
    pc_s = _xattn_sample(q_s, cache_mem_k.reshape(ns, n_mem * XA_HEADS, XA_DIM),
                         cache_mem_v.reshape(ns, n_mem * XA_HEADS, XA_DIM)).reshape(ns, xa_w)

    h2_p = _merge(pa_p, pb_p, pc_p, z_p, col_zg, h1_p, 256, wa, wb, wc, wo)
    h2_s = _merge(pa_s, pb_s, pc_s, z_s, col_zg, h1_s, ns, wa, wb, wc, wo)
    y_prompt, y_sample = _ffn(h2_p, h2_s, tm_p, g2, *f2, final_g=fg)

    ends = [(b + 1) * t_len for b in range(nb)]
    shift_p = _unpack_zr_cols(jnp.stack([z_p[e - 1:e, :ZR_W] for e in ends]))[None]
    pool_p = jnp.stack([z_p[e - nbuf:e, ZR_W:ZR_W + pool_w] for e in ends])[None]
    shift_s = _unpack_zr_cols(z_s[:, :ZR_W])[None, :, None, :]
    pool_s = jnp.concatenate([pool_state[:, 1:], z_s[:, None, ZR_W:ZR_W + pool_w]], axis=1)[None]
    st5 = st_p.reshape(nb, n_heads // 2, 2, HEAD, 2, HEAD)
    wkv_p = jnp.stack([st5[:, :, 0, :, 0, :], st5[:, :, 1, :, 1, :]], axis=2).reshape(1, nb, n_heads, HEAD, HEAD)
    wkv_s = jnp.transpose(wkv_s_t, (3, 0, 1, 2)).reshape(state_wkv.shape)
    mem_k_p = mk_p.reshape(1, nb, n_mem, XA_HEADS, XA_DIM)
    mem_v_p = mv_p.reshape(1, nb, n_mem, XA_HEADS, XA_DIM)
    return (y_prompt.reshape(nb, t_len, d), y_sample.reshape(ns, 1, d), mem_k_p, mem_v_p, wkv_p, shift_p, pool_p,
            wkv_s, shift_s, pool_s)
```

```python
import functools
import math

import jax
import jax.numpy as jnp
from jax import lax
from jax.experimental import pallas as pl
from jax.experimental.pallas import tpu as pltpu

F32 = jnp.float32
BF16 = jnp.bfloat16

RMS_EPS = 1e-6
GN_EPS = 64e-5
POOL_WINDOWS = (2, 4, 8, 16)
HEAD = 64
LANES = 128
SUBLANES = 8
XA_HEADS = 4
XA_DIM = 128
PAST_LEN = 16384
VMEM_LIMIT = 60 * 1024 * 1024
EXP_M05 = math.exp(-0.5)

ZR_W = 3584
RWKV_W = 1024
ZR_TRUE = 3360

NN_DIMS = (((1,), (0,)), ((), ()))
NT_DIMS = (((1,), (1,)), ((), ()))
TN_DIMS = (((0,), (0,)), ((), ()))


def _cparams(sem):
    return pltpu.CompilerParams(dimension_semantics=sem, vmem_limit_bytes=VMEM_LIMIT)


def _rms(x, g):
    ms = jnp.mean(x * x, axis=-1, keepdims=True)
    return x * lax.rsqrt(ms + RMS_EPS) * g


def _bdot(a, b, dims=NN_DIMS):
    return lax.dot_general(a.astype(BF16), b.astype(BF16), dims, preferred_element_type=F32)


def _ffn_kernel(hp_ref, hs_ref, g_ref, fg_ref, wg_ref, wu_ref, wd_ref, wgt_ref, wut_ref, wdt_ref,
                op_ref, os_ref, xp_ref, xs_ref, ap_ref, as_ref, *, nfull, final):
    m = pl.program_id(0)
    f = pl.program_id(1)

    def start(h_ref, x_ref, acc_ref):
        x_ref[...] = _rms(h_ref[...], g_ref[...]).astype(BF16)
        acc_ref[...] = jnp.zeros_like(acc_ref)

    def contribution(x_ref, wg, wu, wd):
        xn = x_ref[...]
        gate = jnp.dot(xn, wg, preferred_element_type=F32)
        up = jnp.dot(xn, wu, preferred_element_type=F32)
        act = (gate * jax.nn.sigmoid(gate) * up).astype(BF16)
        return jnp.dot(act, wd, preferred_element_type=F32)

    def finish(h_ref, acc_ref, o_ref, last):
        out = h_ref[...] + 0.5 * (acc_ref[...] + last)
        if final:
            out = _rms(out, fg_ref[...])
        o_ref[...] = out

    @pl.when(f == 0)
    def _():
        start(hp_ref, xp_ref, ap_ref)

    @pl.when((f == 0) & (m == 0))
    def _():
        start(hs_ref, xs_ref, as_ref)

    @pl.when(f < nfull)
    def _():
        wg, wu, wd = wg_ref[...], wu_ref[...], wd_ref[...]
        ap_ref[...] += contribution(xp_ref, wg, wu, wd)

        @pl.when(m == 0)
        def _():
            as_ref[...] += contribution(xs_ref, wg, wu, wd)

    @pl.when(f == nfull)
    def _():
        wg, wu, wd = wgt_ref[...], wut_ref[...], wdt_ref[...]
        finish(hp_ref, ap_ref, op_ref, contribution(xp_ref, wg, wu, wd))

        @pl.when(m == 0)
        def _():
            finish(hs_ref, as_ref, os_ref, contribution(xs_ref, wg, wu, wd))


def _ffn(hp, hs, tm, g, wg, wu, wd, final_g=None, tf=512):
    mp, d = hp.shape
    ms = hs.shape[0]
    d_ff = wg.shape[1]
    nfull, tail = divmod(d_ff, tf)
    assert tail > 0 and tail % LANES == 0 and mp % tm == 0
    last_main = nfull - 1
    split = nfull * tf
    wgt, wut, wdt = wg[:, split:], wu[:, split:], wd[split:]
    final = final_g is not None
    vec = pl.BlockSpec((1, d), lambda i, f: (0, 0))
    once = pl.Buffered(1)
    in_specs = [
        pl.BlockSpec((tm, d), lambda i, f: (i, 0)),
        pl.BlockSpec((ms, d), lambda i, f: (0, 0), pipeline_mode=once),
        vec, vec,
        pl.BlockSpec((d, tf), lambda i, f: (0, jnp.minimum(f, last_main))),
        pl.BlockSpec((d, tf), lambda i, f: (0, jnp.minimum(f, last_main))),
        pl.BlockSpec((tf, d), lambda i, f: (jnp.minimum(f, last_main), 0)),
        pl.BlockSpec((d, tail), lambda i, f: (0, 0), pipeline_mode=once),
        pl.BlockSpec((d, tail), lambda i, f: (0, 0), pipeline_mode=once),
        pl.BlockSpec((tail, d), lambda i, f: (0, 0), pipeline_mode=once),
    ]
    return pl.pallas_call(
        functools.partial(_ffn_kernel, nfull=nfull, final=final),
        grid=(mp // tm, nfull + 1),
        in_specs=in_specs,
        out_specs=[pl.BlockSpec((tm, d), lambda i, f: (i, 0)), pl.BlockSpec((ms, d), lambda i, f: (0, 0))],
        out_shape=[jax.ShapeDtypeStruct((mp, d), F32), jax.ShapeDtypeStruct((ms, d), F32)],
        scratch_shapes=[pltpu.VMEM((tm, d), BF16), pltpu.VMEM((ms, d), BF16),
                        pltpu.VMEM((tm, d), F32), pltpu.VMEM((ms, d), F32)],
        compiler_params=_cparams(("parallel", "arbitrary")),
        name="ffn",
    )(hp, hs, g, g if final_g is None else final_g, wg, wu, wd, wgt, wut, wdt)


def _norm_matmul_kernel(h_ref, g_ref, wt_ref, o_ref, xn_ref):
    @pl.when(pl.program_id(1) == 0)
    def _():
        xn_ref[...] = _rms(h_ref[...], g_ref[...]).astype(BF16)

    o_ref[...] = lax.dot_general(xn_ref[...], wt_ref[...], NT_DIMS, preferred_element_type=F32)


def _norm_matmul(h, g, wt, tm, tn):
    m, d = h.shape
    n = wt.shape[0]
    return pl.pallas_call(
        _norm_matmul_kernel,
        grid=(m // tm, n // tn),
        in_specs=[
            pl.BlockSpec((tm, d), lambda i, j: (i, 0)),
            pl.BlockSpec((1, d), lambda i, j: (0, 0)),
            pl.BlockSpec((tn, d), lambda i, j: (j, 0)),
        ],
        out_specs=pl.BlockSpec((tm, tn), lambda i, j: (i, j)),
        out_shape=jax.ShapeDtypeStruct((m, n), F32),
        scratch_shapes=[pltpu.VMEM((tm, d), BF16)],
        compiler_params=_cparams(("parallel", "arbitrary")),
        name="norm_matmul",
    )(h, g, wt)


def _pool_mix(pooled_groups, gw_ref, scale_ref, o_ref):
    for gi, pooled in enumerate(pooled_groups):
        sl = slice(gi * LANES, (gi + 1) * LANES)
        mixed = jnp.dot(pooled.astype(BF16), gw_ref[gi], preferred_element_type=F32)
        o_ref[:, sl] = (mixed * scale_ref[:, sl]).astype(o_ref.dtype)


def _pool_prompt_kernel(zp_ref, gw_ref, scale_ref, o_ref, ext_ref, *, tt):
    t = pl.program_id(1)
    hist = 16

    @pl.when(t == 0)
    def _():
        ext_ref[0:hist, :] = jnp.zeros((hist, ext_ref.shape[1]), F32)

    x = zp_ref[...]
    ext_ref[hist:hist + tt, :] = x
    pos = t * tt + lax.broadcasted_iota(jnp.int32, (tt, LANES), 0)
    groups = []
    for gi, w in enumerate(POOL_WINDOWS):
        sl = slice(gi * LANES, (gi + 1) * LANES)
        acc = x[:, sl]
        for k in range(1, w):
            acc = acc + ext_ref[hist - k:hist - k + tt, sl]
        cnt = jnp.minimum(pos + 1, w).astype(F32)
        groups.append(acc / cnt - x[:, sl])
    _pool_mix(groups, gw_ref, scale_ref, o_ref)
    ext_ref[0:hist, :] = ext_ref[tt:tt + hist, :]


def _pool_prompt(z, nb, t_len, col_blk, gw, scale, tt=1024):
    nt = t_len // tt
    pw = gw.shape[0] * LANES
    return pl.pallas_call(
        functools.partial(_pool_prompt_kernel, tt=tt),
        grid=(nb, nt),
        in_specs=[
            pl.BlockSpec((tt, pw), lambda b, t: (b * nt + t, col_blk)),
            pl.BlockSpec(gw.shape, lambda b, t: (0, 0, 0)),
            pl.BlockSpec((1, pw), lambda b, t: (0, 0)),
        ],
        out_specs=pl.BlockSpec((tt, pw), lambda b, t: (b * nt + t, 0)),
        out_shape=jax.ShapeDtypeStruct((nb * t_len, pw), BF16),
        scratch_shapes=[pltpu.VMEM((tt + 16, pw), F32)],
        compiler_params=_cparams(("parallel", "arbitrary")),
        name="pool_prompt",
    )(z, gw, scale)


def _pool_sample_kernel(zp_ref, buf_ref, gw_ref, scale_ref, o_ref):
    x = zp_ref[...]
    nbuf = buf_ref.shape[0]
    groups = []
    for gi, w in enumerate(POOL_WINDOWS):
        sl = slice(gi * LANES, (gi + 1) * LANES)
        acc = x[:, sl]
        for k in range(1, w):
            acc = acc + buf_ref[nbuf - k, :, sl]
        cnt = float(min(PAST_LEN + 1, w))
        groups.append(acc / cnt - x[:, sl])
    _pool_mix(groups, gw_ref, scale_ref, o_ref)


def _pool_sample(z, col_blk, buf_t, gw, scale):
    nrows = z.shape[0]
    pw = gw.shape[0] * LANES
    return pl.pallas_call(
        _pool_sample_kernel,
        grid=(1,),
        in_specs=[
            pl.BlockSpec((nrows, pw), lambda i: (0, col_blk)),
            pl.BlockSpec(buf_t.shape, lambda i: (0, 0, 0)),
            pl.BlockSpec(gw.shape, lambda i: (0, 0, 0)),
            pl.BlockSpec((1, pw), lambda i: (0, 0)),
        ],
        out_specs=pl.BlockSpec((nrows, pw), lambda i: (0, 0)),
        out_shape=jax.ShapeDtypeStruct((nrows, pw), BF16),
        compiler_params=_cparams(("arbitrary",)),
        name="pool_sample",
    )(z, buf_t, gw, scale)


def _head_sum(x):
    head_a = lax.broadcasted_iota(jnp.int32, (x.shape[0], LANES), 1) < HEAD
    cols = []
    for c in range(x.shape[1] // LANES):
        t = x[:, c * LANES:(c + 1) * LANES]
        sa = jnp.sum(jnp.where(head_a, t, 0.0), axis=-1, keepdims=True)
        sb = jnp.sum(jnp.where(head_a, 0.0, t), axis=-1, keepdims=True)
        cols.append(jnp.where(head_a, sa, sb))
    return jnp.concatenate(cols, axis=1)


def _prep_math(x, prev, p_refs):
    mu_ref, w0_ref, wup_ref, a0_ref, aup_ref, gup_ref, kk_ref, ka_ref, rk_ref = p_refs
    xm = x + (prev - x) * mu_ref[...]
    w = RWKV_W
    r = xm[:, 0:w]
    k = xm[:, w:2 * w]
    v = xm[:, 2 * w:3 * w]
    wl = xm[:, 3 * w:3 * w + 128]
    al = xm[:, 3 * w + 128:3 * w + 256]
    gl = xm[:, 3 * w + 256:3 * w + 512]
    dw = w0_ref[...] + jnp.dot(jnp.tanh(wl).astype(BF16), wup_ref[...], preferred_element_type=F32)
    lw = -EXP_M05 * jax.nn.sigmoid(dw)
    a = jax.nn.sigmoid(a0_ref[...] + jnp.dot(al.astype(BF16), aup_ref[...], preferred_element_type=F32))
    g = jnp.dot(jax.nn.sigmoid(gl).astype(BF16), gup_ref[...], preferred_element_type=F32)
    kk = k * kk_ref[...]
    kk = kk * lax.rsqrt(jnp.maximum(_head_sum(kk * kk), 1e-24))
    kmod = k * (1.0 + (a - 1.0) * ka_ref[...])
    bonus = _head_sum(r * kmod * rk_ref[...]) * v
    return r, lw, kmod, v, -kk, kk * a, g, bonus


def _group_norm_gate(y, bonus, g, lng_ref, lnb_ref):
    d = y - _head_sum(y) * (1.0 / HEAD)
    var = _head_sum(d * d) * (1.0 / HEAD)
    yn = d * lax.rsqrt(var + GN_EPS) * lng_ref[...] + lnb_ref[...]
    return (yn + bonus) * g


def _prep_sample_kernel(zr_ref, prev_ref, *refs):
    p_refs, out_refs = refs[:9], refs[9:17]
    for o_ref, val in zip(out_refs, _prep_math(zr_ref[...], prev_ref[...], p_refs)):
        o_ref[...] = val


def _prep_param_specs(params):
    zero = (lambda *idx: (0, 0))
    return [pl.BlockSpec(p.shape, zero) for p in params]


def _prep_sample(z, prev, params):
    nrows = z.shape[0]
    out_spec = pl.BlockSpec((nrows, RWKV_W), lambda i: (0, 0))
    return pl.pallas_call(
        _prep_sample_kernel,
        grid=(1,),
        in_specs=[pl.BlockSpec((nrows, ZR_W), lambda i: (0, 0)),
                  pl.BlockSpec((nrows, ZR_W), lambda i: (0, 0))] + _prep_param_specs(params),
        out_specs=[out_spec] * 8,
        out_shape=[jax.ShapeDtypeStruct((nrows, RWKV_W), F32)] * 8,
        compiler_params=_cparams(("arbitrary",)),
        name="prep_sample",
    )(z, prev, *params)


def _split3(x):
    hi = x.astype(BF16)
    rest = x - hi.astype(F32)
    mid = rest.astype(BF16)
    lo = (rest - mid.astype(F32)).astype(BF16)
    return hi, mid, lo


def _select_dot(sel, x):
    sel = sel.astype(BF16)
    hi, mid, lo = _split3(x)
    return _bdot(sel, hi) + (_bdot(sel, mid) + _bdot(sel, lo))


def _chunk_scan(r, lw, k, v, a, b, s_ref, c_len):
    n2 = 2 * c_len
    assert n2 == LANES
    row = lax.broadcasted_iota(jnp.int32, (n2, n2), 0)
    col = lax.broadcasted_iota(jnp.int32, (n2, n2), 1)
    tr = row & (c_len - 1)
    tc = col & (c_len - 1)
    strict = tr > tc
    incl = tr >= tc
    tri = (lax.broadcasted_iota(jnp.int32, (c_len, c_len), 0)
           >= lax.broadcasted_iota(jnp.int32, (c_len, c_len), 1))
    head_a = lax.broadcasted_iota(jnp.int32, (c_len, LANES), 1) < HEAD

    def stack(x):
        return jnp.concatenate([jnp.where(head_a, x, 0.0), jnp.where(head_a, 0.0, x)], axis=0).astype(BF16)

    prs = range(r.shape[1] // LANES)
    cat = jnp.concatenate
    sls = [slice(q * LANES, (q + 1) * LANES) for q in prs]
    lwq = [lw[:, sl] for sl in sls]
    cum = [_select_dot(tri, x) for x in lwq]
    tot = [x[c_len - 1:c_len, :] for x in cum]
    xr = [stack(r[:, sls[q]] * jnp.exp(cum[q])) for q in prs]
    xa = [stack(a[:, sls[q]] * jnp.exp(cum[q] - lwq[q])) for q in prs]
    e_neg = [jnp.exp(-x) for x in cum]
    e_rem = [jnp.exp(tot[q] - cum[q]) for q in prs]
    yb = [stack(b[:, sls[q]] * e_neg[q]) for q in prs]
    yk = [stack(k[:, sls[q]] * e_neg[q]) for q in prs]
    zb = [stack(b[:, sls[q]] * e_rem[q]) for q in prs]
    zk = [stack(k[:, sls[q]] * e_rem[q]) for q in prs]
    vs = [stack(v[:, sl]) for sl in sls]

    g = [_bdot(cat([xa[q], xr[q]], axis=0), cat([yb[q], yk[q]], axis=0), NT_DIMS) for q in prs]
    m_ab = [jnp.where(strict, x[0:n2, 0:n2], 0.0) for x in g]
    m_ak = [jnp.where(strict, x[0:n2, n2:2 * n2], 0.0).astype(BF16) for x in g]
    n_rb = [jnp.where(incl, x[n2:2 * n2, 0:n2], 0.0) for x in g]
    n_rk = [jnp.where(incl, x[n2:2 * n2, n2:2 * n2], 0.0).astype(BF16) for x in g]

    s_old = [s_ref[q] for q in prs]
    lhs = [cat([cat([xa[q], m_ak[q]], axis=1), cat([xr[q], n_rk[q]], axis=1)], axis=0) for q in prs]
    xy0 = [_bdot(lhs[q], cat([s_old[q].T.astype(BF16), vs[q]], axis=0)) for q in prs]

    x = [xy0[q][0:n2] for q in prs]
    mk = m_ab
    nlev = int(math.log2(c_len))
    for lev in range(nlev):
        if lev < nlev - 1:
            res = [_bdot(mk[q], cat([mk[q], x[q]], axis=1)) for q in prs]
            mk = [t[:, 0:n2] for t in res]
            x = [x[q] + res[q][:, n2:2 * n2] for q in prs]
        else:
            x = [x[q] + _bdot(mk[q], x[q]) for q in prs]
    u = [t.astype(BF16) for t in x]

    y_st = [xy0[q][n2:2 * n2] + _bdot(n_rb[q], u[q]) for q in prs]
    for q in prs:
        s_ref[q] = s_old[q] * jnp.exp(tot[q]) + _bdot(cat([u[q], vs[q]], axis=0), cat([zb[q], zk[q]], axis=0),
                                                     TN_DIMS)
    return cat([t[0:c_len] + t[c_len:n2] for t in y_st], axis=1)


def _rwkv_prompt_kernel(zr_ref, *refs, nc):
    p_refs = refs[:9]
    lng_ref, lnb_ref, yb_ref, sout_ref, s_ref, carry_ref = refs[9:]
    c = pl.program_id(1)
    nseq, c_len, _ = zr_ref.shape

    @pl.when(c == 0)
    def _():
        s_ref[...] = jnp.zeros_like(s_ref)
        carry_ref[...] = jnp.zeros_like(carry_ref)

    first = lax.broadcasted_iota(jnp.int32, (c_len, ZR_W), 0) == 0
    preps = []
    for q in range(nseq):
        x = zr_ref[q]
        rolled = pltpu.roll(x, 1, axis=0)
        prev = jnp.where(first, carry_ref[q, 0:1, :], rolled)
        carry_ref[q] = rolled[0:SUBLANES, :]
        preps.append(_prep_math(x, prev, p_refs))
    r, lw, k, v, a, b = (jnp.concatenate([p[i] for p in preps], axis=1) for i in range(6))
    y = _chunk_scan(r, lw, k, v, a, b, s_ref, c_len)
    for q in range(nseq):
        yq = y[:, q * RWKV_W:(q + 1) * RWKV_W]
        yb_ref[q] = _group_norm_gate(yq, preps[q][7], preps[q][6], lng_ref, lnb_ref).astype(yb_ref.dtype)

    @pl.when(c == nc - 1)
    def _():
        npair = s_ref.shape[0] // nseq
        for q in range(nseq):
            for p in range(npair):
                s = s_ref[q * npair + p]
                sout_ref[q, 2 * p] = s[0:HEAD, 0:HEAD]
                sout_ref[q, 2 * p + 1] = s[HEAD:2 * HEAD, HEAD:2 * HEAD]


def _rwkv_prompt(z, nb, t_len, params, ln_g, ln_b, c_len=64, nseq=2):
    nc = t_len // c_len
    npair = RWKV_W // LANES
    par = pl.BlockSpec((1, RWKV_W), lambda b, c: (0, 0))
    yb, state = pl.pallas_call(
        functools.partial(_rwkv_prompt_kernel, nc=nc),
        grid=(nb // nseq, nc),
        in_specs=[pl.BlockSpec((nseq, c_len, ZR_W), lambda b, c: (b, c, 0))] + _prep_param_specs(params) + [par, par],
        out_specs=[pl.BlockSpec((nseq, c_len, RWKV_W), lambda b, c: (b, c, 0)),
                   pl.BlockSpec((nseq, 2 * npair, HEAD, HEAD), lambda b, c: (b, 0, 0, 0))],
        out_shape=[jax.ShapeDtypeStruct((nb, t_len, RWKV_W), BF16),
                   jax.ShapeDtypeStruct((nb, 2 * npair, HEAD, HEAD), F32)],
        scratch_shapes=[pltpu.VMEM((nseq * npair, LANES, LANES), F32), pltpu.VMEM((nseq, SUBLANES, ZR_W), F32)],
        compiler_params=_cparams(("parallel", "arbitrary")),
        name="rwkv_prompt",
    )(z.reshape(nb, t_len, z.shape[1]), *params, ln_g, ln_b)
    return yb.reshape(nb * t_len, RWKV_W), state


def _wkv_step_kernel(s_ref, r_ref, lw_ref, k_ref, v_ref, a_ref, b_ref, g_ref, bonus_ref, lng_ref, lnb_ref,
                     yb_ref, so_ref, vt_scr, y_scr):
    rt, wt, kt, at, bt = (x[...].T for x in (r_ref, lw_ref, k_ref, a_ref, b_ref))
    wt = jnp.exp(wt)
    vt_scr[...] = v_ref[...].T
    for hh in range(2):
        rows = slice(hh * HEAD, (hh + 1) * HEAD)
        r, w, k, a, b = (x[rows, :] for x in (rt, wt, kt, at, bt))

        def body(i, carry):
            si = s_ref[hh, i]
            sa = jnp.sum(si * a, axis=0, keepdims=True)
            vi = vt_scr[pl.ds(hh * HEAD + i, 1), :]
            s2 = si * w + sa * b + vi * k
            so_ref[hh, i] = s2
            y_scr[pl.ds(hh * HEAD + i, 1), :] = jnp.sum(s2 * r, axis=0, keepdims=True)
            return carry

        lax.fori_loop(0, HEAD, body, 0, unroll=4)

    outs = []
    for hh in range(2):
        y = y_scr[hh * HEAD:(hh + 1) * HEAD, :]
        d = y - jnp.mean(y, axis=0, keepdims=True)
        var = jnp.mean(d * d, axis=0, keepdims=True)
        outs.append(d * lax.rsqrt(var + GN_EPS))
    yn = jnp.concatenate(outs, axis=0).T * lng_ref[...] + lnb_ref[...]
    yb_ref[...] = ((yn + bonus_ref[...]) * g_ref[...]).astype(yb_ref.dtype)


def _wkv_step(state_t, vecs, ln_g, ln_b):
    nh, _, _, nb = state_t.shape
    st = pl.BlockSpec((2, HEAD, HEAD, nb), lambda p: (p, 0, 0, 0))
    vec = pl.BlockSpec((nb, 2 * HEAD), lambda p: (0, p))
    par = pl.BlockSpec((1, 2 * HEAD), lambda p: (0, p))
    return pl.pallas_call(
        _wkv_step_kernel,
        grid=(nh // 2,),
        in_specs=[st] + [vec] * 8 + [par, par],
        out_specs=[vec, st],
        out_shape=[jax.ShapeDtypeStruct((nb, nh * HEAD), BF16), jax.ShapeDtypeStruct(state_t.shape, F32)],
        scratch_shapes=[pltpu.VMEM((2 * HEAD, nb), F32), pltpu.VMEM((2 * HEAD, nb), F32)],
        compiler_params=_cparams(("parallel",)),
        name="wkv_step",
    )(state_t, *vecs, ln_g, ln_b)


def _xattn_prompt_kernel(q_ref, k_ref, v_ref, o_ref):
    scale = XA_DIM ** -0.5
    q = q_ref[...]
    for h in range(XA_HEADS):
        sl = slice(h * XA_DIM, (h + 1) * XA_DIM)
        s = lax.dot_general(q[:, sl].astype(BF16), k_ref[0, :, sl].astype(BF16),
                            NT_DIMS, preferred_element_type=F32) * scale
        p = jnp.exp(s - jnp.max(s, axis=-1, keepdims=True))
        den = jnp.sum(p, axis=-1, keepdims=True)
        o = jnp.dot(p.astype(BF16), v_ref[0, :, sl].astype(BF16), preferred_element_type=F32)
        o_ref[:, sl] = (o / den).astype(o_ref.dtype)


def _xattn_prompt(z, nb, t_len, col_blk, mk, mv, tq=1024):
    nt = t_len // tq
    xw = XA_HEADS * XA_DIM
    nmem = mk.shape[1]
    kv = pl.BlockSpec((1, nmem, xw), lambda b, t: (b, 0, 0))
    return pl.pallas_call(
        _xattn_prompt_kernel,
        grid=(nb, nt),
        in_specs=[pl.BlockSpec((tq, xw), lambda b, t: (b * nt + t, col_blk)), kv, kv],
        out_specs=pl.BlockSpec((tq, xw), lambda b, t: (b * nt + t, 0)),
        out_shape=jax.ShapeDtypeStruct((nb * t_len, xw), BF16),
        compiler_params=_cparams(("parallel", "parallel")),
        name="xattn_prompt",
    )(z, mk, mv)


def _xattn_sample_kernel(q_ref, k_ref, v_ref, o_ref):
    bb = q_ref.shape[0]
    nrow = k_ref.shape[1] // SUBLANES
    full = (bb, nrow, SUBLANES, XA_DIM)
    q = q_ref[...] * (XA_DIM ** -0.5)
    q8 = jnp.concatenate([q, q], axis=1)[:, None]
    k = k_ref[...].reshape(full)
    s = jnp.broadcast_to(jnp.sum(k * q8, axis=-1, keepdims=True), full)
    mx = jnp.max(s, axis=1, keepdims=True)
    mx = jnp.maximum(mx, pltpu.roll(mx, XA_HEADS, axis=2))
    p = jnp.exp(s - mx)
    den = jnp.sum(p, axis=1, keepdims=True)
    den = den + pltpu.roll(den, XA_HEADS, axis=2)
    o = jnp.sum(p * v_ref[...].reshape(full), axis=1, keepdims=True)
    o = o + pltpu.roll(o, XA_HEADS, axis=2)
    o_ref[...] = (o / den)[:, 0, 0:XA_HEADS, :].astype(o_ref.dtype)


def _xattn_sample(q3, mk, mv, bb=8):
    nb, rows, _ = mk.shape
    kv = pl.BlockSpec((bb, rows, XA_DIM), lambda i: (i, 0, 0))
    qs = pl.BlockSpec((bb, XA_HEADS, XA_DIM), lambda i: (i, 0, 0))
    return pl.pallas_call(
        _xattn_sample_kernel,
        grid=(nb // bb,),
        in_specs=[qs, kv, kv],
        out_specs=qs,
        out_shape=jax.ShapeDtypeStruct((nb, XA_HEADS, XA_DIM), BF16),
        compiler_params=_cparams(("parallel",)),
        name="xattn_sample",
    )(q3, mk, mv)


def _merge_kernel(pa_ref, pb_ref, pc_ref, g0_ref, g1_ref, g2_ref, h_ref, wa_ref, wb_ref, wc_ref, wo_ref, o_ref):
    oa = jnp.dot(pa_ref[...], wa_ref[...], preferred_element_type=F32)
    ob = jnp.dot(pb_ref[...], wb_ref[...], preferred_element_type=F32)
    oc = jnp.dot(pc_ref[...], wc_ref[...], preferred_element_type=F32)
    merged = (jax.nn.sigmoid(g0_ref[...]) * oa + jax.nn.sigmoid(g1_ref[...]) * ob
              + jax.nn.sigmoid(g2_ref[...]) * oc)
    o_ref[...] = h_ref[...] + jnp.dot(merged.astype(BF16), wo_ref[...], preferred_element_type=F32)


def _merge(pa, pb, pc, z, zg_blk0, h, tm, wa, wb, wc, wo):
    m, d = h.shape
    const = lambda i: (0, 0)
    resident = lambda w: pl.BlockSpec(w.shape, const, pipeline_mode=pl.Buffered(1))
    in_specs = [
        pl.BlockSpec((tm, pa.shape[1]), lambda i: (i, 0)),
        pl.BlockSpec((tm, pb.shape[1]), lambda i: (i, 0)),
        pl.BlockSpec((tm, pc.shape[1]), lambda i: (i, 0)),
        pl.BlockSpec((tm, d), lambda i: (i, zg_blk0)),
        pl.BlockSpec((tm, d), lambda i: (i, zg_blk0 + 1)),
        pl.BlockSpec((tm, d), lambda i: (i, zg_blk0 + 2)),
        pl.BlockSpec((tm, d), lambda i: (i, 0)),
        resident(wa), resident(wb), resident(wc), resident(wo),
    ]
    return pl.pallas_call(
        _merge_kernel,
        grid=(m // tm,),
        in_specs=in_specs,
        out_specs=pl.BlockSpec((tm, d), lambda i: (i, 0)),
        out_shape=jax.ShapeDtypeStruct((m, d), F32),
        compiler_params=_cparams(("parallel",)),
        name="merge",
    )(pa, pb, pc, z, z, z, h, wa, wb, wc, wo)


def _pack_zr(x, axis):
    w = RWKV_W
    take = lambda a, b: lax.slice_in_dim(x, a, b, axis=axis)

    def pad(n):
        shape = list(x.shape)
        shape[axis] = n
        return jnp.zeros(shape, x.dtype)

    return jnp.concatenate([take(0, 3 * w + 64), pad(64), take(3 * w + 64, 3 * w + 128), pad(64),
                            take(3 * w + 128, ZR_TRUE), pad(96)], axis=axis)


def _unpack_zr_cols(x):
    w = RWKV_W
    return jnp.concatenate([x[..., :3 * w + 64], x[..., 3 * w + 128:3 * w + 192], x[..., 3 * w + 256:3 * w + 416]],
                           axis=-1)


def _pad_rows(x, n):
    return jnp.concatenate([x, jnp.zeros((n - x.shape[0],) + x.shape[1:], x.dtype)], axis=0)


def kernel(x_prompt, x_sample, mem_prompt, cache_mem_k, cache_mem_v, state_wkv, state_shift, state_pool,
           ffn1_norm_g, ffn1_w_gate, ffn1_w_up, ffn1_w_down, mix_norm_g, w_in,
           pool_group_w, pool_scale, pool_out,
           rwkv_mu, rwkv_w0, rwkv_w_up, rwkv_a0, rwkv_a_up, rwkv_g_up, rwkv_k_k, rwkv_k_a, rwkv_r_k,
           rwkv_ln_g, rwkv_ln_b, rwkv_out,
           mem_norm_g, w_mem_k, w_mem_v, xattn_out, w_o,
           ffn2_norm_g, ffn2_w_gate, ffn2_w_up, ffn2_w_down, final_norm_g):
    nb, t_len, d = x_prompt.shape
    ns = x_sample.shape[0]
    assert w_in.shape[0] == 1 and x_sample.shape[1] == 1
    n_mem = mem_prompt.shape[1]
    pool_w = pool_out.shape[1]
    xa_w = xattn_out.shape[1]
    n_heads = RWKV_W // HEAD
    nbuf = state_pool.shape[2]
    rows_p = nb * t_len
    l = 0

    d_ff = ffn1_w_gate.shape[2]
    f1 = [w.reshape(w.shape[1:]).astype(BF16) for w in (ffn1_w_gate, ffn1_w_up, ffn1_w_down)]
    f2 = [w.reshape(w.shape[1:]).astype(BF16) for w in (ffn2_w_gate, ffn2_w_up, ffn2_w_down)]
    wit = jnp.swapaxes(w_in[l], 0, 1)
    o_zr, o_zq, o_zg = pool_w, pool_w + ZR_TRUE, pool_w + ZR_TRUE + xa_w
    w_in_t = jnp.concatenate([_pack_zr(wit[o_zr:o_zq].astype(BF16), 0), wit[:o_zr].astype(BF16),
                              wit[o_zg:].astype(BF16), wit[o_zq:o_zg].astype(BF16)], axis=0)
    col_zp = ZR_W // pool_w
    col_zg = (ZR_W + pool_w) // d
    col_zq = (ZR_W + pool_w + 3 * d) // xa_w
    row = lambda v: v.reshape(1, -1)
    prep_params = [row(_pack_zr(rwkv_mu[l], 0)), row(rwkv_w0[l]), _pad_rows(rwkv_w_up[l], 128).astype(BF16),
                   row(rwkv_a0[l]), _pad_rows(rwkv_a_up[l], 128).astype(BF16),
                   _pad_rows(rwkv_g_up[l], 256).astype(BF16), row(rwkv_k_k[l]), row(rwkv_k_a[l]),
                   row(rwkv_r_k[l])]
    gw = pool_group_w[l].astype(BF16)
    w_kv_t = jnp.concatenate([jnp.swapaxes(w_mem_k[l], 0, 1), jnp.swapaxes(w_mem_v[l], 0, 1)], axis=0).astype(BF16)
    wa, wb, wc, wo = (pool_out[l].astype(BF16), rwkv_out[l].astype(BF16), xattn_out[l].astype(BF16),
                      w_o[l].astype(BF16))
    g1, gm, g2, fg = row(ffn1_norm_g[l]), row(mix_norm_g[l]), row(ffn2_norm_g[l]), row(final_norm_g)
    scale = row(pool_scale[l])
    ln_g, ln_b = rwkv_ln_g[l], rwkv_ln_b[l]

    tm_p = 512
    h1_p, h1_s = _ffn(x_prompt.reshape(rows_p, d), x_sample.reshape(ns, d), tm_p, g1, *f1)
    z_p = _norm_matmul(h1_p, gm, w_in_t, tm=1024, tn=1536)
    z_s = _norm_matmul(h1_s, gm, w_in_t, tm=ns, tn=1536)
    kv = _norm_matmul(mem_prompt.reshape(nb * n_mem, d), row(mem_norm_g[l]), w_kv_t, tm=512, tn=512)
    mk_p = kv[:, :xa_w].reshape(nb, n_mem, xa_w)
    mv_p = kv[:, xa_w:].reshape(nb, n_mem, xa_w)

    pool_state = state_pool.reshape(ns, nbuf, pool_w)
    pa_p = _pool_prompt(z_p, nb, t_len, col_zp, gw, scale)
    pa_s = _pool_sample(z_s, col_zp, jnp.swapaxes(pool_state, 0, 1), gw, scale)

    pb_p, st_p = _rwkv_prompt(z_p, nb, t_len, prep_params, row(ln_g), row(ln_b))
    prep_s = _prep_sample(z_s, _pack_zr(state_shift.reshape(ns, ZR_TRUE), 1), prep_params)
    state_t = jnp.transpose(state_wkv.reshape(ns, n_heads, HEAD, HEAD), (1, 2, 3, 0))
    pb_s, wkv_s_t = _wkv_step(state_t, prep_s, row(ln_g), row(ln_b))

    pc_p = _xattn_prompt(z_p, nb, t_len, col_zq, mk_p, mv_p)
    q_s = z_s[:, ZR_W + pool_w + 3 * d:].reshape(ns, XA_HEADS, XA_DIM)
    pc_s = _xattn_sample(q_s, cache_mem_k.reshape(ns, n_mem * XA_HEADS, XA_DIM),
                         cache_mem_v.reshape(ns, n_mem * XA_HEADS, XA_DIM)).reshape(ns, xa_w)

    h2_p = _merge(pa_p, pb_p, pc_p, z_p, col_zg, h1_p, 256, wa, wb, wc, wo)
    h2_s = _merge(pa_s, pb_s, pc_s, z_s, col_zg, h1_s, ns, wa, wb, wc, wo)
    y_prompt, y_sample = _ffn(h2_p, h2_s, tm_p, g2, *f2, final_g=fg)

    ends = [(b + 1) * t_len for b in range(nb)]
    shift_p = _unpack_zr_cols(jnp.stack([z_p[e - 1:e, :ZR_W] for e in ends]))[None]
    pool_p = jnp.stack([z_p[e - nbuf:e, ZR_W:ZR_W + pool_w] for e in ends])[None]
    shift_s = _unpack_zr_cols(z_s[:, :ZR_W])[None, :, None, :]
    pool_s = jnp.concatenate([pool_state[:, 1:], z_s[:, None, ZR_W:ZR_W + pool_w]], axis=1)[None]
    wkv_p = st_p[None]
    wkv_s = jnp.transpose(wkv_s_t, (3, 0, 1, 2)).reshape(state_wkv.shape)
    mem_k_p = mk_p.reshape(1, nb, n_mem, XA_HEADS, XA_DIM)
    mem_v_p = mv_p.reshape(1, nb, n_mem, XA_HEADS, XA_DIM)
    return (y_prompt.reshape(nb, t_len, d), y_sample.reshape(ns, 1, d), mem_k_p, mem_v_p, wkv_p, shift_p, pool_p,
            wkv_s, shift_s, pool_s)
```

```python
import functools
import math

import jax
import jax.numpy as jnp
from jax import lax
from jax.experimental import pallas as pl
from jax.experimental.pallas import tpu as pltpu

F32 = jnp.float32
BF16 = jnp.bfloat16

RMS_EPS = 1e-6
GN_EPS = 64e-5
POOL_WINDOWS = (2, 4, 8, 16)
HEAD = 64
LANES = 128
SUBLANES = 8
XA_HEADS = 4
XA_DIM = 128
PAST_LEN = 16384
VMEM_LIMIT = 60 * 1024 * 1024
EXP_M05 = math.exp(-0.5)

ZR_W = 3584
RWKV_W = 1024
ZR_TRUE = 3360

NN_DIMS = (((1,), (0,)), ((), ()))
NT_DIMS = (((1,), (1,)), ((), ()))
TN_DIMS = (((0,), (0,)), ((), ()))


def _cparams(sem):
    return pltpu.CompilerParams(dimension_semantics=sem, vmem_limit_bytes=VMEM_LIMIT)


def _rms(x, g):
    ms = jnp.mean(x * x, axis=-1, keepdims=True)
    return x * lax.rsqrt(ms + RMS_EPS) * g


def _bdot(a, b, dims=NN_DIMS):
    return lax.dot_general(a.astype(BF16), b.astype(BF16), dims, preferred_element_type=F32)


def _ffn_kernel(hp_ref, hs_ref, g_ref, fg_ref, wg_ref, wu_ref, wd_ref, wgt_ref, wut_ref, wdt_ref,
                op_ref, os_ref, xp_ref, xs_ref, ap_ref, as_ref, *, nfull, final):
    m = pl.program_id(0)
    f = pl.program_id(1)

    def start(h_ref, x_ref, acc_ref):
        x_ref[...] = _rms(h_ref[...], g_ref[...]).astype(BF16)
        acc_ref[...] = jnp.zeros_like(acc_ref)

    def contribution(x_ref, wg, wu, wd):
        xn = x_ref[...]
        gate = jnp.dot(xn, wg, preferred_element_type=F32)
        up = jnp.dot(xn, wu, preferred_element_type=F32)
        act = (gate * jax.nn.sigmoid(gate) * up).astype(BF16)
        return jnp.dot(act, wd, preferred_element_type=F32)

    def finish(h_ref, acc_ref, o_ref, last):
        out = h_ref[...] + 0.5 * (acc_ref[...] + last)
        if final:
            out = _rms(out, fg_ref[...])
        o_ref[...] = out

    @pl.when(f == 0)
    def _():
        start(hp_ref, xp_ref, ap_ref)

    @pl.when((f == 0) & (m == 0))
    def _():
        start(hs_ref, xs_ref, as_ref)

    @pl.when(f < nfull)
    def _():
        wg, wu, wd = wg_ref[...], wu_ref[...], wd_ref[...]
        ap_ref[...] += contribution(xp_ref, wg, wu, wd)

        @pl.when(m == 0)
        def _():
            as_ref[...] += contribution(xs_ref, wg, wu, wd)

    @pl.when(f == nfull)
    def _():
        wg, wu, wd = wgt_ref[...], wut_ref[...], wdt_ref[...]
        finish(hp_ref, ap_ref, op_ref, contribution(xp_ref, wg, wu, wd))

        @pl.when(m == 0)
        def _():
            finish(hs_ref, as_ref, os_ref, contribution(xs_ref, wg, wu, wd))


def _ffn(hp, hs, tm, g, wg, wu, wd, final_g=None, tf=768):
    mp, d = hp.shape
    ms = hs.shape[0]
    d_ff = wg.shape[1]
    nfull, tail = divmod(d_ff, tf)
    assert tail > 0 and tail % LANES == 0 and mp % tm == 0
    last_main = nfull - 1
    split = nfull * tf
    wgt, wut, wdt = wg[:, split:], wu[:, split:], wd[split:]
    final = final_g is not None
    vec = pl.BlockSpec((1, d), lambda i, f: (0, 0))
    once = pl.Buffered(1)
    in_specs = [
        pl.BlockSpec((tm, d), lambda i, f: (i, 0)),
        pl.BlockSpec((ms, d), lambda i, f: (0, 0), pipeline_mode=once),
        vec, vec,
        pl.BlockSpec((d, tf), lambda i, f: (0, jnp.minimum(f, last_main))),
        pl.BlockSpec((d, tf), lambda i, f: (0, jnp.minimum(f, last_main))),
        pl.BlockSpec((tf, d), lambda i, f: (jnp.minimum(f, last_main), 0)),
        pl.BlockSpec((d, tail), lambda i, f: (0, 0), pipeline_mode=once),
        pl.BlockSpec((d, tail), lambda i, f: (0, 0), pipeline_mode=once),
        pl.BlockSpec((tail, d), lambda i, f: (0, 0), pipeline_mode=once),
    ]
    return pl.pallas_call(
        functools.partial(_ffn_kernel, nfull=nfull, final=final),
        grid=(mp // tm, nfull + 1),
        in_specs=in_specs,
        out_specs=[pl.BlockSpec((tm, d), lambda i, f: (i, 0)), pl.BlockSpec((ms, d), lambda i, f: (0, 0))],
        out_shape=[jax.ShapeDtypeStruct((mp, d), F32), jax.ShapeDtypeStruct((ms, d), F32)],
        scratch_shapes=[pltpu.VMEM((tm, d), BF16), pltpu.VMEM((ms, d), BF16),
                        pltpu.VMEM((tm, d), F32), pltpu.VMEM((ms, d), F32)],
        compiler_params=_cparams(("parallel", "arbitrary")),
        name="ffn",
    )(hp, hs, g, g if final_g is None else final_g, wg, wu, wd, wgt, wut, wdt)


def _norm_matmul_kernel(h_ref, g_ref, wt_ref, o_ref, xn_ref):
    @pl.when(pl.program_id(1) == 0)
    def _():
        xn_ref[...] = _rms(h_ref[...], g_ref[...]).astype(BF16)

    o_ref[...] = lax.dot_general(xn_ref[...], wt_ref[...], NT_DIMS, preferred_element_type=F32)


def _norm_matmul(h, g, wt, tm, tn):
    m, d = h.shape
    n = wt.shape[0]
    return pl.pallas_call(
        _norm_matmul_kernel,
        grid=(m // tm, n // tn),
        in_specs=[
            pl.BlockSpec((tm, d), lambda i, j: (i, 0)),
            pl.BlockSpec((1, d), lambda i, j: (0, 0)),
            pl.BlockSpec((tn, d), lambda i, j: (j, 0)),
        ],
        out_specs=pl.BlockSpec((tm, tn), lambda i, j: (i, j)),
        out_shape=jax.ShapeDtypeStruct((m, n), F32),
        scratch_shapes=[pltpu.VMEM((tm, d), BF16)],
        compiler_params=_cparams(("parallel", "arbitrary")),
        name="norm_matmul",
    )(h, g, wt)


def _in_proj_kernel(hp_ref, hs_ref, g_ref, wt_ref, op_ref, os_ref, xp_ref, xs_ref):
    i = pl.program_id(0)
    j = pl.program_id(1)

    @pl.when(j == 0)
    def _():
        xp_ref[...] = _rms(hp_ref[...], g_ref[...]).astype(BF16)

    @pl.when((j == 0) & (i == 0))
    def _():
        xs_ref[...] = _rms(hs_ref[...], g_ref[...]).astype(BF16)

    w = wt_ref[...]
    op_ref[...] = lax.dot_general(xp_ref[...], w, NT_DIMS, preferred_element_type=F32)

    @pl.when(i == 0)
    def _():
        os_ref[...] = lax.dot_general(xs_ref[...], w, NT_DIMS, preferred_element_type=F32)


def _in_proj(hp, hs, g, wt, tm, tn):
    mp, d = hp.shape
    ms = hs.shape[0]
    n = wt.shape[0]
    nj = n // tn
    return pl.pallas_call(
        _in_proj_kernel,
        grid=(mp // tm, nj),
        in_specs=[
            pl.BlockSpec((tm, d), lambda i, j: (i, 0)),
            pl.BlockSpec((ms, d), lambda i, j: (0, 0)),
            pl.BlockSpec((1, d), lambda i, j: (0, 0)),
            pl.BlockSpec((tn, d), lambda i, j: (j, 0)),
        ],
        out_specs=[pl.BlockSpec((tm, tn), lambda i, j: (i, j)),
                   pl.BlockSpec((ms, tn), lambda i, j: (0, jnp.where(i == 0, j, nj - 1)))],
        out_shape=[jax.ShapeDtypeStruct((mp, n), F32), jax.ShapeDtypeStruct((ms, n), F32)],
        scratch_shapes=[pltpu.VMEM((tm, d), BF16), pltpu.VMEM((ms, d), BF16)],
        compiler_params=_cparams(("arbitrary", "arbitrary")),
        name="in_proj",
    )(hp, hs, g, wt)


def _pool_mix(pooled_groups, gw_ref, scale_ref, o_ref):
    for gi, pooled in enumerate(pooled_groups):
        sl = slice(gi * LANES, (gi + 1) * LANES)
        mixed = jnp.dot(pooled.astype(BF16), gw_ref[gi], preferred_element_type=F32)
        o_ref[:, sl] = (mixed * scale_ref[:, sl]).astype(o_ref.dtype)


def _pool_prompt_kernel(zp_ref, gw_ref, scale_ref, o_ref, ext_ref, *, tt):
    t = pl.program_id(1)
    hist = 16

    @pl.when(t == 0)
    def _():
        ext_ref[0:hist, :] = jnp.zeros((hist, ext_ref.shape[1]), F32)

    x = zp_ref[...]
    ext_ref[hist:hist + tt, :] = x
    pos = t * tt + lax.broadcasted_iota(jnp.int32, (tt, LANES), 0)
    groups = []
    for gi, w in enumerate(POOL_WINDOWS):
        sl = slice(gi * LANES, (gi + 1) * LANES)
        acc = x[:, sl]
        for k in range(1, w):
            acc = acc + ext_ref[hist - k:hist - k + tt, sl]
        cnt = jnp.minimum(pos + 1, w).astype(F32)
        groups.append(acc / cnt - x[:, sl])
    _pool_mix(groups, gw_ref, scale_ref, o_ref)
    ext_ref[0:hist, :] = ext_ref[tt:tt + hist, :]


def _pool_prompt(z, nb, t_len, col_blk, gw, scale, tt=1024):
    nt = t_len // tt
    pw = gw.shape[0] * LANES
    return pl.pallas_call(
        functools.partial(_pool_prompt_kernel, tt=tt),
        grid=(nb, nt),
        in_specs=[
            pl.BlockSpec((tt, pw), lambda b, t: (b * nt + t, col_blk)),
            pl.BlockSpec(gw.shape, lambda b, t: (0, 0, 0)),
            pl.BlockSpec((1, pw), lambda b, t: (0, 0)),
        ],
        out_specs=pl.BlockSpec((tt, pw), lambda b, t: (b * nt + t, 0)),
        out_shape=jax.ShapeDtypeStruct((nb * t_len, pw), BF16),
        scratch_shapes=[pltpu.VMEM((tt + 16, pw), F32)],
        compiler_params=_cparams(("parallel", "arbitrary")),
        name="pool_prompt",
    )(z, gw, scale)


def _pool_sample_kernel(zp_ref, buf_ref, gw_ref, scale_ref, o_ref):
    x = zp_ref[...]
    nbuf = buf_ref.shape[0]
    groups = []
    for gi, w in enumerate(POOL_WINDOWS):
        sl = slice(gi * LANES, (gi + 1) * LANES)
        acc = x[:, sl]
        for k in range(1, w):
            acc = acc + buf_ref[nbuf - k, :, sl]
        cnt = float(min(PAST_LEN + 1, w))
        groups.append(acc / cnt - x[:, sl])
    _pool_mix(groups, gw_ref, scale_ref, o_ref)


def _pool_sample(z, col_blk, buf_t, gw, scale):
    nrows = z.shape[0]
    pw = gw.shape[0] * LANES
    return pl.pallas_call(
        _pool_sample_kernel,
        grid=(1,),
        in_specs=[
            pl.BlockSpec((nrows, pw), lambda i: (0, col_blk)),
            pl.BlockSpec(buf_t.shape, lambda i: (0, 0, 0)),
            pl.BlockSpec(gw.shape, lambda i: (0, 0, 0)),
            pl.BlockSpec((1, pw), lambda i: (0, 0)),
        ],
        out_specs=pl.BlockSpec((nrows, pw), lambda i: (0, 0)),
        out_shape=jax.ShapeDtypeStruct((nrows, pw), BF16),
        compiler_params=_cparams(("arbitrary",)),
        name="pool_sample",
    )(z, buf_t, gw, scale)


def _head_sum(x):
    head_a = lax.broadcasted_iota(jnp.int32, (x.shape[0], LANES), 1) < HEAD
    cols = []
    for c in range(x.shape[1] // LANES):
        t = x[:, c * LANES:(c + 1) * LANES]
        sa = jnp.sum(jnp.where(head_a, t, 0.0), axis=-1, keepdims=True)
        sb = jnp.sum(jnp.where(head_a, 0.0, t), axis=-1, keepdims=True)
        cols.append(jnp.where(head_a, sa, sb))
    return jnp.concatenate(cols, axis=1)


def _prep_math(x, prev, p_refs):
    mu_ref, w0_ref, wup_ref, a0_ref, aup_ref, gup_ref, kk_ref, ka_ref, rk_ref = p_refs
    xm = x + (prev - x) * mu_ref[...]
    w = RWKV_W
    r = xm[:, 0:w]
    k = xm[:, w:2 * w]
    v = xm[:, 2 * w:3 * w]
    wl = xm[:, 3 * w:3 * w + 128]
    al = xm[:, 3 * w + 128:3 * w + 256]
    gl = xm[:, 3 * w + 256:3 * w + 512]
    dw = w0_ref[...] + jnp.dot(jnp.tanh(wl).astype(BF16), wup_ref[...], preferred_element_type=F32)
    lw = -EXP_M05 * jax.nn.sigmoid(dw)
    a = jax.nn.sigmoid(a0_ref[...] + jnp.dot(al.astype(BF16), aup_ref[...], preferred_element_type=F32))
    g = jnp.dot(jax.nn.sigmoid(gl).astype(BF16), gup_ref[...], preferred_element_type=F32)
    kk = k * kk_ref[...]
    kk = kk * lax.rsqrt(jnp.maximum(_head_sum(kk * kk), 1e-24))
    kmod = k * (1.0 + (a - 1.0) * ka_ref[...])
    bonus = _head_sum(r * kmod * rk_ref[...]) * v
    return r, lw, kmod, v, -kk, kk * a, g, bonus


def _group_norm_gate(y, bonus, g, lng_ref, lnb_ref):
    d = y - _head_sum(y) * (1.0 / HEAD)
    var = _head_sum(d * d) * (1.0 / HEAD)
    yn = d * lax.rsqrt(var + GN_EPS) * lng_ref[...] + lnb_ref[...]
    return (yn + bonus) * g


def _prep_sample_kernel(zr_ref, prev_ref, *refs):
    p_refs, out_refs = refs[:9], refs[9:17]
    for o_ref, val in zip(out_refs, _prep_math(zr_ref[...], prev_ref[...], p_refs)):
        o_ref[...] = val


def _prep_param_specs(params):
    zero = (lambda *idx: (0, 0))
    return [pl.BlockSpec(p.shape, zero) for p in params]


def _prep_sample(z, prev, params):
    nrows = z.shape[0]
    out_spec = pl.BlockSpec((nrows, RWKV_W), lambda i: (0, 0))
    return pl.pallas_call(
        _prep_sample_kernel,
        grid=(1,),
        in_specs=[pl.BlockSpec((nrows, ZR_W), lambda i: (0, 0)),
                  pl.BlockSpec((nrows, ZR_W), lambda i: (0, 0))] + _prep_param_specs(params),
        out_specs=[out_spec] * 8,
        out_shape=[jax.ShapeDtypeStruct((nrows, RWKV_W), F32)] * 8,
        compiler_params=_cparams(("arbitrary",)),
        name="prep_sample",
    )(z, prev, *params)


def _split3(x):
    hi = x.astype(BF16)
    rest = x - hi.astype(F32)
    mid = rest.astype(BF16)
    lo = (rest - mid.astype(F32)).astype(BF16)
    return hi, mid, lo


def _select_dot(sel, x):
    sel = sel.astype(BF16)
    hi, mid, lo = _split3(x)
    return _bdot(sel, hi) + (_bdot(sel, mid) + _bdot(sel, lo))


def _chunk_scan(r, lw, k, v, a, b, s_ref, c_len):
    n2 = 2 * c_len
    assert n2 == LANES
    row = lax.broadcasted_iota(jnp.int32, (n2, n2), 0)
    col = lax.broadcasted_iota(jnp.int32, (n2, n2), 1)
    tr = row & (c_len - 1)
    tc = col & (c_len - 1)
    strict = tr > tc
    incl = tr >= tc
    tri = (lax.broadcasted_iota(jnp.int32, (c_len, c_len), 0)
           >= lax.broadcasted_iota(jnp.int32, (c_len, c_len), 1))
    head_a = lax.broadcasted_iota(jnp.int32, (c_len, LANES), 1) < HEAD

    def stack(x):
        return jnp.concatenate([jnp.where(head_a, x, 0.0), jnp.where(head_a, 0.0, x)], axis=0).astype(BF16)

    prs = range(r.shape[1] // LANES)
    cat = jnp.concatenate
    sls = [slice(q * LANES, (q + 1) * LANES) for q in prs]
    lwq = [lw[:, sl] for sl in sls]
    cum = [_select_dot(tri, x) for x in lwq]
    tot = [x[c_len - 1:c_len, :] for x in cum]
    xr = [stack(r[:, sls[q]] * jnp.exp(cum[q])) for q in prs]
    xa = [stack(a[:, sls[q]] * jnp.exp(cum[q] - lwq[q])) for q in prs]
    e_neg = [jnp.exp(-x) for x in cum]
    e_rem = [jnp.exp(tot[q] - cum[q]) for q in prs]
    yb = [stack(b[:, sls[q]] * e_neg[q]) for q in prs]
    yk = [stack(k[:, sls[q]] * e_neg[q]) for q in prs]
    zb = [stack(b[:, sls[q]] * e_rem[q]) for q in prs]
    zk = [stack(k[:, sls[q]] * e_rem[q]) for q in prs]
    vs = [stack(v[:, sl]) for sl in sls]

    g = [_bdot(cat([xa[q], xr[q]], axis=0), cat([yb[q], yk[q]], axis=0), NT_DIMS) for q in prs]
    m_ab = [jnp.where(strict, x[0:n2, 0:n2], 0.0) for x in g]
    m_ak = [jnp.where(strict, x[0:n2, n2:2 * n2], 0.0).astype(BF16) for x in g]
    n_rb = [jnp.where(incl, x[n2:2 * n2, 0:n2], 0.0) for x in g]
    n_rk = [jnp.where(incl, x[n2:2 * n2, n2:2 * n2], 0.0).astype(BF16) for x in g]

    s_old = [s_ref[q] for q in prs]
    lhs = [cat([cat([xa[q], m_ak[q]], axis=1), cat([xr[q], n_rk[q]], axis=1)], axis=0) for q in prs]
    xy0 = [_bdot(lhs[q], cat([s_old[q].T.astype(BF16), vs[q]], axis=0)) for q in prs]

    x = [xy0[q][0:n2] for q in prs]
    mk = m_ab
    nlev = int(math.log2(c_len))
    for lev in range(nlev):
        if lev < nlev - 1:
            res = [_bdot(mk[q], cat([mk[q], x[q]], axis=1)) for q in prs]
            mk = [t[:, 0:n2] for t in res]
            x = [x[q] + res[q][:, n2:2 * n2] for q in prs]
        else:
            x = [x[q] + _bdot(mk[q], x[q]) for q in prs]
    u = [t.astype(BF16) for t in x]

    y_st = [xy0[q][n2:2 * n2] + _bdot(n_rb[q], u[q]) for q in prs]
    for q in prs:
        s_ref[q] = s_old[q] * jnp.exp(tot[q]) + _bdot(cat([u[q], vs[q]], axis=0), cat([zb[q], zk[q]], axis=0),
                                                     TN_DIMS)
    return cat([t[0:c_len] + t[c_len:n2] for t in y_st], axis=1)


def _rwkv_prompt_kernel(zr_ref, *refs, nc):
    p_refs = refs[:9]
    lng_ref, lnb_ref, yb_ref, sout_ref, s_ref, carry_ref = refs[9:]
    c = pl.program_id(1)
    nseq, c_len, _ = zr_ref.shape

    @pl.when(c == 0)
    def _():
        s_ref[...] = jnp.zeros_like(s_ref)
        carry_ref[...] = jnp.zeros_like(carry_ref)

    first = lax.broadcasted_iota(jnp.int32, (c_len, ZR_W), 0) == 0
    preps = []
    for q in range(nseq):
        x = zr_ref[q]
        rolled = pltpu.roll(x, 1, axis=0)
        prev = jnp.where(first, carry_ref[q, 0:1, :], rolled)
        carry_ref[q] = rolled[0:SUBLANES, :]
        preps.append(_prep_math(x, prev, p_refs))
    r, lw, k, v, a, b = (jnp.concatenate([p[i] for p in preps], axis=1) for i in range(6))
    y = _chunk_scan(r, lw, k, v, a, b, s_ref, c_len)
    for q in range(nseq):
        yq = y[:, q * RWKV_W:(q + 1) * RWKV_W]
        yb_ref[q] = _group_norm_gate(yq, preps[q][7], preps[q][6], lng_ref, lnb_ref).astype(yb_ref.dtype)

    @pl.when(c == nc - 1)
    def _():
        npair = s_ref.shape[0] // nseq
        for q in range(nseq):
            for p in range(npair):
                s = s_ref[q * npair + p]
                sout_ref[q, 2 * p] = s[0:HEAD, 0:HEAD]
                sout_ref[q, 2 * p + 1] = s[HEAD:2 * HEAD, HEAD:2 * HEAD]


def _rwkv_prompt(z, nb, t_len, params, ln_g, ln_b, c_len=64, nseq=2):
    nc = t_len // c_len
    npair = RWKV_W // LANES
    par = pl.BlockSpec((1, RWKV_W), lambda b, c: (0, 0))
    yb, state = pl.pallas_call(
        functools.partial(_rwkv_prompt_kernel, nc=nc),
        grid=(nb // nseq, nc),
        in_specs=[pl.BlockSpec((nseq, c_len, ZR_W), lambda b, c: (b, c, 0))] + _prep_param_specs(params) + [par, par],
        out_specs=[pl.BlockSpec((nseq, c_len, RWKV_W), lambda b, c: (b, c, 0)),
                   pl.BlockSpec((nseq, 2 * npair, HEAD, HEAD), lambda b, c: (b, 0, 0, 0))],
        out_shape=[jax.ShapeDtypeStruct((nb, t_len, RWKV_W), BF16),
                   jax.ShapeDtypeStruct((nb, 2 * npair, HEAD, HEAD), F32)],
        scratch_shapes=[pltpu.VMEM((nseq * npair, LANES, LANES), F32), pltpu.VMEM((nseq, SUBLANES, ZR_W), F32)],
        compiler_params=_cparams(("parallel", "arbitrary")),
        name="rwkv_prompt",
    )(z.reshape(nb, t_len, z.shape[1]), *params, ln_g, ln_b)
    return yb.reshape(nb * t_len, RWKV_W), state


def _wkv_step_kernel(s_ref, r_ref, lw_ref, k_ref, v_ref, a_ref, b_ref, g_ref, bonus_ref, lng_ref, lnb_ref,
                     yb_ref, so_ref, vt_scr, y_scr):
    rt, wt, kt, at, bt = (x[...].T for x in (r_ref, lw_ref, k_ref, a_ref, b_ref))
    wt = jnp.exp(wt)
    vt_scr[...] = v_ref[...].T
    for hh in range(2):
        rows = slice(hh * HEAD, (hh + 1) * HEAD)
        r, w, k, a, b = (x[rows, :] for x in (rt, wt, kt, at, bt))

        def body(i, carry):
            si = s_ref[hh, i]
            sa = jnp.sum(si * a, axis=0, keepdims=True)
            vi = vt_scr[pl.ds(hh * HEAD + i, 1), :]
            s2 = si * w + sa * b + vi * k
            so_ref[hh, i] = s2
            y_scr[pl.ds(hh * HEAD + i, 1), :] = jnp.sum(s2 * r, axis=0, keepdims=True)
            return carry

        lax.fori_loop(0, HEAD, body, 0, unroll=4)

    outs = []
    for hh in range(2):
        y = y_scr[hh * HEAD:(hh + 1) * HEAD, :]
        d = y - jnp.mean(y, axis=0, keepdims=True)
        var = jnp.mean(d * d, axis=0, keepdims=True)
        outs.append(d * lax.rsqrt(var + GN_EPS))
    yn = jnp.concatenate(outs, axis=0).T * lng_ref[...] + lnb_ref[...]
    yb_ref[...] = ((yn + bonus_ref[...]) * g_ref[...]).astype(yb_ref.dtype)


def _wkv_step(state_t, vecs, ln_g, ln_b):
    nh, _, _, nb = state_t.shape
    st = pl.BlockSpec((2, HEAD, HEAD, nb), lambda p: (p, 0, 0, 0))
    vec = pl.BlockSpec((nb, 2 * HEAD), lambda p: (0, p))
    par = pl.BlockSpec((1, 2 * HEAD), lambda p: (0, p))
    return pl.pallas_call(
        _wkv_step_kernel,
        grid=(nh // 2,),
        in_specs=[st] + [vec] * 8 + [par, par],
        out_specs=[vec, st],
        out_shape=[jax.ShapeDtypeStruct((nb, nh * HEAD), BF16), jax.ShapeDtypeStruct(state_t.shape, F32)],
        scratch_shapes=[pltpu.VMEM((2 * HEAD, nb), F32), pltpu.VMEM((2 * HEAD, nb), F32)],
        compiler_params=_cparams(("parallel",)),
        name="wkv_step",
    )(state_t, *vecs, ln_g, ln_b)


def _xattn_prompt_kernel(q_ref, k_ref, v_ref, o_ref):
    scale = XA_DIM ** -0.5
    q = q_ref[...]
    for h in range(XA_HEADS):
        sl = slice(h * XA_DIM, (h + 1) * XA_DIM)
        s = lax.dot_general(q[:, sl].astype(BF16), k_ref[0, :, sl].astype(BF16),
                            NT_DIMS, preferred_element_type=F32) * scale
        p = jnp.exp(s - jnp.max(s, axis=-1, keepdims=True))
        den = jnp.sum(p, axis=-1, keepdims=True)
        o = jnp.dot(p.astype(BF16), v_ref[0, :, sl].astype(BF16), preferred_element_type=F32)
        o_ref[:, sl] = (o / den).astype(o_ref.dtype)


def _xattn_prompt(z, nb, t_len, col_blk, mk, mv, tq=1024):
    nt = t_len // tq
    xw = XA_HEADS * XA_DIM
    nmem = mk.shape[1]
    kv = pl.BlockSpec((1, nmem, xw), lambda b, t: (b, 0, 0))
    return pl.pallas_call(
        _xattn_prompt_kernel,
        grid=(nb, nt),
        in_specs=[pl.BlockSpec((tq, xw), lambda b, t: (b * nt + t, col_blk)), kv, kv],
        out_specs=pl.BlockSpec((tq, xw), lambda b, t: (b * nt + t, 0)),
        out_shape=jax.ShapeDtypeStruct((nb * t_len, xw), BF16),
        compiler_params=_cparams(("parallel", "parallel")),
        name="xattn_prompt",
    )(z, mk, mv)


def _xattn_sample_kernel(q_ref, k_ref, v_ref, o_ref):
    bb = q_ref.shape[0]
    nrow = k_ref.shape[1] // SUBLANES
    full = (bb, nrow, SUBLANES, XA_DIM)
    q = q_ref[...] * (XA_DIM ** -0.5)
    q8 = jnp.concatenate([q, q], axis=1)[:, None]
    k = k_ref[...].reshape(full)
    s = jnp.broadcast_to(jnp.sum(k * q8, axis=-1, keepdims=True), full)
    mx = jnp.max(s, axis=1, keepdims=True)
    mx = jnp.maximum(mx, pltpu.roll(mx, XA_HEADS, axis=2))
    p = jnp.exp(s - mx)
    den = jnp.sum(p, axis=1, keepdims=True)
    den = den + pltpu.roll(den, XA_HEADS, axis=2)
    o = jnp.sum(p * v_ref[...].reshape(full), axis=1, keepdims=True)
    o = o + pltpu.roll(o, XA_HEADS, axis=2)
    o_ref[...] = (o / den)[:, 0, 0:XA_HEADS, :].astype(o_ref.dtype)


def _xattn_sample(q3, mk, mv, bb=8):
    nb, rows, _ = mk.shape
    kv = pl.BlockSpec((bb, rows, XA_DIM), lambda i: (i, 0, 0))
    qs = pl.BlockSpec((bb, XA_HEADS, XA_DIM), lambda i: (i, 0, 0))
    return pl.pallas_call(
        _xattn_sample_kernel,
        grid=(nb // bb,),
        in_specs=[qs, kv, kv],
        out_specs=qs,
        out_shape=jax.ShapeDtypeStruct((nb, XA_HEADS, XA_DIM), BF16),
        compiler_params=_cparams(("parallel",)),
        name="xattn_sample",
    )(q3, mk, mv)


def _merge_kernel(pa_ref, pb_ref, pc_ref, g0_ref, g1_ref, g2_ref, h_ref, wa_ref, wb_ref, wc_ref, wo_ref, o_ref):
    oa = jnp.dot(pa_ref[...], wa_ref[...], preferred_element_type=F32)
    ob = jnp.dot(pb_ref[...], wb_ref[...], preferred_element_type=F32)
    oc = jnp.dot(pc_ref[...], wc_ref[...], preferred_element_type=F32)
    merged = (jax.nn.sigmoid(g0_ref[...]) * oa + jax.nn.sigmoid(g1_ref[...]) * ob
              + jax.nn.sigmoid(g2_ref[...]) * oc)
    o_ref[...] = h_ref[...] + jnp.dot(merged.astype(BF16), wo_ref[...], preferred_element_type=F32)


def _merge(pa, pb, pc, z, zg_blk0, h, tm, wa, wb, wc, wo):
    m, d = h.shape
    const = lambda i: (0, 0)
    resident = lambda w: pl.BlockSpec(w.shape, const, pipeline_mode=pl.Buffered(1))
    in_specs = [
        pl.BlockSpec((tm, pa.shape[1]), lambda i: (i, 0)),
        pl.BlockSpec((tm, pb.shape[1]), lambda i: (i, 0)),
        pl.BlockSpec((tm, pc.shape[1]), lambda i: (i, 0)),
        pl.BlockSpec((tm, d), lambda i: (i, zg_blk0)),
        pl.BlockSpec((tm, d), lambda i: (i, zg_blk0 + 1)),
        pl.BlockSpec((tm, d), lambda i: (i, zg_blk0 + 2)),
        pl.BlockSpec((tm, d), lambda i: (i, 0)),
        resident(wa), resident(wb), resident(wc), resident(wo),
    ]
    return pl.pallas_call(
        _merge_kernel,
        grid=(m // tm,),
        in_specs=in_specs,
        out_specs=pl.BlockSpec((tm, d), lambda i: (i, 0)),
        out_shape=jax.ShapeDtypeStruct((m, d), F32),
        compiler_params=_cparams(("parallel",)),
        name="merge",
    )(pa, pb, pc, z, z, z, h, wa, wb, wc, wo)


def _pack_zr(x, axis):
    w = RWKV_W
    take = lambda a, b: lax.slice_in_dim(x, a, b, axis=axis)

    def pad(n):
        shape = list(x.shape)
        shape[axis] = n
        return jnp.zeros(shape, x.dtype)

    return jnp.concatenate([take(0, 3 * w + 64), pad(64), take(3 * w + 64, 3 * w + 128), pad(64),
                            take(3 * w + 128, ZR_TRUE), pad(96)], axis=axis)


def _unpack_zr_cols(x):
    w = RWKV_W
    return jnp.concatenate([x[..., :3 * w + 64], x[..., 3 * w + 128:3 * w + 192], x[..., 3 * w + 256:3 * w + 416]],
                           axis=-1)


def _pad_rows(x, n):
    return jnp.concatenate([x, jnp.zeros((n - x.shape[0],) + x.shape[1:], x.dtype)], axis=0)


def kernel(x_prompt, x_sample, mem_prompt, cache_mem_k, cache_mem_v, state_wkv, state_shift, state_pool,
           ffn1_norm_g, ffn1_w_gate, ffn1_w_up, ffn1_w_down, mix_norm_g, w_in,
           pool_group_w, pool_scale, pool_out,
           rwkv_mu, rwkv_w0, rwkv_w_up, rwkv_a0, rwkv_a_up, rwkv_g_up, rwkv_k_k, rwkv_k_a, rwkv_r_k,
           rwkv_ln_g, rwkv_ln_b, rwkv_out,
           mem_norm_g, w_mem_k, w_mem_v, xattn_out, w_o,
           ffn2_norm_g, ffn2_w_gate, ffn2_w_up, ffn2_w_down, final_norm_g):
    nb, t_len, d = x_prompt.shape
    ns = x_sample.shape[0]
    assert w_in.shape[0] == 1 and x_sample.shape[1] == 1
    n_mem = mem_prompt.shape[1]
    pool_w = pool_out.shape[1]
    xa_w = xattn_out.shape[1]
    n_heads = RWKV_W // HEAD
    nbuf = state_pool.shape[2]
    rows_p = nb * t_len
    l = 0

    d_ff = ffn1_w_gate.shape[2]
    f1 = [w.reshape(w.shape[1:]).astype(BF16) for w in (ffn1_w_gate, ffn1_w_up, ffn1_w_down)]
    f2 = [w.reshape(w.shape[1:]).astype(BF16) for w in (ffn2_w_gate, ffn2_w_up, ffn2_w_down)]
    wit = jnp.swapaxes(w_in[l], 0, 1)
    o_zr, o_zq, o_zg = pool_w, pool_w + ZR_TRUE, pool_w + ZR_TRUE + xa_w
    w_in_t = jnp.concatenate([_pack_zr(wit[o_zr:o_zq].astype(BF16), 0), wit[:o_zr].astype(BF16),
                              wit[o_zg:].astype(BF16), wit[o_zq:o_zg].astype(BF16)], axis=0)
    col_zp = ZR_W // pool_w
    col_zg = (ZR_W + pool_w) // d
    col_zq = (ZR_W + pool_w + 3 * d) // xa_w
    row = lambda v: v.reshape(1, -1)
    prep_params = [row(_pack_zr(rwkv_mu[l], 0)), row(rwkv_w0[l]), _pad_rows(rwkv_w_up[l], 128).astype(BF16),
                   row(rwkv_a0[l]), _pad_rows(rwkv_a_up[l], 128).astype(BF16),
                   _pad_rows(rwkv_g_up[l], 256).astype(BF16), row(rwkv_k_k[l]), row(rwkv_k_a[l]),
                   row(rwkv_r_k[l])]
    gw = pool_group_w[l].astype(BF16)
    w_kv_t = jnp.concatenate([jnp.swapaxes(w_mem_k[l], 0, 1), jnp.swapaxes(w_mem_v[l], 0, 1)], axis=0).astype(BF16)
    wa, wb, wc, wo = (pool_out[l].astype(BF16), rwkv_out[l].astype(BF16), xattn_out[l].astype(BF16),
                      w_o[l].astype(BF16))
    g1, gm, g2, fg = row(ffn1_norm_g[l]), row(mix_norm_g[l]), row(ffn2_norm_g[l]), row(final_norm_g)
    scale = row(pool_scale[l])
    ln_g, ln_b = rwkv_ln_g[l], rwkv_ln_b[l]

    tm_p = 512
    h1_p, h1_s = _ffn(x_prompt.reshape(rows_p, d), x_sample.reshape(ns, d), tm_p, g1, *f1)
    z_p, z_s = _in_proj(h1_p, h1_s, gm, w_in_t, tm=1024, tn=1536)
    kv = _norm_matmul(mem_prompt.reshape(nb * n_mem, d), row(mem_norm_g[l]), w_kv_t, tm=512, tn=512)
    mk_p = kv[:, :xa_w].reshape(nb, n_mem, xa_w)
    mv_p = kv[:, xa_w:].reshape(nb, n_mem, xa_w)

    pool_state = state_pool.reshape(ns, nbuf, pool_w)
    pa_p = _pool_prompt(z_p, nb, t_len, col_zp, gw, scale)
    pa_s = _pool_sample(z_s, col_zp, jnp.swapaxes(pool_state, 0, 1), gw, scale)

    pb_p, st_p = _rwkv_prompt(z_p, nb, t_len, prep_params, row(ln_g), row(ln_b))
    prep_s = _prep_sample(z_s, _pack_zr(state_shift.reshape(ns, ZR_TRUE), 1), prep_params)
    state_t = jnp.transpose(state_wkv.reshape(ns, n_heads, HEAD, HEAD), (1, 2, 3, 0))
    pb_s, wkv_s_t = _wkv_step(state_t, prep_s, row(ln_g), row(ln_b))

    pc_p = _xattn_prompt(z_p, nb, t_len, col_zq, mk_p, mv_p)
    q_s = z_s[:, ZR_W + pool_w + 3 * d:].reshape(ns, XA_HEADS, XA_DIM)
    pc_s = _xattn_sample(q_s, cache_mem_k.reshape(ns, n_mem * XA_HEADS, XA_DIM),
                         cache_mem_v.reshape(ns, n_mem * XA_HEADS, XA_DIM)).reshape(ns, xa_w)

    h2_p = _merge(pa_p, pb_p, pc_p, z_p, col_zg, h1_p, 256, wa, wb, wc, wo)
    h2_s = _merge(pa_s, pb_s, pc_s, z_s, col_zg, h1_s, ns, wa, wb, wc, wo)
    y_prompt, y_sample = _ffn(h2_p, h2_s, tm_p, g2, *f2, final_g=fg)

    ends = [(b + 1) * t_len for b in range(nb)]
    shift_p = _unpack_zr_cols(jnp.stack([z_p[e - 1:e, :ZR_W] for e in ends]))[None]
    pool_p = jnp.stack([z_p[e - nbuf:e, ZR_W:ZR_W + pool_w] for e in ends])[None]
    shift_s = _unpack_zr_cols(z_s[:, :ZR_W])[None, :, None, :]
    pool_s = jnp.concatenate([pool_state[:, 1:], z_s[:, None, ZR_W:ZR_W + pool_w]], axis=1)[None]
    wkv_p = st_p[None]
    wkv_s = jnp.transpose(wkv_s_t, (3, 0, 1, 2)).reshape(state_wkv.shape)
    mem_k_p = mk_p.reshape(1, nb, n_mem, XA_HEADS, XA_DIM)
    mem_v_p = mv_p.reshape(1, nb, n_mem, XA_HEADS, XA_DIM)
    return (y_prompt.reshape(nb, t_len, d), y_sample.reshape(ns, 1, d), mem_k_p, mem_v_p, wkv_p, shift_p, pool_p,
            wkv_s, shift_s, pool_s)
```

```python
import functools
import math

import jax
import jax.numpy as jnp
from jax import lax
from jax.experimental import pallas as pl
from jax.experimental.pallas import tpu as pltpu

F32 = jnp.float32
BF16 = jnp.bfloat16

RMS_EPS = 1e-6
GN_EPS = 64e-5
POOL_WINDOWS = (2, 4, 8, 16)
HEAD = 64
LANES = 128
SUBLANES = 8
XA_HEADS = 4
XA_DIM = 128
PAST_LEN = 16384
VMEM_LIMIT = 60 * 1024 * 1024
EXP_M05 = math.exp(-0.5)

ZR_W = 3584
RWKV_W = 1024
ZR_TRUE = 3360

NN_DIMS = (((1,), (0,)), ((), ()))
NT_DIMS = (((1,), (1,)), ((), ()))
TN_DIMS = (((0,), (0,)), ((), ()))


def _cparams(sem):
    return pltpu.CompilerParams(dimension_semantics=sem, vmem_limit_bytes=VMEM_LIMIT)


def _rms(x, g):
    ms = jnp.mean(x * x, axis=-1, keepdims=True)
    return x * lax.rsqrt(ms + RMS_EPS) * g


def _bdot(a, b, dims=NN_DIMS):
    return lax.dot_general(a.astype(BF16), b.astype(BF16), dims, preferred_element_type=F32)


def _ffn_kernel(hp_ref, hs_ref, g_ref, fg_ref, wg_ref, wu_ref, wd_ref, wgt_ref, wut_ref, wdt_ref,
                op_ref, os_ref, xp_ref, xs_ref, ap_ref, as_ref, *, nfull, final):
    m = pl.program_id(0)
    f = pl.program_id(1)

    def start(h_ref, x_ref, acc_ref):
        x_ref[...] = _rms(h_ref[...], g_ref[...]).astype(BF16)
        acc_ref[...] = jnp.zeros_like(acc_ref)

    def contribution(x_ref, wg, wu, wd):
        xn = x_ref[...]
        gate = jnp.dot(xn, wg, preferred_element_type=F32)
        up = jnp.dot(xn, wu, preferred_element_type=F32)
        act = (gate * jax.nn.sigmoid(gate) * up).astype(BF16)
        return jnp.dot(act, wd, preferred_element_type=F32)

    def finish(h_ref, acc_ref, o_ref, last):
        out = h_ref[...] + 0.5 * (acc_ref[...] + last)
        if final:
            out = _rms(out, fg_ref[...])
        o_ref[...] = out

    @pl.when(f == 0)
    def _():
        start(hp_ref, xp_ref, ap_ref)

    @pl.when((f == 0) & (m == 0))
    def _():
        start(hs_ref, xs_ref, as_ref)

    @pl.when(f < nfull)
    def _():
        wg, wu, wd = wg_ref[...], wu_ref[...], wd_ref[...].astype(BF16)
        ap_ref[...] += contribution(xp_ref, wg, wu, wd)

        @pl.when(m == 0)
        def _():
            as_ref[...] += contribution(xs_ref, wg, wu, wd)

    @pl.when(f == nfull)
    def _():
        wg, wu, wd = wgt_ref[...], wut_ref[...], wdt_ref[...].astype(BF16)
        finish(hp_ref, ap_ref, op_ref, contribution(xp_ref, wg, wu, wd))

        @pl.when(m == 0)
        def _():
            finish(hs_ref, as_ref, os_ref, contribution(xs_ref, wg, wu, wd))


def _ffn(hp, hs, tm, g, wg, wu, wd, final_g=None, tf=512):
    mp, d = hp.shape
    ms = hs.shape[0]
    d_ff = wg.shape[1]
    nfull, tail = divmod(d_ff, tf)
    assert tail > 0 and tail % LANES == 0 and mp % tm == 0
    last_main = nfull - 1
    split = nfull * tf
    wgt, wut, wdt = wg[:, split:], wu[:, split:], wd[split:]
    final = final_g is not None
    vec = pl.BlockSpec((1, d), lambda i, f: (0, 0))
    once = pl.Buffered(1)
    in_specs = [
        pl.BlockSpec((tm, d), lambda i, f: (i, 0)),
        pl.BlockSpec((ms, d), lambda i, f: (0, 0), pipeline_mode=once),
        vec, vec,
        pl.BlockSpec((d, tf), lambda i, f: (0, jnp.minimum(f, last_main))),
        pl.BlockSpec((d, tf), lambda i, f: (0, jnp.minimum(f, last_main))),
        pl.BlockSpec((tf, d), lambda i, f: (jnp.minimum(f, last_main), 0)),
        pl.BlockSpec((d, tail), lambda i, f: (0, 0), pipeline_mode=once),
        pl.BlockSpec((d, tail), lambda i, f: (0, 0), pipeline_mode=once),
        pl.BlockSpec((tail, d), lambda i, f: (0, 0), pipeline_mode=once),
    ]
    return pl.pallas_call(
        functools.partial(_ffn_kernel, nfull=nfull, final=final),
        grid=(mp // tm, nfull + 1),
        in_specs=in_specs,
        out_specs=[pl.BlockSpec((tm, d), lambda i, f: (i, 0)), pl.BlockSpec((ms, d), lambda i, f: (0, 0))],
        out_shape=[jax.ShapeDtypeStruct((mp, d), F32), jax.ShapeDtypeStruct((ms, d), F32)],
        scratch_shapes=[pltpu.VMEM((tm, d), BF16), pltpu.VMEM((ms, d), BF16),
                        pltpu.VMEM((tm, d), F32), pltpu.VMEM((ms, d), F32)],
        compiler_params=_cparams(("parallel", "arbitrary")),
        name="ffn",
    )(hp, hs, g, g if final_g is None else final_g, wg, wu, wd, wgt, wut, wdt)


def _norm_matmul_kernel(h_ref, g_ref, wt_ref, o_ref, xn_ref):
    @pl.when(pl.program_id(1) == 0)
    def _():
        xn_ref[...] = _rms(h_ref[...], g_ref[...]).astype(BF16)

    o_ref[...] = lax.dot_general(xn_ref[...], wt_ref[...], NT_DIMS, preferred_element_type=F32)


def _norm_matmul(h, g, wt, tm, tn):
    m, d = h.shape
    n = wt.shape[0]
    return pl.pallas_call(
        _norm_matmul_kernel,
        grid=(m // tm, n // tn),
        in_specs=[
            pl.BlockSpec((tm, d), lambda i, j: (i, 0)),
            pl.BlockSpec((1, d), lambda i, j: (0, 0)),
            pl.BlockSpec((tn, d), lambda i, j: (j, 0)),
        ],
        out_specs=pl.BlockSpec((tm, tn), lambda i, j: (i, j)),
        out_shape=jax.ShapeDtypeStruct((m, n), F32),
        scratch_shapes=[pltpu.VMEM((tm, d), BF16)],
        compiler_params=_cparams(("parallel", "arbitrary")),
        name="norm_matmul",
    )(h, g, wt)


def _pool_mix(pooled_groups, gw_ref, scale_ref, o_ref):
    for gi, pooled in enumerate(pooled_groups):
        sl = slice(gi * LANES, (gi + 1) * LANES)
        mixed = jnp.dot(pooled.astype(BF16), gw_ref[gi], preferred_element_type=F32)
        o_ref[:, sl] = (mixed * scale_ref[:, sl]).astype(o_ref.dtype)


def _pool_prompt_kernel(zp_ref, gw_ref, scale_ref, o_ref, ext_ref, *, tt):
    t = pl.program_id(1)
    hist = 16

    @pl.when(t == 0)
    def _():
        ext_ref[0:hist, :] = jnp.zeros((hist, ext_ref.shape[1]), F32)

    x = zp_ref[...]
    ext_ref[hist:hist + tt, :] = x
    pos = t * tt + lax.broadcasted_iota(jnp.int32, (tt, LANES), 0)
    groups = []
    for gi, w in enumerate(POOL_WINDOWS):
        sl = slice(gi * LANES, (gi + 1) * LANES)
        acc = x[:, sl]
        for k in range(1, w):
            acc = acc + ext_ref[hist - k:hist - k + tt, sl]
        cnt = jnp.minimum(pos + 1, w).astype(F32)
        groups.append(acc / cnt - x[:, sl])
    _pool_mix(groups, gw_ref, scale_ref, o_ref)
    ext_ref[0:hist, :] = ext_ref[tt:tt + hist, :]


def _pool_prompt(z, nb, t_len, col_blk, gw, scale, tt=1024):
    nt = t_len // tt
    pw = gw.shape[0] * LANES
    return pl.pallas_call(
        functools.partial(_pool_prompt_kernel, tt=tt),
        grid=(nb, nt),
        in_specs=[
            pl.BlockSpec((tt, pw), lambda b, t: (b * nt + t, col_blk)),
            pl.BlockSpec(gw.shape, lambda b, t: (0, 0, 0)),
            pl.BlockSpec((1, pw), lambda b, t: (0, 0)),
        ],
        out_specs=pl.BlockSpec((tt, pw), lambda b, t: (b * nt + t, 0)),
        out_shape=jax.ShapeDtypeStruct((nb * t_len, pw), BF16),
        scratch_shapes=[pltpu.VMEM((tt + 16, pw), F32)],
        compiler_params=_cparams(("parallel", "arbitrary")),
        name="pool_prompt",
    )(z, gw, scale)


def _pool_sample_kernel(zp_ref, buf_ref, gw_ref, scale_ref, o_ref):
    x = zp_ref[...]
    nbuf = buf_ref.shape[0]
    groups = []
    for gi, w in enumerate(POOL_WINDOWS):
        sl = slice(gi * LANES, (gi + 1) * LANES)
        acc = x[:, sl]
        for k in range(1, w):
            acc = acc + buf_ref[nbuf - k, :, sl]
        cnt = float(min(PAST_LEN + 1, w))
        groups.append(acc / cnt - x[:, sl])
    _pool_mix(groups, gw_ref, scale_ref, o_ref)


def _pool_sample(z, col_blk, buf_t, gw, scale):
    nrows = z.shape[0]
    pw = gw.shape[0] * LANES
    return pl.pallas_call(
        _pool_sample_kernel,
        grid=(1,),
        in_specs=[
            pl.BlockSpec((nrows, pw), lambda i: (0, col_blk)),
            pl.BlockSpec(buf_t.shape, lambda i: (0, 0, 0)),
            pl.BlockSpec(gw.shape, lambda i: (0, 0, 0)),
            pl.BlockSpec((1, pw), lambda i: (0, 0)),
        ],
        out_specs=pl.BlockSpec((nrows, pw), lambda i: (0, 0)),
        out_shape=jax.ShapeDtypeStruct((nrows, pw), BF16),
        compiler_params=_cparams(("arbitrary",)),
        name="pool_sample",
    )(z, buf_t, gw, scale)


def _head_sum(x):
    head_a = lax.broadcasted_iota(jnp.int32, (x.shape[0], LANES), 1) < HEAD
    cols = []
    for c in range(x.shape[1] // LANES):
        t = x[:, c * LANES:(c + 1) * LANES]
        sa = jnp.sum(jnp.where(head_a, t, 0.0), axis=-1, keepdims=True)
        sb = jnp.sum(jnp.where(head_a, 0.0, t), axis=-1, keepdims=True)
        cols.append(jnp.where(head_a, sa, sb))
    return jnp.concatenate(cols, axis=1)


def _prep_math(x, prev, p_refs):
    mu_ref, w0_ref, wup_ref, a0_ref, aup_ref, gup_ref, kk_ref, ka_ref, rk_ref = p_refs
    xm = x + (prev - x) * mu_ref[...]
    w = RWKV_W
    r = xm[:, 0:w]
    k = xm[:, w:2 * w]
    v = xm[:, 2 * w:3 * w]
    wl = xm[:, 3 * w:3 * w + 128]
    al = xm[:, 3 * w + 128:3 * w + 256]
    gl = xm[:, 3 * w + 256:3 * w + 512]
    dw = w0_ref[...] + jnp.dot(jnp.tanh(wl).astype(BF16), wup_ref[...], preferred_element_type=F32)
    lw = -EXP_M05 * jax.nn.sigmoid(dw)
    a = jax.nn.sigmoid(a0_ref[...] + jnp.dot(al.astype(BF16), aup_ref[...], preferred_element_type=F32))
    g = jnp.dot(jax.nn.sigmoid(gl).astype(BF16), gup_ref[...], preferred_element_type=F32)
    kk = k * kk_ref[...]
    kk = kk * lax.rsqrt(jnp.maximum(_head_sum(kk * kk), 1e-24))
    kmod = k * (1.0 + (a - 1.0) * ka_ref[...])
    bonus = _head_sum(r * kmod * rk_ref[...]) * v
    return r, lw, kmod, v, -kk, kk * a, g, bonus


def _group_norm_gate(y, bonus, g, lng_ref, lnb_ref):
    d = y - _head_sum(y) * (1.0 / HEAD)
    var = _head_sum(d * d) * (1.0 / HEAD)
    yn = d * lax.rsqrt(var + GN_EPS) * lng_ref[...] + lnb_ref[...]
    return (yn + bonus) * g


def _prep_sample_kernel(zr_ref, prev_ref, *refs):
    p_refs, out_refs = refs[:9], refs[9:17]
    for o_ref, val in zip(out_refs, _prep_math(zr_ref[...], prev_ref[...], p_refs)):
        o_ref[...] = val


def _prep_param_specs(params):
    zero = (lambda *idx: (0, 0))
    return [pl.BlockSpec(p.shape, zero) for p in params]


def _prep_sample(z, prev, params):
    nrows = z.shape[0]
    out_spec = pl.BlockSpec((nrows, RWKV_W), lambda i: (0, 0))
    return pl.pallas_call(
        _prep_sample_kernel,
        grid=(1,),
        in_specs=[pl.BlockSpec((nrows, ZR_W), lambda i: (0, 0)),
                  pl.BlockSpec((nrows, ZR_W), lambda i: (0, 0))] + _prep_param_specs(params),
        out_specs=[out_spec] * 8,
        out_shape=[jax.ShapeDtypeStruct((nrows, RWKV_W), F32)] * 8,
        compiler_params=_cparams(("arbitrary",)),
        name="prep_sample",
    )(z, prev, *params)


def _split3(x):
    hi = x.astype(BF16)
    rest = x - hi.astype(F32)
    mid = rest.astype(BF16)
    lo = (rest - mid.astype(F32)).astype(BF16)
    return hi, mid, lo


def _select_dot(sel, x):
    sel = sel.astype(BF16)
    hi, mid, lo = _split3(x)
    return _bdot(sel, hi) + (_bdot(sel, mid) + _bdot(sel, lo))


def _chunk_scan(r, lw, k, v, a, b, s_ref, c_len):
    n2 = 2 * c_len
    assert n2 == LANES
    row = lax.broadcasted_iota(jnp.int32, (n2, n2), 0)
    col = lax.broadcasted_iota(jnp.int32, (n2, n2), 1)
    tr = row & (c_len - 1)
    tc = col & (c_len - 1)
    strict = tr > tc
    incl = tr >= tc
    tri = (lax.broadcasted_iota(jnp.int32, (c_len, c_len), 0)
           >= lax.broadcasted_iota(jnp.int32, (c_len, c_len), 1))
    head_a = lax.broadcasted_iota(jnp.int32, (c_len, LANES), 1) < HEAD

    def stack(x):
        return jnp.concatenate([jnp.where(head_a, x, 0.0), jnp.where(head_a, 0.0, x)], axis=0).astype(BF16)

    prs = range(r.shape[1] // LANES)
    cat = jnp.concatenate
    sls = [slice(q * LANES, (q + 1) * LANES) for q in prs]
    lwq = [lw[:, sl] for sl in sls]
    cum = [_select_dot(tri, x) for x in lwq]
    tot = [x[c_len - 1:c_len, :] for x in cum]
    xr = [stack(r[:, sls[q]] * jnp.exp(cum[q])) for q in prs]
    xa = [stack(a[:, sls[q]] * jnp.exp(cum[q] - lwq[q])) for q in prs]
    e_neg = [jnp.exp(-x) for x in cum]
    e_rem = [jnp.exp(tot[q] - cum[q]) for q in prs]
    yb = [stack(b[:, sls[q]] * e_neg[q]) for q in prs]
    yk = [stack(k[:, sls[q]] * e_neg[q]) for q in prs]
    zb = [stack(b[:, sls[q]] * e_rem[q]) for q in prs]
    zk = [stack(k[:, sls[q]] * e_rem[q]) for q in prs]
    vs = [stack(v[:, sl]) for sl in sls]

    g = [_bdot(cat([xa[q], xr[q]], axis=0), cat([yb[q], yk[q]], axis=0), NT_DIMS) for q in prs]
    m_ab = [jnp.where(strict, x[0:n2, 0:n2], 0.0) for x in g]
    m_ak = [jnp.where(strict, x[0:n2, n2:2 * n2], 0.0).astype(BF16) for x in g]
    n_rb = [jnp.where(incl, x[n2:2 * n2, 0:n2], 0.0) for x in g]
    n_rk = [jnp.where(incl, x[n2:2 * n2, n2:2 * n2], 0.0).astype(BF16) for x in g]

    s_old = [s_ref[q] for q in prs]
    lhs = [cat([cat([xa[q], m_ak[q]], axis=1), cat([xr[q], n_rk[q]], axis=1)], axis=0) for q in prs]
    xy0 = [_bdot(lhs[q], cat([s_old[q].T.astype(BF16), vs[q]], axis=0)) for q in prs]

    x = [xy0[q][0:n2] for q in prs]
    mk = m_ab
    nlev = int(math.log2(c_len))
    for lev in range(nlev):
        if lev < nlev - 1:
            res = [_bdot(mk[q], cat([mk[q], x[q]], axis=1)) for q in prs]
            mk = [t[:, 0:n2] for t in res]
            x = [x[q] + res[q][:, n2:2 * n2] for q in prs]
        else:
            x = [x[q] + _bdot(mk[q], x[q]) for q in prs]
    u = [t.astype(BF16) for t in x]

    y_st = [xy0[q][n2:2 * n2] + _bdot(n_rb[q], u[q]) for q in prs]
    for q in prs:
        s_ref[q] = s_old[q] * jnp.exp(tot[q]) + _bdot(cat([u[q], vs[q]], axis=0), cat([zb[q], zk[q]], axis=0),
                                                     TN_DIMS)
    return cat([t[0:c_len] + t[c_len:n2] for t in y_st], axis=1)


def _rwkv_prompt_kernel(zr_ref, *refs, nc):
    p_refs = refs[:9]
    lng_ref, lnb_ref, yb_ref, sout_ref, s_ref, carry_ref = refs[9:]
    c = pl.program_id(1)
    nseq, c_len, _ = zr_ref.shape

    @pl.when(c == 0)
    def _():
        s_ref[...] = jnp.zeros_like(s_ref)
        carry_ref[...] = jnp.zeros_like(carry_ref)

    first = lax.broadcasted_iota(jnp.int32, (c_len, ZR_W), 0) == 0
    preps = []
    for q in range(nseq):
        x = zr_ref[q]
        rolled = pltpu.roll(x, 1, axis=0)
        prev = jnp.where(first, carry_ref[q, 0:1, :], rolled)
        carry_ref[q] = rolled[0:SUBLANES, :]
        preps.append(_prep_math(x, prev, p_refs))
    r, lw, k, v, a, b = (jnp.concatenate([p[i] for p in preps], axis=1) for i in range(6))
    y = _chunk_scan(r, lw, k, v, a, b, s_ref, c_len)
    for q in range(nseq):
        yq = y[:, q * RWKV_W:(q + 1) * RWKV_W]
        yb_ref[q] = _group_norm_gate(yq, preps[q][7], preps[q][6], lng_ref, lnb_ref).astype(yb_ref.dtype)

    @pl.when(c == nc - 1)
    def _():
        npair = s_ref.shape[0] // nseq
        for q in range(nseq):
            for p in range(npair):
                s = s_ref[q * npair + p]
                sout_ref[q, 2 * p] = s[0:HEAD, 0:HEAD]
                sout_ref[q, 2 * p + 1] = s[HEAD:2 * HEAD, HEAD:2 * HEAD]


def _rwkv_prompt(z, nb, t_len, params, ln_g, ln_b, c_len=64, nseq=2):
    nc = t_len // c_len
    npair = RWKV_W // LANES
    par = pl.BlockSpec((1, RWKV_W), lambda b, c: (0, 0))
    yb, state = pl.pallas_call(
        functools.partial(_rwkv_prompt_kernel, nc=nc),
        grid=(nb // nseq, nc),
        in_specs=[pl.BlockSpec((nseq, c_len, ZR_W), lambda b, c: (b, c, 0))] + _prep_param_specs(params) + [par, par],
        out_specs=[pl.BlockSpec((nseq, c_len, RWKV_W), lambda b, c: (b, c, 0)),
                   pl.BlockSpec((nseq, 2 * npair, HEAD, HEAD), lambda b, c: (b, 0, 0, 0))],
        out_shape=[jax.ShapeDtypeStruct((nb, t_len, RWKV_W), BF16),
                   jax.ShapeDtypeStruct((nb, 2 * npair, HEAD, HEAD), F32)],
        scratch_shapes=[pltpu.VMEM((nseq * npair, LANES, LANES), F32), pltpu.VMEM((nseq, SUBLANES, ZR_W), F32)],
        compiler_params=_cparams(("parallel", "arbitrary")),
        name="rwkv_prompt",
    )(z.reshape(nb, t_len, z.shape[1]), *params, ln_g, ln_b)
    return yb.reshape(nb * t_len, RWKV_W), state


def _wkv_step_kernel(s_ref, r_ref, lw_ref, k_ref, v_ref, a_ref, b_ref, g_ref, bonus_ref, lng_ref, lnb_ref,
                     yb_ref, so_ref, vt_scr, y_scr):
    rt, wt, kt, at, bt = (x[...].T for x in (r_ref, lw_ref, k_ref, a_ref, b_ref))
    wt = jnp.exp(wt)
    vt_scr[...] = v_ref[...].T
    for hh in range(2):
        rows = slice(hh * HEAD, (hh + 1) * HEAD)
        r, w, k, a, b = (x[rows, :] for x in (rt, wt, kt, at, bt))

        def body(i, carry):
            si = s_ref[hh, i]
            sa = jnp.sum(si * a, axis=0, keepdims=True)
            vi = vt_scr[pl.ds(hh * HEAD + i, 1), :]
            s2 = si * w + sa * b + vi * k
            so_ref[hh, i] = s2
            y_scr[pl.ds(hh * HEAD + i, 1), :] = jnp.sum(s2 * r, axis=0, keepdims=True)
            return carry

        lax.fori_loop(0, HEAD, body, 0, unroll=4)

    outs = []
    for hh in range(2):
        y = y_scr[hh * HEAD:(hh + 1) * HEAD, :]
        d = y - jnp.mean(y, axis=0, keepdims=True)
        var = jnp.mean(d * d, axis=0, keepdims=True)
        outs.append(d * lax.rsqrt(var + GN_EPS))
    yn = jnp.concatenate(outs, axis=0).T * lng_ref[...] + lnb_ref[...]
    yb_ref[...] = ((yn + bonus_ref[...]) * g_ref[...]).astype(yb_ref.dtype)


def _wkv_step(state_t, vecs, ln_g, ln_b):
    nh, _, _, nb = state_t.shape
    st = pl.BlockSpec((2, HEAD, HEAD, nb), lambda p: (p, 0, 0, 0))
    vec = pl.BlockSpec((nb, 2 * HEAD), lambda p: (0, p))
    par = pl.BlockSpec((1, 2 * HEAD), lambda p: (0, p))
    return pl.pallas_call(
        _wkv_step_kernel,
        grid=(nh // 2,),
        in_specs=[st] + [vec] * 8 + [par, par],
        out_specs=[vec, st],
        out_shape=[jax.ShapeDtypeStruct((nb, nh * HEAD), BF16), jax.ShapeDtypeStruct(state_t.shape, F32)],
        scratch_shapes=[pltpu.VMEM((2 * HEAD, nb), F32), pltpu.VMEM((2 * HEAD, nb), F32)],
        compiler_params=_cparams(("parallel",)),
        name="wkv_step",
    )(state_t, *vecs, ln_g, ln_b)


def _xattn_prompt_kernel(q_ref, k_ref, v_ref, o_ref):
    scale = XA_DIM ** -0.5
    q = q_ref[...]
    for h in range(XA_HEADS):
        sl = slice(h * XA_DIM, (h + 1) * XA_DIM)
        s = lax.dot_general(q[:, sl].astype(BF16), k_ref[0, :, sl].astype(BF16),
                            NT_DIMS, preferred_element_type=F32) * scale
        p = jnp.exp(s - jnp.max(s, axis=-1, keepdims=True))
        den = jnp.sum(p, axis=-1, keepdims=True)
        o = jnp.dot(p.astype(BF16), v_ref[0, :, sl].astype(BF16), preferred_element_type=F32)
        o_ref[:, sl] = (o / den).astype(o_ref.dtype)


def _xattn_prompt(z, nb, t_len, col_blk, mk, mv, tq=1024):
    nt = t_len // tq
    xw = XA_HEADS * XA_DIM
    nmem = mk.shape[1]
    kv = pl.BlockSpec((1, nmem, xw), lambda b, t: (b, 0, 0))
    return pl.pallas_call(
        _xattn_prompt_kernel,
        grid=(nb, nt),
        in_specs=[pl.BlockSpec((tq, xw), lambda b, t: (b * nt + t, col_blk)), kv, kv],
        out_specs=pl.BlockSpec((tq, xw), lambda b, t: (b * nt + t, 0)),
        out_shape=jax.ShapeDtypeStruct((nb * t_len, xw), BF16),
        compiler_params=_cparams(("parallel", "parallel")),
        name="xattn_prompt",
    )(z, mk, mv)


def _xattn_sample_kernel(q_ref, k_ref, v_ref, o_ref):
    bb = q_ref.shape[0]
    nrow = k_ref.shape[1] // SUBLANES
    full = (bb, nrow, SUBLANES, XA_DIM)
    q = q_ref[...] * (XA_DIM ** -0.5)
    q8 = jnp.concatenate([q, q], axis=1)[:, None]
    k = k_ref[...].reshape(full)
    s = jnp.broadcast_to(jnp.sum(k * q8, axis=-1, keepdims=True), full)
    mx = jnp.max(s, axis=1, keepdims=True)
    mx = jnp.maximum(mx, pltpu.roll(mx, XA_HEADS, axis=2))
    p = jnp.exp(s - mx)
    den = jnp.sum(p, axis=1, keepdims=True)
    den = den + pltpu.roll(den, XA_HEADS, axis=2)
    o = jnp.sum(p * v_ref[...].reshape(full), axis=1, keepdims=True)
    o = o + pltpu.roll(o, XA_HEADS, axis=2)
    o_ref[...] = (o / den)[:, 0, 0:XA_HEADS, :].astype(o_ref.dtype)


def _xattn_sample(q3, mk, mv, bb=8):
    nb, rows, _ = mk.shape
    kv = pl.BlockSpec((bb, rows, XA_DIM), lambda i: (i, 0, 0))
    qs = pl.BlockSpec((bb, XA_HEADS, XA_DIM), lambda i: (i, 0, 0))
    return pl.pallas_call(
        _xattn_sample_kernel,
        grid=(nb // bb,),
        in_specs=[qs, kv, kv],
        out_specs=qs,
        out_shape=jax.ShapeDtypeStruct((nb, XA_HEADS, XA_DIM), BF16),
        compiler_params=_cparams(("parallel",)),
        name="xattn_sample",
    )(q3, mk, mv)


def _merge_kernel(pa_ref, pb_ref, pc_ref, g0_ref, g1_ref, g2_ref, h_ref, wa_ref, wb_ref, wc_ref, wo_ref, o_ref):
    oa = jnp.dot(pa_ref[...], wa_ref[...], preferred_element_type=F32)
    ob = jnp.dot(pb_ref[...], wb_ref[...], preferred_element_type=F32)
    oc = jnp.dot(pc_ref[...], wc_ref[...], preferred_element_type=F32)
    merged = (jax.nn.sigmoid(g0_ref[...]) * oa + jax.nn.sigmoid(g1_ref[...]) * ob
              + jax.nn.sigmoid(g2_ref[...]) * oc)
    o_ref[...] = h_ref[...] + jnp.dot(merged.astype(BF16), wo_ref[...], preferred_element_type=F32)


def _merge(pa, pb, pc, z, zg_blk0, h, tm, wa, wb, wc, wo):
    m, d = h.shape
    const = lambda i: (0, 0)
    resident = lambda w: pl.BlockSpec(w.shape, const, pipeline_mode=pl.Buffered(1))
    in_specs = [
        pl.BlockSpec((tm, pa.shape[1]), lambda i: (i, 0)),
        pl.BlockSpec((tm, pb.shape[1]), lambda i: (i, 0)),
        pl.BlockSpec((tm, pc.shape[1]), lambda i: (i, 0)),
        pl.BlockSpec((tm, d), lambda i: (i, zg_blk0)),
        pl.BlockSpec((tm, d), lambda i: (i, zg_blk0 + 1)),
        pl.BlockSpec((tm, d), lambda i: (i, zg_blk0 + 2)),
        pl.BlockSpec((tm, d), lambda i: (i, 0)),
        resident(wa), resident(wb), resident(wc), resident(wo),
    ]
    return pl.pallas_call(
        _merge_kernel,
        grid=(m // tm,),
        in_specs=in_specs,
        out_specs=pl.BlockSpec((tm, d), lambda i: (i, 0)),
        out_shape=jax.ShapeDtypeStruct((m, d), F32),
        compiler_params=_cparams(("parallel",)),
        name="merge",
    )(pa, pb, pc, z, z, z, h, wa, wb, wc, wo)


def _pack_zr(x, axis):
    w = RWKV_W
    take = lambda a, b: lax.slice_in_dim(x, a, b, axis=axis)

    def pad(n):
        shape = list(x.shape)
        shape[axis] = n
        return jnp.zeros(shape, x.dtype)

    return jnp.concatenate([take(0, 3 * w + 64), pad(64), take(3 * w + 64, 3 * w + 128), pad(64),
                            take(3 * w + 128, ZR_TRUE), pad(96)], axis=axis)


def _unpack_zr_cols(x):
    w = RWKV_W
    return jnp.concatenate([x[..., :3 * w + 64], x[..., 3 * w + 128:3 * w + 192], x[..., 3 * w + 256:3 * w + 416]],
                           axis=-1)


def _pad_rows(x, n):
    return jnp.concatenate([x, jnp.zeros((n - x.shape[0],) + x.shape[1:], x.dtype)], axis=0)


def kernel(x_prompt, x_sample, mem_prompt, cache_mem_k, cache_mem_v, state_wkv, state_shift, state_pool,
           ffn1_norm_g, ffn1_w_gate, ffn1_w_up, ffn1_w_down, mix_norm_g, w_in,
           pool_group_w, pool_scale, pool_out,
           rwkv_mu, rwkv_w0, rwkv_w_up, rwkv_a0, rwkv_a_up, rwkv_g_up, rwkv_k_k, rwkv_k_a, rwkv_r_k,
           rwkv_ln_g, rwkv_ln_b, rwkv_out,
           mem_norm_g, w_mem_k, w_mem_v, xattn_out, w_o,
           ffn2_norm_g, ffn2_w_gate, ffn2_w_up, ffn2_w_down, final_norm_g):
    nb, t_len, d = x_prompt.shape
    ns = x_sample.shape[0]
    assert w_in.shape[0] == 1 and x_sample.shape[1] == 1
    n_mem = mem_prompt.shape[1]
    pool_w = pool_out.shape[1]
    xa_w = xattn_out.shape[1]
    n_heads = RWKV_W // HEAD
    nbuf = state_pool.shape[2]
    rows_p = nb * t_len
    l = 0

    d_ff = ffn1_w_gate.shape[2]
    f1 = [ffn1_w_gate.reshape(d, d_ff).astype(BF16), ffn1_w_up.reshape(d, d_ff).astype(BF16),
          ffn1_w_down.reshape(d_ff, d)]
    f2 = [ffn2_w_gate.reshape(d, d_ff).astype(BF16), ffn2_w_up.reshape(d, d_ff).astype(BF16),
          ffn2_w_down.reshape(d_ff, d)]
    wit = jnp.swapaxes(w_in[l], 0, 1)
    o_zr, o_zq, o_zg = pool_w, pool_w + ZR_TRUE, pool_w + ZR_TRUE + xa_w
    w_in_t = jnp.concatenate([_pack_zr(wit[o_zr:o_zq].astype(BF16), 0), wit[:o_zr].astype(BF16),
                              wit[o_zg:].astype(BF16), wit[o_zq:o_zg].astype(BF16)], axis=0)
    col_zp = ZR_W // pool_w
    col_zg = (ZR_W + pool_w) // d
    col_zq = (ZR_W + pool_w + 3 * d) // xa_w
    row = lambda v: v.reshape(1, -1)
    prep_params = [row(_pack_zr(rwkv_mu[l], 0)), row(rwkv_w0[l]), _pad_rows(rwkv_w_up[l], 128).astype(BF16),
                   row(rwkv_a0[l]), _pad_rows(rwkv_a_up[l], 128).astype(BF16),
                   _pad_rows(rwkv_g_up[l], 256).astype(BF16), row(rwkv_k_k[l]), row(rwkv_k_a[l]),
                   row(rwkv_r_k[l])]
    gw = pool_group_w[l].astype(BF16)
    w_kv_t = jnp.concatenate([jnp.swapaxes(w_mem_k[l], 0, 1), jnp.swapaxes(w_mem_v[l], 0, 1)], axis=0).astype(BF16)
    wa, wb, wc, wo = (pool_out[l].astype(BF16), rwkv_out[l].astype(BF16), xattn_out[l].astype(BF16),
                      w_o[l].astype(BF16))
    g1, gm, g2, fg = row(ffn1_norm_g[l]), row(mix_norm_g[l]), row(ffn2_norm_g[l]), row(final_norm_g)
    scale = row(pool_scale[l])
    ln_g, ln_b = rwkv_ln_g[l], rwkv_ln_b[l]

    tm_p = 512
    h1_p, h1_s = _ffn(x_prompt.reshape(rows_p, d), x_sample.reshape(ns, d), tm_p, g1, *f1)
    z_p = _norm_matmul(h1_p, gm, w_in_t, tm=1024, tn=1536)
    z_s = _norm_matmul(h1_s, gm, w_in_t, tm=ns, tn=1536)
    kv = _norm_matmul(mem_prompt.reshape(nb * n_mem, d), row(mem_norm_g[l]), w_kv_t, tm=512, tn=512)
    mk_p = kv[:, :xa_w].reshape(nb, n_mem, xa_w)
    mv_p = kv[:, xa_w:].reshape(nb, n_mem, xa_w)

    pool_state = state_pool.reshape(ns, nbuf, pool_w)
    pa_p = _pool_prompt(z_p, nb, t_len, col_zp, gw, scale)
    pa_s = _pool_sample(z_s, col_zp, jnp.swapaxes(pool_state, 0, 1), gw, scale)

    pb_p, st_p = _rwkv_prompt(z_p, nb, t_len, prep_params, row(ln_g), row(ln_b))
    prep_s = _prep_sample(z_s, _pack_zr(state_shift.reshape(ns, ZR_TRUE), 1), prep_params)
    state_t = jnp.transpose(state_wkv.reshape(ns, n_heads, HEAD, HEAD), (1, 2, 3, 0))
    pb_s, wkv_s_t = _wkv_step(state_t, prep_s, row(ln_g), row(ln_b))

    pc_p = _xattn_prompt(z_p, nb, t_len, col_zq, mk_p, mv_p)
    q_s = z_s[:, ZR_W + pool_w + 3 * d:].reshape(ns, XA_HEADS, XA_DIM)
    pc_s = _xattn_sample(q_s, cache_mem_k.reshape(ns, n_mem * XA_HEADS, XA_DIM),
                         cache_mem_v.reshape(ns, n_mem * XA_HEADS, XA_DIM)).reshape(ns, xa_w)

    h2_p = _merge(pa_p, pb_p, pc_p, z_p, col_zg, h1_p, 256, wa, wb, wc, wo)
    h2_s = _merge(pa_s, pb_s, pc_s, z_s, col_zg, h1_s, ns, wa, wb, wc, wo)
    y_prompt, y_sample = _ffn(h2_p, h2_s, tm_p, g2, *f2, final_g=fg)

    ends = [(b + 1) * t_len for b in range(nb)]
    shift_p = _unpack_zr_cols(jnp.stack([z_p[e - 1:e, :ZR_W] for e in ends]))[None]
    pool_p = jnp.stack([z_p[e - nbuf:e, ZR_W:ZR_W + pool_w] for e in ends])[None]
    shift_s = _unpack_zr_cols(z_s[:, :ZR_W])[None, :, None, :]
    pool_s = jnp.concatenate([pool_state[:, 1:], z_s[:, None, ZR_W:ZR_W + pool_w]], axis=1)[None]
    wkv_p = st_p[None]
    wkv_s = jnp.transpose(wkv_s_t, (3, 0, 1, 2)).reshape(state_wkv.shape)
    mem_k_p = mk_p.reshape(1, nb, n_mem, XA_HEADS, XA_DIM)
    mem_v_p = mv_p.reshape(1, nb, n_mem, XA_HEADS, XA_DIM)
    return (y_prompt.reshape(nb, t_len, d), y_sample.reshape(ns, 1, d), mem_k_p, mem_v_p, wkv_p, shift_p, pool_p,
            wkv_s, shift_s, pool_s)
```

```python
import functools
import math

import jax
import jax.numpy as jnp
from jax import lax
from jax.experimental import pallas as pl
from jax.experimental.pallas import tpu as pltpu

F32 = jnp.float32
BF16 = jnp.bfloat16

RMS_EPS = 1e-6
GN_EPS = 64e-5
POOL_WINDOWS = (2, 4, 8, 16)
HEAD = 64
LANES = 128
SUBLANES = 8
XA_HEADS = 4
XA_DIM = 128
PAST_LEN = 16384
VMEM_LIMIT = 60 * 1024 * 1024
EXP_M05 = math.exp(-0.5)

RWKV_W = 1024
ZR_TRUE = 3360
RKV_W = 3 * RWKV_W
LORA_W = 512
ZR_W = RKV_W + LORA_W
GATES_W = 6144
COL_RKV = GATES_W // RKV_W
COL_LORA = (GATES_W + RKV_W) // LORA_W
COL_ZP = COL_LORA + 1
COL_ZQ = COL_LORA + 2

NN_DIMS = (((1,), (0,)), ((), ()))
NT_DIMS = (((1,), (1,)), ((), ()))
TN_DIMS = (((0,), (0,)), ((), ()))


def _cparams(sem):
    return pltpu.CompilerParams(dimension_semantics=sem, vmem_limit_bytes=VMEM_LIMIT)


def _rms(x, g):
    ms = jnp.mean(x * x, axis=-1, keepdims=True)
    return x * lax.rsqrt(ms + RMS_EPS) * g


def _bdot(a, b, dims=NN_DIMS):
    return lax.dot_general(a.astype(BF16), b.astype(BF16), dims, preferred_element_type=F32)


def _ffn_kernel(hp_ref, hs_ref, g_ref, fg_ref, wg_ref, wu_ref, wd_ref, wgt_ref, wut_ref, wdt_ref,
                op_ref, os_ref, xp_ref, xs_ref, ap_ref, as_ref, *, nfull, final):
    m = pl.program_id(0)
    f = pl.program_id(1)

    def start(h_ref, x_ref, acc_ref):
        x_ref[...] = _rms(h_ref[...], g_ref[...]).astype(BF16)
        acc_ref[...] = jnp.zeros_like(acc_ref)

    def contribution(x_ref, wg, wu, wd):
        xn = x_ref[...]
        gate = jnp.dot(xn, wg, preferred_element_type=F32)
        up = jnp.dot(xn, wu, preferred_element_type=F32)
        act = (gate * jax.nn.sigmoid(gate) * up).astype(BF16)
        return jnp.dot(act, wd, preferred_element_type=F32)

    def finish(h_ref, acc_ref, o_ref, last):
        out = h_ref[...] + 0.5 * (acc_ref[...] + last)
        if final:
            out = _rms(out, fg_ref[...])
        o_ref[...] = out

    @pl.when(f == 0)
    def _():
        start(hp_ref, xp_ref, ap_ref)

    @pl.when((f == 0) & (m == 0))
    def _():
        start(hs_ref, xs_ref, as_ref)

    @pl.when(f < nfull)
    def _():
        wg, wu, wd = wg_ref[...], wu_ref[...], wd_ref[...]
        ap_ref[...] += contribution(xp_ref, wg, wu, wd)

        @pl.when(m == 0)
        def _():
            as_ref[...] += contribution(xs_ref, wg, wu, wd)

    @pl.when(f == nfull)
    def _():
        wg, wu, wd = wgt_ref[...], wut_ref[...], wdt_ref[...]
        finish(hp_ref, ap_ref, op_ref, contribution(xp_ref, wg, wu, wd))

        @pl.when(m == 0)
        def _():
            finish(hs_ref, as_ref, os_ref, contribution(xs_ref, wg, wu, wd))


def _ffn(hp, hs, tm, g, wg, wu, wd, final_g=None, tf=512):
    mp, d = hp.shape
    ms = hs.shape[0]
    d_ff = wg.shape[1]
    nfull, tail = divmod(d_ff, tf)
    assert tail > 0 and tail % LANES == 0 and mp % tm == 0
    last_main = nfull - 1
    split = nfull * tf
    wgt, wut, wdt = wg[:, split:], wu[:, split:], wd[split:]
    final = final_g is not None
    vec = pl.BlockSpec((1, d), lambda i, f: (0, 0))
    once = pl.Buffered(1)
    in_specs = [
        pl.BlockSpec((tm, d), lambda i, f: (i, 0)),
        pl.BlockSpec((ms, d), lambda i, f: (0, 0), pipeline_mode=once),
        vec, vec,
        pl.BlockSpec((d, tf), lambda i, f: (0, jnp.minimum(f, last_main))),
        pl.BlockSpec((d, tf), lambda i, f: (0, jnp.minimum(f, last_main))),
        pl.BlockSpec((tf, d), lambda i, f: (jnp.minimum(f, last_main), 0)),
        pl.BlockSpec((d, tail), lambda i, f: (0, 0), pipeline_mode=once),
        pl.BlockSpec((d, tail), lambda i, f: (0, 0), pipeline_mode=once),
        pl.BlockSpec((tail, d), lambda i, f: (0, 0), pipeline_mode=once),
    ]
    return pl.pallas_call(
        functools.partial(_ffn_kernel, nfull=nfull, final=final),
        grid=(mp // tm, nfull + 1),
        in_specs=in_specs,
        out_specs=[pl.BlockSpec((tm, d), lambda i, f: (i, 0)), pl.BlockSpec((ms, d), lambda i, f: (0, 0))],
        out_shape=[jax.ShapeDtypeStruct((mp, d), F32), jax.ShapeDtypeStruct((ms, d), F32)],
        scratch_shapes=[pltpu.VMEM((tm, d), BF16), pltpu.VMEM((ms, d), BF16),
                        pltpu.VMEM((tm, d), F32), pltpu.VMEM((ms, d), F32)],
        compiler_params=_cparams(("parallel", "arbitrary")),
        name="ffn",
    )(hp, hs, g, g if final_g is None else final_g, wg, wu, wd, wgt, wut, wdt)


def _norm_matmul_kernel(h_ref, g_ref, wt_ref, o_ref, xn_ref):
    @pl.when(pl.program_id(1) == 0)
    def _():
        xn_ref[...] = _rms(h_ref[...], g_ref[...]).astype(BF16)

    o_ref[...] = lax.dot_general(xn_ref[...], wt_ref[...], NT_DIMS, preferred_element_type=F32)


def _norm_matmul(h, g, wt, tm, tn):
    m, d = h.shape
    n = wt.shape[0]
    return pl.pallas_call(
        _norm_matmul_kernel,
        grid=(m // tm, n // tn),
        in_specs=[
            pl.BlockSpec((tm, d), lambda i, j: (i, 0)),
            pl.BlockSpec((1, d), lambda i, j: (0, 0)),
            pl.BlockSpec((tn, d), lambda i, j: (j, 0)),
        ],
        out_specs=pl.BlockSpec((tm, tn), lambda i, j: (i, j)),
        out_shape=jax.ShapeDtypeStruct((m, n), F32),
        scratch_shapes=[pltpu.VMEM((tm, d), BF16)],
        compiler_params=_cparams(("parallel", "arbitrary")),
        name="norm_matmul",
    )(h, g, wt)


def _in_proj_kernel(hp_ref, hs_ref, g_ref, wt_ref, wm_ref, op_ref, os_ref, xp_ref, xs_ref, *, n_direct):
    i = pl.program_id(0)
    j = pl.program_id(1)

    @pl.when(j == 0)
    def _():
        xp_ref[...] = _rms(hp_ref[...], g_ref[...]).astype(BF16)

    @pl.when((j == 0) & (i == 0))
    def _():
        xs_ref[...] = _rms(hs_ref[...], g_ref[...]).astype(BF16)

    def project(w):
        op_ref[...] = lax.dot_general(xp_ref[...], w, NT_DIMS, preferred_element_type=F32)

        @pl.when(i == 0)
        def _():
            os_ref[...] = lax.dot_general(xs_ref[...], w, NT_DIMS, preferred_element_type=F32)

    @pl.when(j < n_direct)
    def _():
        project(wt_ref[...].astype(BF16))

    @pl.when(j >= n_direct)
    def _():
        project(wm_ref[...].astype(BF16))


def _in_proj(hp, hs, g, wt, segments, w_misc, tm, tn):
    mp, d = hp.shape
    ms = hs.shape[0]
    tiles = [length // tn for _, length in segments]
    assert all(length % tn == 0 and start % SUBLANES == 0 for start, length in segments)
    n_direct = sum(tiles)
    n_misc = w_misc.shape[0] // tn
    nj = n_direct + n_misc

    def wt_row(j):
        jj = jnp.minimum(j, n_direct - 1)
        first = n_direct - tiles[-1]
        row = segments[-1][0] + (jj - first) * tn
        for (start, _), nt in zip(reversed(segments[:-1]), reversed(tiles[:-1])):
            first -= nt
            row = jnp.where(jj < first + nt, start + (jj - first) * tn, row)
        return pl.multiple_of(row, SUBLANES)

    return pl.pallas_call(
        functools.partial(_in_proj_kernel, n_direct=n_direct),
        grid=(mp // tm, nj),
        in_specs=[
            pl.BlockSpec((tm, d), lambda i, j: (i, 0)),
            pl.BlockSpec((ms, d), lambda i, j: (0, 0)),
            pl.BlockSpec((1, d), lambda i, j: (0, 0)),
            pl.BlockSpec((pl.Element(tn), pl.Element(d)), lambda i, j: (wt_row(j), 0)),
            pl.BlockSpec((tn, d), lambda i, j: (jnp.clip(j - n_direct, 0, n_misc - 1), 0)),
        ],
        out_specs=[pl.BlockSpec((tm, tn), lambda i, j: (i, j)),
                   pl.BlockSpec((ms, tn), lambda i, j: (0, jnp.where(i == 0, j, nj - 1)))],
        out_shape=[jax.ShapeDtypeStruct((mp, nj * tn), F32), jax.ShapeDtypeStruct((ms, nj * tn), F32)],
        scratch_shapes=[pltpu.VMEM((tm, d), BF16), pltpu.VMEM((ms, d), BF16)],
        compiler_params=_cparams(("arbitrary", "arbitrary")),
        name="in_proj",
    )(hp, hs, g, wt, w_misc)


def _pool_mix(pooled_groups, gw_ref, scale_ref, o_ref):
    for gi, pooled in enumerate(pooled_groups):
        sl = slice(gi * LANES, (gi + 1) * LANES)
        mixed = jnp.dot(pooled.astype(BF16), gw_ref[gi], preferred_element_type=F32)
        o_ref[:, sl] = (mixed * scale_ref[:, sl]).astype(o_ref.dtype)


def _pool_prompt_kernel(zp_ref, gw_ref, scale_ref, o_ref, ext_ref, *, tt):
    t = pl.program_id(1)
    hist = 16

    @pl.when(t == 0)
    def _():
        ext_ref[0:hist, :] = jnp.zeros((hist, ext_ref.shape[1]), F32)

    x = zp_ref[...]
    ext_ref[hist:hist + tt, :] = x
    pos = t * tt + lax.broadcasted_iota(jnp.int32, (tt, LANES), 0)
    groups = []
    for gi, w in enumerate(POOL_WINDOWS):
        sl = slice(gi * LANES, (gi + 1) * LANES)
        acc = x[:, sl]
        for k in range(1, w):
            acc = acc + ext_ref[hist - k:hist - k + tt, sl]
        cnt = jnp.minimum(pos + 1, w).astype(F32)
        groups.append(acc / cnt - x[:, sl])
    _pool_mix(groups, gw_ref, scale_ref, o_ref)
    ext_ref[0:hist, :] = ext_ref[tt:tt + hist, :]


def _pool_prompt(z, nb, t_len, col_blk, gw, scale, tt=1024):
    nt = t_len // tt
    pw = gw.shape[0] * LANES
    return pl.pallas_call(
        functools.partial(_pool_prompt_kernel, tt=tt),
        grid=(nb, nt),
        in_specs=[
            pl.BlockSpec((tt, pw), lambda b, t: (b * nt + t, col_blk)),
            pl.BlockSpec(gw.shape, lambda b, t: (0, 0, 0)),
            pl.BlockSpec((1, pw), lambda b, t: (0, 0)),
        ],
        out_specs=pl.BlockSpec((tt, pw), lambda b, t: (b * nt + t, 0)),
        out_shape=jax.ShapeDtypeStruct((nb * t_len, pw), BF16),
        scratch_shapes=[pltpu.VMEM((tt + 16, pw), F32)],
        compiler_params=_cparams(("parallel", "arbitrary")),
        name="pool_prompt",
    )(z, gw, scale)


def _pool_sample_kernel(zp_ref, buf_ref, gw_ref, scale_ref, o_ref):
    x = zp_ref[...]
    nbuf = buf_ref.shape[0]
    groups = []
    for gi, w in enumerate(POOL_WINDOWS):
        sl = slice(gi * LANES, (gi + 1) * LANES)
        acc = x[:, sl]
        for k in range(1, w):
            acc = acc + buf_ref[nbuf - k, :, sl]
        cnt = float(min(PAST_LEN + 1, w))
        groups.append(acc / cnt - x[:, sl])
    _pool_mix(groups, gw_ref, scale_ref, o_ref)


def _pool_sample(z, col_blk, buf_t, gw, scale):
    nrows = z.shape[0]
    pw = gw.shape[0] * LANES
    return pl.pallas_call(
        _pool_sample_kernel,
        grid=(1,),
        in_specs=[
            pl.BlockSpec((nrows, pw), lambda i: (0, col_blk)),
            pl.BlockSpec(buf_t.shape, lambda i: (0, 0, 0)),
            pl.BlockSpec(gw.shape, lambda i: (0, 0, 0)),
            pl.BlockSpec((1, pw), lambda i: (0, 0)),
        ],
        out_specs=pl.BlockSpec((nrows, pw), lambda i: (0, 0)),
        out_shape=jax.ShapeDtypeStruct((nrows, pw), BF16),
        compiler_params=_cparams(("arbitrary",)),
        name="pool_sample",
    )(z, buf_t, gw, scale)


def _head_sum(x):
    head_a = lax.broadcasted_iota(jnp.int32, (x.shape[0], LANES), 1) < HEAD
    cols = []
    for c in range(x.shape[1] // LANES):
        t = x[:, c * LANES:(c + 1) * LANES]
        sa = jnp.sum(jnp.where(head_a, t, 0.0), axis=-1, keepdims=True)
        sb = jnp.sum(jnp.where(head_a, 0.0, t), axis=-1, keepdims=True)
        cols.append(jnp.where(head_a, sa, sb))
    return jnp.concatenate(cols, axis=1)


def _prep_math(x, prev, p_refs):
    mu_ref, w0_ref, wup_ref, a0_ref, aup_ref, gup_ref, kk_ref, ka_ref, rk_ref = p_refs
    xm = x + (prev - x) * mu_ref[...]
    w = RWKV_W
    r = xm[:, 0:w]
    k = xm[:, w:2 * w]
    v = xm[:, 2 * w:3 * w]
    wl = xm[:, 3 * w:3 * w + 128]
    al = xm[:, 3 * w + 128:3 * w + 256]
    gl = xm[:, 3 * w + 256:3 * w + 512]
    dw = w0_ref[...] + jnp.dot(jnp.tanh(wl).astype(BF16), wup_ref[...], preferred_element_type=F32)
    lw = -EXP_M05 * jax.nn.sigmoid(dw)
    a = jax.nn.sigmoid(a0_ref[...] + jnp.dot(al.astype(BF16), aup_ref[...], preferred_element_type=F32))
    g = jnp.dot(jax.nn.sigmoid(gl).astype(BF16), gup_ref[...], preferred_element_type=F32)
    kk = k * kk_ref[...]
    kk = kk * lax.rsqrt(jnp.maximum(_head_sum(kk * kk), 1e-24))
    kmod = k * (1.0 + (a - 1.0) * ka_ref[...])
    bonus = _head_sum(r * kmod * rk_ref[...]) * v
    return r, lw, kmod, v, -kk, kk * a, g, bonus


def _group_norm_gate(y, bonus, g, lng_ref, lnb_ref):
    d = y - _head_sum(y) * (1.0 / HEAD)
    var = _head_sum(d * d) * (1.0 / HEAD)
    yn = d * lax.rsqrt(var + GN_EPS) * lng_ref[...] + lnb_ref[...]
    return (yn + bonus) * g


def _prep_sample_kernel(rkv_ref, lora_ref, prev_ref, *refs):
    p_refs, out_refs = refs[:9], refs[9:17]
    x = jnp.concatenate([rkv_ref[...], lora_ref[...]], axis=1)
    for o_ref, val in zip(out_refs, _prep_math(x, prev_ref[...], p_refs)):
        o_ref[...] = val


def _prep_param_specs(params):
    zero = (lambda *idx: (0, 0))
    return [pl.BlockSpec(p.shape, zero) for p in params]


def _prep_sample(z, prev, params):
    nrows = z.shape[0]
    out_spec = pl.BlockSpec((nrows, RWKV_W), lambda i: (0, 0))
    return pl.pallas_call(
        _prep_sample_kernel,
        grid=(1,),
        in_specs=[pl.BlockSpec((nrows, RKV_W), lambda i: (0, COL_RKV)),
                  pl.BlockSpec((nrows, LORA_W), lambda i: (0, COL_LORA)),
                  pl.BlockSpec((nrows, ZR_W), lambda i: (0, 0))] + _prep_param_specs(params),
        out_specs=[out_spec] * 8,
        out_shape=[jax.ShapeDtypeStruct((nrows, RWKV_W), F32)] * 8,
        compiler_params=_cparams(("arbitrary",)),
        name="prep_sample",
    )(z, z, prev, *params)


def _split3(x):
    hi = x.astype(BF16)
    rest = x - hi.astype(F32)
    mid = rest.astype(BF16)
    lo = (rest - mid.astype(F32)).astype(BF16)
    return hi, mid, lo


def _select_dot(sel, x):
    sel = sel.astype(BF16)
    hi, mid, lo = _split3(x)
    return _bdot(sel, hi) + (_bdot(sel, mid) + _bdot(sel, lo))


def _chunk_scan(r, lw, k, v, a, b, s_ref, c_len):
    n2 = 2 * c_len
    assert n2 == LANES
    row = lax.broadcasted_iota(jnp.int32, (n2, n2), 0)
    col = lax.broadcasted_iota(jnp.int32, (n2, n2), 1)
    tr = row & (c_len - 1)
    tc = col & (c_len - 1)
    strict = tr > tc
    incl = tr >= tc
    tri = (lax.broadcasted_iota(jnp.int32, (c_len, c_len), 0)
           >= lax.broadcasted_iota(jnp.int32, (c_len, c_len), 1))
    head_a = lax.broadcasted_iota(jnp.int32, (c_len, LANES), 1) < HEAD

    def stack(x):
        return jnp.concatenate([jnp.where(head_a, x, 0.0), jnp.where(head_a, 0.0, x)], axis=0).astype(BF16)

    prs = range(r.shape[1] // LANES)
    cat = jnp.concatenate
    sls = [slice(q * LANES, (q + 1) * LANES) for q in prs]
    lwq = [lw[:, sl] for sl in sls]
    cum = [_select_dot(tri, x) for x in lwq]
    tot = [x[c_len - 1:c_len, :] for x in cum]
    xr = [stack(r[:, sls[q]] * jnp.exp(cum[q])) for q in prs]
    xa = [stack(a[:, sls[q]] * jnp.exp(cum[q] - lwq[q])) for q in prs]
    e_neg = [jnp.exp(-x) for x in cum]
    e_rem = [jnp.exp(tot[q] - cum[q]) for q in prs]
    yb = [stack(b[:, sls[q]] * e_neg[q]) for q in prs]
    yk = [stack(k[:, sls[q]] * e_neg[q]) for q in prs]
    zb = [stack(b[:, sls[q]] * e_rem[q]) for q in prs]
    zk = [stack(k[:, sls[q]] * e_rem[q]) for q in prs]
    vs = [stack(v[:, sl]) for sl in sls]

    g = [_bdot(cat([xa[q], xr[q]], axis=0), cat([yb[q], yk[q]], axis=0), NT_DIMS) for q in prs]
    m_ab = [jnp.where(strict, x[0:n2, 0:n2], 0.0) for x in g]
    m_ak = [jnp.where(strict, x[0:n2, n2:2 * n2], 0.0).astype(BF16) for x in g]
    n_rb = [jnp.where(incl, x[n2:2 * n2, 0:n2], 0.0) for x in g]
    n_rk = [jnp.where(incl, x[n2:2 * n2, n2:2 * n2], 0.0).astype(BF16) for x in g]

    s_old = [s_ref[q] for q in prs]
    lhs = [cat([cat([xa[q], m_ak[q]], axis=1), cat([xr[q], n_rk[q]], axis=1)], axis=0) for q in prs]
    xy0 = [_bdot(lhs[q], cat([s_old[q].T.astype(BF16), vs[q]], axis=0)) for q in prs]

    x = [xy0[q][0:n2] for q in prs]
    mk = m_ab
    nlev = int(math.log2(c_len))
    for lev in range(nlev):
        if lev < nlev - 1:
            res = [_bdot(mk[q], cat([mk[q], x[q]], axis=1)) for q in prs]
            mk = [t[:, 0:n2] for t in res]
            x = [x[q] + res[q][:, n2:2 * n2] for q in prs]
        else:
            x = [x[q] + _bdot(mk[q], x[q]) for q in prs]
    u = [t.astype(BF16) for t in x]

    y_st = [xy0[q][n2:2 * n2] + _bdot(n_rb[q], u[q]) for q in prs]
    for q in prs:
        s_ref[q] = s_old[q] * jnp.exp(tot[q]) + _bdot(cat([u[q], vs[q]], axis=0), cat([zb[q], zk[q]], axis=0),
                                                     TN_DIMS)
    return cat([t[0:c_len] + t[c_len:n2] for t in y_st], axis=1)


def _rwkv_prompt_kernel(rkv_ref, lora_ref, *refs, nc):
    p_refs = refs[:9]
    lng_ref, lnb_ref, yb_ref, sout_ref, s_ref, carry_ref = refs[9:]
    c = pl.program_id(1)
    nseq, c_len, _ = rkv_ref.shape

    @pl.when(c == 0)
    def _():
        s_ref[...] = jnp.zeros_like(s_ref)
        carry_ref[...] = jnp.zeros_like(carry_ref)

    first = lax.broadcasted_iota(jnp.int32, (c_len, ZR_W), 0) == 0
    preps = []
    for q in range(nseq):
        x = jnp.concatenate([rkv_ref[q], lora_ref[q]], axis=1)
        rolled = pltpu.roll(x, 1, axis=0)
        prev = jnp.where(first, carry_ref[q, 0:1, :], rolled)
        carry_ref[q] = rolled[0:SUBLANES, :]
        preps.append(_prep_math(x, prev, p_refs))
    r, lw, k, v, a, b = (jnp.concatenate([p[i] for p in preps], axis=1) for i in range(6))
    y = _chunk_scan(r, lw, k, v, a, b, s_ref, c_len)
    for q in range(nseq):
        yq = y[:, q * RWKV_W:(q + 1) * RWKV_W]
        yb_ref[q] = _group_norm_gate(yq, preps[q][7], preps[q][6], lng_ref, lnb_ref).astype(yb_ref.dtype)

    @pl.when(c == nc - 1)
    def _():
        npair = s_ref.shape[0] // nseq
        for q in range(nseq):
            for p in range(npair):
                s = s_ref[q * npair + p]
                sout_ref[q, 2 * p] = s[0:HEAD, 0:HEAD]
                sout_ref[q, 2 * p + 1] = s[HEAD:2 * HEAD, HEAD:2 * HEAD]


def _rwkv_prompt(z, nb, t_len, params, ln_g, ln_b, c_len=64, nseq=2):
    nc = t_len // c_len
    npair = RWKV_W // LANES
    par = pl.BlockSpec((1, RWKV_W), lambda b, c: (0, 0))
    z3 = z.reshape(nb, t_len, z.shape[1])
    yb, state = pl.pallas_call(
        functools.partial(_rwkv_prompt_kernel, nc=nc),
        grid=(nb // nseq, nc),
        in_specs=[pl.BlockSpec((nseq, c_len, RKV_W), lambda b, c: (b, c, COL_RKV)),
                  pl.BlockSpec((nseq, c_len, LORA_W), lambda b, c: (b, c, COL_LORA))]
        + _prep_param_specs(params) + [par, par],
        out_specs=[pl.BlockSpec((nseq, c_len, RWKV_W), lambda b, c: (b, c, 0)),
                   pl.BlockSpec((nseq, 2 * npair, HEAD, HEAD), lambda b, c: (b, 0, 0, 0))],
        out_shape=[jax.ShapeDtypeStruct((nb, t_len, RWKV_W), BF16),
                   jax.ShapeDtypeStruct((nb, 2 * npair, HEAD, HEAD), F32)],
        scratch_shapes=[pltpu.VMEM((nseq * npair, LANES, LANES), F32), pltpu.VMEM((nseq, SUBLANES, ZR_W), F32)],
        compiler_params=_cparams(("parallel", "arbitrary")),
        name="rwkv_prompt",
    )(z3, z3, *params, ln_g, ln_b)
    return yb.reshape(nb * t_len, RWKV_W), state


def _wkv_step_kernel(s_ref, r_ref, lw_ref, k_ref, v_ref, a_ref, b_ref, g_ref, bonus_ref, lng_ref, lnb_ref,
                     yb_ref, so_ref, vt_scr, y_scr):
    rt, wt, kt, at, bt = (x[...].T for x in (r_ref, lw_ref, k_ref, a_ref, b_ref))
    wt = jnp.exp(wt)
    vt_scr[...] = v_ref[...].T
    for hh in range(2):
        rows = slice(hh * HEAD, (hh + 1) * HEAD)
        r, w, k, a, b = (x[rows, :] for x in (rt, wt, kt, at, bt))

        def body(i, carry):
            si = s_ref[hh, i]
            sa = jnp.sum(si * a, axis=0, keepdims=True)
            vi = vt_scr[pl.ds(hh * HEAD + i, 1), :]
            s2 = si * w + sa * b + vi * k
            so_ref[hh, i] = s2
            y_scr[pl.ds(hh * HEAD + i, 1), :] = jnp.sum(s2 * r, axis=0, keepdims=True)
            return carry

        lax.fori_loop(0, HEAD, body, 0, unroll=4)

    outs = []
    for hh in range(2):
        y = y_scr[hh * HEAD:(hh + 1) * HEAD, :]
        d = y - jnp.mean(y, axis=0, keepdims=True)
        var = jnp.mean(d * d, axis=0, keepdims=True)
        outs.append(d * lax.rsqrt(var + GN_EPS))
    yn = jnp.concatenate(outs, axis=0).T * lng_ref[...] + lnb_ref[...]
    yb_ref[...] = ((yn + bonus_ref[...]) * g_ref[...]).astype(yb_ref.dtype)


def _wkv_step(state_t, vecs, ln_g, ln_b):
    nh, _, _, nb = state_t.shape
    st = pl.BlockSpec((2, HEAD, HEAD, nb), lambda p: (p, 0, 0, 0))
    vec = pl.BlockSpec((nb, 2 * HEAD), lambda p: (0, p))
    par = pl.BlockSpec((1, 2 * HEAD), lambda p: (0, p))
    return pl.pallas_call(
        _wkv_step_kernel,
        grid=(nh // 2,),
        in_specs=[st] + [vec] * 8 + [par, par],
        out_specs=[vec, st],
        out_shape=[jax.ShapeDtypeStruct((nb, nh * HEAD), BF16), jax.ShapeDtypeStruct(state_t.shape, F32)],
        scratch_shapes=[pltpu.VMEM((2 * HEAD, nb), F32), pltpu.VMEM((2 * HEAD, nb), F32)],
        compiler_params=_cparams(("parallel",)),
        name="wkv_step",
    )(state_t, *vecs, ln_g, ln_b)


def _xattn_prompt_kernel(q_ref, k_ref, v_ref, o_ref):
    scale = XA_DIM ** -0.5
    q = q_ref[...]
    for h in range(XA_HEADS):
        sl = slice(h * XA_DIM, (h + 1) * XA_DIM)
        s = lax.dot_general(q[:, sl].astype(BF16), k_ref[0, :, sl].astype(BF16),
                            NT_DIMS, preferred_element_type=F32) * scale
        p = jnp.exp(s - jnp.max(s, axis=-1, keepdims=True))
        den = jnp.sum(p, axis=-1, keepdims=True)
        o = jnp.dot(p.astype(BF16), v_ref[0, :, sl].astype(BF16), preferred_element_type=F32)
        o_ref[:, sl] = (o / den).astype(o_ref.dtype)


def _xattn_prompt(z, nb, t_len, col_blk, mk, mv, tq=1024):
    nt = t_len // tq
    xw = XA_HEADS * XA_DIM
    nmem = mk.shape[1]
    kv = pl.BlockSpec((1, nmem, xw), lambda b, t: (b, 0, 0))
    return pl.pallas_call(
        _xattn_prompt_kernel,
        grid=(nb, nt),
        in_specs=[pl.BlockSpec((tq, xw), lambda b, t: (b * nt + t, col_blk)), kv, kv],
        out_specs=pl.BlockSpec((tq, xw), lambda b, t: (b * nt + t, 0)),
        out_shape=jax.ShapeDtypeStruct((nb * t_len, xw), BF16),
        compiler_params=_cparams(("parallel", "parallel")),
        name="xattn_prompt",
    )(z, mk, mv)


def _xattn_sample_kernel(q_ref, k_ref, v_ref, o_ref):
    bb = q_ref.shape[0]
    nrow = k_ref.shape[1] // SUBLANES
    full = (bb, nrow, SUBLANES, XA_DIM)
    q = q_ref[...] * (XA_DIM ** -0.5)
    q8 = jnp.concatenate([q, q], axis=1)[:, None]
    k = k_ref[...].reshape(full)
    s = jnp.broadcast_to(jnp.sum(k * q8, axis=-1, keepdims=True), full)
    mx = jnp.max(s, axis=1, keepdims=True)
    mx = jnp.maximum(mx, pltpu.roll(mx, XA_HEADS, axis=2))
    p = jnp.exp(s - mx)
    den = jnp.sum(p, axis=1, keepdims=True)
    den = den + pltpu.roll(den, XA_HEADS, axis=2)
    o = jnp.sum(p * v_ref[...].reshape(full), axis=1, keepdims=True)
    o = o + pltpu.roll(o, XA_HEADS, axis=2)
    o_ref[...] = (o / den)[:, 0, 0:XA_HEADS, :].astype(o_ref.dtype)


def _xattn_sample(q3, mk, mv, bb=8):
    nb, rows, _ = mk.shape
    kv = pl.BlockSpec((bb, rows, XA_DIM), lambda i: (i, 0, 0))
    qs = pl.BlockSpec((bb, XA_HEADS, XA_DIM), lambda i: (i, 0, 0))
    return pl.pallas_call(
        _xattn_sample_kernel,
        grid=(nb // bb,),
        in_specs=[qs, kv, kv],
        out_specs=qs,
        out_shape=jax.ShapeDtypeStruct((nb, XA_HEADS, XA_DIM), BF16),
        compiler_params=_cparams(("parallel",)),
        name="xattn_sample",
    )(q3, mk, mv)


def _merge_kernel(pa_ref, pb_ref, pc_ref, g0_ref, g1_ref, g2_ref, h_ref, wa_ref, wb_ref, wc_ref, wo_ref, o_ref):
    oa = jnp.dot(pa_ref[...], wa_ref[...], preferred_element_type=F32)
    ob = jnp.dot(pb_ref[...], wb_ref[...], preferred_element_type=F32)
    oc = jnp.dot(pc_ref[...], wc_ref[...], preferred_element_type=F32)
    merged = (jax.nn.sigmoid(g0_ref[...]) * oa + jax.nn.sigmoid(g1_ref[...]) * ob
              + jax.nn.sigmoid(g2_ref[...]) * oc)
    o_ref[...] = h_ref[...] + jnp.dot(merged.astype(BF16), wo_ref[...], preferred_element_type=F32)


def _merge(pa, pb, pc, z, zg_blk0, h, tm, wa, wb, wc, wo):
    m, d = h.shape
    const = lambda i: (0, 0)
    resident = lambda w: pl.BlockSpec(w.shape, const, pipeline_mode=pl.Buffered(1))
    in_specs = [
        pl.BlockSpec((tm, pa.shape[1]), lambda i: (i, 0)),
        pl.BlockSpec((tm, pb.shape[1]), lambda i: (i, 0)),
        pl.BlockSpec((tm, pc.shape[1]), lambda i: (i, 0)),
        pl.BlockSpec((tm, d), lambda i: (i, zg_blk0)),
        pl.BlockSpec((tm, d), lambda i: (i, zg_blk0 + 1)),
        pl.BlockSpec((tm, d), lambda i: (i, zg_blk0 + 2)),
        pl.BlockSpec((tm, d), lambda i: (i, 0)),
        resident(wa), resident(wb), resident(wc), resident(wo),
    ]
    return pl.pallas_call(
        _merge_kernel,
        grid=(m // tm,),
        in_specs=in_specs,
        out_specs=pl.BlockSpec((tm, d), lambda i: (i, 0)),
        out_shape=jax.ShapeDtypeStruct((m, d), F32),
        compiler_params=_cparams(("parallel",)),
        name="merge",
    )(pa, pb, pc, z, z, z, h, wa, wb, wc, wo)


def _pack_lora(x, axis):
    take = lambda a, b: lax.slice_in_dim(x, a, b, axis=axis)

    def pad(n):
        shape = list(x.shape)
        shape[axis] = n
        return jnp.zeros(shape, x.dtype)

    return jnp.concatenate([take(0, 64), pad(64), take(64, 128), pad(64), take(128, ZR_TRUE - RKV_W), pad(96)],
                           axis=axis)


def _pack_zr(x, axis):
    return jnp.concatenate([lax.slice_in_dim(x, 0, RKV_W, axis=axis),
                            _pack_lora(lax.slice_in_dim(x, RKV_W, ZR_TRUE, axis=axis), axis)], axis=axis)


def _unpack_zr_cols(x):
    w = RWKV_W
    return jnp.concatenate([x[..., :3 * w + 64], x[..., 3 * w + 128:3 * w + 192], x[..., 3 * w + 256:3 * w + 416]],
                           axis=-1)


def _pad_rows(x, n):
    return jnp.concatenate([x, jnp.zeros((n - x.shape[0],) + x.shape[1:], x.dtype)], axis=0)


def kernel(x_prompt, x_sample, mem_prompt, cache_mem_k, cache_mem_v, state_wkv, state_shift, state_pool,
           ffn1_norm_g, ffn1_w_gate, ffn1_w_up, ffn1_w_down, mix_norm_g, w_in,
           pool_group_w, pool_scale, pool_out,
           rwkv_mu, rwkv_w0, rwkv_w_up, rwkv_a0, rwkv_a_up, rwkv_g_up, rwkv_k_k, rwkv_k_a, rwkv_r_k,
           rwkv_ln_g, rwkv_ln_b, rwkv_out,
           mem_norm_g, w_mem_k, w_mem_v, xattn_out, w_o,
           ffn2_norm_g, ffn2_w_gate, ffn2_w_up, ffn2_w_down, final_norm_g):
    nb, t_len, d = x_prompt.shape
    ns = x_sample.shape[0]
    assert w_in.shape[0] == 1 and x_sample.shape[1] == 1
    n_mem = mem_prompt.shape[1]
    pool_w = pool_out.shape[1]
    xa_w = xattn_out.shape[1]
    n_heads = RWKV_W // HEAD
    nbuf = state_pool.shape[2]
    rows_p = nb * t_len
    l = 0

    d_ff = ffn1_w_gate.shape[2]
    f1 = [w.reshape(w.shape[1:]).astype(BF16) for w in (ffn1_w_gate, ffn1_w_up, ffn1_w_down)]
    f2 = [w.reshape(w.shape[1:]).astype(BF16) for w in (ffn2_w_gate, ffn2_w_up, ffn2_w_down)]
    wit = jnp.swapaxes(w_in[l], 0, 1)
    o_zr, o_zq, o_zg = pool_w, pool_w + ZR_TRUE, pool_w + ZR_TRUE + xa_w
    assert (pool_w, xa_w, 3 * d) == (LORA_W, LORA_W, GATES_W)
    w_misc = jnp.concatenate([_pack_lora(wit[o_zr + RKV_W:o_zq], 0), wit[:o_zr], wit[o_zq:o_zg]], axis=0)
    in_segments = [(o_zg, GATES_W), (o_zr, RKV_W)]
    o_rkv, o_lora, o_zp_out, o_zq_out = GATES_W, GATES_W + RKV_W, GATES_W + ZR_W, GATES_W + ZR_W + pool_w
    row = lambda v: v.reshape(1, -1)
    prep_params = [row(_pack_zr(rwkv_mu[l], 0)), row(rwkv_w0[l]), _pad_rows(rwkv_w_up[l], 128).astype(BF16),
                   row(rwkv_a0[l]), _pad_rows(rwkv_a_up[l], 128).astype(BF16),
                   _pad_rows(rwkv_g_up[l], 256).astype(BF16), row(rwkv_k_k[l]), row(rwkv_k_a[l]),
                   row(rwkv_r_k[l])]
    gw = pool_group_w[l].astype(BF16)
    w_kv_t = jnp.concatenate([jnp.swapaxes(w_mem_k[l], 0, 1), jnp.swapaxes(w_mem_v[l], 0, 1)], axis=0).astype(BF16)
    wa, wb, wc, wo = (pool_out[l].astype(BF16), rwkv_out[l].astype(BF16), xattn_out[l].astype(BF16),
                      w_o[l].astype(BF16))
    g1, gm, g2, fg = row(ffn1_norm_g[l]), row(mix_norm_g[l]), row(ffn2_norm_g[l]), row(final_norm_g)
    scale = row(pool_scale[l])
    ln_g, ln_b = rwkv_ln_g[l], rwkv_ln_b[l]

    tm_p = 512
    h1_p, h1_s = _ffn(x_prompt.reshape(rows_p, d), x_sample.reshape(ns, d), tm_p, g1, *f1)
    z_p, z_s = _in_proj(h1_p, h1_s, gm, wit, in_segments, w_misc, tm=1024, tn=768)
    kv = _norm_matmul(mem_prompt.reshape(nb * n_mem, d), row(mem_norm_g[l]), w_kv_t, tm=512, tn=512)
    mk_p = kv[:, :xa_w].reshape(nb, n_mem, xa_w)
    mv_p = kv[:, xa_w:].reshape(nb, n_mem, xa_w)

    pool_state = state_pool.reshape(ns, nbuf, pool_w)
    pa_p = _pool_prompt(z_p, nb, t_len, COL_ZP, gw, scale)
    pa_s = _pool_sample(z_s, COL_ZP, jnp.swapaxes(pool_state, 0, 1), gw, scale)

    pb_p, st_p = _rwkv_prompt(z_p, nb, t_len, prep_params, row(ln_g), row(ln_b))
    prep_s = _prep_sample(z_s, _pack_zr(state_shift.reshape(ns, ZR_TRUE), 1), prep_params)
    state_t = jnp.transpose(state_wkv.reshape(ns, n_heads, HEAD, HEAD), (1, 2, 3, 0))
    pb_s, wkv_s_t = _wkv_step(state_t, prep_s, row(ln_g), row(ln_b))

    pc_p = _xattn_prompt(z_p, nb, t_len, COL_ZQ, mk_p, mv_p)
    q_s = z_s[:, o_zq_out:].reshape(ns, XA_HEADS, XA_DIM)
    pc_s = _xattn_sample(q_s, cache_mem_k.reshape(ns, n_mem * XA_HEADS, XA_DIM),
                         cache_mem_v.reshape(ns, n_mem * XA_HEADS, XA_DIM)).reshape(ns, xa_w)

    h2_p = _merge(pa_p, pb_p, pc_p, z_p, 0, h1_p, 256, wa, wb, wc, wo)
    h2_s = _merge(pa_s, pb_s, pc_s, z_s, 0, h1_s, ns, wa, wb, wc, wo)
    y_prompt, y_sample = _ffn(h2_p, h2_s, tm_p, g2, *f2, final_g=fg)

    ends = [(b + 1) * t_len for b in range(nb)]
    shift_p = _unpack_zr_cols(jnp.stack([z_p[e - 1:e, o_rkv:o_zp_out] for e in ends]))[None]
    pool_p = jnp.stack([z_p[e - nbuf:e, o_zp_out:o_zq_out] for e in ends])[None]
    shift_s = _unpack_zr_cols(z_s[:, o_rkv:o_zp_out])[None, :, None, :]
    pool_s = jnp.concatenate([pool_state[:, 1:], z_s[:, None, o_zp_out:o_zq_out]], axis=1)[None]
    wkv_p = st_p[None]
    wkv_s = jnp.transpose(wkv_s_t, (3, 0, 1, 2)).reshape(state_wkv.shape)
    mem_k_p = mk_p.reshape(1, nb, n_mem, XA_HEADS, XA_DIM)
    mem_v_p = mv_p.reshape(1, nb, n_mem, XA_HEADS, XA_DIM)
    return (y_prompt.reshape(nb, t_len, d), y_sample.reshape(ns, 1, d), mem_k_p, mem_v_p, wkv_p, shift_p, pool_p,
            wkv_s, shift_s, pool_s)
```

```python
import functools
import math

import jax
import jax.numpy as jnp
from jax import lax
from jax.experimental import pallas as pl
from jax.experimental.pallas import tpu as pltpu

F32 = jnp.float32
BF16 = jnp.bfloat16

RMS_EPS = 1e-6
GN_EPS = 64e-5
POOL_WINDOWS = (2, 4, 8, 16)
HEAD = 64
LANES = 128
SUBLANES = 8
XA_HEADS = 4
XA_DIM = 128
PAST_LEN = 16384
VMEM_LIMIT = 62 * 1024 * 1024
EXP_M05 = math.exp(-0.5)

ZR_W = 3584
RWKV_W = 1024
ZR_TRUE = 3360

NN_DIMS = (((1,), (0,)), ((), ()))
NT_DIMS = (((1,), (1,)), ((), ()))
TN_DIMS = (((0,), (0,)), ((), ()))


def _cparams(sem):
    return pltpu.CompilerParams(dimension_semantics=sem, vmem_limit_bytes=VMEM_LIMIT)


def _rms(x, g):
    ms = jnp.mean(x * x, axis=-1, keepdims=True)
    return x * lax.rsqrt(ms + RMS_EPS) * g


def _bdot(a, b, dims=NN_DIMS):
    return lax.dot_general(a.astype(BF16), b.astype(BF16), dims, preferred_element_type=F32)


def _ffn_kernel(hp_ref, hs_ref, g_ref, fg_ref, wg_ref, wu_ref, wd_ref, wgt_ref, wut_ref, wdt_ref,
                op_ref, os_ref, xp_ref, xs_ref, *, nfull, final):
    m = pl.program_id(0)
    f = pl.program_id(1)

    def start(h_ref, x_ref, o_ref):
        x_ref[...] = _rms(h_ref[...], g_ref[...]).astype(BF16)
        o_ref[...] = jnp.zeros_like(o_ref)

    def contribution(x_ref, wg, wu, wd):
        xn = x_ref[...]
        gate = jnp.dot(xn, wg, preferred_element_type=F32)
        up = jnp.dot(xn, wu, preferred_element_type=F32)
        act = (gate * jax.nn.sigmoid(gate) * up).astype(BF16)
        return jnp.dot(act, wd, preferred_element_type=F32)

    def finish(h_ref, o_ref, last):
        out = h_ref[...] + 0.5 * (o_ref[...] + last)
        if final:
            out = _rms(out, fg_ref[...])
        o_ref[...] = out

    @pl.when(f == 0)
    def _():
        start(hp_ref, xp_ref, op_ref)

    @pl.when((f == 0) & (m == 0))
    def _():
        start(hs_ref, xs_ref, os_ref)

    @pl.when(f < nfull)
    def _():
        wg, wu, wd = wg_ref[...], wu_ref[...], wd_ref[...]
        op_ref[...] += contribution(xp_ref, wg, wu, wd)

        @pl.when(m == 0)
        def _():
            os_ref[...] += contribution(xs_ref, wg, wu, wd)

    @pl.when(f == nfull)
    def _():
        wg, wu, wd = wgt_ref[...], wut_ref[...], wdt_ref[...]
        finish(hp_ref, op_ref, contribution(xp_ref, wg, wu, wd))

        @pl.when(m == 0)
        def _():
            finish(hs_ref, os_ref, contribution(xs_ref, wg, wu, wd))


def _ffn(hp, hs, tm, g, wg, wu, wd, final_g=None, tf=512):
    mp, d = hp.shape
    ms = hs.shape[0]
    d_ff = wg.shape[1]
    nfull, tail = divmod(d_ff, tf)
    assert tail > 0 and tail % LANES == 0 and mp % tm == 0
    last_main = nfull - 1
    split = nfull * tf
    wgt, wut, wdt = wg[:, split:], wu[:, split:], wd[split:]
    final = final_g is not None
    vec = pl.BlockSpec((1, d), lambda i, f: (0, 0))
    once = pl.Buffered(1)
    in_specs = [
        pl.BlockSpec((tm, d), lambda i, f: (i, 0), pipeline_mode=once),
        pl.BlockSpec((ms, d), lambda i, f: (0, 0), pipeline_mode=once),
        vec, vec,
        pl.BlockSpec((d, tf), lambda i, f: (0, jnp.minimum(f, last_main))),
        pl.BlockSpec((d, tf), lambda i, f: (0, jnp.minimum(f, last_main))),
        pl.BlockSpec((tf, d), lambda i, f: (jnp.minimum(f, last_main), 0)),
        pl.BlockSpec((d, tail), lambda i, f: (0, 0), pipeline_mode=once),
        pl.BlockSpec((d, tail), lambda i, f: (0, 0), pipeline_mode=once),
        pl.BlockSpec((tail, d), lambda i, f: (0, 0), pipeline_mode=once),
    ]
    return pl.pallas_call(
        functools.partial(_ffn_kernel, nfull=nfull, final=final),
        grid=(mp // tm, nfull + 1),
        in_specs=in_specs,
        out_specs=[pl.BlockSpec((tm, d), lambda i, f: (i, 0)), pl.BlockSpec((ms, d), lambda i, f: (0, 0))],
        out_shape=[jax.ShapeDtypeStruct((mp, d), F32), jax.ShapeDtypeStruct((ms, d), F32)],
        scratch_shapes=[pltpu.VMEM((tm, d), BF16), pltpu.VMEM((ms, d), BF16)],
        compiler_params=_cparams(("parallel", "arbitrary")),
        name="ffn",
    )(hp, hs, g, g if final_g is None else final_g, wg, wu, wd, wgt, wut, wdt)


def _norm_matmul_kernel(h_ref, g_ref, wt_ref, o_ref, xn_ref):
    @pl.when(pl.program_id(1) == 0)
    def _():
        xn_ref[...] = _rms(h_ref[...], g_ref[...]).astype(BF16)

    o_ref[...] = lax.dot_general(xn_ref[...], wt_ref[...], NT_DIMS, preferred_element_type=F32)


def _norm_matmul(h, g, wt, tm, tn):
    m, d = h.shape
    n = wt.shape[0]
    return pl.pallas_call(
        _norm_matmul_kernel,
        grid=(m // tm, n // tn),
        in_specs=[
            pl.BlockSpec((tm, d), lambda i, j: (i, 0)),
            pl.BlockSpec((1, d), lambda i, j: (0, 0)),
            pl.BlockSpec((tn, d), lambda i, j: (j, 0)),
        ],
        out_specs=pl.BlockSpec((tm, tn), lambda i, j: (i, j)),
        out_shape=jax.ShapeDtypeStruct((m, n), F32),
        scratch_shapes=[pltpu.VMEM((tm, d), BF16)],
        compiler_params=_cparams(("parallel", "arbitrary")),
        name="norm_matmul",
    )(h, g, wt)


def _pool_mix(pooled_groups, gw_ref, scale_ref, o_ref):
    for gi, pooled in enumerate(pooled_groups):
        sl = slice(gi * LANES, (gi + 1) * LANES)
        mixed = jnp.dot(pooled.astype(BF16), gw_ref[gi], preferred_element_type=F32)
        o_ref[:, sl] = (mixed * scale_ref[:, sl]).astype(o_ref.dtype)


def _pool_prompt_kernel(zp_ref, gw_ref, scale_ref, o_ref, ext_ref, *, tt):
    t = pl.program_id(1)
    hist = 16

    @pl.when(t == 0)
    def _():
        ext_ref[0:hist, :] = jnp.zeros((hist, ext_ref.shape[1]), F32)

    x = zp_ref[...]
    ext_ref[hist:hist + tt, :] = x
    pos = t * tt + lax.broadcasted_iota(jnp.int32, (tt, LANES), 0)
    groups = []
    for gi, w in enumerate(POOL_WINDOWS):
        sl = slice(gi * LANES, (gi + 1) * LANES)
        acc = x[:, sl]
        for k in range(1, w):
            acc = acc + ext_ref[hist - k:hist - k + tt, sl]
        cnt = jnp.minimum(pos + 1, w).astype(F32)
        groups.append(acc / cnt - x[:, sl])
    _pool_mix(groups, gw_ref, scale_ref, o_ref)
    ext_ref[0:hist, :] = ext_ref[tt:tt + hist, :]


def _pool_prompt(z, nb, t_len, col_blk, gw, scale, tt=1024):
    nt = t_len // tt
    pw = gw.shape[0] * LANES
    return pl.pallas_call(
        functools.partial(_pool_prompt_kernel, tt=tt),
        grid=(nb, nt),
        in_specs=[
            pl.BlockSpec((tt, pw), lambda b, t: (b * nt + t, col_blk)),
            pl.BlockSpec(gw.shape, lambda b, t: (0, 0, 0)),
            pl.BlockSpec((1, pw), lambda b, t: (0, 0)),
        ],
        out_specs=pl.BlockSpec((tt, pw), lambda b, t: (b * nt + t, 0)),
        out_shape=jax.ShapeDtypeStruct((nb * t_len, pw), BF16),
        scratch_shapes=[pltpu.VMEM((tt + 16, pw), F32)],
        compiler_params=_cparams(("parallel", "arbitrary")),
        name="pool_prompt",
    )(z, gw, scale)


def _pool_sample_kernel(zp_ref, buf_ref, gw_ref, scale_ref, o_ref):
    x = zp_ref[...]
    nbuf = buf_ref.shape[0]
    groups = []
    for gi, w in enumerate(POOL_WINDOWS):
        sl = slice(gi * LANES, (gi + 1) * LANES)
        acc = x[:, sl]
        for k in range(1, w):
            acc = acc + buf_ref[nbuf - k, :, sl]
        cnt = float(min(PAST_LEN + 1, w))
        groups.append(acc / cnt - x[:, sl])
    _pool_mix(groups, gw_ref, scale_ref, o_ref)


def _pool_sample(z, col_blk, buf_t, gw, scale):
    nrows = z.shape[0]
    pw = gw.shape[0] * LANES
    return pl.pallas_call(
        _pool_sample_kernel,
        grid=(1,),
        in_specs=[
            pl.BlockSpec((nrows, pw), lambda i: (0, col_blk)),
            pl.BlockSpec(buf_t.shape, lambda i: (0, 0, 0)),
            pl.BlockSpec(gw.shape, lambda i: (0, 0, 0)),
            pl.BlockSpec((1, pw), lambda i: (0, 0)),
        ],
        out_specs=pl.BlockSpec((nrows, pw), lambda i: (0, 0)),
        out_shape=jax.ShapeDtypeStruct((nrows, pw), BF16),
        compiler_params=_cparams(("arbitrary",)),
        name="pool_sample",
    )(z, buf_t, gw, scale)


def _head_sum(x):
    head_a = lax.broadcasted_iota(jnp.int32, (x.shape[0], LANES), 1) < HEAD
    cols = []
    for c in range(x.shape[1] // LANES):
        t = x[:, c * LANES:(c + 1) * LANES]
        sa = jnp.sum(jnp.where(head_a, t, 0.0), axis=-1, keepdims=True)
        sb = jnp.sum(jnp.where(head_a, 0.0, t), axis=-1, keepdims=True)
        cols.append(jnp.where(head_a, sa, sb))
    return jnp.concatenate(cols, axis=1)


def _prep_math(x, prev, p_refs):
    mu_ref, w0_ref, wup_ref, a0_ref, aup_ref, gup_ref, kk_ref, ka_ref, rk_ref = p_refs
    xm = x + (prev - x) * mu_ref[...]
    w = RWKV_W
    r = xm[:, 0:w]
    k = xm[:, w:2 * w]
    v = xm[:, 2 * w:3 * w]
    wl = xm[:, 3 * w:3 * w + 128]
    al = xm[:, 3 * w + 128:3 * w + 256]
    gl = xm[:, 3 * w + 256:3 * w + 512]
    dw = w0_ref[...] + jnp.dot(jnp.tanh(wl).astype(BF16), wup_ref[...], preferred_element_type=F32)
    lw = -EXP_M05 * jax.nn.sigmoid(dw)
    a = jax.nn.sigmoid(a0_ref[...] + jnp.dot(al.astype(BF16), aup_ref[...], preferred_element_type=F32))
    g = jnp.dot(jax.nn.sigmoid(gl).astype(BF16), gup_ref[...], preferred_element_type=F32)
    kk = k * kk_ref[...]
    kk = kk * lax.rsqrt(jnp.maximum(_head_sum(kk * kk), 1e-24))
    kmod = k * (1.0 + (a - 1.0) * ka_ref[...])
    bonus = _head_sum(r * kmod * rk_ref[...]) * v
    return r, lw, kmod, v, -kk, kk * a, g, bonus


def _group_norm_gate(y, bonus, g, lng_ref, lnb_ref):
    d = y - _head_sum(y) * (1.0 / HEAD)
    var = _head_sum(d * d) * (1.0 / HEAD)
    yn = d * lax.rsqrt(var + GN_EPS) * lng_ref[...] + lnb_ref[...]
    return (yn + bonus) * g


def _prep_sample_kernel(zr_ref, prev_ref, *refs):
    p_refs, out_refs = refs[:9], refs[9:17]
    for o_ref, val in zip(out_refs, _prep_math(zr_ref[...], prev_ref[...], p_refs)):
        o_ref[...] = val


def _prep_param_specs(params):
    zero = (lambda *idx: (0, 0))
    return [pl.BlockSpec(p.shape, zero) for p in params]


def _prep_sample(z, prev, params):
    nrows = z.shape[0]
    out_spec = pl.BlockSpec((nrows, RWKV_W), lambda i: (0, 0))
    return pl.pallas_call(
        _prep_sample_kernel,
        grid=(1,),
        in_specs=[pl.BlockSpec((nrows, ZR_W), lambda i: (0, 0)),
                  pl.BlockSpec((nrows, ZR_W), lambda i: (0, 0))] + _prep_param_specs(params),
        out_specs=[out_spec] * 8,
        out_shape=[jax.ShapeDtypeStruct((nrows, RWKV_W), F32)] * 8,
        compiler_params=_cparams(("arbitrary",)),
        name="prep_sample",
    )(z, prev, *params)


def _split3(x):
    hi = x.astype(BF16)
    rest = x - hi.astype(F32)
    mid = rest.astype(BF16)
    lo = (rest - mid.astype(F32)).astype(BF16)
    return hi, mid, lo


def _select_dot(sel, x):
    sel = sel.astype(BF16)
    hi, mid, lo = _split3(x)
    return _bdot(sel, hi) + (_bdot(sel, mid) + _bdot(sel, lo))


def _chunk_scan(r, lw, k, v, a, b, s_ref, c_len):
    n2 = 2 * c_len
    assert n2 == LANES
    row = lax.broadcasted_iota(jnp.int32, (n2, n2), 0)
    col = lax.broadcasted_iota(jnp.int32, (n2, n2), 1)
    tr = row & (c_len - 1)
    tc = col & (c_len - 1)
    strict = tr > tc
    incl = tr >= tc
    tri = (lax.broadcasted_iota(jnp.int32, (c_len, c_len), 0)
           >= lax.broadcasted_iota(jnp.int32, (c_len, c_len), 1))
    head_a = lax.broadcasted_iota(jnp.int32, (c_len, LANES), 1) < HEAD

    def stack(x):
        return jnp.concatenate([jnp.where(head_a, x, 0.0), jnp.where(head_a, 0.0, x)], axis=0).astype(BF16)

    prs = range(r.shape[1] // LANES)
    cat = jnp.concatenate
    sls = [slice(q * LANES, (q + 1) * LANES) for q in prs]
    lwq = [lw[:, sl] for sl in sls]
    cum = [_select_dot(tri, x) for x in lwq]
    tot = [x[c_len - 1:c_len, :] for x in cum]
    xr = [stack(r[:, sls[q]] * jnp.exp(cum[q])) for q in prs]
    xa = [stack(a[:, sls[q]] * jnp.exp(cum[q] - lwq[q])) for q in prs]
    e_neg = [jnp.exp(-x) for x in cum]
    e_rem = [jnp.exp(tot[q] - cum[q]) for q in prs]
    yb = [stack(b[:, sls[q]] * e_neg[q]) for q in prs]
    yk = [stack(k[:, sls[q]] * e_neg[q]) for q in prs]
    zb = [stack(b[:, sls[q]] * e_rem[q]) for q in prs]
    zk = [stack(k[:, sls[q]] * e_rem[q]) for q in prs]
    vs = [stack(v[:, sl]) for sl in sls]

    g = [_bdot(cat([xa[q], xr[q]], axis=0), cat([yb[q], yk[q]], axis=0), NT_DIMS) for q in prs]
    m_ab = [jnp.where(strict, x[0:n2, 0:n2], 0.0) for x in g]
    m_ak = [jnp.where(strict, x[0:n2, n2:2 * n2], 0.0).astype(BF16) for x in g]
    n_rb = [jnp.where(incl, x[n2:2 * n2, 0:n2], 0.0) for x in g]
    n_rk = [jnp.where(incl, x[n2:2 * n2, n2:2 * n2], 0.0).astype(BF16) for x in g]

    s_old = [s_ref[q] for q in prs]
    lhs = [cat([cat([xa[q], m_ak[q]], axis=1), cat([xr[q], n_rk[q]], axis=1)], axis=0) for q in prs]
    xy0 = [_bdot(lhs[q], cat([s_old[q].T.astype(BF16), vs[q]], axis=0)) for q in prs]

    x = [xy0[q][0:n2] for q in prs]
    mk = m_ab
    nlev = int(math.log2(c_len))
    for lev in range(nlev):
        if lev < nlev - 1:
            res = [_bdot(mk[q], cat([mk[q], x[q]], axis=1)) for q in prs]
            mk = [t[:, 0:n2] for t in res]
            x = [x[q] + res[q][:, n2:2 * n2] for q in prs]
        else:
            x = [x[q] + _bdot(mk[q], x[q]) for q in prs]
    u = [t.astype(BF16) for t in x]

    y_st = [xy0[q][n2:2 * n2] + _bdot(n_rb[q], u[q]) for q in prs]
    for q in prs:
        s_ref[q] = s_old[q] * jnp.exp(tot[q]) + _bdot(cat([u[q], vs[q]], axis=0), cat([zb[q], zk[q]], axis=0),
                                                     TN_DIMS)
    return cat([t[0:c_len] + t[c_len:n2] for t in y_st], axis=1)


def _rwkv_prompt_kernel(zr_ref, *refs, nc):
    p_refs = refs[:9]
    lng_ref, lnb_ref, yb_ref, sout_ref, s_ref, carry_ref = refs[9:]
    c = pl.program_id(1)
    nseq, c_len, _ = zr_ref.shape

    @pl.when(c == 0)
    def _():
        s_ref[...] = jnp.zeros_like(s_ref)
        carry_ref[...] = jnp.zeros_like(carry_ref)

    first = lax.broadcasted_iota(jnp.int32, (c_len, ZR_W), 0) == 0
    preps = []
    for q in range(nseq):
        x = zr_ref[q]
        rolled = pltpu.roll(x, 1, axis=0)
        prev = jnp.where(first, carry_ref[q, 0:1, :], rolled)
        carry_ref[q] = rolled[0:SUBLANES, :]
        preps.append(_prep_math(x, prev, p_refs))
    r, lw, k, v, a, b = (jnp.concatenate([p[i] for p in preps], axis=1) for i in range(6))
    y = _chunk_scan(r, lw, k, v, a, b, s_ref, c_len)
    for q in range(nseq):
        yq = y[:, q * RWKV_W:(q + 1) * RWKV_W]
        yb_ref[q] = _group_norm_gate(yq, preps[q][7], preps[q][6], lng_ref, lnb_ref).astype(yb_ref.dtype)

    @pl.when(c == nc - 1)
    def _():
        npair = s_ref.shape[0] // nseq
        for q in range(nseq):
            for p in range(npair):
                s = s_ref[q * npair + p]
                sout_ref[q, 2 * p] = s[0:HEAD, 0:HEAD]
                sout_ref[q, 2 * p + 1] = s[HEAD:2 * HEAD, HEAD:2 * HEAD]


def _rwkv_prompt(z, nb, t_len, params, ln_g, ln_b, c_len=64, nseq=2):
    nc = t_len // c_len
    npair = RWKV_W // LANES
    par = pl.BlockSpec((1, RWKV_W), lambda b, c: (0, 0))
    yb, state = pl.pallas_call(
        functools.partial(_rwkv_prompt_kernel, nc=nc),
        grid=(nb // nseq, nc),
        in_specs=[pl.BlockSpec((nseq, c_len, ZR_W), lambda b, c: (b, c, 0))] + _prep_param_specs(params) + [par, par],
        out_specs=[pl.BlockSpec((nseq, c_len, RWKV_W), lambda b, c: (b, c, 0)),
                   pl.BlockSpec((nseq, 2 * npair, HEAD, HEAD), lambda b, c: (b, 0, 0, 0))],
        out_shape=[jax.ShapeDtypeStruct((nb, t_len, RWKV_W), BF16),
                   jax.ShapeDtypeStruct((nb, 2 * npair, HEAD, HEAD), F32)],
        scratch_shapes=[pltpu.VMEM((nseq * npair, LANES, LANES), F32), pltpu.VMEM((nseq, SUBLANES, ZR_W), F32)],
        compiler_params=_cparams(("parallel", "arbitrary")),
        name="rwkv_prompt",
    )(z.reshape(nb, t_len, z.shape[1]), *params, ln_g, ln_b)
    return yb.reshape(nb * t_len, RWKV_W), state


def _wkv_step_kernel(s_ref, r_ref, lw_ref, k_ref, v_ref, a_ref, b_ref, g_ref, bonus_ref, lng_ref, lnb_ref,
                     yb_ref, so_ref, vt_scr, y_scr):
    rt, wt, kt, at, bt = (x[...].T for x in (r_ref, lw_ref, k_ref, a_ref, b_ref))
    wt = jnp.exp(wt)
    vt_scr[...] = v_ref[...].T
    for hh in range(2):
        rows = slice(hh * HEAD, (hh + 1) * HEAD)
        r, w, k, a, b = (x[rows, :] for x in (rt, wt, kt, at, bt))

        def body(i, carry):
            si = s_ref[hh, i]
            sa = jnp.sum(si * a, axis=0, keepdims=True)
            vi = vt_scr[pl.ds(hh * HEAD + i, 1), :]
            s2 = si * w + sa * b + vi * k
            so_ref[hh, i] = s2
            y_scr[pl.ds(hh * HEAD + i, 1), :] = jnp.sum(s2 * r, axis=0, keepdims=True)
            return carry

        lax.fori_loop(0, HEAD, body, 0, unroll=4)

    outs = []
    for hh in range(2):
        y = y_scr[hh * HEAD:(hh + 1) * HEAD, :]
        d = y - jnp.mean(y, axis=0, keepdims=True)
        var = jnp.mean(d * d, axis=0, keepdims=True)
        outs.append(d * lax.rsqrt(var + GN_EPS))
    yn = jnp.concatenate(outs, axis=0).T * lng_ref[...] + lnb_ref[...]
    yb_ref[...] = ((yn + bonus_ref[...]) * g_ref[...]).astype(yb_ref.dtype)


def _wkv_step(state_t, vecs, ln_g, ln_b):
    nh, _, _, nb = state_t.shape
    st = pl.BlockSpec((2, HEAD, HEAD, nb), lambda p: (p, 0, 0, 0))
    vec = pl.BlockSpec((nb, 2 * HEAD), lambda p: (0, p))
    par = pl.BlockSpec((1, 2 * HEAD), lambda p: (0, p))
    return pl.pallas_call(
        _wkv_step_kernel,
        grid=(nh // 2,),
        in_specs=[st] + [vec] * 8 + [par, par],
        out_specs=[vec, st],
        out_shape=[jax.ShapeDtypeStruct((nb, nh * HEAD), BF16), jax.ShapeDtypeStruct(state_t.shape, F32)],
        scratch_shapes=[pltpu.VMEM((2 * HEAD, nb), F32), pltpu.VMEM((2 * HEAD, nb), F32)],
        compiler_params=_cparams(("parallel",)),
        name="wkv_step",
    )(state_t, *vecs, ln_g, ln_b)


def _xattn_prompt_kernel(q_ref, k_ref, v_ref, o_ref):
    scale = XA_DIM ** -0.5
    q = q_ref[...]
    for h in range(XA_HEADS):
        sl = slice(h * XA_DIM, (h + 1) * XA_DIM)
        s = lax.dot_general(q[:, sl].astype(BF16), k_ref[0, :, sl].astype(BF16),
                            NT_DIMS, preferred_element_type=F32) * scale
        p = jnp.exp(s - jnp.max(s, axis=-1, keepdims=True))
        den = jnp.sum(p, axis=-1, keepdims=True)
        o = jnp.dot(p.astype(BF16), v_ref[0, :, sl].astype(BF16), preferred_element_type=F32)
        o_ref[:, sl] = (o / den).astype(o_ref.dtype)


def _xattn_prompt(z, nb, t_len, col_blk, mk, mv, tq=1024):
    nt = t_len // tq
    xw = XA_HEADS * XA_DIM
    nmem = mk.shape[1]
    kv = pl.BlockSpec((1, nmem, xw), lambda b, t: (b, 0, 0))
    return pl.pallas_call(
        _xattn_prompt_kernel,
        grid=(nb, nt),
        in_specs=[pl.BlockSpec((tq, xw), lambda b, t: (b * nt + t, col_blk)), kv, kv],
        out_specs=pl.BlockSpec((tq, xw), lambda b, t: (b * nt + t, 0)),
        out_shape=jax.ShapeDtypeStruct((nb * t_len, xw), BF16),
        compiler_params=_cparams(("parallel", "parallel")),
        name="xattn_prompt",
    )(z, mk, mv)


def _xattn_sample_kernel(q_ref, k_ref, v_ref, o_ref):
    bb = q_ref.shape[0]
    nrow = k_ref.shape[1] // SUBLANES
    full = (bb, nrow, SUBLANES, XA_DIM)
    q = q_ref[...] * (XA_DIM ** -0.5)
    q8 = jnp.concatenate([q, q], axis=1)[:, None]
    k = k_ref[...].reshape(full)
    s = jnp.broadcast_to(jnp.sum(k * q8, axis=-1, keepdims=True), full)
    mx = jnp.max(s, axis=1, keepdims=True)
    mx = jnp.maximum(mx, pltpu.roll(mx, XA_HEADS, axis=2))
    p = jnp.exp(s - mx)
    den = jnp.sum(p, axis=1, keepdims=True)
    den = den + pltpu.roll(den, XA_HEADS, axis=2)
    o = jnp.sum(p * v_ref[...].reshape(full), axis=1, keepdims=True)
    o = o + pltpu.roll(o, XA_HEADS, axis=2)
    o_ref[...] = (o / den)[:, 0, 0:XA_HEADS, :].astype(o_ref.dtype)


def _xattn_sample(q3, mk, mv, bb=8):
    nb, rows, _ = mk.shape
    kv = pl.BlockSpec((bb, rows, XA_DIM), lambda i: (i, 0, 0))
    qs = pl.BlockSpec((bb, XA_HEADS, XA_DIM), lambda i: (i, 0, 0))
    return pl.pallas_call(
        _xattn_sample_kernel,
        grid=(nb // bb,),
        in_specs=[qs, kv, kv],
        out_specs=qs,
        out_shape=jax.ShapeDtypeStruct((nb, XA_HEADS, XA_DIM), BF16),
        compiler_params=_cparams(("parallel",)),
        name="xattn_sample",
    )(q3, mk, mv)


def _merge_kernel(pa_ref, pb_ref, pc_ref, g0_ref, g1_ref, g2_ref, h_ref, wa_ref, wb_ref, wc_ref, wo_ref, o_ref):
    oa = jnp.dot(pa_ref[...], wa_ref[...], preferred_element_type=F32)
    ob = jnp.dot(pb_ref[...], wb_ref[...], preferred_element_type=F32)
    oc = jnp.dot(pc_ref[...], wc_ref[...], preferred_element_type=F32)
    merged = (jax.nn.sigmoid(g0_ref[...]) * oa + jax.nn.sigmoid(g1_ref[...]) * ob
              + jax.nn.sigmoid(g2_ref[...]) * oc)
    o_ref[...] = h_ref[...] + jnp.dot(merged.astype(BF16), wo_ref[...], preferred_element_type=F32)


def _merge(pa, pb, pc, z, zg_blk0, h, tm, wa, wb, wc, wo):
    m, d = h.shape
    const = lambda i: (0, 0)
    resident = lambda w: pl.BlockSpec(w.shape, const, pipeline_mode=pl.Buffered(1))
    in_specs = [
        pl.BlockSpec((tm, pa.shape[1]), lambda i: (i, 0)),
        pl.BlockSpec((tm, pb.shape[1]), lambda i: (i, 0)),
        pl.BlockSpec((tm, pc.shape[1]), lambda i: (i, 0)),
        pl.BlockSpec((tm, d), lambda i: (i, zg_blk0)),
        pl.BlockSpec((tm, d), lambda i: (i, zg_blk0 + 1)),
        pl.BlockSpec((tm, d), lambda i: (i, zg_blk0 + 2)),
        pl.BlockSpec((tm, d), lambda i: (i, 0)),
        resident(wa), resident(wb), resident(wc), resident(wo),
    ]
    return pl.pallas_call(
        _merge_kernel,
        grid=(m // tm,),
        in_specs=in_specs,
        out_specs=pl.BlockSpec((tm, d), lambda i: (i, 0)),
        out_shape=jax.ShapeDtypeStruct((m, d), F32),
        compiler_params=_cparams(("parallel",)),
        name="merge",
    )(pa, pb, pc, z, z, z, h, wa, wb, wc, wo)


def _pack_zr(x, axis):
    w = RWKV_W
    take = lambda a, b: lax.slice_in_dim(x, a, b, axis=axis)

    def pad(n):
        shape = list(x.shape)
        shape[axis] = n
        return jnp.zeros(shape, x.dtype)

    return jnp.concatenate([take(0, 3 * w + 64), pad(64), take(3 * w + 64, 3 * w + 128), pad(64),
                            take(3 * w + 128, ZR_TRUE), pad(96)], axis=axis)


def _unpack_zr_cols(x):
    w = RWKV_W
    return jnp.concatenate([x[..., :3 * w + 64], x[..., 3 * w + 128:3 * w + 192], x[..., 3 * w + 256:3 * w + 416]],
                           axis=-1)


def _pad_rows(x, n):
    return jnp.concatenate([x, jnp.zeros((n - x.shape[0],) + x.shape[1:], x.dtype)], axis=0)


def kernel(x_prompt, x_sample, mem_prompt, cache_mem_k, cache_mem_v, state_wkv, state_shift, state_pool,
           ffn1_norm_g, ffn1_w_gate, ffn1_w_up, ffn1_w_down, mix_norm_g, w_in,
           pool_group_w, pool_scale, pool_out,
           rwkv_mu, rwkv_w0, rwkv_w_up, rwkv_a0, rwkv_a_up, rwkv_g_up, rwkv_k_k, rwkv_k_a, rwkv_r_k,
           rwkv_ln_g, rwkv_ln_b, rwkv_out,
           mem_norm_g, w_mem_k, w_mem_v, xattn_out, w_o,
           ffn2_norm_g, ffn2_w_gate, ffn2_w_up, ffn2_w_down, final_norm_g):
    nb, t_len, d = x_prompt.shape
    ns = x_sample.shape[0]
    assert w_in.shape[0] == 1 and x_sample.shape[1] == 1
    n_mem = mem_prompt.shape[1]
    pool_w = pool_out.shape[1]
    xa_w = xattn_out.shape[1]
    n_heads = RWKV_W // HEAD
    nbuf = state_pool.shape[2]
    rows_p = nb * t_len
    l = 0

    f1 = [w.reshape(w.shape[1:]).astype(BF16) for w in (ffn1_w_gate, ffn1_w_up, ffn1_w_down)]
    f2 = [w.reshape(w.shape[1:]).astype(BF16) for w in (ffn2_w_gate, ffn2_w_up, ffn2_w_down)]
    wit = jnp.swapaxes(w_in[l], 0, 1)
    o_zr, o_zq, o_zg = pool_w, pool_w + ZR_TRUE, pool_w + ZR_TRUE + xa_w
    w_in_t = jnp.concatenate([_pack_zr(wit[o_zr:o_zq].astype(BF16), 0), wit[:o_zr].astype(BF16),
                              wit[o_zg:].astype(BF16), wit[o_zq:o_zg].astype(BF16)], axis=0)
    col_zp = ZR_W // pool_w
    col_zg = (ZR_W + pool_w) // d
    col_zq = (ZR_W + pool_w + 3 * d) // xa_w
    row = lambda v: v.reshape(1, -1)
    prep_params = [row(_pack_zr(rwkv_mu[l], 0)), row(rwkv_w0[l]), _pad_rows(rwkv_w_up[l], 128).astype(BF16),
                   row(rwkv_a0[l]), _pad_rows(rwkv_a_up[l], 128).astype(BF16),
                   _pad_rows(rwkv_g_up[l], 256).astype(BF16), row(rwkv_k_k[l]), row(rwkv_k_a[l]),
                   row(rwkv_r_k[l])]
    gw = pool_group_w[l].astype(BF16)
    w_kv_t = jnp.concatenate([jnp.swapaxes(w_mem_k[l], 0, 1), jnp.swapaxes(w_mem_v[l], 0, 1)], axis=0).astype(BF16)
    wa, wb, wc, wo = (pool_out[l].astype(BF16), rwkv_out[l].astype(BF16), xattn_out[l].astype(BF16),
                      w_o[l].astype(BF16))
    g1, gm, g2, fg = row(ffn1_norm_g[l]), row(mix_norm_g[l]), row(ffn2_norm_g[l]), row(final_norm_g)
    scale = row(pool_scale[l])
    ln_g, ln_b = rwkv_ln_g[l], rwkv_ln_b[l]

    tm_p = 1024
    h1_p, h1_s = _ffn(x_prompt.reshape(rows_p, d), x_sample.reshape(ns, d), tm_p, g1, *f1)
    z_p = _norm_matmul(h1_p, gm, w_in_t, tm=1024, tn=1536)
    z_s = _norm_matmul(h1_s, gm, w_in_t, tm=ns, tn=1536)
    kv = _norm_matmul(mem_prompt.reshape(nb * n_mem, d), row(mem_norm_g[l]), w_kv_t, tm=512, tn=512)
    mk_p = kv[:, :xa_w].reshape(nb, n_mem, xa_w)
    mv_p = kv[:, xa_w:].reshape(nb, n_mem, xa_w)

    pool_state = state_pool.reshape(ns, nbuf, pool_w)
    pa_p = _pool_prompt(z_p, nb, t_len, col_zp, gw, scale)
    pa_s = _pool_sample(z_s, col_zp, jnp.swapaxes(pool_state, 0, 1), gw, scale)

    pb_p, st_p = _rwkv_prompt(z_p, nb, t_len, prep_params, row(ln_g), row(ln_b))
    prep_s = _prep_sample(z_s, _pack_zr(state_shift.reshape(ns, ZR_TRUE), 1), prep_params)
    state_t = jnp.transpose(state_wkv.reshape(ns, n_heads, HEAD, HEAD), (1, 2, 3, 0))
    pb_s, wkv_s_t = _wkv_step(state_t, prep_s, row(ln_g), row(ln_b))

    pc_p = _xattn_prompt(z_p, nb, t_len, col_zq, mk_p, mv_p)
    q_s = z_s[:, ZR_W + pool_w + 3 * d:].reshape(ns, XA_HEADS, XA_DIM)
    pc_s = _xattn_sample(q_s, cache_mem_k.reshape(ns, n_mem * XA_HEADS, XA_DIM),
                         cache_mem_v.reshape(ns, n_mem * XA_HEADS, XA_DIM)).reshape(ns, xa_w)

    h2_p = _merge(pa_p, pb_p, pc_p, z_p, col_zg, h1_p, 256, wa, wb, wc, wo)
    h2_s = _merge(pa_s, pb_s, pc_s, z_s, col_zg, h1_s, ns, wa, wb, wc, wo)
    y_prompt, y_sample = _ffn(h2_p, h2_s, tm_p, g2, *f2, final_g=fg)

    ends = [(b + 1) * t_len for b in range(nb)]
    shift_p = _unpack_zr_cols(jnp.stack([z_p[e - 1:e, :ZR_W] for e in ends]))[None]
    pool_p = jnp.stack([z_p[e - nbuf:e, ZR_W:ZR_W + pool_w] for e in ends])[None]
    shift_s = _unpack_zr_cols(z_s[:, :ZR_W])[None, :, None, :]
    pool_s = jnp.concatenate([pool_state[:, 1:], z_s[:, None, ZR_W:ZR_W + pool_w]], axis=1)[None]
    wkv_p = st_p[None]
    wkv_s = jnp.transpose(wkv_s_t, (3, 0, 1, 2)).reshape(state_wkv.shape)
    mem_k_p = mk_p.reshape(1, nb, n_mem, XA_HEADS, XA_DIM)
    mem_v_p = mv_p.reshape(1, nb, n_mem, XA_HEADS, XA_DIM)
    return (y_prompt.reshape(nb, t_len, d), y_sample.reshape(ns, 1, d), mem_k_p, mem_v_p, wkv_p, shift_p, pool_p,
            wkv_s, shift_s, pool_s)
```

```python
import functools
import math

import jax
import jax.numpy as jnp
from jax import lax
from jax.experimental import pallas as pl
from jax.experimental.pallas import tpu as pltpu

F32 = jnp.float32
BF16 = jnp.bfloat16

RMS_EPS = 1e-6
GN_EPS = 64e-5
POOL_WINDOWS = (2, 4, 8, 16)
HEAD = 64
LANES = 128
SUBLANES = 8
XA_HEADS = 4
XA_DIM = 128
PAST_LEN = 16384
VMEM_LIMIT = 60 * 1024 * 1024
EXP_M05 = math.exp(-0.5)

ZR_W = 3584
RWKV_W = 1024
ZR_TRUE = 3360

NN_DIMS = (((1,), (0,)), ((), ()))
NT_DIMS = (((1,), (1,)), ((), ()))
TN_DIMS = (((0,), (0,)), ((), ()))


def _cparams(sem):
    return pltpu.CompilerParams(dimension_semantics=sem, vmem_limit_bytes=VMEM_LIMIT)


def _rms(x, g):
    ms = jnp.mean(x * x, axis=-1, keepdims=True)
    return x * lax.rsqrt(ms + RMS_EPS) * g


def _bdot(a, b, dims=NN_DIMS):
    return lax.dot_general(a.astype(BF16), b.astype(BF16), dims, preferred_element_type=F32)


def _ffn_kernel(hp_ref, hs_ref, g_ref, fg_ref, wg_ref, wu_ref, wd_ref, wgt_ref, wut_ref, wdt_ref,
                op_ref, os_ref, xp_ref, xs_ref, ap_ref, as_ref, *, nfull, final):
    m = pl.program_id(0)
    f = pl.program_id(1)

    def start(h_ref, x_ref, acc_ref):
        x_ref[...] = _rms(h_ref[...], g_ref[...]).astype(BF16)
        acc_ref[...] = jnp.zeros_like(acc_ref)

    def contribution(x_ref, wg, wu, wd):
        xn = x_ref[...]
        gate = jnp.dot(xn, wg, preferred_element_type=F32)
        up = jnp.dot(xn, wu, preferred_element_type=F32)
        act = (gate * jax.nn.sigmoid(gate) * up).astype(BF16)
        return jnp.dot(act, wd, preferred_element_type=F32)

    def finish(h_ref, acc_ref, o_ref, last):
        out = h_ref[...] + 0.5 * (acc_ref[...] + last)
        if final:
            out = _rms(out, fg_ref[...])
        o_ref[...] = out

    @pl.when(f == 0)
    def _():
        start(hp_ref, xp_ref, ap_ref)

    @pl.when((f == 0) & (m == 0))
    def _():
        start(hs_ref, xs_ref, as_ref)

    @pl.when(f < nfull)
    def _():
        wg, wu, wd = wg_ref[...], wu_ref[...], wd_ref[...]
        ap_ref[...] += contribution(xp_ref, wg, wu, wd)

        @pl.when(m == 0)
        def _():
            as_ref[...] += contribution(xs_ref, wg, wu, wd)

    @pl.when(f == nfull)
    def _():
        wg, wu, wd = wgt_ref[...], wut_ref[...], wdt_ref[...]
        finish(hp_ref, ap_ref, op_ref, contribution(xp_ref, wg, wu, wd))

        @pl.when(m == 0)
        def _():
            finish(hs_ref, as_ref, os_ref, contribution(xs_ref, wg, wu, wd))


def _ffn(hp, hs, tm, g, wg, wu, wd, final_g=None, tf=512):
    mp, d = hp.shape
    ms = hs.shape[0]
    d_ff = wg.shape[1]
    nfull, tail = divmod(d_ff, tf)
    assert tail > 0 and tail % LANES == 0 and mp % tm == 0
    last_main = nfull - 1
    split = nfull * tf
    final = final_g is not None
    vec = pl.BlockSpec((1, d), lambda i, f: (0, 0))
    once = pl.Buffered(1)
    in_specs = [
        pl.BlockSpec((tm, d), lambda i, f: (i, 0)),
        pl.BlockSpec((ms, d), lambda i, f: (0, 0), pipeline_mode=once),
        vec, vec,
        pl.BlockSpec((d, tf), lambda i, f: (0, jnp.minimum(f, last_main))),
        pl.BlockSpec((d, tf), lambda i, f: (0, jnp.minimum(f, last_main))),
        pl.BlockSpec((tf, d), lambda i, f: (jnp.minimum(f, last_main), 0)),
        pl.BlockSpec((pl.Element(d), pl.Element(tail)), lambda i, f: (0, split), pipeline_mode=once),
        pl.BlockSpec((pl.Element(d), pl.Element(tail)), lambda i, f: (0, split), pipeline_mode=once),
        pl.BlockSpec((pl.Element(tail), pl.Element(d)), lambda i, f: (split, 0), pipeline_mode=once),
    ]
    return pl.pallas_call(
        functools.partial(_ffn_kernel, nfull=nfull, final=final),
        grid=(mp // tm, nfull + 1),
        in_specs=in_specs,
        out_specs=[pl.BlockSpec((tm, d), lambda i, f: (i, 0)), pl.BlockSpec((ms, d), lambda i, f: (0, 0))],
        out_shape=[jax.ShapeDtypeStruct((mp, d), F32), jax.ShapeDtypeStruct((ms, d), F32)],
        scratch_shapes=[pltpu.VMEM((tm, d), BF16), pltpu.VMEM((ms, d), BF16),
                        pltpu.VMEM((tm, d), F32), pltpu.VMEM((ms, d), F32)],
        compiler_params=_cparams(("parallel", "arbitrary")),
        name="ffn",
    )(hp, hs, g, g if final_g is None else final_g, wg, wu, wd, wg, wu, wd)


def _norm_matmul_kernel(h_ref, g_ref, wt_ref, o_ref, xn_ref):
    @pl.when(pl.program_id(1) == 0)
    def _():
        xn_ref[...] = _rms(h_ref[...], g_ref[...]).astype(BF16)

    o_ref[...] = lax.dot_general(xn_ref[...], wt_ref[...], NT_DIMS, preferred_element_type=F32)


def _norm_matmul(h, g, wt, tm, tn):
    m, d = h.shape
    n = wt.shape[0]
    return pl.pallas_call(
        _norm_matmul_kernel,
        grid=(m // tm, n // tn),
        in_specs=[
            pl.BlockSpec((tm, d), lambda i, j: (i, 0)),
            pl.BlockSpec((1, d), lambda i, j: (0, 0)),
            pl.BlockSpec((tn, d), lambda i, j: (j, 0)),
        ],
        out_specs=pl.BlockSpec((tm, tn), lambda i, j: (i, j)),
        out_shape=jax.ShapeDtypeStruct((m, n), F32),
        scratch_shapes=[pltpu.VMEM((tm, d), BF16)],
        compiler_params=_cparams(("parallel", "arbitrary")),
        name="norm_matmul",
    )(h, g, wt)


def _pool_mix(pooled_groups, gw_ref, scale_ref, o_ref):
    for gi, pooled in enumerate(pooled_groups):
        sl = slice(gi * LANES, (gi + 1) * LANES)
        mixed = jnp.dot(pooled.astype(BF16), gw_ref[gi], preferred_element_type=F32)
        o_ref[:, sl] = (mixed * scale_ref[:, sl]).astype(o_ref.dtype)


def _pool_prompt_kernel(zp_ref, gw_ref, scale_ref, o_ref, ext_ref, *, tt):
    t = pl.program_id(1)
    hist = 16

    @pl.when(t == 0)
    def _():
        ext_ref[0:hist, :] = jnp.zeros((hist, ext_ref.shape[1]), F32)

    x = zp_ref[...]
    ext_ref[hist:hist + tt, :] = x
    pos = t * tt + lax.broadcasted_iota(jnp.int32, (tt, LANES), 0)
    groups = []
    for gi, w in enumerate(POOL_WINDOWS):
        sl = slice(gi * LANES, (gi + 1) * LANES)
        acc = x[:, sl]
        for k in range(1, w):
            acc = acc + ext_ref[hist - k:hist - k + tt, sl]
        cnt = jnp.minimum(pos + 1, w).astype(F32)
        groups.append(acc / cnt - x[:, sl])
    _pool_mix(groups, gw_ref, scale_ref, o_ref)
    ext_ref[0:hist, :] = ext_ref[tt:tt + hist, :]


def _pool_prompt(z, nb, t_len, col_blk, gw, scale, tt=1024):
    nt = t_len // tt
    pw = gw.shape[0] * LANES
    return pl.pallas_call(
        functools.partial(_pool_prompt_kernel, tt=tt),
        grid=(nb, nt),
        in_specs=[
            pl.BlockSpec((tt, pw), lambda b, t: (b * nt + t, col_blk)),
            pl.BlockSpec(gw.shape, lambda b, t: (0, 0, 0)),
            pl.BlockSpec((1, pw), lambda b, t: (0, 0)),
        ],
        out_specs=pl.BlockSpec((tt, pw), lambda b, t: (b * nt + t, 0)),
        out_shape=jax.ShapeDtypeStruct((nb * t_len, pw), BF16),
        scratch_shapes=[pltpu.VMEM((tt + 16, pw), F32)],
        compiler_params=_cparams(("parallel", "arbitrary")),
        name="pool_prompt",
    )(z, gw, scale)


def _pool_sample_kernel(zp_ref, buf_ref, gw_ref, scale_ref, o_ref):
    x = zp_ref[...]
    nbuf = buf_ref.shape[0]
    groups = []
    for gi, w in enumerate(POOL_WINDOWS):
        sl = slice(gi * LANES, (gi + 1) * LANES)
        acc = x[:, sl]
        for k in range(1, w):
            acc = acc + buf_ref[nbuf - k, :, sl]
        cnt = float(min(PAST_LEN + 1, w))
        groups.append(acc / cnt - x[:, sl])
    _pool_mix(groups, gw_ref, scale_ref, o_ref)


def _pool_sample(z, col_blk, buf_t, gw, scale):
    nrows = z.shape[0]
    pw = gw.shape[0] * LANES
    return pl.pallas_call(
        _pool_sample_kernel,
        grid=(1,),
        in_specs=[
            pl.BlockSpec((nrows, pw), lambda i: (0, col_blk)),
            pl.BlockSpec(buf_t.shape, lambda i: (0, 0, 0)),
            pl.BlockSpec(gw.shape, lambda i: (0, 0, 0)),
            pl.BlockSpec((1, pw), lambda i: (0, 0)),
        ],
        out_specs=pl.BlockSpec((nrows, pw), lambda i: (0, 0)),
        out_shape=jax.ShapeDtypeStruct((nrows, pw), BF16),
        compiler_params=_cparams(("arbitrary",)),
        name="pool_sample",
    )(z, buf_t, gw, scale)


def _head_sum(x):
    head_a = lax.broadcasted_iota(jnp.int32, (x.shape[0], LANES), 1) < HEAD
    cols = []
    for c in range(x.shape[1] // LANES):
        t = x[:, c * LANES:(c + 1) * LANES]
        sa = jnp.sum(jnp.where(head_a, t, 0.0), axis=-1, keepdims=True)
        sb = jnp.sum(jnp.where(head_a, 0.0, t), axis=-1, keepdims=True)
        cols.append(jnp.where(head_a, sa, sb))
    return jnp.concatenate(cols, axis=1)


def _prep_math(x, prev, p_refs):
    mu_ref, w0_ref, wup_ref, a0_ref, aup_ref, gup_ref, kk_ref, ka_ref, rk_ref = p_refs
    xm = x + (prev - x) * mu_ref[...]
    w = RWKV_W
    r = xm[:, 0:w]
    k = xm[:, w:2 * w]
    v = xm[:, 2 * w:3 * w]
    wl = xm[:, 3 * w:3 * w + 128]
    al = xm[:, 3 * w + 128:3 * w + 256]
    gl = xm[:, 3 * w + 256:3 * w + 512]
    dw = w0_ref[...] + jnp.dot(jnp.tanh(wl).astype(BF16), wup_ref[...], preferred_element_type=F32)
    lw = -EXP_M05 * jax.nn.sigmoid(dw)
    a = jax.nn.sigmoid(a0_ref[...] + jnp.dot(al.astype(BF16), aup_ref[...], preferred_element_type=F32))
    g = jnp.dot(jax.nn.sigmoid(gl).astype(BF16), gup_ref[...], preferred_element_type=F32)
    kk = k * kk_ref[...]
    kk = kk * lax.rsqrt(jnp.maximum(_head_sum(kk * kk), 1e-24))
    kmod = k * (1.0 + (a - 1.0) * ka_ref[...])
    bonus = _head_sum(r * kmod * rk_ref[...]) * v
    return r, lw, kmod, v, -kk, kk * a, g, bonus


def _group_norm_gate(y, bonus, g, lng_ref, lnb_ref):
    d = y - _head_sum(y) * (1.0 / HEAD)
    var = _head_sum(d * d) * (1.0 / HEAD)
    yn = d * lax.rsqrt(var + GN_EPS) * lng_ref[...] + lnb_ref[...]
    return (yn + bonus) * g


def _prep_sample_kernel(zr_ref, prev_ref, *refs):
    p_refs, out_refs = refs[:9], refs[9:17]
    for o_ref, val in zip(out_refs, _prep_math(zr_ref[...], prev_ref[...], p_refs)):
        o_ref[...] = val


def _prep_param_specs(params):
    zero = (lambda *idx: (0, 0))
    return [pl.BlockSpec(p.shape, zero) for p in params]


def _prep_sample(z, prev, params):
    nrows = z.shape[0]
    out_spec = pl.BlockSpec((nrows, RWKV_W), lambda i: (0, 0))
    return pl.pallas_call(
        _prep_sample_kernel,
        grid=(1,),
        in_specs=[pl.BlockSpec((nrows, ZR_W), lambda i: (0, 0)),
                  pl.BlockSpec((nrows, ZR_W), lambda i: (0, 0))] + _prep_param_specs(params),
        out_specs=[out_spec] * 8,
        out_shape=[jax.ShapeDtypeStruct((nrows, RWKV_W), F32)] * 8,
        compiler_params=_cparams(("arbitrary",)),
        name="prep_sample",
    )(z, prev, *params)


def _split3(x):
    hi = x.astype(BF16)
    rest = x - hi.astype(F32)
    mid = rest.astype(BF16)
    lo = (rest - mid.astype(F32)).astype(BF16)
    return hi, mid, lo


def _select_dot(sel, x):
    sel = sel.astype(BF16)
    hi, mid, lo = _split3(x)
    return _bdot(sel, hi) + (_bdot(sel, mid) + _bdot(sel, lo))


def _chunk_scan(r, lw, k, v, a, b, s_ref, c_len):
    n2 = 2 * c_len
    assert n2 == LANES
    row = lax.broadcasted_iota(jnp.int32, (n2, n2), 0)
    col = lax.broadcasted_iota(jnp.int32, (n2, n2), 1)
    tr = row & (c_len - 1)
    tc = col & (c_len - 1)
    strict = tr > tc
    incl = tr >= tc
    tri = (lax.broadcasted_iota(jnp.int32, (c_len, c_len), 0)
           >= lax.broadcasted_iota(jnp.int32, (c_len, c_len), 1))
    head_a = lax.broadcasted_iota(jnp.int32, (c_len, LANES), 1) < HEAD

    def stack(x):
        return jnp.concatenate([jnp.where(head_a, x, 0.0), jnp.where(head_a, 0.0, x)], axis=0).astype(BF16)

    prs = range(r.shape[1] // LANES)
    cat = jnp.concatenate
    sls = [slice(q * LANES, (q + 1) * LANES) for q in prs]
    lwq = [lw[:, sl] for sl in sls]
    cum = [_select_dot(tri, x) for x in lwq]
    tot = [x[c_len - 1:c_len, :] for x in cum]
    xr = [stack(r[:, sls[q]] * jnp.exp(cum[q])) for q in prs]
    xa = [stack(a[:, sls[q]] * jnp.exp(cum[q] - lwq[q])) for q in prs]
    e_neg = [jnp.exp(-x) for x in cum]
    e_rem = [jnp.exp(tot[q] - cum[q]) for q in prs]
    yb = [stack(b[:, sls[q]] * e_neg[q]) for q in prs]
    yk = [stack(k[:, sls[q]] * e_neg[q]) for q in prs]
    zb = [stack(b[:, sls[q]] * e_rem[q]) for q in prs]
    zk = [stack(k[:, sls[q]] * e_rem[q]) for q in prs]
    vs = [stack(v[:, sl]) for sl in sls]

    g = [_bdot(cat([xa[q], xr[q]], axis=0), cat([yb[q], yk[q]], axis=0), NT_DIMS) for q in prs]
    m_ab = [jnp.where(strict, x[0:n2, 0:n2], 0.0) for x in g]
    m_ak = [jnp.where(strict, x[0:n2, n2:2 * n2], 0.0).astype(BF16) for x in g]
    n_rb = [jnp.where(incl, x[n2:2 * n2, 0:n2], 0.0) for x in g]
    n_rk = [jnp.where(incl, x[n2:2 * n2, n2:2 * n2], 0.0).astype(BF16) for x in g]

    s_old = [s_ref[q] for q in prs]
    lhs = [cat([cat([xa[q], m_ak[q]], axis=1), cat([xr[q], n_rk[q]], axis=1)], axis=0) for q in prs]
    xy0 = [_bdot(lhs[q], cat([s_old[q].T.astype(BF16), vs[q]], axis=0)) for q in prs]

    x = [xy0[q][0:n2] for q in prs]
    mk = m_ab
    nlev = int(math.log2(c_len))
    for lev in range(nlev):
        if lev < nlev - 1:
            res = [_bdot(mk[q], cat([mk[q], x[q]], axis=1)) for q in prs]
            mk = [t[:, 0:n2] for t in res]
            x = [x[q] + res[q][:, n2:2 * n2] for q in prs]
        else:
            x = [x[q] + _bdot(mk[q], x[q]) for q in prs]
    u = [t.astype(BF16) for t in x]

    y_st = [xy0[q][n2:2 * n2] + _bdot(n_rb[q], u[q]) for q in prs]
    for q in prs:
        s_ref[q] = s_old[q] * jnp.exp(tot[q]) + _bdot(cat([u[q], vs[q]], axis=0), cat([zb[q], zk[q]], axis=0),
                                                     TN_DIMS)
    return cat([t[0:c_len] + t[c_len:n2] for t in y_st], axis=1)


def _rwkv_prompt_kernel(zr_ref, *refs, nc):
    p_refs = refs[:9]
    lng_ref, lnb_ref, yb_ref, sout_ref, s_ref, carry_ref = refs[9:]
    c = pl.program_id(1)
    nseq, c_len, _ = zr_ref.shape

    @pl.when(c == 0)
    def _():
        s_ref[...] = jnp.zeros_like(s_ref)
        carry_ref[...] = jnp.zeros_like(carry_ref)

    first = lax.broadcasted_iota(jnp.int32, (c_len, ZR_W), 0) == 0
    preps = []
    for q in range(nseq):
        x = zr_ref[q]
        rolled = pltpu.roll(x, 1, axis=0)
        prev = jnp.where(first, carry_ref[q, 0:1, :], rolled)
        carry_ref[q] = rolled[0:SUBLANES, :]
        preps.append(_prep_math(x, prev, p_refs))
    r, lw, k, v, a, b = (jnp.concatenate([p[i] for p in preps], axis=1) for i in range(6))
    y = _chunk_scan(r, lw, k, v, a, b, s_ref, c_len)
    for q in range(nseq):
        yq = y[:, q * RWKV_W:(q + 1) * RWKV_W]
        yb_ref[q] = _group_norm_gate(yq, preps[q][7], preps[q][6], lng_ref, lnb_ref).astype(yb_ref.dtype)

    @pl.when(c == nc - 1)
    def _():
        npair = s_ref.shape[0] // nseq
        for q in range(nseq):
            for p in range(npair):
                s = s_ref[q * npair + p]
                sout_ref[q, 2 * p] = s[0:HEAD, 0:HEAD]
                sout_ref[q, 2 * p + 1] = s[HEAD:2 * HEAD, HEAD:2 * HEAD]


def _rwkv_prompt(z, nb, t_len, params, ln_g, ln_b, c_len=64, nseq=2):
    nc = t_len // c_len
    npair = RWKV_W // LANES
    par = pl.BlockSpec((1, RWKV_W), lambda b, c: (0, 0))
    yb, state = pl.pallas_call(
        functools.partial(_rwkv_prompt_kernel, nc=nc),
        grid=(nb // nseq, nc),
        in_specs=[pl.BlockSpec((nseq, c_len, ZR_W), lambda b, c: (b, c, 0))] + _prep_param_specs(params) + [par, par],
        out_specs=[pl.BlockSpec((nseq, c_len, RWKV_W), lambda b, c: (b, c, 0)),
                   pl.BlockSpec((nseq, 2 * npair, HEAD, HEAD), lambda b, c: (b, 0, 0, 0))],
        out_shape=[jax.ShapeDtypeStruct((nb, t_len, RWKV_W), BF16),
                   jax.ShapeDtypeStruct((nb, 2 * npair, HEAD, HEAD), F32)],
        scratch_shapes=[pltpu.VMEM((nseq * npair, LANES, LANES), F32), pltpu.VMEM((nseq, SUBLANES, ZR_W), F32)],
        compiler_params=_cparams(("parallel", "arbitrary")),
        name="rwkv_prompt",
    )(z.reshape(nb, t_len, z.shape[1]), *params, ln_g, ln_b)
    return yb.reshape(nb * t_len, RWKV_W), state


def _wkv_step_kernel(s_ref, r_ref, lw_ref, k_ref, v_ref, a_ref, b_ref, g_ref, bonus_ref, lng_ref, lnb_ref,
                     yb_ref, so_ref, vt_scr, y_scr):
    rt, wt, kt, at, bt = (x[...].T for x in (r_ref, lw_ref, k_ref, a_ref, b_ref))
    wt = jnp.exp(wt)
    vt_scr[...] = v_ref[...].T
    for hh in range(2):
        rows = slice(hh * HEAD, (hh + 1) * HEAD)
        r, w, k, a, b = (x[rows, :] for x in (rt, wt, kt, at, bt))

        def body(i, carry):
            si = s_ref[hh, i]
            sa = jnp.sum(si * a, axis=0, keepdims=True)
            vi = vt_scr[pl.ds(hh * HEAD + i, 1), :]
            s2 = si * w + sa * b + vi * k
            so_ref[hh, i] = s2
            y_scr[pl.ds(hh * HEAD + i, 1), :] = jnp.sum(s2 * r, axis=0, keepdims=True)
            return carry

        lax.fori_loop(0, HEAD, body, 0, unroll=4)

    outs = []
    for hh in range(2):
        y = y_scr[hh * HEAD:(hh + 1) * HEAD, :]
        d = y - jnp.mean(y, axis=0, keepdims=True)
        var = jnp.mean(d * d, axis=0, keepdims=True)
        outs.append(d * lax.rsqrt(var + GN_EPS))
    yn = jnp.concatenate(outs, axis=0).T * lng_ref[...] + lnb_ref[...]
    yb_ref[...] = ((yn + bonus_ref[...]) * g_ref[...]).astype(yb_ref.dtype)


def _wkv_step(state_t, vecs, ln_g, ln_b):
    nh, _, _, nb = state_t.shape
    st = pl.BlockSpec((2, HEAD, HEAD, nb), lambda p: (p, 0, 0, 0))
    vec = pl.BlockSpec((nb, 2 * HEAD), lambda p: (0, p))
    par = pl.BlockSpec((1, 2 * HEAD), lambda p: (0, p))
    return pl.pallas_call(
        _wkv_step_kernel,
        grid=(nh // 2,),
        in_specs=[st] + [vec] * 8 + [par, par],
        out_specs=[vec, st],
        out_shape=[jax.ShapeDtypeStruct((nb, nh * HEAD), BF16), jax.ShapeDtypeStruct(state_t.shape, F32)],
        scratch_shapes=[pltpu.VMEM((2 * HEAD, nb), F32), pltpu.VMEM((2 * HEAD, nb), F32)],
        compiler_params=_cparams(("parallel",)),
        name="wkv_step",
    )(state_t, *vecs, ln_g, ln_b)


def _xattn_prompt_kernel(q_ref, k_ref, v_ref, o_ref):
    scale = XA_DIM ** -0.5
    q = q_ref[...]
    for h in range(XA_HEADS):
        sl = slice(h * XA_DIM, (h + 1) * XA_DIM)
        s = lax.dot_general(q[:, sl].astype(BF16), k_ref[0, :, sl].astype(BF16),
                            NT_DIMS, preferred_element_type=F32) * scale
        p = jnp.exp(s - jnp.max(s, axis=-1, keepdims=True))
        den = jnp.sum(p, axis=-1, keepdims=True)
        o = jnp.dot(p.astype(BF16), v_ref[0, :, sl].astype(BF16), preferred_element_type=F32)
        o_ref[:, sl] = (o / den).astype(o_ref.dtype)


def _xattn_prompt(z, nb, t_len, col_blk, mk, mv, tq=1024):
    nt = t_len // tq
    xw = XA_HEADS * XA_DIM
    nmem = mk.shape[1]
    kv = pl.BlockSpec((1, nmem, xw), lambda b, t: (b, 0, 0))
    return pl.pallas_call(
        _xattn_prompt_kernel,
        grid=(nb, nt),
        in_specs=[pl.BlockSpec((tq, xw), lambda b, t: (b * nt + t, col_blk)), kv, kv],
        out_specs=pl.BlockSpec((tq, xw), lambda b, t: (b * nt + t, 0)),
        out_shape=jax.ShapeDtypeStruct((nb * t_len, xw), BF16),
        compiler_params=_cparams(("parallel", "parallel")),
        name="xattn_prompt",
    )(z, mk, mv)


def _xattn_sample_kernel(q_ref, k_ref, v_ref, o_ref):
    bb = q_ref.shape[0]
    nrow = k_ref.shape[1] // SUBLANES
    full = (bb, nrow, SUBLANES, XA_DIM)
    q = q_ref[...] * (XA_DIM ** -0.5)
    q8 = jnp.concatenate([q, q], axis=1)[:, None]
    k = k_ref[...].reshape(full)
    s = jnp.broadcast_to(jnp.sum(k * q8, axis=-1, keepdims=True), full)
    mx = jnp.max(s, axis=1, keepdims=True)
    mx = jnp.maximum(mx, pltpu.roll(mx, XA_HEADS, axis=2))
    p = jnp.exp(s - mx)
    den = jnp.sum(p, axis=1, keepdims=True)
    den = den + pltpu.roll(den, XA_HEADS, axis=2)
    o = jnp.sum(p * v_ref[...].reshape(full), axis=1, keepdims=True)
    o = o + pltpu.roll(o, XA_HEADS, axis=2)
    o_ref[...] = (o / den)[:, 0, 0:XA_HEADS, :].astype(o_ref.dtype)


def _xattn_sample(q3, mk, mv, bb=16):
    nb, rows, _ = mk.shape
    kv = pl.BlockSpec((bb, rows, XA_DIM), lambda i: (i, 0, 0))
    qs = pl.BlockSpec((bb, XA_HEADS, XA_DIM), lambda i: (i, 0, 0))
    return pl.pallas_call(
        _xattn_sample_kernel,
        grid=(nb // bb,),
        in_specs=[qs, kv, kv],
        out_specs=qs,
        out_shape=jax.ShapeDtypeStruct((nb, XA_HEADS, XA_DIM), BF16),
        compiler_params=_cparams(("parallel",)),
        name="xattn_sample",
    )(q3, mk, mv)


def _merge_kernel(pa_ref, pb_ref, pc_ref, g0_ref, g1_ref, g2_ref, h_ref, wa_ref, wb_ref, wc_ref, wo_ref, o_ref):
    oa = jnp.dot(pa_ref[...], wa_ref[...], preferred_element_type=F32)
    ob = jnp.dot(pb_ref[...], wb_ref[...], preferred_element_type=F32)
    oc = jnp.dot(pc_ref[...], wc_ref[...], preferred_element_type=F32)
    merged = (jax.nn.sigmoid(g0_ref[...]) * oa + jax.nn.sigmoid(g1_ref[...]) * ob
              + jax.nn.sigmoid(g2_ref[...]) * oc)
    o_ref[...] = h_ref[...] + jnp.dot(merged.astype(BF16), wo_ref[...], preferred_element_type=F32)


def _merge(pa, pb, pc, z, zg_blk0, h, tm, wa, wb, wc, wo):
    m, d = h.shape
    const = lambda i: (0, 0)
    resident = lambda w: pl.BlockSpec(w.shape, const, pipeline_mode=pl.Buffered(1))
    in_specs = [
        pl.BlockSpec((tm, pa.shape[1]), lambda i: (i, 0)),
        pl.BlockSpec((tm, pb.shape[1]), lambda i: (i, 0)),
        pl.BlockSpec((tm, pc.shape[1]), lambda i: (i, 0)),
        pl.BlockSpec((tm, d), lambda i: (i, zg_blk0)),
        pl.BlockSpec((tm, d), lambda i: (i, zg_blk0 + 1)),
        pl.BlockSpec((tm, d), lambda i: (i, zg_blk0 + 2)),
        pl.BlockSpec((tm, d), lambda i: (i, 0)),
        resident(wa), resident(wb), resident(wc), resident(wo),
    ]
    return pl.pallas_call(
        _merge_kernel,
        grid=(m // tm,),
        in_specs=in_specs,
        out_specs=pl.BlockSpec((tm, d), lambda i: (i, 0)),
        out_shape=jax.ShapeDtypeStruct((m, d), F32),
        compiler_params=_cparams(("parallel",)),
        name="merge",
    )(pa, pb, pc, z, z, z, h, wa, wb, wc, wo)


def _pack_zr(x, axis):
    w = RWKV_W
    take = lambda a, b: lax.slice_in_dim(x, a, b, axis=axis)

    def pad(n):
        shape = list(x.shape)
        shape[axis] = n
        return jnp.zeros(shape, x.dtype)

    return jnp.concatenate([take(0, 3 * w + 64), pad(64), take(3 * w + 64, 3 * w + 128), pad(64),
                            take(3 * w + 128, ZR_TRUE), pad(96)], axis=axis)


def _unpack_zr_cols(x):
    w = RWKV_W
    return jnp.concatenate([x[..., :3 * w + 64], x[..., 3 * w + 128:3 * w + 192], x[..., 3 * w + 256:3 * w + 416]],
                           axis=-1)


def _pad_rows(x, n):
    return jnp.concatenate([x, jnp.zeros((n - x.shape[0],) + x.shape[1:], x.dtype)], axis=0)


def kernel(x_prompt, x_sample, mem_prompt, cache_mem_k, cache_mem_v, state_wkv, state_shift, state_pool,
           ffn1_norm_g, ffn1_w_gate, ffn1_w_up, ffn1_w_down, mix_norm_g, w_in,
           pool_group_w, pool_scale, pool_out,
           rwkv_mu, rwkv_w0, rwkv_w_up, rwkv_a0, rwkv_a_up, rwkv_g_up, rwkv_k_k, rwkv_k_a, rwkv_r_k,
           rwkv_ln_g, rwkv_ln_b, rwkv_out,
           mem_norm_g, w_mem_k, w_mem_v, xattn_out, w_o,
           ffn2_norm_g, ffn2_w_gate, ffn2_w_up, ffn2_w_down, final_norm_g):
    nb, t_len, d = x_prompt.shape
    ns = x_sample.shape[0]
    assert w_in.shape[0] == 1 and x_sample.shape[1] == 1
    n_mem = mem_prompt.shape[1]
    pool_w = pool_out.shape[1]
    xa_w = xattn_out.shape[1]
    n_heads = RWKV_W // HEAD
    nbuf = state_pool.shape[2]
    rows_p = nb * t_len
    l = 0

    d_ff = ffn1_w_gate.shape[2]
    f1 = [w.reshape(w.shape[1:]).astype(BF16) for w in (ffn1_w_gate, ffn1_w_up, ffn1_w_down)]
    f2 = [w.reshape(w.shape[1:]).astype(BF16) for w in (ffn2_w_gate, ffn2_w_up, ffn2_w_down)]
    wit = jnp.swapaxes(w_in[l], 0, 1)
    o_zr, o_zq, o_zg = pool_w, pool_w + ZR_TRUE, pool_w + ZR_TRUE + xa_w
    w_in_t = jnp.concatenate([_pack_zr(wit[o_zr:o_zq].astype(BF16), 0), wit[:o_zr].astype(BF16),
                              wit[o_zg:].astype(BF16), wit[o_zq:o_zg].astype(BF16)], axis=0)
    col_zp = ZR_W // pool_w
    col_zg = (ZR_W + pool_w) // d
    col_zq = (ZR_W + pool_w + 3 * d) // xa_w
    row = lambda v: v.reshape(1, -1)
    prep_params = [row(_pack_zr(rwkv_mu[l], 0)), row(rwkv_w0[l]), _pad_rows(rwkv_w_up[l], 128).astype(BF16),
                   row(rwkv_a0[l]), _pad_rows(rwkv_a_up[l], 128).astype(BF16),
                   _pad_rows(rwkv_g_up[l], 256).astype(BF16), row(rwkv_k_k[l]), row(rwkv_k_a[l]),
                   row(rwkv_r_k[l])]
    gw = pool_group_w[l].astype(BF16)
    w_kv_t = jnp.concatenate([jnp.swapaxes(w_mem_k[l], 0, 1), jnp.swapaxes(w_mem_v[l], 0, 1)], axis=0).astype(BF16)
    wa, wb, wc, wo = (pool_out[l].astype(BF16), rwkv_out[l].astype(BF16), xattn_out[l].astype(BF16),
                      w_o[l].astype(BF16))
    g1, gm, g2, fg = row(ffn1_norm_g[l]), row(mix_norm_g[l]), row(ffn2_norm_g[l]), row(final_norm_g)
    scale = row(pool_scale[l])
    ln_g, ln_b = rwkv_ln_g[l], rwkv_ln_b[l]

    tm_p = 512
    h1_p, h1_s = _ffn(x_prompt.reshape(rows_p, d), x_sample.reshape(ns, d), tm_p, g1, *f1)
    z_p = _norm_matmul(h1_p, gm, w_in_t, tm=1024, tn=1536)
    z_s = _norm_matmul(h1_s, gm, w_in_t, tm=ns, tn=1536)
    kv = _norm_matmul(mem_prompt.reshape(nb * n_mem, d), row(mem_norm_g[l]), w_kv_t, tm=512, tn=512)
    mk_p = kv[:, :xa_w].reshape(nb, n_mem, xa_w)
    mv_p = kv[:, xa_w:].reshape(nb, n_mem, xa_w)

    pool_state = state_pool.reshape(ns, nbuf, pool_w)
    pa_p = _pool_prompt(z_p, nb, t_len, col_zp, gw, scale)
    pa_s = _pool_sample(z_s, col_zp, jnp.swapaxes(pool_state, 0, 1), gw, scale)

    pb_p, st_p = _rwkv_prompt(z_p, nb, t_len, prep_params, row(ln_g), row(ln_b))
    prep_s = _prep_sample(z_s, _pack_zr(state_shift.reshape(ns, ZR_TRUE), 1), prep_params)
    state_t = jnp.transpose(state_wkv.reshape(ns, n_heads, HEAD, HEAD), (1, 2, 3, 0))
    pb_s, wkv_s_t = _wkv_step(state_t, prep_s, row(ln_g), row(ln_b))

    pc_p = _xattn_prompt(z_p, nb, t_len, col_zq, mk_p, mv_p)
    q_s = z_s[:, ZR_W + pool_w + 3 * d:].reshape(ns, XA_HEADS, XA_DIM)
    pc_s = _xattn_sample(q_s, cache_mem_k.reshape(ns, n_mem * XA_HEADS, XA_DIM),
                         cache_mem_v.reshape(ns, n_mem * XA_HEADS, XA_DIM)).reshape(ns, xa_w)

    h2_p = _merge(pa_p, pb_p, pc_p, z_p, col_zg, h1_p, 256, wa, wb, wc, wo)
    h2_s = _merge(pa_s, pb_s, pc_s, z_s, col_zg, h1_s, ns, wa, wb, wc, wo)
    y_prompt, y_sample = _ffn(h2_p, h2_s, tm_p, g2, *f2, final_g=fg)

    ends = [(b + 1) * t_len for b in range(nb)]
    shift_p = _unpack_zr_cols(jnp.stack([z_p[e - 1:e, :ZR_W] for e in ends]))[None]
    pool_p = jnp.stack([z_p[e - nbuf:e, ZR_W:ZR_W + pool_w] for e in ends])[None]
    shift_s = _unpack_zr_cols(z_s[:, :ZR_W])[None, :, None, :]
    pool_s = jnp.concatenate([pool_state[:, 1:], z_s[:, None, ZR_W:ZR_W + pool_w]], axis=1)[None]
    wkv_p = st_p[None]
    wkv_s = jnp.transpose(wkv_s_t, (3, 0, 1, 2)).reshape(state_wkv.shape)
    mem_k_p = mk_p.reshape(1, nb, n_mem, XA_HEADS, XA_DIM)
    mem_v_p = mv_p.reshape(1, nb, n_mem, XA_HEADS, XA_DIM)
    return (y_prompt.reshape(nb, t_len, d), y_sample.reshape(ns, 1, d), mem_k_p, mem_v_p, wkv_p, shift_p, pool_p,
            wkv_s, shift_s, pool_s)
```

```python
import functools
import math

import jax
import jax.numpy as jnp
from jax import lax
from jax.experimental import pallas as pl
from jax.experimental.pallas import tpu as pltpu

F32 = jnp.float32
BF16 = jnp.bfloat16

RMS_EPS = 1e-6
GN_EPS = 64e-5
POOL_WINDOWS = (2, 4, 8, 16)
HEAD = 64
LANES = 128
SUBLANES = 8
XA_HEADS = 4
XA_DIM = 128
PAST_LEN = 16384
VMEM_LIMIT = 62 * 1024 * 1024
EXP_M05 = math.exp(-0.5)

RWKV_W = 1024
ZR_TRUE = 3360
RKV_W = 3 * RWKV_W
LORA_W = 512
ZR_W = RKV_W + LORA_W
GATES_W = 6144
COL_RKV = GATES_W // RKV_W
COL_LORA = (GATES_W + RKV_W) // LORA_W
COL_ZP = COL_LORA + 1
COL_ZQ = COL_LORA + 2

NN_DIMS = (((1,), (0,)), ((), ()))
NT_DIMS = (((1,), (1,)), ((), ()))
TN_DIMS = (((0,), (0,)), ((), ()))


def _cparams(sem):
    return pltpu.CompilerParams(dimension_semantics=sem, vmem_limit_bytes=VMEM_LIMIT)


def _rms(x, g):
    ms = jnp.mean(x * x, axis=-1, keepdims=True)
    return x * lax.rsqrt(ms + RMS_EPS) * g


def _bdot(a, b, dims=NN_DIMS):
    return lax.dot_general(a.astype(BF16), b.astype(BF16), dims, preferred_element_type=F32)


def _ffn_kernel(hp_ref, hs_ref, g_ref, fg_ref, wg_ref, wu_ref, wd_ref, wgt_ref, wut_ref, wdt_ref,
                op_ref, os_ref, xp_ref, xs_ref, ap_ref, as_ref, *, nfull, final):
    m = pl.program_id(0)
    f = pl.program_id(1)

    def start(h_ref, x_ref, acc_ref):
        x_ref[...] = _rms(h_ref[...], g_ref[...]).astype(BF16)
        acc_ref[...] = jnp.zeros_like(acc_ref)

    def contribution(x_ref, wg, wu, wd):
        xn = x_ref[...]
        gate = jnp.dot(xn, wg, preferred_element_type=F32)
        up = jnp.dot(xn, wu, preferred_element_type=F32)
        act = (gate * jax.nn.sigmoid(gate) * up).astype(BF16)
        return jnp.dot(act, wd, preferred_element_type=F32)

    def finish(h_ref, acc_ref, o_ref, last):
        out = h_ref[...] + 0.5 * (acc_ref[...] + last)
        if final:
            out = _rms(out, fg_ref[...])
        o_ref[...] = out

    @pl.when(f == 0)
    def _():
        start(hp_ref, xp_ref, ap_ref)

    @pl.when((f == 0) & (m == 0))
    def _():
        start(hs_ref, xs_ref, as_ref)

    @pl.when(f < nfull)
    def _():
        wg, wu, wd = wg_ref[...], wu_ref[...], wd_ref[...]
        ap_ref[...] += contribution(xp_ref, wg, wu, wd)

        @pl.when(m == 0)
        def _():
            as_ref[...] += contribution(xs_ref, wg, wu, wd)

    @pl.when(f == nfull)
    def _():
        wg, wu, wd = wgt_ref[...], wut_ref[...], wdt_ref[...]
        finish(hp_ref, ap_ref, op_ref, contribution(xp_ref, wg, wu, wd))

        @pl.when(m == 0)
        def _():
            finish(hs_ref, as_ref, os_ref, contribution(xs_ref, wg, wu, wd))


def _ffn(hp, hs, tm, g, wg, wu, wd, final_g=None, tf=512):
    mp, d = hp.shape
    ms = hs.shape[0]
    d_ff = wg.shape[1]
    nfull, tail = divmod(d_ff, tf)
    assert tail > 0 and tail % LANES == 0 and mp % tm == 0
    last_main = nfull - 1
    split = nfull * tf
    final = final_g is not None
    vec = pl.BlockSpec((1, d), lambda i, f: (0, 0))
    once = pl.Buffered(1)
    in_specs = [
        pl.BlockSpec((tm, d), lambda i, f: (i, 0)),
        pl.BlockSpec((ms, d), lambda i, f: (0, 0), pipeline_mode=once),
        vec, vec,
        pl.BlockSpec((d, tf), lambda i, f: (0, jnp.minimum(f, last_main))),
        pl.BlockSpec((d, tf), lambda i, f: (0, jnp.minimum(f, last_main))),
        pl.BlockSpec((tf, d), lambda i, f: (jnp.minimum(f, last_main), 0)),
        pl.BlockSpec((pl.Element(d), pl.Element(tail)), lambda i, f: (0, split), pipeline_mode=once),
        pl.BlockSpec((pl.Element(d), pl.Element(tail)), lambda i, f: (0, split), pipeline_mode=once),
        pl.BlockSpec((pl.Element(tail), pl.Element(d)), lambda i, f: (split, 0), pipeline_mode=once),
    ]
    return pl.pallas_call(
        functools.partial(_ffn_kernel, nfull=nfull, final=final),
        grid=(mp // tm, nfull + 1),
        in_specs=in_specs,
        out_specs=[pl.BlockSpec((tm, d), lambda i, f: (i, 0)), pl.BlockSpec((ms, d), lambda i, f: (0, 0))],
        out_shape=[jax.ShapeDtypeStruct((mp, d), F32), jax.ShapeDtypeStruct((ms, d), F32)],
        scratch_shapes=[pltpu.VMEM((tm, d), BF16), pltpu.VMEM((ms, d), BF16),
                        pltpu.VMEM((tm, d), F32), pltpu.VMEM((ms, d), F32)],
        compiler_params=_cparams(("parallel", "arbitrary")),
        name="ffn",
    )(hp, hs, g, g if final_g is None else final_g, wg, wu, wd, wg, wu, wd)


def _norm_matmul_kernel(h_ref, g_ref, wt_ref, o_ref, xn_ref):
    @pl.when(pl.program_id(1) == 0)
    def _():
        xn_ref[...] = _rms(h_ref[...], g_ref[...]).astype(BF16)

    o_ref[...] = lax.dot_general(xn_ref[...], wt_ref[...], NT_DIMS, preferred_element_type=F32)


def _norm_matmul(h, g, wt, tm, tn):
    m, d = h.shape
    n = wt.shape[0]
    return pl.pallas_call(
        _norm_matmul_kernel,
        grid=(m // tm, n // tn),
        in_specs=[
            pl.BlockSpec((tm, d), lambda i, j: (i, 0)),
            pl.BlockSpec((1, d), lambda i, j: (0, 0)),
            pl.BlockSpec((tn, d), lambda i, j: (j, 0)),
        ],
        out_specs=pl.BlockSpec((tm, tn), lambda i, j: (i, j)),
        out_shape=jax.ShapeDtypeStruct((m, n), F32),
        scratch_shapes=[pltpu.VMEM((tm, d), BF16)],
        compiler_params=_cparams(("parallel", "arbitrary")),
        name="norm_matmul",
    )(h, g, wt)


def _in_proj_kernel(hp_ref, hs_ref, g_ref, wt_ref, wm_ref, op_ref, os_ref, xp_ref, xs_ref, *, n_direct):
    i = pl.program_id(0)
    j = pl.program_id(1)

    @pl.when(j == 0)
    def _():
        xp_ref[...] = _rms(hp_ref[...], g_ref[...]).astype(BF16)

    @pl.when((j == 0) & (i == 0))
    def _():
        xs_ref[...] = _rms(hs_ref[...], g_ref[...]).astype(BF16)

    def project(w):
        op_ref[...] = lax.dot_general(xp_ref[...], w, NT_DIMS, preferred_element_type=F32)

        @pl.when(i == 0)
        def _():
            os_ref[...] = lax.dot_general(xs_ref[...], w, NT_DIMS, preferred_element_type=F32)

    @pl.when(j < n_direct)
    def _():
        project(wt_ref[...])

    @pl.when(j >= n_direct)
    def _():
        project(wm_ref[...])


def _in_proj(hp, hs, g, wt, segments, w_misc, tm, tn):
    mp, d = hp.shape
    ms = hs.shape[0]
    tiles = [length // tn for _, length in segments]
    assert all(length % tn == 0 and start % SUBLANES == 0 for start, length in segments)
    n_direct = sum(tiles)
    n_misc = w_misc.shape[0] // tn
    nj = n_direct + n_misc

    def wt_row(j):
        jj = jnp.minimum(j, n_direct - 1)
        first = n_direct - tiles[-1]
        row = segments[-1][0] + (jj - first) * tn
        for (start, _), nt in zip(reversed(segments[:-1]), reversed(tiles[:-1])):
            first -= nt
            row = jnp.where(jj < first + nt, start + (jj - first) * tn, row)
        return pl.multiple_of(row, SUBLANES)

    return pl.pallas_call(
        functools.partial(_in_proj_kernel, n_direct=n_direct),
        grid=(mp // tm, nj),
        in_specs=[
            pl.BlockSpec((tm, d), lambda i, j: (i, 0)),
            pl.BlockSpec((ms, d), lambda i, j: (0, 0), pipeline_mode=pl.Buffered(1)),
            pl.BlockSpec((1, d), lambda i, j: (0, 0)),
            pl.BlockSpec((pl.Element(tn), pl.Element(d)), lambda i, j: (wt_row(j), 0)),
            pl.BlockSpec((tn, d), lambda i, j: (jnp.clip(j - n_direct, 0, n_misc - 1), 0),
                         pipeline_mode=pl.Buffered(1 if n_misc == 1 else 2)),
        ],
        out_specs=[pl.BlockSpec((tm, tn), lambda i, j: (i, j)),
                   pl.BlockSpec((ms, tn), lambda i, j: (0, jnp.where(i == 0, j, nj - 1)))],
        out_shape=[jax.ShapeDtypeStruct((mp, nj * tn), F32), jax.ShapeDtypeStruct((ms, nj * tn), F32)],
        scratch_shapes=[pltpu.VMEM((tm, d), BF16), pltpu.VMEM((ms, d), BF16)],
        compiler_params=_cparams(("arbitrary", "arbitrary")),
        name="in_proj",
    )(hp, hs, g, wt, w_misc)


def _pool_mix(pooled_groups, gw_ref, scale_ref, o_ref):
    for gi, pooled in enumerate(pooled_groups):
        sl = slice(gi * LANES, (gi + 1) * LANES)
        mixed = jnp.dot(pooled.astype(BF16), gw_ref[gi], preferred_element_type=F32)
        o_ref[:, sl] = (mixed * scale_ref[:, sl]).astype(o_ref.dtype)


def _pool_prompt_kernel(zp_ref, gw_ref, scale_ref, o_ref, ext_ref, *, tt):
    t = pl.program_id(1)
    hist = 16

    @pl.when(t == 0)
    def _():
        ext_ref[0:hist, :] = jnp.zeros((hist, ext_ref.shape[1]), F32)

    x = zp_ref[...]
    ext_ref[hist:hist + tt, :] = x
    pos = t * tt + lax.broadcasted_iota(jnp.int32, (tt, LANES), 0)
    groups = []
    for gi, w in enumerate(POOL_WINDOWS):
        sl = slice(gi * LANES, (gi + 1) * LANES)
        acc = x[:, sl]
        for k in range(1, w):
            acc = acc + ext_ref[hist - k:hist - k + tt, sl]
        cnt = jnp.minimum(pos + 1, w).astype(F32)
        groups.append(acc / cnt - x[:, sl])
    _pool_mix(groups, gw_ref, scale_ref, o_ref)
    ext_ref[0:hist, :] = ext_ref[tt:tt + hist, :]


def _pool_prompt(z, nb, t_len, col_blk, gw, scale, tt=1024):
    nt = t_len // tt
    pw = gw.shape[0] * LANES
    return pl.pallas_call(
        functools.partial(_pool_prompt_kernel, tt=tt),
        grid=(nb, nt),
        in_specs=[
            pl.BlockSpec((tt, pw), lambda b, t: (b * nt + t, col_blk)),
            pl.BlockSpec(gw.shape, lambda b, t: (0, 0, 0)),
            pl.BlockSpec((1, pw), lambda b, t: (0, 0)),
        ],
        out_specs=pl.BlockSpec((tt, pw), lambda b, t: (b * nt + t, 0)),
        out_shape=jax.ShapeDtypeStruct((nb * t_len, pw), BF16),
        scratch_shapes=[pltpu.VMEM((tt + 16, pw), F32)],
        compiler_params=_cparams(("parallel", "arbitrary")),
        name="pool_prompt",
    )(z, gw, scale)


def _pool_sample_kernel(zp_ref, buf_ref, gw_ref, scale_ref, o_ref):
    x = zp_ref[...]
    nbuf = buf_ref.shape[0]
    groups = []
    for gi, w in enumerate(POOL_WINDOWS):
        sl = slice(gi * LANES, (gi + 1) * LANES)
        acc = x[:, sl]
        for k in range(1, w):
            acc = acc + buf_ref[nbuf - k, :, sl]
        cnt = float(min(PAST_LEN + 1, w))
        groups.append(acc / cnt - x[:, sl])
    _pool_mix(groups, gw_ref, scale_ref, o_ref)


def _pool_sample(z, col_blk, buf_t, gw, scale):
    nrows = z.shape[0]
    pw = gw.shape[0] * LANES
    return pl.pallas_call(
        _pool_sample_kernel,
        grid=(1,),
        in_specs=[
            pl.BlockSpec((nrows, pw), lambda i: (0, col_blk)),
            pl.BlockSpec(buf_t.shape, lambda i: (0, 0, 0)),
            pl.BlockSpec(gw.shape, lambda i: (0, 0, 0)),
            pl.BlockSpec((1, pw), lambda i: (0, 0)),
        ],
        out_specs=pl.BlockSpec((nrows, pw), lambda i: (0, 0)),
        out_shape=jax.ShapeDtypeStruct((nrows, pw), BF16),
        compiler_params=_cparams(("arbitrary",)),
        name="pool_sample",
    )(z, buf_t, gw, scale)


def _head_sum(x):
    head_a = lax.broadcasted_iota(jnp.int32, (x.shape[0], LANES), 1) < HEAD
    cols = []
    for c in range(x.shape[1] // LANES):
        t = x[:, c * LANES:(c + 1) * LANES]
        sa = jnp.sum(jnp.where(head_a, t, 0.0), axis=-1, keepdims=True)
        sb = jnp.sum(jnp.where(head_a, 0.0, t), axis=-1, keepdims=True)
        cols.append(jnp.where(head_a, sa, sb))
    return jnp.concatenate(cols, axis=1)


def _prep_math(x, prev, p_refs):
    mu_ref, w0_ref, wup_ref, a0_ref, aup_ref, gup_ref, kk_ref, ka_ref, rk_ref = p_refs
    xm = x + (prev - x) * mu_ref[...]
    w = RWKV_W
    r = xm[:, 0:w]
    k = xm[:, w:2 * w]
    v = xm[:, 2 * w:3 * w]
    wl = xm[:, 3 * w:3 * w + 128]
    al = xm[:, 3 * w + 128:3 * w + 256]
    gl = xm[:, 3 * w + 256:3 * w + 512]
    dw = w0_ref[...] + jnp.dot(jnp.tanh(wl).astype(BF16), wup_ref[...], preferred_element_type=F32)
    lw = -EXP_M05 * jax.nn.sigmoid(dw)
    a = jax.nn.sigmoid(a0_ref[...] + jnp.dot(al.astype(BF16), aup_ref[...], preferred_element_type=F32))
    g = jnp.dot(jax.nn.sigmoid(gl).astype(BF16), gup_ref[...], preferred_element_type=F32)
    kk = k * kk_ref[...]
    kk = kk * lax.rsqrt(jnp.maximum(_head_sum(kk * kk), 1e-24))
    kmod = k * (1.0 + (a - 1.0) * ka_ref[...])
    bonus = _head_sum(r * kmod * rk_ref[...]) * v
    return r, lw, kmod, v, -kk, kk * a, g, bonus


def _group_norm_gate(y, bonus, g, lng_ref, lnb_ref):
    d = y - _head_sum(y) * (1.0 / HEAD)
    var = _head_sum(d * d) * (1.0 / HEAD)
    yn = d * lax.rsqrt(var + GN_EPS) * lng_ref[...] + lnb_ref[...]
    return (yn + bonus) * g


def _prep_sample_kernel(rkv_ref, lora_ref, prev_ref, *refs):
    p_refs, out_refs = refs[:9], refs[9:17]
    x = jnp.concatenate([rkv_ref[...], lora_ref[...]], axis=1)
    for o_ref, val in zip(out_refs, _prep_math(x, prev_ref[...], p_refs)):
        o_ref[...] = val


def _prep_param_specs(params):
    zero = (lambda *idx: (0, 0))
    return [pl.BlockSpec(p.shape, zero) for p in params]


def _prep_sample(z, prev, params):
    nrows = z.shape[0]
    out_spec = pl.BlockSpec((nrows, RWKV_W), lambda i: (0, 0))
    return pl.pallas_call(
        _prep_sample_kernel,
        grid=(1,),
        in_specs=[pl.BlockSpec((nrows, RKV_W), lambda i: (0, COL_RKV)),
                  pl.BlockSpec((nrows, LORA_W), lambda i: (0, COL_LORA)),
                  pl.BlockSpec((nrows, ZR_W), lambda i: (0, 0))] + _prep_param_specs(params),
        out_specs=[out_spec] * 8,
        out_shape=[jax.ShapeDtypeStruct((nrows, RWKV_W), F32)] * 8,
        compiler_params=_cparams(("arbitrary",)),
        name="prep_sample",
    )(z, z, prev, *params)


def _split3(x):
    hi = x.astype(BF16)
    rest = x - hi.astype(F32)
    mid = rest.astype(BF16)
    lo = (rest - mid.astype(F32)).astype(BF16)
    return hi, mid, lo


def _select_dot(sel, x):
    sel = sel.astype(BF16)
    hi, mid, lo = _split3(x)
    return _bdot(sel, hi) + (_bdot(sel, mid) + _bdot(sel, lo))


def _chunk_scan(r, lw, k, v, a, b, s_ref, c_len):
    n2 = 2 * c_len
    assert n2 == LANES
    row = lax.broadcasted_iota(jnp.int32, (n2, n2), 0)
    col = lax.broadcasted_iota(jnp.int32, (n2, n2), 1)
    tr = row & (c_len - 1)
    tc = col & (c_len - 1)
    strict = tr > tc
    incl = tr >= tc
    tri = (lax.broadcasted_iota(jnp.int32, (c_len, c_len), 0)
           >= lax.broadcasted_iota(jnp.int32, (c_len, c_len), 1))
    head_a = lax.broadcasted_iota(jnp.int32, (c_len, LANES), 1) < HEAD

    def stack(x):
        return jnp.concatenate([jnp.where(head_a, x, 0.0), jnp.where(head_a, 0.0, x)], axis=0).astype(BF16)

    prs = range(r.shape[1] // LANES)
    cat = jnp.concatenate
    sls = [slice(q * LANES, (q + 1) * LANES) for q in prs]
    lwq = [lw[:, sl] for sl in sls]
    cum = [_select_dot(tri, x) for x in lwq]
    tot = [x[c_len - 1:c_len, :] for x in cum]
    xr = [stack(r[:, sls[q]] * jnp.exp(cum[q])) for q in prs]
    xa = [stack(a[:, sls[q]] * jnp.exp(cum[q] - lwq[q])) for q in prs]
    e_neg = [jnp.exp(-x) for x in cum]
    e_rem = [jnp.exp(tot[q] - cum[q]) for q in prs]
    yb = [stack(b[:, sls[q]] * e_neg[q]) for q in prs]
    yk = [stack(k[:, sls[q]] * e_neg[q]) for q in prs]
    zb = [stack(b[:, sls[q]] * e_rem[q]) for q in prs]
    zk = [stack(k[:, sls[q]] * e_rem[q]) for q in prs]
    vs = [stack(v[:, sl]) for sl in sls]

    g = [_bdot(cat([xa[q], xr[q]], axis=0), cat([yb[q], yk[q]], axis=0), NT_DIMS) for q in prs]
    m_ab = [jnp.where(strict, x[0:n2, 0:n2], 0.0) for x in g]
    m_ak = [jnp.where(strict, x[0:n2, n2:2 * n2], 0.0).astype(BF16) for x in g]
    n_rb = [jnp.where(incl, x[n2:2 * n2, 0:n2], 0.0) for x in g]
    n_rk = [jnp.where(incl, x[n2:2 * n2, n2:2 * n2], 0.0).astype(BF16) for x in g]

    s_old = [s_ref[q] for q in prs]
    lhs = [cat([cat([xa[q], m_ak[q]], axis=1), cat([xr[q], n_rk[q]], axis=1)], axis=0) for q in prs]
    xy0 = [_bdot(lhs[q], cat([s_old[q].T.astype(BF16), vs[q]], axis=0)) for q in prs]

    x = [xy0[q][0:n2] for q in prs]
    mk = m_ab
    nlev = int(math.log2(c_len))
    for lev in range(nlev):
        if lev < nlev - 1:
            res = [_bdot(mk[q], cat([mk[q], x[q]], axis=1)) for q in prs]
            mk = [t[:, 0:n2] for t in res]
            x = [x[q] + res[q][:, n2:2 * n2] for q in prs]
        else:
            x = [x[q] + _bdot(mk[q], x[q]) for q in prs]
    u = [t.astype(BF16) for t in x]

    y_st = [xy0[q][n2:2 * n2] + _bdot(n_rb[q], u[q]) for q in prs]
    for q in prs:
        s_ref[q] = s_old[q] * jnp.exp(tot[q]) + _bdot(cat([u[q], vs[q]], axis=0), cat([zb[q], zk[q]], axis=0),
                                                     TN_DIMS)
    return cat([t[0:c_len] + t[c_len:n2] for t in y_st], axis=1)


def _rwkv_prompt_kernel(rkv_ref, lora_ref, *refs, nc):
    p_refs = refs[:9]
    lng_ref, lnb_ref, yb_ref, sout_ref, s_ref, carry_ref = refs[9:]
    c = pl.program_id(1)
    nseq, c_len, _ = rkv_ref.shape

    @pl.when(c == 0)
    def _():
        s_ref[...] = jnp.zeros_like(s_ref)
        carry_ref[...] = jnp.zeros_like(carry_ref)

    first = lax.broadcasted_iota(jnp.int32, (c_len, ZR_W), 0) == 0
    preps = []
    for q in range(nseq):
        x = jnp.concatenate([rkv_ref[q], lora_ref[q]], axis=1)
        rolled = pltpu.roll(x, 1, axis=0)
        prev = jnp.where(first, carry_ref[q, 0:1, :], rolled)
        carry_ref[q] = rolled[0:SUBLANES, :]
        preps.append(_prep_math(x, prev, p_refs))
    r, lw, k, v, a, b = (jnp.concatenate([p[i] for p in preps], axis=1) for i in range(6))
    y = _chunk_scan(r, lw, k, v, a, b, s_ref, c_len)
    for q in range(nseq):
        yq = y[:, q * RWKV_W:(q + 1) * RWKV_W]
        yb_ref[q] = _group_norm_gate(yq, preps[q][7], preps[q][6], lng_ref, lnb_ref).astype(yb_ref.dtype)

    @pl.when(c == nc - 1)
    def _():
        npair = s_ref.shape[0] // nseq
        for q in range(nseq):
            for p in range(npair):
                s = s_ref[q * npair + p]
                sout_ref[q, 2 * p] = s[0:HEAD, 0:HEAD]
                sout_ref[q, 2 * p + 1] = s[HEAD:2 * HEAD, HEAD:2 * HEAD]


def _rwkv_prompt(z, nb, t_len, params, ln_g, ln_b, c_len=64, nseq=2):
    nc = t_len // c_len
    npair = RWKV_W // LANES
    par = pl.BlockSpec((1, RWKV_W), lambda b, c: (0, 0))
    z3 = z.reshape(nb, t_len, z.shape[1])
    yb, state = pl.pallas_call(
        functools.partial(_rwkv_prompt_kernel, nc=nc),
        grid=(nb // nseq, nc),
        in_specs=[pl.BlockSpec((nseq, c_len, RKV_W), lambda b, c: (b, c, COL_RKV)),
                  pl.BlockSpec((nseq, c_len, LORA_W), lambda b, c: (b, c, COL_LORA))]
        + _prep_param_specs(params) + [par, par],
        out_specs=[pl.BlockSpec((nseq, c_len, RWKV_W), lambda b, c: (b, c, 0)),
                   pl.BlockSpec((nseq, 2 * npair, HEAD, HEAD), lambda b, c: (b, 0, 0, 0))],
        out_shape=[jax.ShapeDtypeStruct((nb, t_len, RWKV_W), BF16),
                   jax.ShapeDtypeStruct((nb, 2 * npair, HEAD, HEAD), F32)],
        scratch_shapes=[pltpu.VMEM((nseq * npair, LANES, LANES), F32), pltpu.VMEM((nseq, SUBLANES, ZR_W), F32)],
        compiler_params=_cparams(("parallel", "arbitrary")),
        name="rwkv_prompt",
    )(z3, z3, *params, ln_g, ln_b)
    return yb.reshape(nb * t_len, RWKV_W), state


def _wkv_step_kernel(s_ref, r_ref, lw_ref, k_ref, v_ref, a_ref, b_ref, g_ref, bonus_ref, lng_ref, lnb_ref,
                     yb_ref, so_ref, vt_scr, y_scr):
    rt, wt, kt, at, bt = (x[...].T for x in (r_ref, lw_ref, k_ref, a_ref, b_ref))
    wt = jnp.exp(wt)
    vt_scr[...] = v_ref[...].T
    for hh in range(2):
        rows = slice(hh * HEAD, (hh + 1) * HEAD)
        r, w, k, a, b = (x[rows, :] for x in (rt, wt, kt, at, bt))

        def body(i, carry):
            si = s_ref[hh, i]
            sa = jnp.sum(si * a, axis=0, keepdims=True)
            vi = vt_scr[pl.ds(hh * HEAD + i, 1), :]
            s2 = si * w + sa * b + vi * k
            so_ref[hh, i] = s2
            y_scr[pl.ds(hh * HEAD + i, 1), :] = jnp.sum(s2 * r, axis=0, keepdims=True)
            return carry

        lax.fori_loop(0, HEAD, body, 0, unroll=4)

    outs = []
    for hh in range(2):
        y = y_scr[hh * HEAD:(hh + 1) * HEAD, :]
        d = y - jnp.mean(y, axis=0, keepdims=True)
        var = jnp.mean(d * d, axis=0, keepdims=True)
        outs.append(d * lax.rsqrt(var + GN_EPS))
    yn = jnp.concatenate(outs, axis=0).T * lng_ref[...] + lnb_ref[...]
    yb_ref[...] = ((yn + bonus_ref[...]) * g_ref[...]).astype(yb_ref.dtype)


def _wkv_step(state_t, vecs, ln_g, ln_b):
    nh, _, _, nb = state_t.shape
    st = pl.BlockSpec((2, HEAD, HEAD, nb), lambda p: (p, 0, 0, 0))
    vec = pl.BlockSpec((nb, 2 * HEAD), lambda p: (0, p))
    par = pl.BlockSpec((1, 2 * HEAD), lambda p: (0, p))
    return pl.pallas_call(
        _wkv_step_kernel,
        grid=(nh // 2,),
        in_specs=[st] + [vec] * 8 + [par, par],
        out_specs=[vec, st],
        out_shape=[jax.ShapeDtypeStruct((nb, nh * HEAD), BF16), jax.ShapeDtypeStruct(state_t.shape, F32)],
        scratch_shapes=[pltpu.VMEM((2 * HEAD, nb), F32), pltpu.VMEM((2 * HEAD, nb), F32)],
        compiler_params=_cparams(("parallel",)),
        name="wkv_step",
    )(state_t, *vecs, ln_g, ln_b)


def _xattn_prompt_kernel(q_ref, k_ref, v_ref, o_ref):
    scale = XA_DIM ** -0.5
    q = q_ref[...]
    for h in range(XA_HEADS):
        sl = slice(h * XA_DIM, (h + 1) * XA_DIM)
        s = lax.dot_general(q[:, sl].astype(BF16), k_ref[0, :, sl].astype(BF16),
                            NT_DIMS, preferred_element_type=F32) * scale
        p = jnp.exp(s - jnp.max(s, axis=-1, keepdims=True))
        den = jnp.sum(p, axis=-1, keepdims=True)
        o = jnp.dot(p.astype(BF16), v_ref[0, :, sl].astype(BF16), preferred_element_type=F32)
        o_ref[:, sl] = (o / den).astype(o_ref.dtype)


def _xattn_prompt(z, nb, t_len, col_blk, mk, mv, tq=1024):
    nt = t_len // tq
    xw = XA_HEADS * XA_DIM
    nmem = mk.shape[1]
    kv = pl.BlockSpec((1, nmem, xw), lambda b, t: (b, 0, 0))
    return pl.pallas_call(
        _xattn_prompt_kernel,
        grid=(nb, nt),
        in_specs=[pl.BlockSpec((tq, xw), lambda b, t: (b * nt + t, col_blk)), kv, kv],
        out_specs=pl.BlockSpec((tq, xw), lambda b, t: (b * nt + t, 0)),
        out_shape=jax.ShapeDtypeStruct((nb * t_len, xw), BF16),
        compiler_params=_cparams(("parallel", "parallel")),
        name="xattn_prompt",
    )(z, mk, mv)


def _xattn_sample_kernel(q_ref, k_ref, v_ref, o_ref):
    bb = q_ref.shape[0]
    nrow = k_ref.shape[1] // SUBLANES
    full = (bb, nrow, SUBLANES, XA_DIM)
    q = q_ref[...] * (XA_DIM ** -0.5)
    q8 = jnp.concatenate([q, q], axis=1)[:, None]
    k = k_ref[...].reshape(full)
    s = jnp.broadcast_to(jnp.sum(k * q8, axis=-1, keepdims=True), full)
    mx = jnp.max(s, axis=1, keepdims=True)
    mx = jnp.maximum(mx, pltpu.roll(mx, XA_HEADS, axis=2))
    p = jnp.exp(s - mx)
    den = jnp.sum(p, axis=1, keepdims=True)
    den = den + pltpu.roll(den, XA_HEADS, axis=2)
    o = jnp.sum(p * v_ref[...].reshape(full), axis=1, keepdims=True)
    o = o + pltpu.roll(o, XA_HEADS, axis=2)
    o_ref[...] = (o / den)[:, 0, 0:XA_HEADS, :].astype(o_ref.dtype)


def _xattn_sample(q3, mk, mv, bb=16):
    nb, rows, _ = mk.shape
    kv = pl.BlockSpec((bb, rows, XA_DIM), lambda i: (i, 0, 0))
    qs = pl.BlockSpec((bb, XA_HEADS, XA_DIM), lambda i: (i, 0, 0))
    return pl.pallas_call(
        _xattn_sample_kernel,
        grid=(nb // bb,),
        in_specs=[qs, kv, kv],
        out_specs=qs,
        out_shape=jax.ShapeDtypeStruct((nb, XA_HEADS, XA_DIM), BF16),
        compiler_params=_cparams(("parallel",)),
        name="xattn_sample",
    )(q3, mk, mv)


def _merge_kernel(pa_ref, pb_ref, pc_ref, g0_ref, g1_ref, g2_ref, h_ref, wa_ref, wb_ref, wc_ref, wo_ref, o_ref):
    oa = jnp.dot(pa_ref[...], wa_ref[...], preferred_element_type=F32)
    ob = jnp.dot(pb_ref[...], wb_ref[...], preferred_element_type=F32)
    oc = jnp.dot(pc_ref[...], wc_ref[...], preferred_element_type=F32)
    merged = (jax.nn.sigmoid(g0_ref[...]) * oa + jax.nn.sigmoid(g1_ref[...]) * ob
              + jax.nn.sigmoid(g2_ref[...]) * oc)
    o_ref[...] = h_ref[...] + jnp.dot(merged.astype(BF16), wo_ref[...], preferred_element_type=F32)


def _merge(pa, pb, pc, z, zg_blk0, h, tm, wa, wb, wc, wo):
    m, d = h.shape
    const = lambda i: (0, 0)
    resident = lambda w: pl.BlockSpec(w.shape, const, pipeline_mode=pl.Buffered(1))
    in_specs = [
        pl.BlockSpec((tm, pa.shape[1]), lambda i: (i, 0)),
        pl.BlockSpec((tm, pb.shape[1]), lambda i: (i, 0)),
        pl.BlockSpec((tm, pc.shape[1]), lambda i: (i, 0)),
        pl.BlockSpec((tm, d), lambda i: (i, zg_blk0)),
        pl.BlockSpec((tm, d), lambda i: (i, zg_blk0 + 1)),
        pl.BlockSpec((tm, d), lambda i: (i, zg_blk0 + 2)),
        pl.BlockSpec((tm, d), lambda i: (i, 0)),
        resident(wa), resident(wb), resident(wc), resident(wo),
    ]
    return pl.pallas_call(
        _merge_kernel,
        grid=(m // tm,),
        in_specs=in_specs,
        out_specs=pl.BlockSpec((tm, d), lambda i: (i, 0)),
        out_shape=jax.ShapeDtypeStruct((m, d), F32),
        compiler_params=_cparams(("parallel",)),
        name="merge",
    )(pa, pb, pc, z, z, z, h, wa, wb, wc, wo)


def _pack_lora(x, axis):
    take = lambda a, b: lax.slice_in_dim(x, a, b, axis=axis)

    def pad(n):
        shape = list(x.shape)
        shape[axis] = n
        return jnp.zeros(shape, x.dtype)

    return jnp.concatenate([take(0, 64), pad(64), take(64, 128), pad(64), take(128, ZR_TRUE - RKV_W), pad(96)],
                           axis=axis)


def _pack_zr(x, axis):
    return jnp.concatenate([lax.slice_in_dim(x, 0, RKV_W, axis=axis),
                            _pack_lora(lax.slice_in_dim(x, RKV_W, ZR_TRUE, axis=axis), axis)], axis=axis)


def _unpack_zr_cols(x):
    w = RWKV_W
    return jnp.concatenate([x[..., :3 * w + 64], x[..., 3 * w + 128:3 * w + 192], x[..., 3 * w + 256:3 * w + 416]],
                           axis=-1)


def _pad_rows(x, n):
    return jnp.concatenate([x, jnp.zeros((n - x.shape[0],) + x.shape[1:], x.dtype)], axis=0)


def kernel(x_prompt, x_sample, mem_prompt, cache_mem_k, cache_mem_v, state_wkv, state_shift, state_pool,
           ffn1_norm_g, ffn1_w_gate, ffn1_w_up, ffn1_w_down, mix_norm_g, w_in,
           pool_group_w, pool_scale, pool_out,
           rwkv_mu, rwkv_w0, rwkv_w_up, rwkv_a0, rwkv_a_up, rwkv_g_up, rwkv_k_k, rwkv_k_a, rwkv_r_k,
           rwkv_ln_g, rwkv_ln_b, rwkv_out,
           mem_norm_g, w_mem_k, w_mem_v, xattn_out, w_o,
           ffn2_norm_g, ffn2_w_gate, ffn2_w_up, ffn2_w_down, final_norm_g):
    nb, t_len, d = x_prompt.shape
    ns = x_sample.shape[0]
    assert w_in.shape[0] == 1 and x_sample.shape[1] == 1
    n_mem = mem_prompt.shape[1]
    pool_w = pool_out.shape[1]
    xa_w = xattn_out.shape[1]
    n_heads = RWKV_W // HEAD
    nbuf = state_pool.shape[2]
    rows_p = nb * t_len
    l = 0

    d_ff = ffn1_w_gate.shape[2]
    f1 = [w.reshape(w.shape[1:]).astype(BF16) for w in (ffn1_w_gate, ffn1_w_up, ffn1_w_down)]
    f2 = [w.reshape(w.shape[1:]).astype(BF16) for w in (ffn2_w_gate, ffn2_w_up, ffn2_w_down)]
    wit = jnp.swapaxes(w_in[l], 0, 1).astype(BF16)
    o_zr, o_zq, o_zg = pool_w, pool_w + ZR_TRUE, pool_w + ZR_TRUE + xa_w
    assert (pool_w, xa_w, 3 * d) == (LORA_W, LORA_W, GATES_W)
    w_misc = jnp.concatenate([_pack_lora(wit[o_zr + RKV_W:o_zq], 0), wit[:o_zr], wit[o_zq:o_zg]], axis=0)
    in_segments = [(o_zg, GATES_W), (o_zr, RKV_W)]
    o_rkv, o_lora, o_zp_out, o_zq_out = GATES_W, GATES_W + RKV_W, GATES_W + ZR_W, GATES_W + ZR_W + pool_w
    row = lambda v: v.reshape(1, -1)
    prep_params = [row(_pack_zr(rwkv_mu[l], 0)), row(rwkv_w0[l]), _pad_rows(rwkv_w_up[l], 128).astype(BF16),
                   row(rwkv_a0[l]), _pad_rows(rwkv_a_up[l], 128).astype(BF16),
                   _pad_rows(rwkv_g_up[l], 256).astype(BF16), row(rwkv_k_k[l]), row(rwkv_k_a[l]),
                   row(rwkv_r_k[l])]
    gw = pool_group_w[l].astype(BF16)
    w_kv_t = jnp.concatenate([jnp.swapaxes(w_mem_k[l], 0, 1), jnp.swapaxes(w_mem_v[l], 0, 1)], axis=0).astype(BF16)
    wa, wb, wc, wo = (pool_out[l].astype(BF16), rwkv_out[l].astype(BF16), xattn_out[l].astype(BF16),
                      w_o[l].astype(BF16))
    g1, gm, g2, fg = row(ffn1_norm_g[l]), row(mix_norm_g[l]), row(ffn2_norm_g[l]), row(final_norm_g)
    scale = row(pool_scale[l])
    ln_g, ln_b = rwkv_ln_g[l], rwkv_ln_b[l]

    tm_p = 512
    h1_p, h1_s = _ffn(x_prompt.reshape(rows_p, d), x_sample.reshape(ns, d), tm_p, g1, *f1)
    z_p, z_s = _in_proj(h1_p, h1_s, gm, wit, in_segments, w_misc, tm=1024, tn=1536)
    kv = _norm_matmul(mem_prompt.reshape(nb * n_mem, d), row(mem_norm_g[l]), w_kv_t, tm=512, tn=512)
    mk_p = kv[:, :xa_w].reshape(nb, n_mem, xa_w)
    mv_p = kv[:, xa_w:].reshape(nb, n_mem, xa_w)

    pool_state = state_pool.reshape(ns, nbuf, pool_w)
    pa_p = _pool_prompt(z_p, nb, t_len, COL_ZP, gw, scale)
    pa_s = _pool_sample(z_s, COL_ZP, jnp.swapaxes(pool_state, 0, 1), gw, scale)

    pb_p, st_p = _rwkv_prompt(z_p, nb, t_len, prep_params, row(ln_g), row(ln_b))
    prep_s = _prep_sample(z_s, _pack_zr(state_shift.reshape(ns, ZR_TRUE), 1), prep_params)
    state_t = jnp.transpose(state_wkv.reshape(ns, n_heads, HEAD, HEAD), (1, 2, 3, 0))
    pb_s, wkv_s_t = _wkv_step(state_t, prep_s, row(ln_g), row(ln_b))

    pc_p = _xattn_prompt(z_p, nb, t_len, COL_ZQ, mk_p, mv_p)
    q_s = z_s[:, o_zq_out:].reshape(ns, XA_HEADS, XA_DIM)
    pc_s = _xattn_sample(q_s, cache_mem_k.reshape(ns, n_mem * XA_HEADS, XA_DIM),
                         cache_mem_v.reshape(ns, n_mem * XA_HEADS, XA_DIM)).reshape(ns, xa_w)

    h2_p = _merge(pa_p, pb_p, pc_p, z_p, 0, h1_p, 256, wa, wb, wc, wo)
    h2_s = _merge(pa_s, pb_s, pc_s, z_s, 0, h1_s, ns, wa, wb, wc, wo)
    y_prompt, y_sample = _ffn(h2_p, h2_s, tm_p, g2, *f2, final_g=fg)

    ends = [(b + 1) * t_len for b in range(nb)]
    shift_p = _unpack_zr_cols(jnp.stack([z_p[e - 1:e, o_rkv:o_zp_out] for e in ends]))[None]
    pool_p = jnp.stack([z_p[e - nbuf:e, o_zp_out:o_zq_out] for e in ends])[None]
    shift_s = _unpack_zr_cols(z_s[:, o_rkv:o_zp_out])[None, :, None, :]
    pool_s = jnp.concatenate([pool_state[:, 1:], z_s[:, None, o_zp_out:o_zq_out]], axis=1)[None]
    wkv_p = st_p[None]
    wkv_s = jnp.transpose(wkv_s_t, (3, 0, 1, 2)).reshape(state_wkv.shape)
    mem_k_p = mk_p.reshape(1, nb, n_mem, XA_HEADS, XA_DIM)
    mem_v_p = mv_p.reshape(1, nb, n_mem, XA_HEADS, XA_DIM)
    return (y_prompt.reshape(nb, t_len, d), y_sample.reshape(ns, 1, d), mem_k_p, mem_v_p, wkv_p, shift_p, pool_p,
            wkv_s, shift_s, pool_s)
```

```python
import functools
import math

import jax
import jax.numpy as jnp
from jax import lax
from jax.experimental import pallas as pl
from jax.experimental.pallas import tpu as pltpu

F32 = jnp.float32
BF16 = jnp.bfloat16

RMS_EPS = 1e-6
GN_EPS = 64e-5
POOL_WINDOWS = (2, 4, 8, 16)
HEAD = 64
LANES = 128
SUBLANES = 8
XA_HEADS = 4
XA_DIM = 128
PAST_LEN = 16384
VMEM_LIMIT = 62 * 1024 * 1024
EXP_M05 = math.exp(-0.5)

RWKV_W = 1024
ZR_TRUE = 3360
RKV_W = 3 * RWKV_W
LORA_W = 512
ZR_W = RKV_W + LORA_W
GATES_W = 6144
COL_RKV = GATES_W // RKV_W
COL_LORA = (GATES_W + RKV_W) // LORA_W
COL_ZP = COL_LORA + 1
COL_ZQ = COL_LORA + 2

NN_DIMS = (((1,), (0,)), ((), ()))
NT_DIMS = (((1,), (1,)), ((), ()))
TN_DIMS = (((0,), (0,)), ((), ()))


def _cparams(sem):
    return pltpu.CompilerParams(dimension_semantics=sem, vmem_limit_bytes=VMEM_LIMIT)


def _rms(x, g):
    ms = jnp.mean(x * x, axis=-1, keepdims=True)
    return x * lax.rsqrt(ms + RMS_EPS) * g


def _bdot(a, b, dims=NN_DIMS):
    return lax.dot_general(a.astype(BF16), b.astype(BF16), dims, preferred_element_type=F32)


def _ffn_kernel(hp_ref, hs_ref, g_ref, fg_ref, wg_ref, wu_ref, wd_ref, wgt_ref, wut_ref, wdt_ref,
                op_ref, os_ref, xp_ref, xs_ref, ap_ref, as_ref, *, nfull, final):
    m = pl.program_id(0)
    f = pl.program_id(1)

    def start(h_ref, x_ref, acc_ref):
        x_ref[...] = _rms(h_ref[...], g_ref[...]).astype(BF16)
        acc_ref[...] = jnp.zeros_like(acc_ref)

    def contribution(x_ref, wg, wu, wd):
        xn = x_ref[...]
        gate = jnp.dot(xn, wg, preferred_element_type=F32)
        up = jnp.dot(xn, wu, preferred_element_type=F32)
        act = (gate * jax.nn.sigmoid(gate) * up).astype(BF16)
        return jnp.dot(act, wd, preferred_element_type=F32)

    def finish(h_ref, acc_ref, o_ref, last):
        out = h_ref[...] + 0.5 * (acc_ref[...] + last)
        if final:
            out = _rms(out, fg_ref[...])
        o_ref[...] = out

    @pl.when(f == 0)
    def _():
        start(hp_ref, xp_ref, ap_ref)

    @pl.when((f == 0) & (m == 0))
    def _():
        start(hs_ref, xs_ref, as_ref)

    @pl.when(f < nfull)
    def _():
        wg, wu, wd = wg_ref[...], wu_ref[...], wd_ref[...]
        ap_ref[...] += contribution(xp_ref, wg, wu, wd)

        @pl.when(m == 0)
        def _():
            as_ref[...] += contribution(xs_ref, wg, wu, wd)

    @pl.when(f == nfull)
    def _():
        wg, wu, wd = wgt_ref[...], wut_ref[...], wdt_ref[...]
        finish(hp_ref, ap_ref, op_ref, contribution(xp_ref, wg, wu, wd))

        @pl.when(m == 0)
        def _():
            finish(hs_ref, as_ref, os_ref, contribution(xs_ref, wg, wu, wd))


def _ffn(hp, hs, tm, g, wg, wu, wd, final_g=None, tf=512):
    mp, d = hp.shape
    ms = hs.shape[0]
    d_ff = wg.shape[1]
    nfull, tail = divmod(d_ff, tf)
    assert tail > 0 and tail % LANES == 0 and mp % tm == 0
    last_main = nfull - 1
    split = nfull * tf
    final = final_g is not None
    vec = pl.BlockSpec((1, d), lambda i, f: (0, 0))
    once = pl.Buffered(1)
    in_specs = [
        pl.BlockSpec((tm, d), lambda i, f: (i, 0)),
        pl.BlockSpec((ms, d), lambda i, f: (0, 0), pipeline_mode=once),
        vec, vec,
        pl.BlockSpec((d, tf), lambda i, f: (0, jnp.minimum(f, last_main))),
        pl.BlockSpec((d, tf), lambda i, f: (0, jnp.minimum(f, last_main))),
        pl.BlockSpec((tf, d), lambda i, f: (jnp.minimum(f, last_main), 0)),
        pl.BlockSpec((pl.Element(d), pl.Element(tail)), lambda i, f: (0, split), pipeline_mode=once),
        pl.BlockSpec((pl.Element(d), pl.Element(tail)), lambda i, f: (0, split), pipeline_mode=once),
        pl.BlockSpec((pl.Element(tail), pl.Element(d)), lambda i, f: (split, 0), pipeline_mode=once),
    ]
    return pl.pallas_call(
        functools.partial(_ffn_kernel, nfull=nfull, final=final),
        grid=(mp // tm, nfull + 1),
        in_specs=in_specs,
        out_specs=[pl.BlockSpec((tm, d), lambda i, f: (i, 0)), pl.BlockSpec((ms, d), lambda i, f: (0, 0))],
        out_shape=[jax.ShapeDtypeStruct((mp, d), F32), jax.ShapeDtypeStruct((ms, d), F32)],
        scratch_shapes=[pltpu.VMEM((tm, d), BF16), pltpu.VMEM((ms, d), BF16),
                        pltpu.VMEM((tm, d), F32), pltpu.VMEM((ms, d), F32)],
        compiler_params=_cparams(("parallel", "arbitrary")),
        name="ffn",
    )(hp, hs, g, g if final_g is None else final_g, wg, wu, wd, wg, wu, wd)


def _norm_matmul_kernel(h_ref, g_ref, wt_ref, o_ref, xn_ref):
    @pl.when(pl.program_id(1) == 0)
    def _():
        xn_ref[...] = _rms(h_ref[...], g_ref[...]).astype(BF16)

    o_ref[...] = lax.dot_general(xn_ref[...], wt_ref[...], NT_DIMS, preferred_element_type=F32)


def _norm_matmul(h, g, wt, tm, tn):
    m, d = h.shape
    n = wt.shape[0]
    return pl.pallas_call(
        _norm_matmul_kernel,
        grid=(m // tm, n // tn),
        in_specs=[
            pl.BlockSpec((tm, d), lambda i, j: (i, 0)),
            pl.BlockSpec((1, d), lambda i, j: (0, 0)),
            pl.BlockSpec((tn, d), lambda i, j: (j, 0)),
        ],
        out_specs=pl.BlockSpec((tm, tn), lambda i, j: (i, j)),
        out_shape=jax.ShapeDtypeStruct((m, n), F32),
        scratch_shapes=[pltpu.VMEM((tm, d), BF16)],
        compiler_params=_cparams(("parallel", "arbitrary")),
        name="norm_matmul",
    )(h, g, wt)


def _in_proj_kernel(hp_ref, hs_ref, g_ref, wt_ref, wm_ref, op_ref, os_ref, xp_ref, xs_ref, *, n_direct):
    i = pl.program_id(0)
    j = pl.program_id(1)

    @pl.when(j == 0)
    def _():
        xp_ref[...] = _rms(hp_ref[...], g_ref[...]).astype(BF16)

    @pl.when((j == 0) & (i == 0))
    def _():
        xs_ref[...] = _rms(hs_ref[...], g_ref[...]).astype(BF16)

    def project(w):
        op_ref[...] = lax.dot_general(xp_ref[...], w, NT_DIMS, preferred_element_type=F32)

        @pl.when(i == 0)
        def _():
            os_ref[...] = lax.dot_general(xs_ref[...], w, NT_DIMS, preferred_element_type=F32)

    @pl.when(j < n_direct)
    def _():
        project(wt_ref[...])

    @pl.when(j >= n_direct)
    def _():
        project(wm_ref[...])


def _in_proj(hp, hs, g, wt, segments, w_misc, tm, tn):
    mp, d = hp.shape
    ms = hs.shape[0]
    tiles = [length // tn for _, length in segments]
    assert all(length % tn == 0 and start % SUBLANES == 0 for start, length in segments)
    n_direct = sum(tiles)
    n_misc = w_misc.shape[0] // tn
    nj = n_direct + n_misc

    def wt_row(j):
        jj = jnp.minimum(j, n_direct - 1)
        first = n_direct - tiles[-1]
        row = segments[-1][0] + (jj - first) * tn
        for (start, _), nt in zip(reversed(segments[:-1]), reversed(tiles[:-1])):
            first -= nt
            row = jnp.where(jj < first + nt, start + (jj - first) * tn, row)
        return pl.multiple_of(row, SUBLANES)

    return pl.pallas_call(
        functools.partial(_in_proj_kernel, n_direct=n_direct),
        grid=(mp // tm, nj),
        in_specs=[
            pl.BlockSpec((tm, d), lambda i, j: (i, 0)),
            pl.BlockSpec((ms, d), lambda i, j: (0, 0), pipeline_mode=pl.Buffered(1)),
            pl.BlockSpec((1, d), lambda i, j: (0, 0)),
            pl.BlockSpec((pl.Element(tn), pl.Element(d)), lambda i, j: (wt_row(j), 0)),
            pl.BlockSpec((tn, d), lambda i, j: (jnp.clip(j - n_direct, 0, n_misc - 1), 0),
                         pipeline_mode=pl.Buffered(1 if n_misc == 1 else 2)),
        ],
        out_specs=[pl.BlockSpec((tm, tn), lambda i, j: (i, j)),
                   pl.BlockSpec((ms, tn), lambda i, j: (0, jnp.where(i == 0, j, nj - 1)))],
        out_shape=[jax.ShapeDtypeStruct((mp, nj * tn), F32), jax.ShapeDtypeStruct((ms, nj * tn), F32)],
        scratch_shapes=[pltpu.VMEM((tm, d), BF16), pltpu.VMEM((ms, d), BF16)],
        compiler_params=_cparams(("arbitrary", "arbitrary")),
        name="in_proj",
    )(hp, hs, g, wt, w_misc)


def _pool_mix(pooled_groups, gw_ref, scale_ref, o_ref):
    for gi, pooled in enumerate(pooled_groups):
        sl = slice(gi * LANES, (gi + 1) * LANES)
        mixed = jnp.dot(pooled.astype(BF16), gw_ref[gi], preferred_element_type=F32)
        o_ref[:, sl] = (mixed * scale_ref[:, sl]).astype(o_ref.dtype)


def _pool_prompt_kernel(zp_ref, gw_ref, scale_ref, o_ref, ext_ref, *, tt):
    t = pl.program_id(1)
    hist = 16

    @pl.when(t == 0)
    def _():
        ext_ref[0:hist, :] = jnp.zeros((hist, ext_ref.shape[1]), F32)

    x = zp_ref[...]
    ext_ref[hist:hist + tt, :] = x
    pos = t * tt + lax.broadcasted_iota(jnp.int32, (tt, LANES), 0)
    groups = []
    for gi, w in enumerate(POOL_WINDOWS):
        sl = slice(gi * LANES, (gi + 1) * LANES)
        acc = x[:, sl]
        for k in range(1, w):
            acc = acc + ext_ref[hist - k:hist - k + tt, sl]
        cnt = jnp.minimum(pos + 1, w).astype(F32)
        groups.append(acc / cnt - x[:, sl])
    _pool_mix(groups, gw_ref, scale_ref, o_ref)
    ext_ref[0:hist, :] = ext_ref[tt:tt + hist, :]


def _pool_prompt(z, nb, t_len, col_blk, gw, scale, tt=1024):
    nt = t_len // tt
    pw = gw.shape[0] * LANES
    return pl.pallas_call(
        functools.partial(_pool_prompt_kernel, tt=tt),
        grid=(nb, nt),
        in_specs=[
            pl.BlockSpec((tt, pw), lambda b, t: (b * nt + t, col_blk)),
            pl.BlockSpec(gw.shape, lambda b, t: (0, 0, 0)),
            pl.BlockSpec((1, pw), lambda b, t: (0, 0)),
        ],
        out_specs=pl.BlockSpec((tt, pw), lambda b, t: (b * nt + t, 0)),
        out_shape=jax.ShapeDtypeStruct((nb * t_len, pw), BF16),
        scratch_shapes=[pltpu.VMEM((tt + 16, pw), F32)],
        compiler_params=_cparams(("parallel", "arbitrary")),
        name="pool_prompt",
    )(z, gw, scale)


def _pool_sample_kernel(zp_ref, buf_ref, gw_ref, scale_ref, o_ref):
    x = zp_ref[...]
    nbuf = buf_ref.shape[0]
    groups = []
    for gi, w in enumerate(POOL_WINDOWS):
        sl = slice(gi * LANES, (gi + 1) * LANES)
        acc = x[:, sl]
        for k in range(1, w):
            acc = acc + buf_ref[nbuf - k, :, sl]
        cnt = float(min(PAST_LEN + 1, w))
        groups.append(acc / cnt - x[:, sl])
    _pool_mix(groups, gw_ref, scale_ref, o_ref)


def _pool_sample(z, col_blk, buf_t, gw, scale):
    nrows = z.shape[0]
    pw = gw.shape[0] * LANES
    return pl.pallas_call(
        _pool_sample_kernel,
        grid=(1,),
        in_specs=[
            pl.BlockSpec((nrows, pw), lambda i: (0, col_blk)),
            pl.BlockSpec(buf_t.shape, lambda i: (0, 0, 0)),
            pl.BlockSpec(gw.shape, lambda i: (0, 0, 0)),
            pl.BlockSpec((1, pw), lambda i: (0, 0)),
        ],
        out_specs=pl.BlockSpec((nrows, pw), lambda i: (0, 0)),
        out_shape=jax.ShapeDtypeStruct((nrows, pw), BF16),
        compiler_params=_cparams(("arbitrary",)),
        name="pool_sample",
    )(z, buf_t, gw, scale)


def _head_sum(x):
    head_a = lax.broadcasted_iota(jnp.int32, (x.shape[0], LANES), 1) < HEAD
    cols = []
    for c in range(x.shape[1] // LANES):
        t = x[:, c * LANES:(c + 1) * LANES]
        sa = jnp.sum(jnp.where(head_a, t, 0.0), axis=-1, keepdims=True)
        sb = jnp.sum(jnp.where(head_a, 0.0, t), axis=-1, keepdims=True)
        cols.append(jnp.where(head_a, sa, sb))
    return jnp.concatenate(cols, axis=1)


def _prep_math(x, prev, p_refs):
    mu_ref, w0_ref, wup_ref, a0_ref, aup_ref, gup_ref, kk_ref, ka_ref, rk_ref = p_refs
    xm = x + (prev - x) * mu_ref[...]
    w = RWKV_W
    r = xm[:, 0:w]
    k = xm[:, w:2 * w]
    v = xm[:, 2 * w:3 * w]
    wl = xm[:, 3 * w:3 * w + 128]
    al = xm[:, 3 * w + 128:3 * w + 256]
    gl = xm[:, 3 * w + 256:3 * w + 512]
    dw = w0_ref[...] + jnp.dot(jnp.tanh(wl).astype(BF16), wup_ref[...], preferred_element_type=F32)
    lw = -EXP_M05 * jax.nn.sigmoid(dw)
    a = jax.nn.sigmoid(a0_ref[...] + jnp.dot(al.astype(BF16), aup_ref[...], preferred_element_type=F32))
    g = jnp.dot(jax.nn.sigmoid(gl).astype(BF16), gup_ref[...], preferred_element_type=F32)
    kk = k * kk_ref[...]
    kk = kk * lax.rsqrt(jnp.maximum(_head_sum(kk * kk), 1e-24))
    kmod = k * (1.0 + (a - 1.0) * ka_ref[...])
    bonus = _head_sum(r * kmod * rk_ref[...]) * v
    return r, lw, kmod, v, -kk, kk * a, g, bonus


def _group_norm_gate(y, bonus, g, lng_ref, lnb_ref):
    d = y - _head_sum(y) * (1.0 / HEAD)
    var = _head_sum(d * d) * (1.0 / HEAD)
    yn = d * lax.rsqrt(var + GN_EPS) * lng_ref[...] + lnb_ref[...]
    return (yn + bonus) * g


def _prep_sample_kernel(rkv_ref, lora_ref, prev_ref, *refs):
    p_refs, out_refs = refs[:9], refs[9:17]
    x = jnp.concatenate([rkv_ref[...], lora_ref[...]], axis=1)
    for o_ref, val in zip(out_refs, _prep_math(x, prev_ref[...], p_refs)):
        o_ref[...] = val


def _prep_param_specs(params):
    zero = (lambda *idx: (0, 0))
    return [pl.BlockSpec(p.shape, zero) for p in params]


def _prep_sample(z, prev, params):
    nrows = z.shape[0]
    out_spec = pl.BlockSpec((nrows, RWKV_W), lambda i: (0, 0))
    return pl.pallas_call(
        _prep_sample_kernel,
        grid=(1,),
        in_specs=[pl.BlockSpec((nrows, RKV_W), lambda i: (0, COL_RKV)),
                  pl.BlockSpec((nrows, LORA_W), lambda i: (0, COL_LORA)),
                  pl.BlockSpec((nrows, ZR_W), lambda i: (0, 0))] + _prep_param_specs(params),
        out_specs=[out_spec] * 8,
        out_shape=[jax.ShapeDtypeStruct((nrows, RWKV_W), F32)] * 8,
        compiler_params=_cparams(("arbitrary",)),
        name="prep_sample",
    )(z, z, prev, *params)


def _split3(x):
    hi = x.astype(BF16)
    rest = x - hi.astype(F32)
    mid = rest.astype(BF16)
    lo = (rest - mid.astype(F32)).astype(BF16)
    return hi, mid, lo


def _select_dot(sel, x):
    sel = sel.astype(BF16)
    hi, mid, lo = _split3(x)
    return _bdot(sel, hi) + (_bdot(sel, mid) + _bdot(sel, lo))


def _chunk_scan(r, lw, k, v, a, b, s_ref, c_len):
    n2 = 2 * c_len
    assert n2 == LANES
    row = lax.broadcasted_iota(jnp.int32, (n2, n2), 0)
    col = lax.broadcasted_iota(jnp.int32, (n2, n2), 1)
    tr = row & (c_len - 1)
    tc = col & (c_len - 1)
    strict = tr > tc
    incl = tr >= tc
    tri = (lax.broadcasted_iota(jnp.int32, (c_len, c_len), 0)
           >= lax.broadcasted_iota(jnp.int32, (c_len, c_len), 1))
    head_a = lax.broadcasted_iota(jnp.int32, (c_len, LANES), 1) < HEAD

    def stack(x):
        return jnp.concatenate([jnp.where(head_a, x, 0.0), jnp.where(head_a, 0.0, x)], axis=0).astype(BF16)

    prs = range(r.shape[1] // LANES)
    cat = jnp.concatenate
    sls = [slice(q * LANES, (q + 1) * LANES) for q in prs]
    lwq = [lw[:, sl] for sl in sls]
    cum = [_select_dot(tri, x) for x in lwq]
    tot = [x[c_len - 1:c_len, :] for x in cum]
    xr = [stack(r[:, sls[q]] * jnp.exp(cum[q])) for q in prs]
    xa = [stack(a[:, sls[q]] * jnp.exp(cum[q] - lwq[q])) for q in prs]
    e_neg = [jnp.exp(-x) for x in cum]
    e_rem = [jnp.exp(tot[q] - cum[q]) for q in prs]
    yb = [stack(b[:, sls[q]] * e_neg[q]) for q in prs]
    yk = [stack(k[:, sls[q]] * e_neg[q]) for q in prs]
    zb = [stack(b[:, sls[q]] * e_rem[q]) for q in prs]
    zk = [stack(k[:, sls[q]] * e_rem[q]) for q in prs]
    vs = [stack(v[:, sl]) for sl in sls]

    g = [_bdot(cat([xa[q], xr[q]], axis=0), cat([yb[q], yk[q]], axis=0), NT_DIMS) for q in prs]
    m_ab = [jnp.where(strict, x[0:n2, 0:n2], 0.0) for x in g]
    m_ak = [jnp.where(strict, x[0:n2, n2:2 * n2], 0.0).astype(BF16) for x in g]
    n_rb = [jnp.where(incl, x[n2:2 * n2, 0:n2], 0.0) for x in g]
    n_rk = [jnp.where(incl, x[n2:2 * n2, n2:2 * n2], 0.0).astype(BF16) for x in g]

    s_old = [s_ref[q] for q in prs]
    lhs = [cat([cat([xa[q], m_ak[q]], axis=1), cat([xr[q], n_rk[q]], axis=1)], axis=0) for q in prs]
    xy0 = [_bdot(lhs[q], cat([s_old[q].T.astype(BF16), vs[q]], axis=0)) for q in prs]

    x = [xy0[q][0:n2] for q in prs]
    mk = m_ab
    nlev = int(math.log2(c_len))
    for lev in range(nlev):
        if lev < nlev - 1:
            res = [_bdot(mk[q], cat([mk[q], x[q]], axis=1)) for q in prs]
            mk = [t[:, 0:n2] for t in res]
            x = [x[q] + res[q][:, n2:2 * n2] for q in prs]
        else:
            x = [x[q] + _bdot(mk[q], x[q]) for q in prs]
    u = [t.astype(BF16) for t in x]

    y_st = [xy0[q][n2:2 * n2] + _bdot(n_rb[q], u[q]) for q in prs]
    for q in prs:
        s_ref[q] = s_old[q] * jnp.exp(tot[q]) + _bdot(cat([u[q], vs[q]], axis=0), cat([zb[q], zk[q]], axis=0),
                                                     TN_DIMS)
    return cat([t[0:c_len] + t[c_len:n2] for t in y_st], axis=1)


def _rwkv_prompt_kernel(rkv_ref, lora_ref, *refs, nc):
    p_refs = refs[:9]
    lng_ref, lnb_ref, yb_ref, sout_ref, s_ref, carry_ref = refs[9:]
    c = pl.program_id(1)
    nseq, c_len, _ = rkv_ref.shape

    @pl.when(c == 0)
    def _():
        s_ref[...] = jnp.zeros_like(s_ref)
        carry_ref[...] = jnp.zeros_like(carry_ref)

    first = lax.broadcasted_iota(jnp.int32, (c_len, ZR_W), 0) == 0
    preps = []
    for q in range(nseq):
        x = jnp.concatenate([rkv_ref[q], lora_ref[q]], axis=1)
        rolled = pltpu.roll(x, 1, axis=0)
        prev = jnp.where(first, carry_ref[q, 0:1, :], rolled)
        carry_ref[q] = rolled[0:SUBLANES, :]
        preps.append(_prep_math(x, prev, p_refs))
    r, lw, k, v, a, b = (jnp.concatenate([p[i] for p in preps], axis=1) for i in range(6))
    y = _chunk_scan(r, lw, k, v, a, b, s_ref, c_len)
    for q in range(nseq):
        yq = y[:, q * RWKV_W:(q + 1) * RWKV_W]
        yb_ref[q] = _group_norm_gate(yq, preps[q][7], preps[q][6], lng_ref, lnb_ref).astype(yb_ref.dtype)

    @pl.when(c == nc - 1)
    def _():
        npair = s_ref.shape[0] // nseq
        for q in range(nseq):
            for p in range(npair):
                s = s_ref[q * npair + p]
                sout_ref[q, 2 * p] = s[0:HEAD, 0:HEAD]
                sout_ref[q, 2 * p + 1] = s[HEAD:2 * HEAD, HEAD:2 * HEAD]


def _rwkv_prompt(z, nb, t_len, params, ln_g, ln_b, c_len=64, nseq=2):
    nc = t_len // c_len
    npair = RWKV_W // LANES
    par = pl.BlockSpec((1, RWKV_W), lambda b, c: (0, 0))
    z3 = z.reshape(nb, t_len, z.shape[1])
    yb, state = pl.pallas_call(
        functools.partial(_rwkv_prompt_kernel, nc=nc),
        grid=(nb // nseq, nc),
        in_specs=[pl.BlockSpec((nseq, c_len, RKV_W), lambda b, c: (b, c, COL_RKV)),
                  pl.BlockSpec((nseq, c_len, LORA_W), lambda b, c: (b, c, COL_LORA))]
        + _prep_param_specs(params) + [par, par],
        out_specs=[pl.BlockSpec((nseq, c_len, RWKV_W), lambda b, c: (b, c, 0)),
                   pl.BlockSpec((nseq, 2 * npair, HEAD, HEAD), lambda b, c: (b, 0, 0, 0))],
        out_shape=[jax.ShapeDtypeStruct((nb, t_len, RWKV_W), BF16),
                   jax.ShapeDtypeStruct((nb, 2 * npair, HEAD, HEAD), F32)],
        scratch_shapes=[pltpu.VMEM((nseq * npair, LANES, LANES), F32), pltpu.VMEM((nseq, SUBLANES, ZR_W), F32)],
        compiler_params=_cparams(("parallel", "arbitrary")),
        name="rwkv_prompt",
    )(z3, z3, *params, ln_g, ln_b)
    return yb.reshape(nb * t_len, RWKV_W), state


def _wkv_step_kernel(s_ref, r_ref, lw_ref, k_ref, v_ref, a_ref, b_ref, g_ref, bonus_ref, lng_ref, lnb_ref,
                     yb_ref, so_ref, vt_scr, y_scr):
    heads = s_ref.shape[0]
    rt, wt, kt, at, bt = (x[...].T for x in (r_ref, lw_ref, k_ref, a_ref, b_ref))
    wt = jnp.exp(wt)
    vt_scr[...] = v_ref[...].T
    for hh in range(heads):
        rows = slice(hh * HEAD, (hh + 1) * HEAD)
        r, w, k, a, b = (x[rows, :] for x in (rt, wt, kt, at, bt))

        def body(i, carry):
            si = s_ref[hh, i]
            sa = jnp.sum(si * a, axis=0, keepdims=True)
            vi = vt_scr[pl.ds(hh * HEAD + i, 1), :]
            s2 = si * w + sa * b + vi * k
            so_ref[hh, i] = s2
            y_scr[pl.ds(hh * HEAD + i, 1), :] = jnp.sum(s2 * r, axis=0, keepdims=True)
            return carry

        lax.fori_loop(0, HEAD, body, 0, unroll=4)

    outs = []
    for hh in range(heads):
        y = y_scr[hh * HEAD:(hh + 1) * HEAD, :]
        d = y - jnp.mean(y, axis=0, keepdims=True)
        var = jnp.mean(d * d, axis=0, keepdims=True)
        outs.append(d * lax.rsqrt(var + GN_EPS))
    yn = jnp.concatenate(outs, axis=0).T * lng_ref[...] + lnb_ref[...]
    yb_ref[...] = ((yn + bonus_ref[...]) * g_ref[...]).astype(yb_ref.dtype)


def _wkv_step(state_t, vecs, ln_g, ln_b, heads=4):
    nh, _, _, nb = state_t.shape
    st = pl.BlockSpec((heads, HEAD, HEAD, nb), lambda p: (p, 0, 0, 0))
    vec = pl.BlockSpec((nb, heads * HEAD), lambda p: (0, p))
    par = pl.BlockSpec((1, heads * HEAD), lambda p: (0, p))
    return pl.pallas_call(
        _wkv_step_kernel,
        grid=(nh // heads,),
        in_specs=[st] + [vec] * 8 + [par, par],
        out_specs=[vec, st],
        out_shape=[jax.ShapeDtypeStruct((nb, nh * HEAD), BF16), jax.ShapeDtypeStruct(state_t.shape, F32)],
        scratch_shapes=[pltpu.VMEM((heads * HEAD, nb), F32), pltpu.VMEM((heads * HEAD, nb), F32)],
        compiler_params=_cparams(("parallel",)),
        name="wkv_step",
    )(state_t, *vecs, ln_g, ln_b)


def _xattn_prompt_kernel(q_ref, k_ref, v_ref, o_ref):
    scale = XA_DIM ** -0.5
    q = q_ref[...]
    for h in range(XA_HEADS):
        sl = slice(h * XA_DIM, (h + 1) * XA_DIM)
        s = lax.dot_general(q[:, sl].astype(BF16), k_ref[0, :, sl].astype(BF16),
                            NT_DIMS, preferred_element_type=F32) * scale
        p = jnp.exp(s - jnp.max(s, axis=-1, keepdims=True))
        den = jnp.sum(p, axis=-1, keepdims=True)
        o = jnp.dot(p.astype(BF16), v_ref[0, :, sl].astype(BF16), preferred_element_type=F32)
        o_ref[:, sl] = (o / den).astype(o_ref.dtype)


def _xattn_prompt(z, nb, t_len, col_blk, mk, mv, tq=1024):
    nt = t_len // tq
    xw = XA_HEADS * XA_DIM
    nmem = mk.shape[1]
    kv = pl.BlockSpec((1, nmem, xw), lambda b, t: (b, 0, 0))
    return pl.pallas_call(
        _xattn_prompt_kernel,
        grid=(nb, nt),
        in_specs=[pl.BlockSpec((tq, xw), lambda b, t: (b * nt + t, col_blk)), kv, kv],
        out_specs=pl.BlockSpec((tq, xw), lambda b, t: (b * nt + t, 0)),
        out_shape=jax.ShapeDtypeStruct((nb * t_len, xw), BF16),
        compiler_params=_cparams(("parallel", "parallel")),
        name="xattn_prompt",
    )(z, mk, mv)


def _xattn_sample_kernel(q_ref, k_ref, v_ref, o_ref):
    bb = q_ref.shape[0]
    nrow = k_ref.shape[1] // SUBLANES
    full = (bb, nrow, SUBLANES, XA_DIM)
    q = q_ref[...] * (XA_DIM ** -0.5)
    q8 = jnp.concatenate([q, q], axis=1)[:, None]
    k = k_ref[...].reshape(full)
    s = jnp.broadcast_to(jnp.sum(k * q8, axis=-1, keepdims=True), full)
    mx = jnp.max(s, axis=1, keepdims=True)
    mx = jnp.maximum(mx, pltpu.roll(mx, XA_HEADS, axis=2))
    p = jnp.exp(s - mx)
    den = jnp.sum(p, axis=1, keepdims=True)
    den = den + pltpu.roll(den, XA_HEADS, axis=2)
    o = jnp.sum(p * v_ref[...].reshape(full), axis=1, keepdims=True)
    o = o + pltpu.roll(o, XA_HEADS, axis=2)
    o_ref[...] = (o / den)[:, 0, 0:XA_HEADS, :].astype(o_ref.dtype)


def _xattn_sample(q3, mk, mv, bb=16):
    nb, rows, _ = mk.shape
    kv = pl.BlockSpec((bb, rows, XA_DIM), lambda i: (i, 0, 0))
    qs = pl.BlockSpec((bb, XA_HEADS, XA_DIM), lambda i: (i, 0, 0))
    return pl.pallas_call(
        _xattn_sample_kernel,
        grid=(nb // bb,),
        in_specs=[qs, kv, kv],
        out_specs=qs,
        out_shape=jax.ShapeDtypeStruct((nb, XA_HEADS, XA_DIM), BF16),
        compiler_params=_cparams(("parallel",)),
        name="xattn_sample",
    )(q3, mk, mv)


def _merge_kernel(pa_ref, pb_ref, pc_ref, g0_ref, g1_ref, g2_ref, h_ref, wa_ref, wb_ref, wc_ref, wo_ref, o_ref):
    oa = jnp.dot(pa_ref[...], wa_ref[...], preferred_element_type=F32)
    ob = jnp.dot(pb_ref[...], wb_ref[...], preferred_element_type=F32)
    oc = jnp.dot(pc_ref[...], wc_ref[...], preferred_element_type=F32)
    merged = (jax.nn.sigmoid(g0_ref[...]) * oa + jax.nn.sigmoid(g1_ref[...]) * ob
              + jax.nn.sigmoid(g2_ref[...]) * oc)
    o_ref[...] = h_ref[...] + jnp.dot(merged.astype(BF16), wo_ref[...], preferred_element_type=F32)


def _merge(pa, pb, pc, z, zg_blk0, h, tm, wa, wb, wc, wo):
    m, d = h.shape
    const = lambda i: (0, 0)
    resident = lambda w: pl.BlockSpec(w.shape, const, pipeline_mode=pl.Buffered(1))
    in_specs = [
        pl.BlockSpec((tm, pa.shape[1]), lambda i: (i, 0)),
        pl.BlockSpec((tm, pb.shape[1]), lambda i: (i, 0)),
        pl.BlockSpec((tm, pc.shape[1]), lambda i: (i, 0)),
        pl.BlockSpec((tm, d), lambda i: (i, zg_blk0)),
        pl.BlockSpec((tm, d), lambda i: (i, zg_blk0 + 1)),
        pl.BlockSpec((tm, d), lambda i: (i, zg_blk0 + 2)),
        pl.BlockSpec((tm, d), lambda i: (i, 0)),
        resident(wa), resident(wb), resident(wc), resident(wo),
    ]
    return pl.pallas_call(
        _merge_kernel,
        grid=(m // tm,),
        in_specs=in_specs,
        out_specs=pl.BlockSpec((tm, d), lambda i: (i, 0)),
        out_shape=jax.ShapeDtypeStruct((m, d), F32),
        compiler_params=_cparams(("parallel",)),
        name="merge",
    )(pa, pb, pc, z, z, z, h, wa, wb, wc, wo)


def _pack_lora(x, axis):
    take = lambda a, b: lax.slice_in_dim(x, a, b, axis=axis)

    def pad(n):
        shape = list(x.shape)
        shape[axis] = n
        return jnp.zeros(shape, x.dtype)

    return jnp.concatenate([take(0, 64), pad(64), take(64, 128), pad(64), take(128, ZR_TRUE - RKV_W), pad(96)],
                           axis=axis)


def _pack_zr(x, axis):
    return jnp.concatenate([lax.slice_in_dim(x, 0, RKV_W, axis=axis),
                            _pack_lora(lax.slice_in_dim(x, RKV_W, ZR_TRUE, axis=axis), axis)], axis=axis)


def _unpack_zr_cols(x):
    w = RWKV_W
    return jnp.concatenate([x[..., :3 * w + 64], x[..., 3 * w + 128:3 * w + 192], x[..., 3 * w + 256:3 * w + 416]],
                           axis=-1)


def _pad_rows(x, n):
    return jnp.concatenate([x, jnp.zeros((n - x.shape[0],) + x.shape[1:], x.dtype)], axis=0)


def kernel(x_prompt, x_sample, mem_prompt, cache_mem_k, cache_mem_v, state_wkv, state_shift, state_pool,
           ffn1_norm_g, ffn1_w_gate, ffn1_w_up, ffn1_w_down, mix_norm_g, w_in,
           pool_group_w, pool_scale, pool_out,
           rwkv_mu, rwkv_w0, rwkv_w_up, rwkv_a0, rwkv_a_up, rwkv_g_up, rwkv_k_k, rwkv_k_a, rwkv_r_k,
           rwkv_ln_g, rwkv_ln_b, rwkv_out,
           mem_norm_g, w_mem_k, w_mem_v, xattn_out, w_o,
           ffn2_norm_g, ffn2_w_gate, ffn2_w_up, ffn2_w_down, final_norm_g):
    nb, t_len, d = x_prompt.shape
    ns = x_sample.shape[0]
    assert w_in.shape[0] == 1 and x_sample.shape[1] == 1
    n_mem = mem_prompt.shape[1]
    pool_w = pool_out.shape[1]
    xa_w = xattn_out.shape[1]
    n_heads = RWKV_W // HEAD
    nbuf = state_pool.shape[2]
    rows_p = nb * t_len
    l = 0

    f1 = [w.reshape(w.shape[1:]).astype(BF16) for w in (ffn1_w_gate, ffn1_w_up, ffn1_w_down)]
    f2 = [w.reshape(w.shape[1:]).astype(BF16) for w in (ffn2_w_gate, ffn2_w_up, ffn2_w_down)]
    wit = jnp.swapaxes(w_in[l], 0, 1).astype(BF16)
    o_zr, o_zq, o_zg = pool_w, pool_w + ZR_TRUE, pool_w + ZR_TRUE + xa_w
    assert (pool_w, xa_w, 3 * d) == (LORA_W, LORA_W, GATES_W)
    w_misc = jnp.concatenate([_pack_lora(wit[o_zr + RKV_W:o_zq], 0), wit[:o_zr], wit[o_zq:o_zg]], axis=0)
    in_segments = [(o_zg, GATES_W), (o_zr, RKV_W)]
    o_rkv, o_zp_out, o_zq_out = GATES_W, GATES_W + ZR_W, GATES_W + ZR_W + pool_w
    row = lambda v: v.reshape(1, -1)
    prep_params = [row(_pack_zr(rwkv_mu[l], 0)), row(rwkv_w0[l]), _pad_rows(rwkv_w_up[l], 128).astype(BF16),
                   row(rwkv_a0[l]), _pad_rows(rwkv_a_up[l], 128).astype(BF16),
                   _pad_rows(rwkv_g_up[l], 256).astype(BF16), row(rwkv_k_k[l]), row(rwkv_k_a[l]),
                   row(rwkv_r_k[l])]
    gw = pool_group_w[l].astype(BF16)
    w_kv_t = jnp.concatenate([jnp.swapaxes(w_mem_k[l], 0, 1), jnp.swapaxes(w_mem_v[l], 0, 1)], axis=0).astype(BF16)
    wa, wb, wc, wo = (pool_out[l].astype(BF16), rwkv_out[l].astype(BF16), xattn_out[l].astype(BF16),
                      w_o[l].astype(BF16))
    g1, gm, g2, fg = row(ffn1_norm_g[l]), row(mix_norm_g[l]), row(ffn2_norm_g[l]), row(final_norm_g)
    scale = row(pool_scale[l])
    ln_g, ln_b = rwkv_ln_g[l], rwkv_ln_b[l]

    tm_p = 512
    h1_p, h1_s = _ffn(x_prompt.reshape(rows_p, d), x_sample.reshape(ns, d), tm_p, g1, *f1)
    z_p, z_s = _in_proj(h1_p, h1_s, gm, wit, in_segments, w_misc, tm=1024, tn=1536)
    kv = _norm_matmul(mem_prompt.reshape(nb * n_mem, d), row(mem_norm_g[l]), w_kv_t, tm=512, tn=512)
    mk_p = kv[:, :xa_w].reshape(nb, n_mem, xa_w)
    mv_p = kv[:, xa_w:].reshape(nb, n_mem, xa_w)

    pool_state = state_pool.reshape(ns, nbuf, pool_w)
    pa_p = _pool_prompt(z_p, nb, t_len, COL_ZP, gw, scale)
    pa_s = _pool_sample(z_s, COL_ZP, jnp.swapaxes(pool_state, 0, 1), gw, scale)

    pb_p, st_p = _rwkv_prompt(z_p, nb, t_len, prep_params, row(ln_g), row(ln_b))
    prep_s = _prep_sample(z_s, _pack_zr(state_shift.reshape(ns, ZR_TRUE), 1), prep_params)
    state_t = jnp.transpose(state_wkv.reshape(ns, n_heads, HEAD, HEAD), (1, 2, 3, 0))
    pb_s, wkv_s_t = _wkv_step(state_t, prep_s, row(ln_g), row(ln_b))

    pc_p = _xattn_prompt(z_p, nb, t_len, COL_ZQ, mk_p, mv_p)
    q_s = z_s[:, o_zq_out:].reshape(ns, XA_HEADS, XA_DIM)
    pc_s = _xattn_sample(q_s, cache_mem_k.reshape(ns, n_mem * XA_HEADS, XA_DIM),
                         cache_mem_v.reshape(ns, n_mem * XA_HEADS, XA_DIM)).reshape(ns, xa_w)

    h2_p = _merge(pa_p, pb_p, pc_p, z_p, 0, h1_p, 256, wa, wb, wc, wo)
    h2_s = _merge(pa_s, pb_s, pc_s, z_s, 0, h1_s, ns, wa, wb, wc, wo)
    y_prompt, y_sample = _ffn(h2_p, h2_s, tm_p, g2, *f2, final_g=fg)

    ends = [(b + 1) * t_len for b in range(nb)]
    shift_p = _unpack_zr_cols(jnp.stack([z_p[e - 1:e, o_rkv:o_zp_out] for e in ends]))[None]
    pool_p = jnp.stack([z_p[e - nbuf:e, o_zp_out:o_zq_out] for e in ends])[None]
    shift_s = _unpack_zr_cols(z_s[:, o_rkv:o_zp_out])[None, :, None, :]
    pool_s = jnp.concatenate([pool_state[:, 1:], z_s[:, None, o_zp_out:o_zq_out]], axis=1)[None]
    wkv_p = st_p[None]
    wkv_s = jnp.transpose(wkv_s_t, (3, 0, 1, 2)).reshape(state_wkv.shape)
    mem_k_p = mk_p.reshape(1, nb, n_mem, XA_HEADS, XA_DIM)
    mem_v_p = mv_p.reshape(1, nb, n_mem, XA_HEADS, XA_DIM)
    return (y_prompt.reshape(nb, t_len, d), y_sample.reshape(ns, 1, d), mem_k_p, mem_v_p, wkv_p, shift_p, pool_p,
            wkv_s, shift_s, pool_s)
```

```python
import functools
import math

import jax
import jax.numpy as jnp
from jax import lax
from jax.experimental import pallas as pl
from jax.experimental.pallas import tpu as pltpu

F32 = jnp.float32
BF16 = jnp.bfloat16

RMS_EPS = 1e-6
GN_EPS = 64e-5
POOL_WINDOWS = (2, 4, 8, 16)
HEAD = 64
LANES = 128
SUBLANES = 8
XA_HEADS = 4
XA_DIM = 128
PAST_LEN = 16384
VMEM_LIMIT = 62 * 1024 * 1024
EXP_M05 = math.exp(-0.5)

RWKV_W = 1024
ZR_TRUE = 3360
RKV_W = 3 * RWKV_W
LORA_W = 512
ZR_W = RKV_W + LORA_W
GATES_W = 6144
COL_RKV = GATES_W // RKV_W
COL_LORA = (GATES_W + RKV_W) // LORA_W
COL_ZP = COL_LORA + 1
COL_ZQ = COL_LORA + 2

NN_DIMS = (((1,), (0,)), ((), ()))
NT_DIMS = (((1,), (1,)), ((), ()))
TN_DIMS = (((0,), (0,)), ((), ()))


def _cparams(sem):
    return pltpu.CompilerParams(dimension_semantics=sem, vmem_limit_bytes=VMEM_LIMIT)


def _rms(x, g):
    ms = jnp.mean(x * x, axis=-1, keepdims=True)
    return x * lax.rsqrt(ms + RMS_EPS) * g


def _bdot(a, b, dims=NN_DIMS):
    return lax.dot_general(a.astype(BF16), b.astype(BF16), dims, preferred_element_type=F32)


def _ffn_kernel(hp_ref, hs_ref, g_ref, fg_ref, wg_ref, wu_ref, wd_ref, wgt_ref, wut_ref, wdt_ref,
                op_ref, os_ref, xp_ref, xs_ref, ap_ref, as_ref, *, nfull, final):
    m = pl.program_id(0)
    f = pl.program_id(1)

    def start(h_ref, x_ref, acc_ref):
        x_ref[...] = _rms(h_ref[...], g_ref[...]).astype(BF16)
        acc_ref[...] = jnp.zeros_like(acc_ref)

    def contribution(x_ref, wg, wu, wd):
        xn = x_ref[...]
        gate = jnp.dot(xn, wg, preferred_element_type=F32)
        up = jnp.dot(xn, wu, preferred_element_type=F32)
        act = (gate * jax.nn.sigmoid(gate) * up).astype(BF16)
        return jnp.dot(act, wd, preferred_element_type=F32)

    def finish(h_ref, acc_ref, o_ref, last):
        out = h_ref[...] + 0.5 * (acc_ref[...] + last)
        if final:
            out = _rms(out, fg_ref[...])
        o_ref[...] = out

    @pl.when(f == 0)
    def _():
        start(hp_ref, xp_ref, ap_ref)

    @pl.when((f == 0) & (m == 0))
    def _():
        start(hs_ref, xs_ref, as_ref)

    @pl.when(f < nfull)
    def _():
        wg, wu, wd = wg_ref[...], wu_ref[...], wd_ref[...]
        ap_ref[...] += contribution(xp_ref, wg, wu, wd)

        @pl.when(m == 0)
        def _():
            as_ref[...] += contribution(xs_ref, wg, wu, wd)

    @pl.when(f == nfull)
    def _():
        wg, wu, wd = wgt_ref[...], wut_ref[...], wdt_ref[...]
        finish(hp_ref, ap_ref, op_ref, contribution(xp_ref, wg, wu, wd))

        @pl.when(m == 0)
        def _():
            finish(hs_ref, as_ref, os_ref, contribution(xs_ref, wg, wu, wd))


def _ffn(hp, hs, tm, g, wg, wu, wd, final_g=None, tf=512):
    mp, d = hp.shape
    ms = hs.shape[0]
    d_ff = wg.shape[1]
    nfull, tail = divmod(d_ff, tf)
    assert tail > 0 and tail % LANES == 0 and mp % tm == 0
    last_main = nfull - 1
    split = nfull * tf
    final = final_g is not None
    vec = pl.BlockSpec((1, d), lambda i, f: (0, 0))
    once = pl.Buffered(1)
    in_specs = [
        pl.BlockSpec((tm, d), lambda i, f: (i, 0)),
        pl.BlockSpec((ms, d), lambda i, f: (0, 0), pipeline_mode=once),
        vec, vec,
        pl.BlockSpec((d, tf), lambda i, f: (0, jnp.minimum(f, last_main))),
        pl.BlockSpec((d, tf), lambda i, f: (0, jnp.minimum(f, last_main))),
        pl.BlockSpec((tf, d), lambda i, f: (jnp.minimum(f, last_main), 0)),
        pl.BlockSpec((pl.Element(d), pl.Element(tail)), lambda i, f: (0, split), pipeline_mode=once),
        pl.BlockSpec((pl.Element(d), pl.Element(tail)), lambda i, f: (0, split), pipeline_mode=once),
        pl.BlockSpec((pl.Element(tail), pl.Element(d)), lambda i, f: (split, 0), pipeline_mode=once),
    ]
    return pl.pallas_call(
        functools.partial(_ffn_kernel, nfull=nfull, final=final),
        grid=(mp // tm, nfull + 1),
        in_specs=in_specs,
        out_specs=[pl.BlockSpec((tm, d), lambda i, f: (i, 0)), pl.BlockSpec((ms, d), lambda i, f: (0, 0))],
        out_shape=[jax.ShapeDtypeStruct((mp, d), F32), jax.ShapeDtypeStruct((ms, d), F32)],
        scratch_shapes=[pltpu.VMEM((tm, d), BF16), pltpu.VMEM((ms, d), BF16),
                        pltpu.VMEM((tm, d), F32), pltpu.VMEM((ms, d), F32)],
        compiler_params=_cparams(("parallel", "arbitrary")),
        name="ffn",
    )(hp, hs, g, g if final_g is None else final_g, wg, wu, wd, wg, wu, wd)


def _ffn_up_kernel(hp_ref, hs_ref, g_ref, wg_ref, wu_ref, ap_ref, as_ref, xp_ref, xs_ref):
    i = pl.program_id(0)
    f = pl.program_id(1)

    @pl.when(f == 0)
    def _():
        xp_ref[...] = _rms(hp_ref[...], g_ref[...]).astype(BF16)

    @pl.when((f == 0) & (i == 0))
    def _():
        xs_ref[...] = _rms(hs_ref[...], g_ref[...]).astype(BF16)

    wg, wu = wg_ref[...], wu_ref[...]

    def act(x_ref):
        xn = x_ref[...]
        gate = jnp.dot(xn, wg, preferred_element_type=F32)
        up = jnp.dot(xn, wu, preferred_element_type=F32)
        return (gate * jax.nn.sigmoid(gate) * up).astype(BF16)

    ap_ref[...] = act(xp_ref)

    @pl.when(i == 0)
    def _():
        as_ref[...] = act(xs_ref)


def _ffn_down_kernel(ap_ref, as_ref, hp_ref, hs_ref, fg_ref, wd_ref, op_ref, os_ref, *, nn, final):
    n = pl.program_id(1)
    wd = wd_ref[...]
    tn = wd.shape[1]

    def part(a_ref, h_ref, o_ref):
        for k in range(nn):
            @pl.when(n == k)
            def _():
                sl = slice(k * tn, (k + 1) * tn)
                o_ref[:, sl] = h_ref[:, sl] + 0.5 * jnp.dot(a_ref[...], wd, preferred_element_type=F32)

        if final:
            @pl.when(n == nn - 1)
            def _():
                o_ref[...] = _rms(o_ref[...], fg_ref[...])

    part(ap_ref, hp_ref, op_ref)

    @pl.when(pl.program_id(0) == 0)
    def _():
        part(as_ref, hs_ref, os_ref)


def _ffn_split(hp, hs, g, wg, wu, wd, final_g=None, tm_up=1024, tf=512, tm_down=512, tn_down=512):
    mp, d = hp.shape
    ms = hs.shape[0]
    d_ff = wg.shape[1]
    nf = pl.cdiv(d_ff, tf)
    col = lambda f: pl.multiple_of(jnp.minimum(f * tf, d_ff - tf), LANES)
    nrow = mp // tm_up
    act_p, act_s = pl.pallas_call(
        _ffn_up_kernel,
        grid=(nrow, nf),
        in_specs=[
            pl.BlockSpec((tm_up, d), lambda i, f: (i, 0)),
            pl.BlockSpec((ms, d), lambda i, f: (0, 0), pipeline_mode=pl.Buffered(1)),
            pl.BlockSpec((1, d), lambda i, f: (0, 0)),
            pl.BlockSpec((pl.Element(d), pl.Element(tf)), lambda i, f: (0, col(f))),
            pl.BlockSpec((pl.Element(d), pl.Element(tf)), lambda i, f: (0, col(f))),
        ],
        out_specs=[
            pl.BlockSpec((pl.Element(tm_up), pl.Element(tf)), lambda i, f: (pl.multiple_of(i * tm_up, tm_up), col(f))),
            pl.BlockSpec((pl.Element(ms), pl.Element(tf)), lambda i, f: (0, col(jnp.where(i == 0, f, nf - 1)))),
        ],
        out_shape=[jax.ShapeDtypeStruct((mp, d_ff), BF16), jax.ShapeDtypeStruct((ms, d_ff), BF16)],
        scratch_shapes=[pltpu.VMEM((tm_up, d), BF16), pltpu.VMEM((ms, d), BF16)],
        compiler_params=_cparams(("arbitrary", "arbitrary")),
        name="ffn_up",
    )(hp, hs, g, wg, wu)
    final = final_g is not None
    once = pl.Buffered(1)
    nn = d // tn_down
    return pl.pallas_call(
        functools.partial(_ffn_down_kernel, nn=nn, final=final),
        grid=(mp // tm_down, nn),
        in_specs=[
            pl.BlockSpec((tm_down, d_ff), lambda i, n: (i, 0)),
            pl.BlockSpec((ms, d_ff), lambda i, n: (0, 0), pipeline_mode=once),
            pl.BlockSpec((tm_down, d), lambda i, n: (i, 0)),
            pl.BlockSpec((ms, d), lambda i, n: (0, 0), pipeline_mode=once),
            pl.BlockSpec((1, d), lambda i, n: (0, 0)),
            pl.BlockSpec((d_ff, tn_down), lambda i, n: (0, n)),
        ],
        out_specs=[pl.BlockSpec((tm_down, d), lambda i, n: (i, 0)), pl.BlockSpec((ms, d), lambda i, n: (0, 0))],
        out_shape=[jax.ShapeDtypeStruct((mp, d), F32), jax.ShapeDtypeStruct((ms, d), F32)],
        compiler_params=_cparams(("arbitrary", "arbitrary")),
        name="ffn_down",
    )(act_p, act_s, hp, hs, g if final_g is None else final_g, wd)


def _norm_matmul_kernel(h_ref, g_ref, wt_ref, o_ref, xn_ref):
    @pl.when(pl.program_id(1) == 0)
    def _():
        xn_ref[...] = _rms(h_ref[...], g_ref[...]).astype(BF16)

    o_ref[...] = lax.dot_general(xn_ref[...], wt_ref[...], NT_DIMS, preferred_element_type=F32)


def _norm_matmul(h, g, wt, tm, tn):
    m, d = h.shape
    n = wt.shape[0]
    return pl.pallas_call(
        _norm_matmul_kernel,
        grid=(m // tm, n // tn),
        in_specs=[
            pl.BlockSpec((tm, d), lambda i, j: (i, 0)),
            pl.BlockSpec((1, d), lambda i, j: (0, 0)),
            pl.BlockSpec((tn, d), lambda i, j: (j, 0)),
        ],
        out_specs=pl.BlockSpec((tm, tn), lambda i, j: (i, j)),
        out_shape=jax.ShapeDtypeStruct((m, n), F32),
        scratch_shapes=[pltpu.VMEM((tm, d), BF16)],
        compiler_params=_cparams(("parallel", "arbitrary")),
        name="norm_matmul",
    )(h, g, wt)


def _in_proj_kernel(hp_ref, hs_ref, g_ref, wt_ref, wm_ref, op_ref, os_ref, xp_ref, xs_ref, *, n_direct):
    i = pl.program_id(0)
    j = pl.program_id(1)

    @pl.when(j == 0)
    def _():
        xp_ref[...] = _rms(hp_ref[...], g_ref[...]).astype(BF16)

    @pl.when((j == 0) & (i == 0))
    def _():
        xs_ref[...] = _rms(hs_ref[...], g_ref[...]).astype(BF16)

    def project(w):
        op_ref[...] = lax.dot_general(xp_ref[...], w, NT_DIMS, preferred_element_type=F32)

        @pl.when(i == 0)
        def _():
            os_ref[...] = lax.dot_general(xs_ref[...], w, NT_DIMS, preferred_element_type=F32)

    @pl.when(j < n_direct)
    def _():
        project(wt_ref[...])

    @pl.when(j >= n_direct)
    def _():
        project(wm_ref[...])


def _in_proj(hp, hs, g, wt, segments, w_misc, tm, tn):
    mp, d = hp.shape
    ms = hs.shape[0]
    tiles = [length // tn for _, length in segments]
    assert all(length % tn == 0 and start % SUBLANES == 0 for start, length in segments)
    n_direct = sum(tiles)
    n_misc = w_misc.shape[0] // tn
    nj = n_direct + n_misc

    def wt_row(j):
        jj = jnp.minimum(j, n_direct - 1)
        first = n_direct - tiles[-1]
        row = segments[-1][0] + (jj - first) * tn
        for (start, _), nt in zip(reversed(segments[:-1]), reversed(tiles[:-1])):
            first -= nt
            row = jnp.where(jj < first + nt, start + (jj - first) * tn, row)
        return pl.multiple_of(row, SUBLANES)

    return pl.pallas_call(
        functools.partial(_in_proj_kernel, n_direct=n_direct),
        grid=(mp // tm, nj),
        in_specs=[
            pl.BlockSpec((tm, d), lambda i, j: (i, 0)),
            pl.BlockSpec((ms, d), lambda i, j: (0, 0), pipeline_mode=pl.Buffered(1)),
            pl.BlockSpec((1, d), lambda i, j: (0, 0)),
            pl.BlockSpec((pl.Element(tn), pl.Element(d)), lambda i, j: (wt_row(j), 0)),
            pl.BlockSpec((tn, d), lambda i, j: (jnp.clip(j - n_direct, 0, n_misc - 1), 0),
                         pipeline_mode=pl.Buffered(1 if n_misc == 1 else 2)),
        ],
        out_specs=[pl.BlockSpec((tm, tn), lambda i, j: (i, j)),
                   pl.BlockSpec((ms, tn), lambda i, j: (0, jnp.where(i == 0, j, nj - 1)))],
        out_shape=[jax.ShapeDtypeStruct((mp, nj * tn), F32), jax.ShapeDtypeStruct((ms, nj * tn), F32)],
        scratch_shapes=[pltpu.VMEM((tm, d), BF16), pltpu.VMEM((ms, d), BF16)],
        compiler_params=_cparams(("arbitrary", "arbitrary")),
        name="in_proj",
    )(hp, hs, g, wt, w_misc)


def _pool_mix(pooled_groups, gw_ref, scale_ref, o_ref):
    for gi, pooled in enumerate(pooled_groups):
        sl = slice(gi * LANES, (gi + 1) * LANES)
        mixed = jnp.dot(pooled.astype(BF16), gw_ref[gi], preferred_element_type=F32)
        o_ref[:, sl] = (mixed * scale_ref[:, sl]).astype(o_ref.dtype)


def _pool_prompt_kernel(zp_ref, gw_ref, scale_ref, o_ref, ext_ref, *, tt):
    t = pl.program_id(1)
    hist = 16

    @pl.when(t == 0)
    def _():
        ext_ref[0:hist, :] = jnp.zeros((hist, ext_ref.shape[1]), F32)

    x = zp_ref[...]
    ext_ref[hist:hist + tt, :] = x
    pos = t * tt + lax.broadcasted_iota(jnp.int32, (tt, LANES), 0)
    groups = []
    for gi, w in enumerate(POOL_WINDOWS):
        sl = slice(gi * LANES, (gi + 1) * LANES)
        acc = x[:, sl]
        for k in range(1, w):
            acc = acc + ext_ref[hist - k:hist - k + tt, sl]
        cnt = jnp.minimum(pos + 1, w).astype(F32)
        groups.append(acc / cnt - x[:, sl])
    _pool_mix(groups, gw_ref, scale_ref, o_ref)
    ext_ref[0:hist, :] = ext_ref[tt:tt + hist, :]


def _pool_prompt(z, nb, t_len, col_blk, gw, scale, tt=1024):
    nt = t_len // tt
    pw = gw.shape[0] * LANES
    return pl.pallas_call(
        functools.partial(_pool_prompt_kernel, tt=tt),
        grid=(nb, nt),
        in_specs=[
            pl.BlockSpec((tt, pw), lambda b, t: (b * nt + t, col_blk)),
            pl.BlockSpec(gw.shape, lambda b, t: (0, 0, 0)),
            pl.BlockSpec((1, pw), lambda b, t: (0, 0)),
        ],
        out_specs=pl.BlockSpec((tt, pw), lambda b, t: (b * nt + t, 0)),
        out_shape=jax.ShapeDtypeStruct((nb * t_len, pw), BF16),
        scratch_shapes=[pltpu.VMEM((tt + 16, pw), F32)],
        compiler_params=_cparams(("parallel", "arbitrary")),
        name="pool_prompt",
    )(z, gw, scale)


def _pool_sample_kernel(zp_ref, buf_ref, gw_ref, scale_ref, o_ref):
    x = zp_ref[...]
    nbuf = buf_ref.shape[0]
    groups = []
    for gi, w in enumerate(POOL_WINDOWS):
        sl = slice(gi * LANES, (gi + 1) * LANES)
        acc = x[:, sl]
        for k in range(1, w):
            acc = acc + buf_ref[nbuf - k, :, sl]
        cnt = float(min(PAST_LEN + 1, w))
        groups.append(acc / cnt - x[:, sl])
    _pool_mix(groups, gw_ref, scale_ref, o_ref)


def _pool_sample(z, col_blk, buf_t, gw, scale):
    nrows = z.shape[0]
    pw = gw.shape[0] * LANES
    return pl.pallas_call(
        _pool_sample_kernel,
        grid=(1,),
        in_specs=[
            pl.BlockSpec((nrows, pw), lambda i: (0, col_blk)),
            pl.BlockSpec(buf_t.shape, lambda i: (0, 0, 0)),
            pl.BlockSpec(gw.shape, lambda i: (0, 0, 0)),
            pl.BlockSpec((1, pw), lambda i: (0, 0)),
        ],
        out_specs=pl.BlockSpec((nrows, pw), lambda i: (0, 0)),
        out_shape=jax.ShapeDtypeStruct((nrows, pw), BF16),
        compiler_params=_cparams(("arbitrary",)),
        name="pool_sample",
    )(z, buf_t, gw, scale)


def _head_sum(x):
    head_a = lax.broadcasted_iota(jnp.int32, (x.shape[0], LANES), 1) < HEAD
    cols = []
    for c in range(x.shape[1] // LANES):
        t = x[:, c * LANES:(c + 1) * LANES]
        sa = jnp.sum(jnp.where(head_a, t, 0.0), axis=-1, keepdims=True)
        sb = jnp.sum(jnp.where(head_a, 0.0, t), axis=-1, keepdims=True)
        cols.append(jnp.where(head_a, sa, sb))
    return jnp.concatenate(cols, axis=1)


def _prep_math(x, prev, p_refs):
    mu_ref, w0_ref, wup_ref, a0_ref, aup_ref, gup_ref, kk_ref, ka_ref, rk_ref = p_refs
    xm = x + (prev - x) * mu_ref[...]
    w = RWKV_W
    r = xm[:, 0:w]
    k = xm[:, w:2 * w]
    v = xm[:, 2 * w:3 * w]
    wl = xm[:, 3 * w:3 * w + 128]
    al = xm[:, 3 * w + 128:3 * w + 256]
    gl = xm[:, 3 * w + 256:3 * w + 512]
    dw = w0_ref[...] + jnp.dot(jnp.tanh(wl).astype(BF16), wup_ref[...], preferred_element_type=F32)
    lw = -EXP_M05 * jax.nn.sigmoid(dw)
    a = jax.nn.sigmoid(a0_ref[...] + jnp.dot(al.astype(BF16), aup_ref[...], preferred_element_type=F32))
    g = jnp.dot(jax.nn.sigmoid(gl).astype(BF16), gup_ref[...], preferred_element_type=F32)
    kk = k * kk_ref[...]
    kk = kk * lax.rsqrt(jnp.maximum(_head_sum(kk * kk), 1e-24))
    kmod = k * (1.0 + (a - 1.0) * ka_ref[...])
    bonus = _head_sum(r * kmod * rk_ref[...]) * v
    return r, lw, kmod, v, -kk, kk * a, g, bonus


def _group_norm_gate(y, bonus, g, lng_ref, lnb_ref):
    d = y - _head_sum(y) * (1.0 / HEAD)
    var = _head_sum(d * d) * (1.0 / HEAD)
    yn = d * lax.rsqrt(var + GN_EPS) * lng_ref[...] + lnb_ref[...]
    return (yn + bonus) * g


def _prep_sample_kernel(rkv_ref, lora_ref, prev_ref, *refs):
    p_refs, out_refs = refs[:9], refs[9:17]
    x = jnp.concatenate([rkv_ref[...], lora_ref[...]], axis=1)
    for o_ref, val in zip(out_refs, _prep_math(x, prev_ref[...], p_refs)):
        o_ref[...] = val


def _prep_param_specs(params):
    zero = (lambda *idx: (0, 0))
    return [pl.BlockSpec(p.shape, zero) for p in params]


def _prep_sample(z, prev, params):
    nrows = z.shape[0]
    out_spec = pl.BlockSpec((nrows, RWKV_W), lambda i: (0, 0))
    return pl.pallas_call(
        _prep_sample_kernel,
        grid=(1,),
        in_specs=[pl.BlockSpec((nrows, RKV_W), lambda i: (0, COL_RKV)),
                  pl.BlockSpec((nrows, LORA_W), lambda i: (0, COL_LORA)),
                  pl.BlockSpec((nrows, ZR_W), lambda i: (0, 0))] + _prep_param_specs(params),
        out_specs=[out_spec] * 8,
        out_shape=[jax.ShapeDtypeStruct((nrows, RWKV_W), F32)] * 8,
        compiler_params=_cparams(("arbitrary",)),
        name="prep_sample",
    )(z, z, prev, *params)


def _split3(x):
    hi = x.astype(BF16)
    rest = x - hi.astype(F32)
    mid = rest.astype(BF16)
    lo = (rest - mid.astype(F32)).astype(BF16)
    return hi, mid, lo


def _select_dot(sel, x):
    sel = sel.astype(BF16)
    hi, mid, lo = _split3(x)
    return _bdot(sel, hi) + (_bdot(sel, mid) + _bdot(sel, lo))


def _chunk_scan(r, lw, k, v, a, b, s_ref, c_len):
    n2 = 2 * c_len
    assert n2 == LANES
    row = lax.broadcasted_iota(jnp.int32, (n2, n2), 0)
    col = lax.broadcasted_iota(jnp.int32, (n2, n2), 1)
    tr = row & (c_len - 1)
    tc = col & (c_len - 1)
    strict = tr > tc
    incl = tr >= tc
    tri = (lax.broadcasted_iota(jnp.int32, (c_len, c_len), 0)
           >= lax.broadcasted_iota(jnp.int32, (c_len, c_len), 1))
    head_a = lax.broadcasted_iota(jnp.int32, (c_len, LANES), 1) < HEAD

    def stack(x):
        return jnp.concatenate([jnp.where(head_a, x, 0.0), jnp.where(head_a, 0.0, x)], axis=0).astype(BF16)

    prs = range(r.shape[1] // LANES)
    cat = jnp.concatenate
    sls = [slice(q * LANES, (q + 1) * LANES) for q in prs]
    lwq = [lw[:, sl] for sl in sls]
    cum = [_select_dot(tri, x) for x in lwq]
    tot = [x[c_len - 1:c_len, :] for x in cum]
    xr = [stack(r[:, sls[q]] * jnp.exp(cum[q])) for q in prs]
    xa = [stack(a[:, sls[q]] * jnp.exp(cum[q] - lwq[q])) for q in prs]
    e_neg = [jnp.exp(-x) for x in cum]
    e_rem = [jnp.exp(tot[q] - cum[q]) for q in prs]
    yb = [stack(b[:, sls[q]] * e_neg[q]) for q in prs]
    yk = [stack(k[:, sls[q]] * e_neg[q]) for q in prs]
    zb = [stack(b[:, sls[q]] * e_rem[q]) for q in prs]
    zk = [stack(k[:, sls[q]] * e_rem[q]) for q in prs]
    vs = [stack(v[:, sl]) for sl in sls]

    g = [_bdot(cat([xa[q], xr[q]], axis=0), cat([yb[q], yk[q]], axis=0), NT_DIMS) for q in prs]
    m_ab = [jnp.where(strict, x[0:n2, 0:n2], 0.0) for x in g]
    m_ak = [jnp.where(strict, x[0:n2, n2:2 * n2], 0.0).astype(BF16) for x in g]
    n_rb = [jnp.where(incl, x[n2:2 * n2, 0:n2], 0.0) for x in g]
    n_rk = [jnp.where(incl, x[n2:2 * n2, n2:2 * n2], 0.0).astype(BF16) for x in g]

    s_old = [s_ref[q] for q in prs]
    lhs = [cat([cat([xa[q], m_ak[q]], axis=1), cat([xr[q], n_rk[q]], axis=1)], axis=0) for q in prs]
    xy0 = [_bdot(lhs[q], cat([s_old[q].T.astype(BF16), vs[q]], axis=0)) for q in prs]

    x = [xy0[q][0:n2] for q in prs]
    mk = m_ab
    nlev = int(math.log2(c_len))
    for lev in range(nlev):
        if lev < nlev - 1:
            res = [_bdot(mk[q], cat([mk[q], x[q]], axis=1)) for q in prs]
            mk = [t[:, 0:n2] for t in res]
            x = [x[q] + res[q][:, n2:2 * n2] for q in prs]
        else:
            x = [x[q] + _bdot(mk[q], x[q]) for q in prs]
    u = [t.astype(BF16) for t in x]

    y_st = [xy0[q][n2:2 * n2] + _bdot(n_rb[q], u[q]) for q in prs]
    for q in prs:
        s_ref[q] = s_old[q] * jnp.exp(tot[q]) + _bdot(cat([u[q], vs[q]], axis=0), cat([zb[q], zk[q]], axis=0),
                                                     TN_DIMS)
    return cat([t[0:c_len] + t[c_len:n2] for t in y_st], axis=1)


def _rwkv_prompt_kernel(rkv_ref, lora_ref, *refs, nc):
    p_refs = refs[:9]
    lng_ref, lnb_ref, yb_ref, sout_ref, s_ref, carry_ref = refs[9:]
    c = pl.program_id(1)
    nseq, c_len, _ = rkv_ref.shape

    @pl.when(c == 0)
    def _():
        s_ref[...] = jnp.zeros_like(s_ref)
        carry_ref[...] = jnp.zeros_like(carry_ref)

    first = lax.broadcasted_iota(jnp.int32, (c_len, ZR_W), 0) == 0
    preps = []
    for q in range(nseq):
        x = jnp.concatenate([rkv_ref[q], lora_ref[q]], axis=1)
        rolled = pltpu.roll(x, 1, axis=0)
        prev = jnp.where(first, carry_ref[q, 0:1, :], rolled)
        carry_ref[q] = rolled[0:SUBLANES, :]
        preps.append(_prep_math(x, prev, p_refs))
    r, lw, k, v, a, b = (jnp.concatenate([p[i] for p in preps], axis=1) for i in range(6))
    y = _chunk_scan(r, lw, k, v, a, b, s_ref, c_len)
    for q in range(nseq):
        yq = y[:, q * RWKV_W:(q + 1) * RWKV_W]
        yb_ref[q] = _group_norm_gate(yq, preps[q][7], preps[q][6], lng_ref, lnb_ref).astype(yb_ref.dtype)

    @pl.when(c == nc - 1)
    def _():
        npair = s_ref.shape[0] // nseq
        for q in range(nseq):
            for p in range(npair):
                s = s_ref[q * npair + p]
                sout_ref[q, 2 * p] = s[0:HEAD, 0:HEAD]
                sout_ref[q, 2 * p + 1] = s[HEAD:2 * HEAD, HEAD:2 * HEAD]


def _rwkv_prompt(z, nb, t_len, params, ln_g, ln_b, c_len=64, nseq=2):
    nc = t_len // c_len
    npair = RWKV_W // LANES
    par = pl.BlockSpec((1, RWKV_W), lambda b, c: (0, 0))
    z3 = z.reshape(nb, t_len, z.shape[1])
    yb, state = pl.pallas_call(
        functools.partial(_rwkv_prompt_kernel, nc=nc),
        grid=(nb // nseq, nc),
        in_specs=[pl.BlockSpec((nseq, c_len, RKV_W), lambda b, c: (b, c, COL_RKV)),
                  pl.BlockSpec((nseq, c_len, LORA_W), lambda b, c: (b, c, COL_LORA))]
        + _prep_param_specs(params) + [par, par],
        out_specs=[pl.BlockSpec((nseq, c_len, RWKV_W), lambda b, c: (b, c, 0)),
                   pl.BlockSpec((nseq, 2 * npair, HEAD, HEAD), lambda b, c: (b, 0, 0, 0))],
        out_shape=[jax.ShapeDtypeStruct((nb, t_len, RWKV_W), BF16),
                   jax.ShapeDtypeStruct((nb, 2 * npair, HEAD, HEAD), F32)],
        scratch_shapes=[pltpu.VMEM((nseq * npair, LANES, LANES), F32), pltpu.VMEM((nseq, SUBLANES, ZR_W), F32)],
        compiler_params=_cparams(("parallel", "arbitrary")),
        name="rwkv_prompt",
    )(z3, z3, *params, ln_g, ln_b)
    return yb.reshape(nb * t_len, RWKV_W), state


def _wkv_step_kernel(s_ref, r_ref, lw_ref, k_ref, v_ref, a_ref, b_ref, g_ref, bonus_ref, lng_ref, lnb_ref,
                     yb_ref, so_ref, vt_scr, y_scr):
    heads = s_ref.shape[0]
    rt, wt, kt, at, bt = (x[...].T for x in (r_ref, lw_ref, k_ref, a_ref, b_ref))
    wt = jnp.exp(wt)
    vt_scr[...] = v_ref[...].T
    for hh in range(heads):
        rows = slice(hh * HEAD, (hh + 1) * HEAD)
        r, w, k, a, b = (x[rows, :] for x in (rt, wt, kt, at, bt))

        def body(i, carry):
            si = s_ref[hh, i]
            sa = jnp.sum(si * a, axis=0, keepdims=True)
            vi = vt_scr[pl.ds(hh * HEAD + i, 1), :]
            s2 = si * w + sa * b + vi * k
            so_ref[hh, i] = s2
            y_scr[pl.ds(hh * HEAD + i, 1), :] = jnp.sum(s2 * r, axis=0, keepdims=True)
            return carry

        lax.fori_loop(0, HEAD, body, 0, unroll=4)

    outs = []
    for hh in range(heads):
        y = y_scr[hh * HEAD:(hh + 1) * HEAD, :]
        d = y - jnp.mean(y, axis=0, keepdims=True)
        var = jnp.mean(d * d, axis=0, keepdims=True)
        outs.append(d * lax.rsqrt(var + GN_EPS))
    yn = jnp.concatenate(outs, axis=0).T * lng_ref[...] + lnb_ref[...]
    yb_ref[...] = ((yn + bonus_ref[...]) * g_ref[...]).astype(yb_ref.dtype)


def _wkv_step(state_t, vecs, ln_g, ln_b, heads=4):
    nh, _, _, nb = state_t.shape
    st = pl.BlockSpec((heads, HEAD, HEAD, nb), lambda p: (p, 0, 0, 0))
    vec = pl.BlockSpec((nb, heads * HEAD), lambda p: (0, p))
    par = pl.BlockSpec((1, heads * HEAD), lambda p: (0, p))
    return pl.pallas_call(
        _wkv_step_kernel,
        grid=(nh // heads,),
        in_specs=[st] + [vec] * 8 + [par, par],
        out_specs=[vec, st],
        out_shape=[jax.ShapeDtypeStruct((nb, nh * HEAD), BF16), jax.ShapeDtypeStruct(state_t.shape, F32)],
        scratch_shapes=[pltpu.VMEM((heads * HEAD, nb), F32), pltpu.VMEM((heads * HEAD, nb), F32)],
        compiler_params=_cparams(("parallel",)),
        name="wkv_step",
    )(state_t, *vecs, ln_g, ln_b)


def _xattn_prompt_kernel(q_ref, k_ref, v_ref, o_ref):
    scale = XA_DIM ** -0.5
    q = q_ref[...]
    for h in range(XA_HEADS):
        sl = slice(h * XA_DIM, (h + 1) * XA_DIM)
        s = lax.dot_general(q[:, sl].astype(BF16), k_ref[0, :, sl].astype(BF16),
                            NT_DIMS, preferred_element_type=F32) * scale
        p = jnp.exp(s - jnp.max(s, axis=-1, keepdims=True))
        den = jnp.sum(p, axis=-1, keepdims=True)
        o = jnp.dot(p.astype(BF16), v_ref[0, :, sl].astype(BF16), preferred_element_type=F32)
        o_ref[:, sl] = (o / den).astype(o_ref.dtype)


def _xattn_prompt(z, nb, t_len, col_blk, mk, mv, tq=1024):
    nt = t_len // tq
    xw = XA_HEADS * XA_DIM
    nmem = mk.shape[1]
    kv = pl.BlockSpec((1, nmem, xw), lambda b, t: (b, 0, 0))
    return pl.pallas_call(
        _xattn_prompt_kernel,
        grid=(nb, nt),
        in_specs=[pl.BlockSpec((tq, xw), lambda b, t: (b * nt + t, col_blk)), kv, kv],
        out_specs=pl.BlockSpec((tq, xw), lambda b, t: (b * nt + t, 0)),
        out_shape=jax.ShapeDtypeStruct((nb * t_len, xw), BF16),
        compiler_params=_cparams(("parallel", "parallel")),
        name="xattn_prompt",
    )(z, mk, mv)


def _xattn_sample_kernel(q_ref, k_ref, v_ref, o_ref):
    bb = q_ref.shape[0]
    nrow = k_ref.shape[1] // SUBLANES
    full = (bb, nrow, SUBLANES, XA_DIM)
    q = q_ref[...] * (XA_DIM ** -0.5)
    q8 = jnp.concatenate([q, q], axis=1)[:, None]
    k = k_ref[...].reshape(full)
    s = jnp.broadcast_to(jnp.sum(k * q8, axis=-1, keepdims=True), full)
    mx = jnp.max(s, axis=1, keepdims=True)
    mx = jnp.maximum(mx, pltpu.roll(mx, XA_HEADS, axis=2))
    p = jnp.exp(s - mx)
    den = jnp.sum(p, axis=1, keepdims=True)
    den = den + pltpu.roll(den, XA_HEADS, axis=2)
    o = jnp.sum(p * v_ref[...].reshape(full), axis=1, keepdims=True)
    o = o + pltpu.roll(o, XA_HEADS, axis=2)
    o_ref[...] = (o / den)[:, 0, 0:XA_HEADS, :].astype(o_ref.dtype)


def _xattn_sample(q3, mk, mv, bb=16):
    nb, rows, _ = mk.shape
    kv = pl.BlockSpec((bb, rows, XA_DIM), lambda i: (i, 0, 0))
    qs = pl.BlockSpec((bb, XA_HEADS, XA_DIM), lambda i: (i, 0, 0))
    return pl.pallas_call(
        _xattn_sample_kernel,
        grid=(nb // bb,),
        in_specs=[qs, kv, kv],
        out_specs=qs,
        out_shape=jax.ShapeDtypeStruct((nb, XA_HEADS, XA_DIM), BF16),
        compiler_params=_cparams(("parallel",)),
        name="xattn_sample",
    )(q3, mk, mv)


def _merge_kernel(pa_ref, pb_ref, pc_ref, g0_ref, g1_ref, g2_ref, h_ref, wa_ref, wb_ref, wc_ref, wo_ref, o_ref):
    oa = jnp.dot(pa_ref[...], wa_ref[...], preferred_element_type=F32)
    ob = jnp.dot(pb_ref[...], wb_ref[...], preferred_element_type=F32)
    oc = jnp.dot(pc_ref[...], wc_ref[...], preferred_element_type=F32)
    merged = (jax.nn.sigmoid(g0_ref[...]) * oa + jax.nn.sigmoid(g1_ref[...]) * ob
              + jax.nn.sigmoid(g2_ref[...]) * oc)
    o_ref[...] = h_ref[...] + jnp.dot(merged.astype(BF16), wo_ref[...], preferred_element_type=F32)


def _merge(pa, pb, pc, z, zg_blk0, h, tm, wa, wb, wc, wo):
    m, d = h.shape
    const = lambda i: (0, 0)
    resident = lambda w: pl.BlockSpec(w.shape, const, pipeline_mode=pl.Buffered(1))
    in_specs = [
        pl.BlockSpec((tm, pa.shape[1]), lambda i: (i, 0)),
        pl.BlockSpec((tm, pb.shape[1]), lambda i: (i, 0)),
        pl.BlockSpec((tm, pc.shape[1]), lambda i: (i, 0)),
        pl.BlockSpec((tm, d), lambda i: (i, zg_blk0)),
        pl.BlockSpec((tm, d), lambda i: (i, zg_blk0 + 1)),
        pl.BlockSpec((tm, d), lambda i: (i, zg_blk0 + 2)),
        pl.BlockSpec((tm, d), lambda i: (i, 0)),
        resident(wa), resident(wb), resident(wc), resident(wo),
    ]
    return pl.pallas_call(
        _merge_kernel,
        grid=(m // tm,),
        in_specs=in_specs,
        out_specs=pl.BlockSpec((tm, d), lambda i: (i, 0)),
        out_shape=jax.ShapeDtypeStruct((m, d), F32),
        compiler_params=_cparams(("parallel",)),
        name="merge",
    )(pa, pb, pc, z, z, z, h, wa, wb, wc, wo)


def _pack_lora(x, axis):
    take = lambda a, b: lax.slice_in_dim(x, a, b, axis=axis)

    def pad(n):
        shape = list(x.shape)
        shape[axis] = n
        return jnp.zeros(shape, x.dtype)

    return jnp.concatenate([take(0, 64), pad(64), take(64, 128), pad(64), take(128, ZR_TRUE - RKV_W), pad(96)],
                           axis=axis)


def _pack_zr(x, axis):
    return jnp.concatenate([lax.slice_in_dim(x, 0, RKV_W, axis=axis),
                            _pack_lora(lax.slice_in_dim(x, RKV_W, ZR_TRUE, axis=axis), axis)], axis=axis)


def _unpack_zr_cols(x):
    w = RWKV_W
    return jnp.concatenate([x[..., :3 * w + 64], x[..., 3 * w + 128:3 * w + 192], x[..., 3 * w + 256:3 * w + 416]],
                           axis=-1)


def _pad_rows(x, n):
    return jnp.concatenate([x, jnp.zeros((n - x.shape[0],) + x.shape[1:], x.dtype)], axis=0)


def kernel(x_prompt, x_sample, mem_prompt, cache_mem_k, cache_mem_v, state_wkv, state_shift, state_pool,
           ffn1_norm_g, ffn1_w_gate, ffn1_w_up, ffn1_w_down, mix_norm_g, w_in,
           pool_group_w, pool_scale, pool_out,
           rwkv_mu, rwkv_w0, rwkv_w_up, rwkv_a0, rwkv_a_up, rwkv_g_up, rwkv_k_k, rwkv_k_a, rwkv_r_k,
           rwkv_ln_g, rwkv_ln_b, rwkv_out,
           mem_norm_g, w_mem_k, w_mem_v, xattn_out, w_o,
           ffn2_norm_g, ffn2_w_gate, ffn2_w_up, ffn2_w_down, final_norm_g):
    nb, t_len, d = x_prompt.shape
    ns = x_sample.shape[0]
    assert w_in.shape[0] == 1 and x_sample.shape[1] == 1
    n_mem = mem_prompt.shape[1]
    pool_w = pool_out.shape[1]
    xa_w = xattn_out.shape[1]
    n_heads = RWKV_W // HEAD
    nbuf = state_pool.shape[2]
    rows_p = nb * t_len
    l = 0

    f1 = [w.reshape(w.shape[1:]).astype(BF16) for w in (ffn1_w_gate, ffn1_w_up, ffn1_w_down)]
    f2 = [w.reshape(w.shape[1:]).astype(BF16) for w in (ffn2_w_gate, ffn2_w_up, ffn2_w_down)]
    wit = jnp.swapaxes(w_in[l], 0, 1).astype(BF16)
    o_zr, o_zq, o_zg = pool_w, pool_w + ZR_TRUE, pool_w + ZR_TRUE + xa_w
    assert (pool_w, xa_w, 3 * d) == (LORA_W, LORA_W, GATES_W)
    w_misc = jnp.concatenate([_pack_lora(wit[o_zr + RKV_W:o_zq], 0), wit[:o_zr], wit[o_zq:o_zg]], axis=0)
    in_segments = [(o_zg, GATES_W), (o_zr, RKV_W)]
    o_rkv, o_zp_out, o_zq_out = GATES_W, GATES_W + ZR_W, GATES_W + ZR_W + pool_w
    row = lambda v: v.reshape(1, -1)
    prep_params = [row(_pack_zr(rwkv_mu[l], 0)), row(rwkv_w0[l]), _pad_rows(rwkv_w_up[l], 128).astype(BF16),
                   row(rwkv_a0[l]), _pad_rows(rwkv_a_up[l], 128).astype(BF16),
                   _pad_rows(rwkv_g_up[l], 256).astype(BF16), row(rwkv_k_k[l]), row(rwkv_k_a[l]),
                   row(rwkv_r_k[l])]
    gw = pool_group_w[l].astype(BF16)
    w_kv_t = jnp.concatenate([jnp.swapaxes(w_mem_k[l], 0, 1), jnp.swapaxes(w_mem_v[l], 0, 1)], axis=0).astype(BF16)
    wa, wb, wc, wo = (pool_out[l].astype(BF16), rwkv_out[l].astype(BF16), xattn_out[l].astype(BF16),
                      w_o[l].astype(BF16))
    g1, gm, g2, fg = row(ffn1_norm_g[l]), row(mix_norm_g[l]), row(ffn2_norm_g[l]), row(final_norm_g)
    scale = row(pool_scale[l])
    ln_g, ln_b = rwkv_ln_g[l], rwkv_ln_b[l]

    tm_p = 512
    h1_p, h1_s = _ffn_split(x_prompt.reshape(rows_p, d), x_sample.reshape(ns, d), g1, *f1)
    z_p, z_s = _in_proj(h1_p, h1_s, gm, wit, in_segments, w_misc, tm=1024, tn=1536)
    kv = _norm_matmul(mem_prompt.reshape(nb * n_mem, d), row(mem_norm_g[l]), w_kv_t, tm=512, tn=512)
    mk_p = kv[:, :xa_w].reshape(nb, n_mem, xa_w)
    mv_p = kv[:, xa_w:].reshape(nb, n_mem, xa_w)

    pool_state = state_pool.reshape(ns, nbuf, pool_w)
    pa_p = _pool_prompt(z_p, nb, t_len, COL_ZP, gw, scale)
    pa_s = _pool_sample(z_s, COL_ZP, jnp.swapaxes(pool_state, 0, 1), gw, scale)

    pb_p, st_p = _rwkv_prompt(z_p, nb, t_len, prep_params, row(ln_g), row(ln_b))
    prep_s = _prep_sample(z_s, _pack_zr(state_shift.reshape(ns, ZR_TRUE), 1), prep_params)
    state_t = jnp.transpose(state_wkv.reshape(ns, n_heads, HEAD, HEAD), (1, 2, 3, 0))
    pb_s, wkv_s_t = _wkv_step(state_t, prep_s, row(ln_g), row(ln_b))

    pc_p = _xattn_prompt(z_p, nb, t_len, COL_ZQ, mk_p, mv_p)
    q_s = z_s[:, o_zq_out:].reshape(ns, XA_HEADS, XA_DIM)
    pc_s = _xattn_sample(q_s, cache_mem_k.reshape(ns, n_mem * XA_HEADS, XA_DIM),
                         cache_mem_v.reshape(ns, n_mem * XA_HEADS, XA_DIM)).reshape(ns, xa_w)

    h2_p = _merge(pa_p, pb_p, pc_p, z_p, 0, h1_p, 256, wa, wb, wc, wo)
    h2_s = _merge(pa_s, pb_s, pc_s, z_s, 0, h1_s, ns, wa, wb, wc, wo)
    y_prompt, y_sample = _ffn_split(h2_p, h2_s, g2, *f2, final_g=fg)

    ends = [(b + 1) * t_len for b in range(nb)]
    shift_p = _unpack_zr_cols(jnp.stack([z_p[e - 1:e, o_rkv:o_zp_out] for e in ends]))[None]
    pool_p = jnp.stack([z_p[e - nbuf:e, o_zp_out:o_zq_out] for e in ends])[None]
    shift_s = _unpack_zr_cols(z_s[:, o_rkv:o_zp_out])[None, :, None, :]
    pool_s = jnp.concatenate([pool_state[:, 1:], z_s[:, None, o_zp_out:o_zq_out]], axis=1)[None]
    wkv_p = st_p[None]
    wkv_s = jnp.transpose(wkv_s_t, (3, 0, 1, 2)).reshape(state_wkv.shape)
    mem_k_p = mk_p.reshape(1, nb, n_mem, XA_HEADS, XA_DIM)
    mem_v_p = mv_p.reshape(1, nb, n_mem, XA_HEADS, XA_DIM)
    return (y_prompt.reshape(nb, t_len, d), y_sample.reshape(ns, 1, d), mem_k_p, mem_v_p, wkv_p, shift_p, pool_p,
            wkv_s, shift_s, pool_s)
```

```python
import functools
import math

import jax
import jax.numpy as jnp
from jax import lax
from jax.experimental import pallas as pl
from jax.experimental.pallas import tpu as pltpu

F32 = jnp.float32
BF16 = jnp.bfloat16

RMS_EPS = 1e-6
GN_EPS = 64e-5
POOL_WINDOWS = (2, 4, 8, 16)
HEAD = 64
LANES = 128
SUBLANES = 8
XA_HEADS = 4
XA_DIM = 128
PAST_LEN = 16384
VMEM_LIMIT = 62 * 1024 * 1024
EXP_M05 = math.exp(-0.5)

RWKV_W = 1024
ZR_TRUE = 3360
RKV_W = 3 * RWKV_W
LORA_W = 512
ZR_W = RKV_W + LORA_W
GATES_W = 6144
COL_RKV = GATES_W // RKV_W
COL_LORA = (GATES_W + RKV_W) // LORA_W
COL_ZP = COL_LORA + 1
COL_ZQ = COL_LORA + 2

NN_DIMS = (((1,), (0,)), ((), ()))
NT_DIMS = (((1,), (1,)), ((), ()))
TN_DIMS = (((0,), (0,)), ((), ()))


def _cparams(sem):
    return pltpu.CompilerParams(dimension_semantics=sem, vmem_limit_bytes=VMEM_LIMIT)


def _rms(x, g):
    ms = jnp.mean(x * x, axis=-1, keepdims=True)
    return x * lax.rsqrt(ms + RMS_EPS) * g


def _bdot(a, b, dims=NN_DIMS):
    return lax.dot_general(a.astype(BF16), b.astype(BF16), dims, preferred_element_type=F32)


RING = 3

def _ffn_kernel(hp_ref, hs_ref, g_ref, fg_ref, wg_ref, wu_ref, wd_ref, wgt_ref, wut_ref, wdt_ref,
                op_ref, os_ref, xp_ref, xs_ref, ap_ref, as_ref, wd_buf, wd_sem, *, nfull, final):
    m = pl.program_id(0)
    f = pl.program_id(1)
    tf = wd_buf.shape[1]
    n_main = pl.num_programs(0) * nfull
    step = m * nfull + f

    def wd_copy(n):
        blk = lax.rem(n, nfull)
        slot = lax.rem(n, RING)
        return pltpu.make_async_copy(wd_ref.at[pl.ds(pl.multiple_of(blk * tf, tf), tf), :], wd_buf.at[slot],
                                     wd_sem.at[slot])

    @pl.when((m == 0) & (f == 0))
    def _():
        for n in range(RING):
            wd_copy(n).start()

    def start(h_ref, x_ref, acc_ref):
        x_ref[...] = _rms(h_ref[...], g_ref[...]).astype(BF16)
        acc_ref[...] = jnp.zeros_like(acc_ref)

    def contribution(x_ref, wg, wu, wd):
        xn = x_ref[...]
        gate = jnp.dot(xn, wg, preferred_element_type=F32)
        up = jnp.dot(xn, wu, preferred_element_type=F32)
        act = (gate * jax.nn.sigmoid(gate) * up).astype(BF16)
        return jnp.dot(act, wd, preferred_element_type=F32)

    def finish(h_ref, acc_ref, o_ref, last):
        out = h_ref[...] + 0.5 * (acc_ref[...] + last)
        if final:
            out = _rms(out, fg_ref[...])
        o_ref[...] = out

    @pl.when(f == 0)
    def _():
        start(hp_ref, xp_ref, ap_ref)

    @pl.when((f == 0) & (m == 0))
    def _():
        start(hs_ref, xs_ref, as_ref)

    @pl.when(f < nfull)
    def _():
        wd_copy(step).wait()
        wg, wu, wd = wg_ref[...], wu_ref[...], wd_buf[lax.rem(step, RING)]
        ap_ref[...] += contribution(xp_ref, wg, wu, wd)

        @pl.when(m == 0)
        def _():
            as_ref[...] += contribution(xs_ref, wg, wu, wd)

        @pl.when(step + RING < n_main)
        def _():
            wd_copy(step + RING).start()

    @pl.when(f == nfull)
    def _():
        wg, wu, wd = wgt_ref[...], wut_ref[...], wdt_ref[...]
        finish(hp_ref, ap_ref, op_ref, contribution(xp_ref, wg, wu, wd))

        @pl.when(m == 0)
        def _():
            finish(hs_ref, as_ref, os_ref, contribution(xs_ref, wg, wu, wd))


def _ffn(hp, hs, tm, g, wg, wu, wd, final_g=None, tf=512):
    mp, d = hp.shape
    ms = hs.shape[0]
    d_ff = wg.shape[1]
    nfull, tail = divmod(d_ff, tf)
    assert tail > 0 and tail % LANES == 0 and mp % tm == 0
    last_main = nfull - 1
    split = nfull * tf
    final = final_g is not None
    vec = pl.BlockSpec((1, d), lambda i, f: (0, 0))
    once = pl.Buffered(1)
    in_specs = [
        pl.BlockSpec((tm, d), lambda i, f: (i, 0)),
        pl.BlockSpec((ms, d), lambda i, f: (0, 0), pipeline_mode=once),
        vec, vec,
        pl.BlockSpec((d, tf), lambda i, f: (0, jnp.minimum(f, last_main))),
        pl.BlockSpec((d, tf), lambda i, f: (0, jnp.minimum(f, last_main))),
        pl.BlockSpec(memory_space=pl.ANY),
        pl.BlockSpec((pl.Element(d), pl.Element(tail)), lambda i, f: (0, split), pipeline_mode=once),
        pl.BlockSpec((pl.Element(d), pl.Element(tail)), lambda i, f: (0, split), pipeline_mode=once),
        pl.BlockSpec((pl.Element(tail), pl.Element(d)), lambda i, f: (split, 0), pipeline_mode=once),
    ]
    return pl.pallas_call(
        functools.partial(_ffn_kernel, nfull=nfull, final=final),
        grid=(mp // tm, nfull + 1),
        in_specs=in_specs,
        out_specs=[pl.BlockSpec((tm, d), lambda i, f: (i, 0)), pl.BlockSpec((ms, d), lambda i, f: (0, 0))],
        out_shape=[jax.ShapeDtypeStruct((mp, d), F32), jax.ShapeDtypeStruct((ms, d), F32)],
        scratch_shapes=[pltpu.VMEM((tm, d), BF16), pltpu.VMEM((ms, d), BF16),
                        pltpu.VMEM((tm, d), F32), pltpu.VMEM((ms, d), F32),
                        pltpu.VMEM((RING, tf, d), BF16), pltpu.SemaphoreType.DMA((RING,))],
        compiler_params=_cparams(("arbitrary", "arbitrary")),
        name="ffn",
    )(hp, hs, g, g if final_g is None else final_g, wg, wu, wd, wg, wu, wd)


def _norm_matmul_kernel(h_ref, g_ref, wt_ref, o_ref, xn_ref):
    @pl.when(pl.program_id(1) == 0)
    def _():
        xn_ref[...] = _rms(h_ref[...], g_ref[...]).astype(BF16)

    o_ref[...] = lax.dot_general(xn_ref[...], wt_ref[...], NT_DIMS, preferred_element_type=F32)


def _norm_matmul(h, g, wt, tm, tn):
    m, d = h.shape
    n = wt.shape[0]
    return pl.pallas_call(
        _norm_matmul_kernel,
        grid=(m // tm, n // tn),
        in_specs=[
            pl.BlockSpec((tm, d), lambda i, j: (i, 0)),
            pl.BlockSpec((1, d), lambda i, j: (0, 0)),
            pl.BlockSpec((tn, d), lambda i, j: (j, 0)),
        ],
        out_specs=pl.BlockSpec((tm, tn), lambda i, j: (i, j)),
        out_shape=jax.ShapeDtypeStruct((m, n), F32),
        scratch_shapes=[pltpu.VMEM((tm, d), BF16)],
        compiler_params=_cparams(("parallel", "arbitrary")),
        name="norm_matmul",
    )(h, g, wt)


def _in_proj_kernel(hp_ref, hs_ref, g_ref, wt_ref, wm_ref, op_ref, os_ref, xp_ref, xs_ref, *, n_direct):
    i = pl.program_id(0)
    j = pl.program_id(1)

    @pl.when(j == 0)
    def _():
        xp_ref[...] = _rms(hp_ref[...], g_ref[...]).astype(BF16)

    @pl.when((j == 0) & (i == 0))
    def _():
        xs_ref[...] = _rms(hs_ref[...], g_ref[...]).astype(BF16)

    def project(w):
        op_ref[...] = lax.dot_general(xp_ref[...], w, NT_DIMS, preferred_element_type=F32)

        @pl.when(i == 0)
        def _():
            os_ref[...] = lax.dot_general(xs_ref[...], w, NT_DIMS, preferred_element_type=F32)

    @pl.when(j < n_direct)
    def _():
        project(wt_ref[...])

    @pl.when(j >= n_direct)
    def _():
        project(wm_ref[...])


def _in_proj(hp, hs, g, wt, segments, w_misc, tm, tn):
    mp, d = hp.shape
    ms = hs.shape[0]
    tiles = [length // tn for _, length in segments]
    assert all(length % tn == 0 and start % SUBLANES == 0 for start, length in segments)
    n_direct = sum(tiles)
    n_misc = w_misc.shape[0] // tn
    nj = n_direct + n_misc

    def wt_row(j):
        jj = jnp.minimum(j, n_direct - 1)
        first = n_direct - tiles[-1]
        row = segments[-1][0] + (jj - first) * tn
        for (start, _), nt in zip(reversed(segments[:-1]), reversed(tiles[:-1])):
            first -= nt
            row = jnp.where(jj < first + nt, start + (jj - first) * tn, row)
        return pl.multiple_of(row, SUBLANES)

    return pl.pallas_call(
        functools.partial(_in_proj_kernel, n_direct=n_direct),
        grid=(mp // tm, nj),
        in_specs=[
            pl.BlockSpec((tm, d), lambda i, j: (i, 0)),
            pl.BlockSpec((ms, d), lambda i, j: (0, 0), pipeline_mode=pl.Buffered(1)),
            pl.BlockSpec((1, d), lambda i, j: (0, 0)),
            pl.BlockSpec((pl.Element(tn), pl.Element(d)), lambda i, j: (wt_row(j), 0)),
            pl.BlockSpec((tn, d), lambda i, j: (jnp.clip(j - n_direct, 0, n_misc - 1), 0),
                         pipeline_mode=pl.Buffered(1 if n_misc == 1 else 2)),
        ],
        out_specs=[pl.BlockSpec((tm, tn), lambda i, j: (i, j)),
                   pl.BlockSpec((ms, tn), lambda i, j: (0, jnp.where(i == 0, j, nj - 1)))],
        out_shape=[jax.ShapeDtypeStruct((mp, nj * tn), F32), jax.ShapeDtypeStruct((ms, nj * tn), F32)],
        scratch_shapes=[pltpu.VMEM((tm, d), BF16), pltpu.VMEM((ms, d), BF16)],
        compiler_params=_cparams(("arbitrary", "arbitrary")),
        name="in_proj",
    )(hp, hs, g, wt, w_misc)


def _pool_mix(pooled_groups, gw_ref, scale_ref, o_ref):
    for gi, pooled in enumerate(pooled_groups):
        sl = slice(gi * LANES, (gi + 1) * LANES)
        mixed = jnp.dot(pooled.astype(BF16), gw_ref[gi], preferred_element_type=F32)
        o_ref[:, sl] = (mixed * scale_ref[:, sl]).astype(o_ref.dtype)


POOL_HIST = 16


def _pool_prompt_tile(x, t, ext_ref, gw_ref, scale_ref):
    tt = x.shape[0]
    hist = POOL_HIST

    @pl.when(t == 0)
    def _():
        ext_ref[0:hist, :] = jnp.zeros((hist, ext_ref.shape[1]), F32)

    ext_ref[hist:hist + tt, :] = x
    pos = t * tt + lax.broadcasted_iota(jnp.int32, (tt, LANES), 0)
    cols = []
    for gi, w in enumerate(POOL_WINDOWS):
        sl = slice(gi * LANES, (gi + 1) * LANES)
        acc = x[:, sl]
        for k in range(1, w):
            acc = acc + ext_ref[hist - k:hist - k + tt, sl]
        cnt = jnp.minimum(pos + 1, w).astype(F32)
        pooled = acc / cnt - x[:, sl]
        mixed = jnp.dot(pooled.astype(BF16), gw_ref[gi], preferred_element_type=F32)
        cols.append((mixed * scale_ref[:, sl]).astype(BF16))
    ext_ref[0:hist, :] = ext_ref[tt:tt + hist, :]
    return jnp.concatenate(cols, axis=1)


def _pool_sample_kernel(zp_ref, buf_ref, gw_ref, scale_ref, o_ref):
    x = zp_ref[...]
    nbuf = buf_ref.shape[0]
    groups = []
    for gi, w in enumerate(POOL_WINDOWS):
        sl = slice(gi * LANES, (gi + 1) * LANES)
        acc = x[:, sl]
        for k in range(1, w):
            acc = acc + buf_ref[nbuf - k, :, sl]
        cnt = float(min(PAST_LEN + 1, w))
        groups.append(acc / cnt - x[:, sl])
    _pool_mix(groups, gw_ref, scale_ref, o_ref)


def _pool_sample(z, col_blk, buf_t, gw, scale):
    nrows = z.shape[0]
    pw = gw.shape[0] * LANES
    return pl.pallas_call(
        _pool_sample_kernel,
        grid=(1,),
        in_specs=[
            pl.BlockSpec((nrows, pw), lambda i: (0, col_blk)),
            pl.BlockSpec(buf_t.shape, lambda i: (0, 0, 0)),
            pl.BlockSpec(gw.shape, lambda i: (0, 0, 0)),
            pl.BlockSpec((1, pw), lambda i: (0, 0)),
        ],
        out_specs=pl.BlockSpec((nrows, pw), lambda i: (0, 0)),
        out_shape=jax.ShapeDtypeStruct((nrows, pw), BF16),
        compiler_params=_cparams(("arbitrary",)),
        name="pool_sample",
    )(z, buf_t, gw, scale)


def _head_sum(x):
    head_a = lax.broadcasted_iota(jnp.int32, (x.shape[0], LANES), 1) < HEAD
    cols = []
    for c in range(x.shape[1] // LANES):
        t = x[:, c * LANES:(c + 1) * LANES]
        sa = jnp.sum(jnp.where(head_a, t, 0.0), axis=-1, keepdims=True)
        sb = jnp.sum(jnp.where(head_a, 0.0, t), axis=-1, keepdims=True)
        cols.append(jnp.where(head_a, sa, sb))
    return jnp.concatenate(cols, axis=1)


def _prep_math(x, prev, p_refs):
    mu_ref, w0_ref, wup_ref, a0_ref, aup_ref, gup_ref, kk_ref, ka_ref, rk_ref = p_refs
    xm = x + (prev - x) * mu_ref[...]
    w = RWKV_W
    r = xm[:, 0:w]
    k = xm[:, w:2 * w]
    v = xm[:, 2 * w:3 * w]
    wl = xm[:, 3 * w:3 * w + 128]
    al = xm[:, 3 * w + 128:3 * w + 256]
    gl = xm[:, 3 * w + 256:3 * w + 512]
    dw = w0_ref[...] + jnp.dot(jnp.tanh(wl).astype(BF16), wup_ref[...], preferred_element_type=F32)
    lw = -EXP_M05 * jax.nn.sigmoid(dw)
    a = jax.nn.sigmoid(a0_ref[...] + jnp.dot(al.astype(BF16), aup_ref[...], preferred_element_type=F32))
    g = jnp.dot(jax.nn.sigmoid(gl).astype(BF16), gup_ref[...], preferred_element_type=F32)
    kk = k * kk_ref[...]
    kk = kk * lax.rsqrt(jnp.maximum(_head_sum(kk * kk), 1e-24))
    kmod = k * (1.0 + (a - 1.0) * ka_ref[...])
    bonus = _head_sum(r * kmod * rk_ref[...]) * v
    return r, lw, kmod, v, -kk, kk * a, g, bonus


def _group_norm_gate(y, bonus, g, lng_ref, lnb_ref):
    d = y - _head_sum(y) * (1.0 / HEAD)
    var = _head_sum(d * d) * (1.0 / HEAD)
    yn = d * lax.rsqrt(var + GN_EPS) * lng_ref[...] + lnb_ref[...]
    return (yn + bonus) * g


def _prep_sample_kernel(rkv_ref, lora_ref, prev_ref, *refs):
    p_refs, out_refs = refs[:9], refs[9:17]
    x = jnp.concatenate([rkv_ref[...], lora_ref[...]], axis=1)
    for o_ref, val in zip(out_refs, _prep_math(x, prev_ref[...], p_refs)):
        o_ref[...] = val


def _prep_param_specs(params):
    zero = (lambda *idx: (0, 0))
    return [pl.BlockSpec(p.shape, zero) for p in params]


def _prep_sample(z, prev, params):
    nrows = z.shape[0]
    out_spec = pl.BlockSpec((nrows, RWKV_W), lambda i: (0, 0))
    return pl.pallas_call(
        _prep_sample_kernel,
        grid=(1,),
        in_specs=[pl.BlockSpec((nrows, RKV_W), lambda i: (0, COL_RKV)),
                  pl.BlockSpec((nrows, LORA_W), lambda i: (0, COL_LORA)),
                  pl.BlockSpec((nrows, ZR_W), lambda i: (0, 0))] + _prep_param_specs(params),
        out_specs=[out_spec] * 8,
        out_shape=[jax.ShapeDtypeStruct((nrows, RWKV_W), F32)] * 8,
        compiler_params=_cparams(("arbitrary",)),
        name="prep_sample",
    )(z, z, prev, *params)


def _split3(x):
    hi = x.astype(BF16)
    rest = x - hi.astype(F32)
    mid = rest.astype(BF16)
    lo = (rest - mid.astype(F32)).astype(BF16)
    return hi, mid, lo


def _select_dot(sel, x):
    sel = sel.astype(BF16)
    hi, mid, lo = _split3(x)
    return _bdot(sel, hi) + (_bdot(sel, mid) + _bdot(sel, lo))


def _chunk_scan(r, lw, k, v, a, b, s_ref, c_len):
    n2 = 2 * c_len
    assert n2 == LANES
    row = lax.broadcasted_iota(jnp.int32, (n2, n2), 0)
    col = lax.broadcasted_iota(jnp.int32, (n2, n2), 1)
    tr = row & (c_len - 1)
    tc = col & (c_len - 1)
    strict = tr > tc
    incl = tr >= tc
    tri = (lax.broadcasted_iota(jnp.int32, (c_len, c_len), 0)
           >= lax.broadcasted_iota(jnp.int32, (c_len, c_len), 1))
    head_a = lax.broadcasted_iota(jnp.int32, (c_len, LANES), 1) < HEAD

    def stack(x):
        return jnp.concatenate([jnp.where(head_a, x, 0.0), jnp.where(head_a, 0.0, x)], axis=0).astype(BF16)

    prs = range(r.shape[1] // LANES)
    cat = jnp.concatenate
    sls = [slice(q * LANES, (q + 1) * LANES) for q in prs]
    lwq = [lw[:, sl] for sl in sls]
    cum = [_select_dot(tri, x) for x in lwq]
    tot = [x[c_len - 1:c_len, :] for x in cum]
    xr = [stack(r[:, sls[q]] * jnp.exp(cum[q])) for q in prs]
    xa = [stack(a[:, sls[q]] * jnp.exp(cum[q] - lwq[q])) for q in prs]
    e_neg = [jnp.exp(-x) for x in cum]
    e_rem = [jnp.exp(tot[q] - cum[q]) for q in prs]
    yb = [stack(b[:, sls[q]] * e_neg[q]) for q in prs]
    yk = [stack(k[:, sls[q]] * e_neg[q]) for q in prs]
    zb = [stack(b[:, sls[q]] * e_rem[q]) for q in prs]
    zk = [stack(k[:, sls[q]] * e_rem[q]) for q in prs]
    vs = [stack(v[:, sl]) for sl in sls]

    g = [_bdot(cat([xa[q], xr[q]], axis=0), cat([yb[q], yk[q]], axis=0), NT_DIMS) for q in prs]
    m_ab = [jnp.where(strict, x[0:n2, 0:n2], 0.0) for x in g]
    m_ak = [jnp.where(strict, x[0:n2, n2:2 * n2], 0.0).astype(BF16) for x in g]
    n_rb = [jnp.where(incl, x[n2:2 * n2, 0:n2], 0.0) for x in g]
    n_rk = [jnp.where(incl, x[n2:2 * n2, n2:2 * n2], 0.0).astype(BF16) for x in g]

    s_old = [s_ref[q] for q in prs]
    lhs = [cat([cat([xa[q], m_ak[q]], axis=1), cat([xr[q], n_rk[q]], axis=1)], axis=0) for q in prs]
    xy0 = [_bdot(lhs[q], cat([s_old[q].T.astype(BF16), vs[q]], axis=0)) for q in prs]

    x = [xy0[q][0:n2] for q in prs]
    mk = m_ab
    nlev = int(math.log2(c_len))
    for lev in range(nlev):
        if lev < nlev - 1:
            res = [_bdot(mk[q], cat([mk[q], x[q]], axis=1)) for q in prs]
            mk = [t[:, 0:n2] for t in res]
            x = [x[q] + res[q][:, n2:2 * n2] for q in prs]
        else:
            x = [x[q] + _bdot(mk[q], x[q]) for q in prs]
    u = [t.astype(BF16) for t in x]

    y_st = [xy0[q][n2:2 * n2] + _bdot(n_rb[q], u[q]) for q in prs]
    for q in prs:
        s_ref[q] = s_old[q] * jnp.exp(tot[q]) + _bdot(cat([u[q], vs[q]], axis=0), cat([zb[q], zk[q]], axis=0),
                                                     TN_DIMS)
    return cat([t[0:c_len] + t[c_len:n2] for t in y_st], axis=1)


def _rwkv_prompt_kernel(rkv_ref, lora_ref, *refs, nc):
    p_refs = refs[:9]
    lng_ref, lnb_ref, yb_ref, sout_ref, s_ref, carry_ref = refs[9:]
    c = pl.program_id(1)
    nseq, c_len, _ = rkv_ref.shape

    @pl.when(c == 0)
    def _():
        s_ref[...] = jnp.zeros_like(s_ref)
        carry_ref[...] = jnp.zeros_like(carry_ref)

    first = lax.broadcasted_iota(jnp.int32, (c_len, ZR_W), 0) == 0
    preps = []
    for q in range(nseq):
        x = jnp.concatenate([rkv_ref[q], lora_ref[q]], axis=1)
        rolled = pltpu.roll(x, 1, axis=0)
        prev = jnp.where(first, carry_ref[q, 0:1, :], rolled)
        carry_ref[q] = rolled[0:SUBLANES, :]
        preps.append(_prep_math(x, prev, p_refs))
    r, lw, k, v, a, b = (jnp.concatenate([p[i] for p in preps], axis=1) for i in range(6))
    y = _chunk_scan(r, lw, k, v, a, b, s_ref, c_len)
    for q in range(nseq):
        yq = y[:, q * RWKV_W:(q + 1) * RWKV_W]
        yb_ref[q] = _group_norm_gate(yq, preps[q][7], preps[q][6], lng_ref, lnb_ref).astype(yb_ref.dtype)

    @pl.when(c == nc - 1)
    def _():
        npair = s_ref.shape[0] // nseq
        for q in range(nseq):
            for p in range(npair):
                s = s_ref[q * npair + p]
                sout_ref[q, 2 * p] = s[0:HEAD, 0:HEAD]
                sout_ref[q, 2 * p + 1] = s[HEAD:2 * HEAD, HEAD:2 * HEAD]


def _rwkv_prompt(z, nb, t_len, params, ln_g, ln_b, c_len=64, nseq=2):
    nc = t_len // c_len
    npair = RWKV_W // LANES
    par = pl.BlockSpec((1, RWKV_W), lambda b, c: (0, 0))
    z3 = z.reshape(nb, t_len, z.shape[1])
    yb, state = pl.pallas_call(
        functools.partial(_rwkv_prompt_kernel, nc=nc),
        grid=(nb // nseq, nc),
        in_specs=[pl.BlockSpec((nseq, c_len, RKV_W), lambda b, c: (b, c, COL_RKV)),
                  pl.BlockSpec((nseq, c_len, LORA_W), lambda b, c: (b, c, COL_LORA))]
        + _prep_param_specs(params) + [par, par],
        out_specs=[pl.BlockSpec((nseq, c_len, RWKV_W), lambda b, c: (b, c, 0)),
                   pl.BlockSpec((nseq, 2 * npair, HEAD, HEAD), lambda b, c: (b, 0, 0, 0))],
        out_shape=[jax.ShapeDtypeStruct((nb, t_len, RWKV_W), BF16),
                   jax.ShapeDtypeStruct((nb, 2 * npair, HEAD, HEAD), F32)],
        scratch_shapes=[pltpu.VMEM((nseq * npair, LANES, LANES), F32), pltpu.VMEM((nseq, SUBLANES, ZR_W), F32)],
        compiler_params=_cparams(("parallel", "arbitrary")),
        name="rwkv_prompt",
    )(z3, z3, *params, ln_g, ln_b)
    return yb.reshape(nb * t_len, RWKV_W), state


def _wkv_step_kernel(s_ref, r_ref, lw_ref, k_ref, v_ref, a_ref, b_ref, g_ref, bonus_ref, lng_ref, lnb_ref,
                     yb_ref, so_ref, vt_scr, y_scr):
    heads = s_ref.shape[0]
    rt, wt, kt, at, bt = (x[...].T for x in (r_ref, lw_ref, k_ref, a_ref, b_ref))
    wt = jnp.exp(wt)
    vt_scr[...] = v_ref[...].T
    for hh in range(heads):
        rows = slice(hh * HEAD, (hh + 1) * HEAD)
        r, w, k, a, b = (x[rows, :] for x in (rt, wt, kt, at, bt))

        def body(i, carry):
            si = s_ref[hh, i]
            sa = jnp.sum(si * a, axis=0, keepdims=True)
            vi = vt_scr[pl.ds(hh * HEAD + i, 1), :]
            s2 = si * w + sa * b + vi * k
            so_ref[hh, i] = s2
            y_scr[pl.ds(hh * HEAD + i, 1), :] = jnp.sum(s2 * r, axis=0, keepdims=True)
            return carry

        lax.fori_loop(0, HEAD, body, 0, unroll=4)

    outs = []
    for hh in range(heads):
        y = y_scr[hh * HEAD:(hh + 1) * HEAD, :]
        d = y - jnp.mean(y, axis=0, keepdims=True)
        var = jnp.mean(d * d, axis=0, keepdims=True)
        outs.append(d * lax.rsqrt(var + GN_EPS))
    yn = jnp.concatenate(outs, axis=0).T * lng_ref[...] + lnb_ref[...]
    yb_ref[...] = ((yn + bonus_ref[...]) * g_ref[...]).astype(yb_ref.dtype)


def _wkv_step(state_t, vecs, ln_g, ln_b, heads=4):
    nh, _, _, nb = state_t.shape
    st = pl.BlockSpec((heads, HEAD, HEAD, nb), lambda p: (p, 0, 0, 0))
    vec = pl.BlockSpec((nb, heads * HEAD), lambda p: (0, p))
    par = pl.BlockSpec((1, heads * HEAD), lambda p: (0, p))
    return pl.pallas_call(
        _wkv_step_kernel,
        grid=(nh // heads,),
        in_specs=[st] + [vec] * 8 + [par, par],
        out_specs=[vec, st],
        out_shape=[jax.ShapeDtypeStruct((nb, nh * HEAD), BF16), jax.ShapeDtypeStruct(state_t.shape, F32)],
        scratch_shapes=[pltpu.VMEM((heads * HEAD, nb), F32), pltpu.VMEM((heads * HEAD, nb), F32)],
        compiler_params=_cparams(("parallel",)),
        name="wkv_step",
    )(state_t, *vecs, ln_g, ln_b)


def _xattn_prompt_tile(q, k_ref, v_ref):
    scale = XA_DIM ** -0.5
    cols = []
    for h in range(XA_HEADS):
        sl = slice(h * XA_DIM, (h + 1) * XA_DIM)
        s = lax.dot_general(q[:, sl].astype(BF16), k_ref[0, :, sl].astype(BF16),
                            NT_DIMS, preferred_element_type=F32) * scale
        p = jnp.exp(s - jnp.max(s, axis=-1, keepdims=True))
        den = jnp.sum(p, axis=-1, keepdims=True)
        o = jnp.dot(p.astype(BF16), v_ref[0, :, sl].astype(BF16), preferred_element_type=F32)
        cols.append((o / den).astype(BF16))
    return jnp.concatenate(cols, axis=1)


def _xattn_sample_kernel(q_ref, k_ref, v_ref, o_ref):
    bb = q_ref.shape[0]
    nrow = k_ref.shape[1] // SUBLANES
    full = (bb, nrow, SUBLANES, XA_DIM)
    q = q_ref[...] * (XA_DIM ** -0.5)
    q8 = jnp.concatenate([q, q], axis=1)[:, None]
    k = k_ref[...].reshape(full)
    s = jnp.broadcast_to(jnp.sum(k * q8, axis=-1, keepdims=True), full)
    mx = jnp.max(s, axis=1, keepdims=True)
    mx = jnp.maximum(mx, pltpu.roll(mx, XA_HEADS, axis=2))
    p = jnp.exp(s - mx)
    den = jnp.sum(p, axis=1, keepdims=True)
    den = den + pltpu.roll(den, XA_HEADS, axis=2)
    o = jnp.sum(p * v_ref[...].reshape(full), axis=1, keepdims=True)
    o = o + pltpu.roll(o, XA_HEADS, axis=2)
    o_ref[...] = (o / den)[:, 0, 0:XA_HEADS, :].astype(o_ref.dtype)


def _xattn_sample(q3, mk, mv, bb=16):
    nb, rows, _ = mk.shape
    kv = pl.BlockSpec((bb, rows, XA_DIM), lambda i: (i, 0, 0))
    qs = pl.BlockSpec((bb, XA_HEADS, XA_DIM), lambda i: (i, 0, 0))
    return pl.pallas_call(
        _xattn_sample_kernel,
        grid=(nb // bb,),
        in_specs=[qs, kv, kv],
        out_specs=qs,
        out_shape=jax.ShapeDtypeStruct((nb, XA_HEADS, XA_DIM), BF16),
        compiler_params=_cparams(("parallel",)),
        name="xattn_sample",
    )(q3, mk, mv)


def _merge_math(pa, pb, pc, g0_ref, g1_ref, g2_ref, h_ref, wa_ref, wb_ref, wc_ref, wo_ref, o_ref):
    oa = jnp.dot(pa, wa_ref[...], preferred_element_type=F32)
    ob = jnp.dot(pb, wb_ref[...], preferred_element_type=F32)
    oc = jnp.dot(pc, wc_ref[...], preferred_element_type=F32)
    merged = (jax.nn.sigmoid(g0_ref[...]) * oa + jax.nn.sigmoid(g1_ref[...]) * ob
              + jax.nn.sigmoid(g2_ref[...]) * oc)
    o_ref[...] = h_ref[...] + jnp.dot(merged.astype(BF16), wo_ref[...], preferred_element_type=F32)


def _merge_kernel(pa_ref, pb_ref, pc_ref, *refs):
    _merge_math(pa_ref[...], pb_ref[...], pc_ref[...], *refs)


def _merge_prompt_kernel(zp_ref, zq_ref, mk_ref, mv_ref, gw_ref, scale_ref, pb_ref, *refs, tiles_per_seq):
    merge_refs, ext_ref = refs[:-1], refs[-1]
    t = pl.program_id(0) % tiles_per_seq
    pa = _pool_prompt_tile(zp_ref[...], t, ext_ref, gw_ref, scale_ref)
    pc = _xattn_prompt_tile(zq_ref[...], mk_ref, mv_ref)
    _merge_math(pa, pb_ref[...], pc, *merge_refs)


def _merge_prompt(z, nb, t_len, pb, h, tm, mk, mv, gw, scale, wa, wb, wc, wo):
    m, d = h.shape
    tiles_per_seq = t_len // tm
    const = lambda i: (0, 0)
    resident = lambda w: pl.BlockSpec(w.shape, const, pipeline_mode=pl.Buffered(1))
    kv = pl.BlockSpec((1,) + mk.shape[1:], lambda i: (i // tiles_per_seq, 0, 0))
    pw = gw.shape[0] * LANES
    in_specs = [
        pl.BlockSpec((tm, pw), lambda i: (i, COL_ZP)),
        pl.BlockSpec((tm, mk.shape[2]), lambda i: (i, COL_ZQ)),
        kv, kv,
        pl.BlockSpec(gw.shape, lambda i: (0, 0, 0)),
        pl.BlockSpec((1, pw), const),
        pl.BlockSpec((tm, pb.shape[1]), lambda i: (i, 0)),
        pl.BlockSpec((tm, d), lambda i: (i, 0)),
        pl.BlockSpec((tm, d), lambda i: (i, 1)),
        pl.BlockSpec((tm, d), lambda i: (i, 2)),
        pl.BlockSpec((tm, d), lambda i: (i, 0)),
        resident(wa), resident(wb), resident(wc), resident(wo),
    ]
    return pl.pallas_call(
        functools.partial(_merge_prompt_kernel, tiles_per_seq=tiles_per_seq),
        grid=(m // tm,),
        in_specs=in_specs,
        out_specs=pl.BlockSpec((tm, d), lambda i: (i, 0)),
        out_shape=jax.ShapeDtypeStruct((m, d), F32),
        scratch_shapes=[pltpu.VMEM((tm + POOL_HIST, pw), F32)],
        compiler_params=_cparams(("arbitrary",)),
        name="merge_prompt",
    )(z, z, mk, mv, gw, scale, pb, z, z, z, h, wa, wb, wc, wo)


def _merge(pa, pb, pc, z, zg_blk0, h, tm, wa, wb, wc, wo):
    m, d = h.shape
    const = lambda i: (0, 0)
    resident = lambda w: pl.BlockSpec(w.shape, const, pipeline_mode=pl.Buffered(1))
    in_specs = [
        pl.BlockSpec((tm, pa.shape[1]), lambda i: (i, 0)),
        pl.BlockSpec((tm, pb.shape[1]), lambda i: (i, 0)),
        pl.BlockSpec((tm, pc.shape[1]), lambda i: (i, 0)),
        pl.BlockSpec((tm, d), lambda i: (i, zg_blk0)),
        pl.BlockSpec((tm, d), lambda i: (i, zg_blk0 + 1)),
        pl.BlockSpec((tm, d), lambda i: (i, zg_blk0 + 2)),
        pl.BlockSpec((tm, d), lambda i: (i, 0)),
        resident(wa), resident(wb), resident(wc), resident(wo),
    ]
    return pl.pallas_call(
        _merge_kernel,
        grid=(m // tm,),
        in_specs=in_specs,
        out_specs=pl.BlockSpec((tm, d), lambda i: (i, 0)),
        out_shape=jax.ShapeDtypeStruct((m, d), F32),
        compiler_params=_cparams(("parallel",)),
        name="merge",
    )(pa, pb, pc, z, z, z, h, wa, wb, wc, wo)


def _pack_lora(x, axis):
    take = lambda a, b: lax.slice_in_dim(x, a, b, axis=axis)

    def pad(n):
        shape = list(x.shape)
        shape[axis] = n
        return jnp.zeros(shape, x.dtype)

    return jnp.concatenate([take(0, 64), pad(64), take(64, 128), pad(64), take(128, ZR_TRUE - RKV_W), pad(96)],
                           axis=axis)


def _pack_zr(x, axis):
    return jnp.concatenate([lax.slice_in_dim(x, 0, RKV_W, axis=axis),
                            _pack_lora(lax.slice_in_dim(x, RKV_W, ZR_TRUE, axis=axis), axis)], axis=axis)


def _unpack_zr_cols(x):
    w = RWKV_W
    return jnp.concatenate([x[..., :3 * w + 64], x[..., 3 * w + 128:3 * w + 192], x[..., 3 * w + 256:3 * w + 416]],
                           axis=-1)


def _pad_rows(x, n):
    return jnp.concatenate([x, jnp.zeros((n - x.shape[0],) + x.shape[1:], x.dtype)], axis=0)


def kernel(x_prompt, x_sample, mem_prompt, cache_mem_k, cache_mem_v, state_wkv, state_shift, state_pool,
           ffn1_norm_g, ffn1_w_gate, ffn1_w_up, ffn1_w_down, mix_norm_g, w_in,
           pool_group_w, pool_scale, pool_out,
           rwkv_mu, rwkv_w0, rwkv_w_up, rwkv_a0, rwkv_a_up, rwkv_g_up, rwkv_k_k, rwkv_k_a, rwkv_r_k,
           rwkv_ln_g, rwkv_ln_b, rwkv_out,
           mem_norm_g, w_mem_k, w_mem_v, xattn_out, w_o,
           ffn2_norm_g, ffn2_w_gate, ffn2_w_up, ffn2_w_down, final_norm_g):
    nb, t_len, d = x_prompt.shape
    ns = x_sample.shape[0]
    assert w_in.shape[0] == 1 and x_sample.shape[1] == 1
    n_mem = mem_prompt.shape[1]
    pool_w = pool_out.shape[1]
    xa_w = xattn_out.shape[1]
    n_heads = RWKV_W // HEAD
    nbuf = state_pool.shape[2]
    rows_p = nb * t_len
    l = 0

    f1 = [w.reshape(w.shape[1:]).astype(BF16) for w in (ffn1_w_gate, ffn1_w_up, ffn1_w_down)]
    f2 = [w.reshape(w.shape[1:]).astype(BF16) for w in (ffn2_w_gate, ffn2_w_up, ffn2_w_down)]
    wit = jnp.swapaxes(w_in[l], 0, 1).astype(BF16)
    o_zr, o_zq, o_zg = pool_w, pool_w + ZR_TRUE, pool_w + ZR_TRUE + xa_w
    assert (pool_w, xa_w, 3 * d) == (LORA_W, LORA_W, GATES_W)
    w_misc = jnp.concatenate([_pack_lora(wit[o_zr + RKV_W:o_zq], 0), wit[:o_zr], wit[o_zq:o_zg]], axis=0)
    in_segments = [(o_zg, GATES_W), (o_zr, RKV_W)]
    o_rkv, o_zp_out, o_zq_out = GATES_W, GATES_W + ZR_W, GATES_W + ZR_W + pool_w
    row = lambda v: v.reshape(1, -1)
    prep_params = [row(_pack_zr(rwkv_mu[l], 0)), row(rwkv_w0[l]), _pad_rows(rwkv_w_up[l], 128).astype(BF16),
                   row(rwkv_a0[l]), _pad_rows(rwkv_a_up[l], 128).astype(BF16),
                   _pad_rows(rwkv_g_up[l], 256).astype(BF16), row(rwkv_k_k[l]), row(rwkv_k_a[l]),
                   row(rwkv_r_k[l])]
    gw = pool_group_w[l].astype(BF16)
    w_kv_t = jnp.concatenate([jnp.swapaxes(w_mem_k[l], 0, 1), jnp.swapaxes(w_mem_v[l], 0, 1)], axis=0).astype(BF16)
    wa, wb, wc, wo = (pool_out[l].astype(BF16), rwkv_out[l].astype(BF16), xattn_out[l].astype(BF16),
                      w_o[l].astype(BF16))
    g1, gm, g2, fg = row(ffn1_norm_g[l]), row(mix_norm_g[l]), row(ffn2_norm_g[l]), row(final_norm_g)
    scale = row(pool_scale[l])
    ln_g, ln_b = rwkv_ln_g[l], rwkv_ln_b[l]

    tm_p = 512
    h1_p, h1_s = _ffn(x_prompt.reshape(rows_p, d), x_sample.reshape(ns, d), tm_p, g1, *f1)
    z_p, z_s = _in_proj(h1_p, h1_s, gm, wit, in_segments, w_misc, tm=1024, tn=1536)
    kv = _norm_matmul(mem_prompt.reshape(nb * n_mem, d), row(mem_norm_g[l]), w_kv_t, tm=512, tn=512)
    mk_p = kv[:, :xa_w].reshape(nb, n_mem, xa_w)
    mv_p = kv[:, xa_w:].reshape(nb, n_mem, xa_w)

    pool_state = state_pool.reshape(ns, nbuf, pool_w)
    pa_s = _pool_sample(z_s, COL_ZP, jnp.swapaxes(pool_state, 0, 1), gw, scale)

    pb_p, st_p = _rwkv_prompt(z_p, nb, t_len, prep_params, row(ln_g), row(ln_b))
    prep_s = _prep_sample(z_s, _pack_zr(state_shift.reshape(ns, ZR_TRUE), 1), prep_params)
    state_t = jnp.transpose(state_wkv.reshape(ns, n_heads, HEAD, HEAD), (1, 2, 3, 0))
    pb_s, wkv_s_t = _wkv_step(state_t, prep_s, row(ln_g), row(ln_b))

    q_s = z_s[:, o_zq_out:].reshape(ns, XA_HEADS, XA_DIM)
    pc_s = _xattn_sample(q_s, cache_mem_k.reshape(ns, n_mem * XA_HEADS, XA_DIM),
                         cache_mem_v.reshape(ns, n_mem * XA_HEADS, XA_DIM)).reshape(ns, xa_w)

    h2_p = _merge_prompt(z_p, nb, t_len, pb_p, h1_p, 256, mk_p, mv_p, gw, scale, wa, wb, wc, wo)
    h2_s = _merge(pa_s, pb_s, pc_s, z_s, 0, h1_s, ns, wa, wb, wc, wo)
    y_prompt, y_sample = _ffn(h2_p, h2_s, tm_p, g2, *f2, final_g=fg)

    ends = [(b + 1) * t_len for b in range(nb)]
    shift_p = _unpack_zr_cols(jnp.stack([z_p[e - 1:e, o_rkv:o_zp_out] for e in ends]))[None]
    pool_p = jnp.stack([z_p[e - nbuf:e, o_zp_out:o_zq_out] for e in ends])[None]
    shift_s = _unpack_zr_cols(z_s[:, o_rkv:o_zp_out])[None, :, None, :]
    pool_s = jnp.concatenate([pool_state[:, 1:], z_s[:, None, o_zp_out:o_zq_out]], axis=1)[None]
    wkv_p = st_p[None]
    wkv_s = jnp.transpose(wkv_s_t, (3, 0, 1, 2)).reshape(state_wkv.shape)
    mem_k_p = mk_p.reshape(1, nb, n_mem, XA_HEADS, XA_DIM)
    mem_v_p = mv_p.reshape(1, nb, n_mem, XA_HEADS, XA_DIM)
    return (y_prompt.reshape(nb, t_len, d), y_sample.reshape(ns, 1, d), mem_k_p, mem_v_p, wkv_p, shift_p, pool_p,
            wkv_s, shift_s, pool_s)
```
